```python
import jax
import jax.numpy as jnp
from jax import lax
import numpy as np

D_MODEL = 2048
BATCH = 8
SEQ = 2048
DEPTH = 2

GRID_W = 64
CTX_LEN = 256
EPS = 1e-6

NA_HEAD_DIM = 128
NA_HEADS = D_MODEL // 256
NA_WIDTH = NA_HEADS * NA_HEAD_DIM
NA_WIN_ROWS = 8
NA_WIN_COLS = 16
NA_QCOL_BLOCK = 16
NA_KCOL_BLOCK = NA_QCOL_BLOCK + NA_WIN_COLS

HG_HEADS = D_MODEL // 512
HG_KEY_DIM = 128
HG_VAL_DIM = 128
HG_KEY_WIDTH = HG_HEADS * HG_KEY_DIM
HG_WIDTH = HG_HEADS * HG_VAL_DIM
HG_CHUNK = 64

GM_GROUPS = D_MODEL // 512
GM_GROUP_DIM = 128
GM_WIDTH = GM_GROUPS * GM_GROUP_DIM
GM_CHUNK = 128

MIX_WIDTH = NA_WIDTH + HG_WIDTH + GM_WIDTH
IN_SPLITS = [NA_WIDTH] * 3 + [HG_KEY_WIDTH] * 3 + [HG_WIDTH] * 2 + [GM_WIDTH] * 2
IN_WIDTH = sum(IN_SPLITS)
MLP_HIDDEN = 4 * D_MODEL

kernel_name = "hybrid_na_hgrn2_gmlp_dit_trunk"


def _rmsnorm(x, w):
    xf = x.astype(jnp.float32)
    y = xf * lax.rsqrt(jnp.mean(xf * xf, axis=-1, keepdims=True) + EPS)
    return (y * w.astype(jnp.float32)).astype(x.dtype)


def _ada(cond, w, b):
    m = jnp.einsum('...d,de->...e', jax.nn.silu(cond), w) + b
    return jnp.split(m[..., None, :], 6, axis=-1)


def _modulate(h, shift, scale):
    return h * (1 + scale) + shift


def _split_heads(t, h):
    bsz, n, _ = t.shape
    return t.reshape(bsz, n, h, -1).transpose(0, 2, 1, 3)


def _merge_heads(t):
    bsz, h, n, d = t.shape
    return t.transpose(0, 2, 1, 3).reshape(bsz, n, h * d)


def _dense_attention(q, k, v):
    s = jnp.einsum('bhqd,bhkd->bhqk', q, k).astype(jnp.float32) * (q.shape[-1] ** -0.5)
    p = jax.nn.softmax(s, axis=-1).astype(v.dtype)
    return jnp.einsum('bhqk,bhkd->bhqd', p, v)


def _na_latent(q, k, v, k_ctx, v_ctx, rpb):
    bsz, h, t, dh = q.shape
    rows = t // GRID_W
    kr = min(NA_WIN_ROWS, rows)
    n_cb = GRID_W // NA_QCOL_BLOCK
    scale = dh ** -0.5
    qcol = np.arange(GRID_W).reshape(n_cb, NA_QCOL_BLOCK)
    kstart = np.clip(np.arange(n_cb) * NA_QCOL_BLOCK - NA_WIN_COLS // 2, 0, GRID_W - NA_KCOL_BLOCK)
    kcol = kstart[:, None] + np.arange(NA_KCOL_BLOCK)[None, :]
    wstart = np.clip(qcol - NA_WIN_COLS // 2, 0, GRID_W - NA_WIN_COLS)
    col_valid = (kcol[:, None, :] >= wstart[:, :, None]) & (kcol[:, None, :] < wstart[:, :, None] + NA_WIN_COLS)
    dcol = np.clip(kcol[:, None, :] - qcol[:, :, None], 1 - NA_WIN_COLS, NA_WIN_COLS - 1) + NA_WIN_COLS - 1
    col_bias = rpb.astype(jnp.float32)[:, :, dcol]
    qg = q.reshape(bsz, h, rows, n_cb, NA_QCOL_BLOCK, dh)
    kg = k.reshape(bsz, h, rows, GRID_W, dh)
    vg = v.reshape(bsz, h, rows, GRID_W, dh)
    n_lat = kr * NA_KCOL_BLOCK

    def row_block(r):
        r0 = jnp.clip(r - kr // 2, 0, rows - kr)
        q_r = lax.dynamic_index_in_dim(qg, r, axis=2, keepdims=False)
        k_r = lax.dynamic_slice_in_dim(kg, r0, kr, axis=2)[:, :, :, kcol]
        v_r = lax.dynamic_slice_in_dim(vg, r0, kr, axis=2)[:, :, :, kcol]
        s_lat = jnp.einsum('bhjqd,bhrjkd->bhjqrk', q_r, k_r).astype(jnp.float32) * scale
        drow = r0 + jnp.arange(kr) - r + (NA_WIN_ROWS - 1)
        bias = jnp.take(col_bias, drow, axis=1).transpose(0, 2, 3, 1, 4)
        s_lat = jnp.where(col_valid[:, :, None, :], s_lat + bias, -jnp.inf)
        s_ctx = jnp.einsum('bhjqd,bhcd->bhjqc', q_r, k_ctx).astype(jnp.float32) * scale
        s_all = jnp.concatenate([s_lat.reshape(bsz, h, n_cb, NA_QCOL_BLOCK, n_lat), s_ctx], axis=-1)
        p = jax.nn.softmax(s_all, axis=-1).astype(v.dtype)
        p_lat = p[..., :n_lat].reshape(bsz, h, n_cb, NA_QCOL_BLOCK, kr, NA_KCOL_BLOCK)
        o = (jnp.einsum('bhjqrk,bhrjkd->bhjqd', p_lat, v_r)
             + jnp.einsum('bhjqc,bhcd->bhjqd', p[..., n_lat:], v_ctx))
        return o.reshape(bsz, h, GRID_W, dh)

    o = lax.map(row_block, jnp.arange(rows))
    return o.transpose(1, 2, 0, 3, 4).reshape(bsz, h, t, dh)


def _forget_gate(f_logits, lb):
    lb = lb.reshape(HG_HEADS, 1, HG_KEY_DIM).astype(jnp.float32)
    log_f = jnp.logaddexp(jnp.log(lb), jnp.log1p(-lb) + jax.nn.log_sigmoid(f_logits))
    return log_f, -jnp.expm1(log_f)


def _gla_chunk_scan(q, k, v, log_f, s0):
    bsz, h, t, dk = q.shape
    dv = v.shape[-1]
    n = t // HG_CHUNK

    def chunks(a):
        return a.reshape(bsz, h, n, HG_CHUNK, a.shape[-1]).transpose(2, 0, 1, 3, 4)

    incl = np.tril(np.ones((HG_CHUNK, HG_CHUNK), dtype=bool))[:, :, None]

    def step(s, xs):
        qc, kc, vc, gc = xs
        b = jnp.cumsum(gc, axis=2)
        diff = b[:, :, :, None, :] - b[:, :, None, :, :]
        decay = jnp.exp(jnp.where(incl, diff, -jnp.inf))
        att = jnp.einsum('bhtd,bhsd,bhtsd->bhts', qc, kc, decay)
        o = (jnp.einsum('bhts,bhsv->bhtv', att, vc)
             + jnp.einsum('bhtd,bhdv->bhtv', qc * jnp.exp(b), s))
        b_end = b[:, :, -1:, :]
        s_new = (jnp.exp(b_end[:, :, 0, :, None]) * s
                 + jnp.einsum('bhsd,bhsv->bhdv', kc * jnp.exp(b_end - b), vc))
        return s_new, o

    s_fin, o = lax.scan(step, s0, (chunks(q), chunks(k), chunks(v), chunks(log_f)))
    return o.transpose(1, 2, 0, 3, 4).reshape(bsz, h, t, dv), s_fin


def _gate_norm(o, g, w):
    o = o * lax.rsqrt(jnp.mean(o * o, axis=-1, keepdims=True) + EPS) * w.astype(jnp.float32)
    return (_merge_heads(o) * jax.nn.silu(g.astype(jnp.float32))).astype(g.dtype)


def _hgrn2(parts_c, parts_l, lb_fw, lb_bw, norm_w, need_ctx):
    scale = HG_KEY_DIM ** -0.5

    def prep(parts):
        q, f_fw, f_bw, i, g = parts
        hk = lambda t: _split_heads(t, HG_HEADS).astype(jnp.float32)
        return jax.nn.silu(hk(q)) * scale, hk(f_fw), hk(f_bw), hk(i), g

    qc, fwc, bwc, ic, gc = prep(parts_c)
    ql, fwl, bwl, il, gl = prep(parts_l)
    flip = lambda a: jnp.flip(a, axis=2)
    outs_c, outs_l = [], []
    for fc, fl, lb, rev in ((fwc, fwl, lb_fw, False), (bwc, bwl, lb_bw, True)):
        logf_c, k_c = _forget_gate(fc, lb)
        logf_l, k_l = _forget_gate(fl, lb)
        seq_c = (qc, k_c, ic, logf_c)
        seq_l = (ql, k_l, il, logf_l)
        if rev:
            seq_c = tuple(flip(a) for a in seq_c)
            seq_l = tuple(flip(a) for a in seq_l)
        s0 = jnp.zeros((qc.shape[0], HG_HEADS, HG_KEY_DIM, HG_VAL_DIM), jnp.float32)
        o_c, s_ctx = _gla_chunk_scan(seq_c[0], seq_c[1], seq_c[2], seq_c[3], s0)
        o_l, _ = _gla_chunk_scan(seq_l[0], seq_l[1], seq_l[2], seq_l[3], s_ctx)
        if rev:
            o_c, o_l = flip(o_c), flip(o_l)
        outs_c.append(o_c)
        outs_l.append(o_l)
    out_l = _gate_norm(outs_l[0] + outs_l[1], gl, norm_w)
    out_c = _gate_norm(outs_c[0] + outs_c[1], gc, norm_w) if need_ctx else None
    return out_c, out_l


def _chunk_gmlp(u, v, ln_w, ws, bs):
    bsz, t, _ = u.shape
    n = t // GM_CHUNK
    uf = jax.nn.gelu(u.astype(jnp.float32))
    vf = jax.nn.gelu(v.astype(jnp.float32)).reshape(bsz, n, GM_CHUNK, GM_GROUPS, GM_GROUP_DIM)
    mu = jnp.mean(vf, axis=-1, keepdims=True)
    var = jnp.mean(jnp.square(vf - mu), axis=-1, keepdims=True)
    vn = (vf - mu) * lax.rsqrt(var + EPS) * ln_w.astype(jnp.float32).reshape(GM_GROUPS, GM_GROUP_DIM)
    mixed = jnp.einsum('gpq,bnqgd->bnpgd', ws.astype(jnp.float32), vn) + bs.astype(jnp.float32).T[:, :, None]
    return (uf * mixed.reshape(bsz, t, GM_WIDTH)).astype(u.dtype)


def _token_mixers(hc, hl, w_in, w_out, rpb, lb_fw, lb_bw, hg_norm_w, gm_ln_w, gm_ws, gm_bs, need_ctx):
    dt = hl.dtype
    idx = np.cumsum(IN_SPLITS)[:-1].tolist()
    pc = jnp.split(jnp.einsum('btd,de->bte', hc, w_in), idx, axis=-1)
    pl = jnp.split(jnp.einsum('btd,de->bte', hl, w_in), idx, axis=-1)
    ka_c, va_c = _split_heads(pc[1], NA_HEADS), _split_heads(pc[2], NA_HEADS)
    oa_l = _na_latent(_split_heads(pl[0], NA_HEADS), _split_heads(pl[1], NA_HEADS),
                      _split_heads(pl[2], NA_HEADS), ka_c, va_c, rpb)
    ob_c, ob_l = _hgrn2(pc[3:8], pl[3:8], lb_fw, lb_bw, hg_norm_w, need_ctx)
    oc_l = _chunk_gmlp(pl[8], pl[9], gm_ln_w, gm_ws, gm_bs)
    y_l = jnp.einsum('bte,ed->btd', jnp.concatenate([_merge_heads(oa_l).astype(dt), ob_l.astype(dt), oc_l.astype(dt)], axis=-1), w_out)
    if not need_ctx:
        return None, y_l
    oa_c = _dense_attention(_split_heads(pc[0], NA_HEADS), ka_c, va_c)
    oc_c = _chunk_gmlp(pc[8], pc[9], gm_ln_w, gm_ws, gm_bs)
    y_c = jnp.einsum('bte,ed->btd', jnp.concatenate([_merge_heads(oa_c).astype(dt), ob_c.astype(dt), oc_c.astype(dt)], axis=-1), w_out)
    return y_c, y_l


def _sq_relu_mlp(h, w1, w2):
    a = jax.nn.relu(jnp.einsum('btd,df->btf', h, w1))
    return jnp.einsum('btf,fd->btd', a * a, w2)


def setup_inputs(seed: int = 0) -> dict:
    key = jax.random.key(seed)
    ks = jax.random.split(key, 19)
    nrm = jax.random.normal
    f32 = jnp.float32
    d = D_MODEL
    return {
        "x": nrm(ks[0], (BATCH, SEQ, d), f32),
        "c": nrm(ks[1], (BATCH, d), f32),
        "ctx": nrm(ks[2], (BATCH, CTX_LEN, d), f32),
        "c_ctx": nrm(ks[3], (d,), f32),
        "ada_w": nrm(ks[4], (DEPTH, d, 6 * d), f32) * (0.5 * d ** -0.5),
        "ada_b": 0.02 * nrm(ks[5], (DEPTH, 6 * d), f32),
        "norm1_w": 1.0 + 0.02 * nrm(ks[6], (DEPTH, d), f32),
        "norm2_w": 1.0 + 0.02 * nrm(ks[7], (DEPTH, d), f32),
        "w_in": nrm(ks[8], (DEPTH, d, IN_WIDTH), f32) * d ** -0.5,
        "na_rpb": 0.1 * nrm(ks[9], (DEPTH, NA_HEADS, 2 * NA_WIN_ROWS - 1, 2 * NA_WIN_COLS - 1), f32),
        "hg_lb_logits": 0.5 * nrm(ks[10], (DEPTH, 2, HG_KEY_WIDTH), f32),
        "hg_norm_w": 1.0 + 0.02 * nrm(ks[11], (DEPTH, HG_VAL_DIM), f32),
        "gm_ln_w": 1.0 + 0.02 * nrm(ks[12], (DEPTH, GM_WIDTH), f32),
        "gm_ws": nrm(ks[13], (DEPTH, GM_GROUPS, GM_CHUNK, GM_CHUNK), f32) * GM_CHUNK ** -0.5,
        "gm_bs": 1.0 + 0.02 * nrm(ks[14], (DEPTH, GM_GROUPS, GM_CHUNK), f32),
        "w_out": nrm(ks[15], (DEPTH, MIX_WIDTH, d), f32) * MIX_WIDTH ** -0.5,
        "mlp_w1": nrm(ks[16], (DEPTH, d, MLP_HIDDEN), f32) * d ** -0.5,
        "mlp_w2": nrm(ks[17], (DEPTH, MLP_HIDDEN, d), f32) * MLP_HIDDEN ** -0.5,
        "final_norm_w": 1.0 + 0.02 * nrm(ks[18], (d,), f32),
    }


def reference(x, c, ctx, c_ctx, ada_w, ada_b, norm1_w, norm2_w, w_in, na_rpb, hg_lb_logits,
              hg_norm_w, gm_ln_w, gm_ws, gm_bs, w_out, mlp_w1, mlp_w2, final_norm_w):
    lb_all = jnp.cumsum(jax.nn.softmax(hg_lb_logits.astype(jnp.float32), axis=0), axis=0)
    lb_all = lb_all - lb_all[:1]
    for l in range(DEPTH):
        need_ctx = l < DEPTH - 1
        sh1, sc1, g1, sh2, sc2, g2 = _ada(c, ada_w[l], ada_b[l])
        csh1, csc1, cg1, csh2, csc2, cg2 = _ada(c_ctx, ada_w[l], ada_b[l])
        hl = _modulate(_rmsnorm(x, norm1_w[l]), sh1, sc1)
        hc = _modulate(_rmsnorm(ctx, norm1_w[l]), csh1, csc1)
        y_c, y_l = _token_mixers(hc, hl, w_in[l], w_out[l], na_rpb[l], lb_all[l, 0], lb_all[l, 1],
                                 hg_norm_w[l], gm_ln_w[l], gm_ws[l], gm_bs[l], need_ctx)
        x = x + g1 * y_l
        x = x + g2 * _sq_relu_mlp(_modulate(_rmsnorm(x, norm2_w[l]), sh2, sc2), mlp_w1[l], mlp_w2[l])
        if need_ctx:
            ctx = ctx + cg1 * y_c
            ctx = ctx + cg2 * _sq_relu_mlp(_modulate(_rmsnorm(ctx, norm2_w[l]), csh2, csc2), mlp_w1[l], mlp_w2[l])
    return _rmsnorm(x, final_norm_w)
```

```python
import functools

import numpy as np
import jax
import jax.numpy as jnp
from jax import lax
from jax.experimental import pallas as pl
from jax.experimental.pallas import tpu as pltpu

F32 = jnp.float32
BF16 = jnp.bfloat16

D_MODEL = 2048
DEPTH = 2
GRID_W = 64
EPS = 1e-6

NA_HEAD_DIM = 128
NA_HEADS = 8
NA_WIDTH = NA_HEADS * NA_HEAD_DIM
NA_WIN_ROWS = 8
NA_WIN_COLS = 16

HG_HEADS = 4
HG_DIM = 128
HG_WIDTH = HG_HEADS * HG_DIM
HG_CHUNK = 64

GM_GROUPS = 4
GM_DIM = 128
GM_WIDTH = GM_GROUPS * GM_DIM
GM_CHUNK = 128

IN_WIDTH = 3 * NA_WIDTH + 5 * HG_WIDTH + 2 * GM_WIDTH
MLP_HIDDEN = 4 * D_MODEL
LANE = 128

_QA, _KA, _VA = 0, NA_HEADS, 2 * NA_HEADS
_HG0 = 3 * NA_HEADS
_GM_U = (3 * NA_WIDTH + 5 * HG_WIDTH) // GM_WIDTH
_GM_V = _GM_U + 1

VMEM_LIMIT_V7X = 56 * 1024 * 1024


def _cparams(sem):
    return pltpu.CompilerParams(dimension_semantics=sem, vmem_limit_bytes=VMEM_LIMIT_V7X)


def _silu(x):
    return x / (1.0 + jnp.exp(-x))


def _gelu_tanh(x):
    return 0.5 * x * (1.0 + jnp.tanh(0.7978845608028654 * (x + 0.044715 * (x * x * x))))


def _norm_modulate(x, nw, sh, sc):
    ms = jnp.mean(x * x, axis=-1, keepdims=True)
    y = x * lax.rsqrt(ms + EPS) * nw
    return y * (1.0 + sc) + sh


def _ada_kernel(c_ref, w_ref, b_ref, o_ref):
    s = _silu(c_ref[...]).astype(BF16)
    o_ref[0] = jnp.dot(s, w_ref[0].astype(BF16), preferred_element_type=F32) + b_ref[0]


def _ada(cond, ada_w, ada_b):
    depth, d, n = ada_w.shape
    tn = 1536
    return pl.pallas_call(
        _ada_kernel,
        grid=(depth, n // tn),
        in_specs=[
            pl.BlockSpec((cond.shape[0], d), lambda l, j: (0, 0)),
            pl.BlockSpec((1, d, tn), lambda l, j: (l, 0, j)),
            pl.BlockSpec((1, 1, tn), lambda l, j: (l, 0, j)),
        ],
        out_specs=pl.BlockSpec((1, cond.shape[0], tn), lambda l, j: (l, 0, j)),
        out_shape=jax.ShapeDtypeStruct((depth, cond.shape[0], n), F32),
        compiler_params=_cparams(("arbitrary", "arbitrary")),
        name="ada",
    )(cond, ada_w, ada_b.reshape(depth, 1, n))


def _inproj_kernel(x_ref, nw_ref, sh_ref, sc_ref, w_ref, o_ref, h_ref):
    @pl.when(pl.program_id(1) == 0)
    def _():
        h_ref[...] = _norm_modulate(x_ref[...], nw_ref[...], sh_ref[0], sc_ref[0]).astype(BF16)

    o_ref[...] = jnp.dot(h_ref[...], w_ref[...], preferred_element_type=F32)


def _inproj(x2d, nw, mod, w, rows_per_mod, mod_row0, tm):
    m, d = x2d.shape
    n = w.shape[1]
    tn = 512
    mrow = lambda i: mod_row0 + (i * tm) // rows_per_mod
    return pl.pallas_call(
        _inproj_kernel,
        grid=(m // tm, n // tn),
        in_specs=[
            pl.BlockSpec((tm, d), lambda i, j: (i, 0)),
            pl.BlockSpec((1, d), lambda i, j: (0, 0)),
            pl.BlockSpec((1, 1, d), lambda i, j: (mrow(i), 0, 0)),
            pl.BlockSpec((1, 1, d), lambda i, j: (mrow(i), 0, 1)),
            pl.BlockSpec((d, tn), lambda i, j: (0, j)),
        ],
        out_specs=pl.BlockSpec((tm, tn), lambda i, j: (i, j)),
        out_shape=jax.ShapeDtypeStruct((m, n), F32),
        scratch_shapes=[pltpu.VMEM((tm, d), BF16)],
        compiler_params=_cparams(("arbitrary", "arbitrary")),
        name="inproj",
    )(x2d, nw, mod, mod, w)


def _na_bias_table(rpb):
    qc = np.arange(GRID_W)[:, None]
    kc = np.arange(GRID_W)[None, :]
    wstart = np.clip(qc - NA_WIN_COLS // 2, 0, GRID_W - NA_WIN_COLS)
    valid = (kc >= wstart) & (kc < wstart + NA_WIN_COLS)
    dcol = np.clip(kc - qc, 1 - NA_WIN_COLS, NA_WIN_COLS - 1) + NA_WIN_COLS - 1
    t = jnp.where(valid, rpb.astype(F32)[:, :, dcol], -1e30)
    idx = np.arange(NA_WIN_ROWS)[:, None] + np.arange(NA_WIN_ROWS)[None, :]
    tb = t[:, idx]
    h = rpb.shape[0]
    return tb.transpose(0, 1, 3, 2, 4).reshape(h, NA_WIN_ROWS, GRID_W, NA_WIN_ROWS * GRID_W)


def _na_kernel(q_ref, k_ref, v_ref, kc_ref, vc_ref, bias_ref, o_ref, kb_ref, vb_ref):
    rows = q_ref.shape[1] // GRID_W
    nk = NA_WIN_ROWS * GRID_W
    scale = NA_HEAD_DIM ** -0.5
    kb_ref[...] = k_ref[0].astype(BF16)
    vb_ref[...] = v_ref[0].astype(BF16)
    kc = kc_ref[0].astype(BF16)
    vc = vc_ref[0].astype(BF16)
    nt = (((1,), (1,)), ((), ()))

    def body(r, carry):
        r0 = jnp.clip(r - NA_WIN_ROWS // 2, 0, rows - NA_WIN_ROWS)
        ds = r0 - r + (NA_WIN_ROWS - 1)
        qoff = pl.multiple_of(r * GRID_W, GRID_W)
        koff = pl.multiple_of(r0 * GRID_W, GRID_W)
        q = q_ref[0, pl.ds(qoff, GRID_W), :].astype(BF16)
        s_lat = lax.dot_general(q, kb_ref[pl.ds(koff, nk), :], nt, preferred_element_type=F32) * scale
        s_lat = s_lat + bias_ref[0, ds]
        s_ctx = lax.dot_general(q, kc, nt, preferred_element_type=F32) * scale
        m = jnp.maximum(jnp.max(s_lat, axis=-1, keepdims=True), jnp.max(s_ctx, axis=-1, keepdims=True))
        p_lat = jnp.exp(s_lat - m)
        p_ctx = jnp.exp(s_ctx - m)
        l = jnp.sum(p_lat, axis=-1, keepdims=True) + jnp.sum(p_ctx, axis=-1, keepdims=True)
        o = jnp.dot(p_lat.astype(BF16), vb_ref[pl.ds(koff, nk), :], preferred_element_type=F32)
        o = o + jnp.dot(p_ctx.astype(BF16), vc, preferred_element_type=F32)
        o_ref[0, pl.ds(qoff, GRID_W), :] = (o / l).astype(o_ref.dtype)
        return carry

    lax.fori_loop(0, rows, body, 0)


def _na_latent(pl3, pc3, bias):
    b, t, _ = pl3.shape
    ctx = pc3.shape[1]
    hd = NA_HEAD_DIM
    return pl.pallas_call(
        _na_kernel,
        grid=(b, NA_HEADS),
        in_specs=[
            pl.BlockSpec((1, t, hd), lambda i, h: (i, 0, _QA + h)),
            pl.BlockSpec((1, t, hd), lambda i, h: (i, 0, _KA + h)),
            pl.BlockSpec((1, t, hd), lambda i, h: (i, 0, _VA + h)),
            pl.BlockSpec((1, ctx, hd), lambda i, h: (i, 0, _KA + h)),
            pl.BlockSpec((1, ctx, hd), lambda i, h: (i, 0, _VA + h)),
            pl.BlockSpec((1,) + bias.shape[1:], lambda i, h: (h, 0, 0, 0)),
        ],
        out_specs=pl.BlockSpec((1, t, hd), lambda i, h: (i, 0, h)),
        out_shape=jax.ShapeDtypeStruct((b, t, NA_WIDTH), BF16),
        scratch_shapes=[pltpu.VMEM((t, hd), BF16), pltpu.VMEM((t, hd), BF16)],
        compiler_params=_cparams(("arbitrary", "arbitrary")),
        name="na_latent",
    )(pl3, pl3, pl3, pc3, pc3, bias)


def _ctx_attn_kernel(q_ref, k_ref, v_ref, o_ref):
    scale = NA_HEAD_DIM ** -0.5
    q = q_ref[0].astype(BF16)
    k = k_ref[0].astype(BF16)
    v = v_ref[0].astype(BF16)
    s = lax.dot_general(q, k, (((1,), (1,)), ((), ())), preferred_element_type=F32) * scale
    p = jnp.exp(s - jnp.max(s, axis=-1, keepdims=True))
    l = jnp.sum(p, axis=-1, keepdims=True)
    o = jnp.dot(p.astype(BF16), v, preferred_element_type=F32)
    o_ref[0] = (o / l).astype(o_ref.dtype)


def _ctx_attention(pc3):
    b, ctx, _ = pc3.shape
    hd = NA_HEAD_DIM
    return pl.pallas_call(
        _ctx_attn_kernel,
        grid=(b, NA_HEADS),
        in_specs=[
            pl.BlockSpec((1, ctx, hd), lambda i, h: (i, 0, _QA + h)),
            pl.BlockSpec((1, ctx, hd), lambda i, h: (i, 0, _KA + h)),
            pl.BlockSpec((1, ctx, hd), lambda i, h: (i, 0, _VA + h)),
        ],
        out_specs=pl.BlockSpec((1, ctx, hd), lambda i, h: (i, 0, h)),
        out_shape=jax.ShapeDtypeStruct((b, ctx, NA_WIDTH), BF16),
        compiler_params=_cparams(("arbitrary", "arbitrary")),
        name="ctx_attention",
    )(pc3, pc3, pc3)


def _gm_kernel(u_ref, v_ref, lnw_ref, ws_ref, bs_ref, o_ref):
    uf = _gelu_tanh(u_ref[0])
    vf = _gelu_tanh(v_ref[0])
    for g in range(GM_GROUPS):
        sl = slice(g * GM_DIM, (g + 1) * GM_DIM)
        vg = vf[:, sl]
        mu = jnp.mean(vg, axis=-1, keepdims=True)
        dv = vg - mu
        var = jnp.mean(dv * dv, axis=-1, keepdims=True)
        vn = dv * lax.rsqrt(var + EPS) * lnw_ref[:, sl]
        mixed = jnp.dot(ws_ref[g].astype(BF16), vn.astype(BF16), preferred_element_type=F32) + bs_ref[g]
        o_ref[0, :, sl] = (uf[:, sl] * mixed).astype(o_ref.dtype)


def _chunk_gmlp(p3, ln_w, ws, bs):
    b, t, _ = p3.shape
    return pl.pallas_call(
        _gm_kernel,
        grid=(b, t // GM_CHUNK),
        in_specs=[
            pl.BlockSpec((1, GM_CHUNK, GM_WIDTH), lambda i, c: (i, c, _GM_U)),
            pl.BlockSpec((1, GM_CHUNK, GM_WIDTH), lambda i, c: (i, c, _GM_V)),
            pl.BlockSpec((1, GM_WIDTH), lambda i, c: (0, 0)),
            pl.BlockSpec((GM_GROUPS, GM_CHUNK, GM_CHUNK), lambda i, c: (0, 0, 0)),
            pl.BlockSpec((GM_GROUPS, GM_CHUNK, 1), lambda i, c: (0, 0, 0)),
        ],
        out_specs=pl.BlockSpec((1, GM_CHUNK, GM_WIDTH), lambda i, c: (i, c, 0)),
        out_shape=jax.ShapeDtypeStruct((b, t, GM_WIDTH), BF16),
        compiler_params=_cparams(("arbitrary", "arbitrary")),
        name="chunk_gmlp",
    )(p3, p3, ln_w.reshape(1, GM_WIDTH), ws, bs.reshape(GM_GROUPS, GM_CHUNK, 1))


def _split3_dot(tri, g):
    g0 = g.astype(BF16)
    r1 = g - g0.astype(F32)
    g1 = r1.astype(BF16)
    g2 = (r1 - g1.astype(F32)).astype(BF16)
    dot = functools.partial(jnp.dot, preferred_element_type=F32)
    return dot(tri, g0) + dot(tri, g1) + dot(tri, g2)


def _hg_kernel(ql_ref, ffl_ref, fbl_ref, il_ref, gl_ref,
               qc_ref, ffc_ref, fbc_ref, ic_ref, gc_ref,
               lbp_ref, nw_ref, ol_ref, oc_ref,
               qh_s, v_s, kf_s, kb_s, ef_s, eb_s, of_s, ob_s, st_s):
    c = HG_CHUNK
    n_ctx = qc_ref.shape[1]
    n_lat = ql_ref.shape[1]
    n_tot = n_ctx + n_lat
    ctx_chunks = n_ctx // c
    n_chunks = n_tot // c
    scale = HG_DIM ** -0.5
    nt = (((1,), (1,)), ((), ()))
    tn = (((0,), (0,)), ((), ()))

    def prep(q_ref, ff_ref, fb_ref, i_ref, base, n):
        step = 256
        for t0 in range(0, n, step):
            src = slice(t0, t0 + step)
            dst = slice(base + t0, base + t0 + step)
            q = q_ref[0, src, :]
            qh_s[dst, :] = _silu(q) * scale
            v_s[dst, :] = i_ref[0, src, :]
            for d, (f_ref, k_s, e_s) in enumerate(((ff_ref, kf_s, ef_s), (fb_ref, kb_s, eb_s))):
                x = f_ref[0, src, :]
                sp = jnp.maximum(-x, 0.0) + jnp.log(1.0 + jnp.exp(-jnp.abs(x)))
                log_lb = lbp_ref[d, 0:1, :]
                y = lbp_ref[d, 1:2, :] - sp
                mx = jnp.maximum(log_lb, y)
                e_s[dst, :] = mx + jnp.log(1.0 + jnp.exp(-jnp.abs(log_lb - y)))
                k_s[dst, :] = lbp_ref[d, 2:3, :] * jnp.exp(-(sp + x))

    prep(qc_ref, ffc_ref, fbc_ref, ic_ref, 0, n_ctx)
    prep(ql_ref, ffl_ref, fbl_ref, il_ref, n_ctx, n_lat)

    row = lax.broadcasted_iota(jnp.int32, (c, c), 0)
    col = lax.broadcasted_iota(jnp.int32, (c, c), 1)
    tril = (col <= row).astype(BF16)
    triu = (col >= row).astype(BF16)

    def cum_body(i, carry):
        off = pl.multiple_of(i * c, c)
        ef_s[pl.ds(off, c), :] = _split3_dot(tril, ef_s[pl.ds(off, c), :])
        eb_s[pl.ds(off, c), :] = _split3_dot(triu, eb_s[pl.ds(off, c), :])
        return carry

    lax.fori_loop(0, n_chunks, cum_body, 0)

    blk = 8
    levels = (32, 16, 8)
    diag_code = len(levels) + 1
    same8 = (row // blk) == (col // blk)
    code_f = jnp.where(same8 & (col <= row), diag_code, 0)
    code_b = jnp.where(same8 & (col >= row), diag_code, 0)
    for li, w in enumerate(levels):
        same = (row // (2 * w)) == (col // (2 * w))
        t_hi = (row % (2 * w)) >= w
        s_hi = (col % (2 * w)) >= w
        code_f = jnp.where(same & t_hi & (~s_hi), li + 1, code_f)
        code_b = jnp.where(same & (~t_hi) & s_hi, li + 1, code_b)
    codes = (code_f, code_b)
    lane8 = lax.broadcasted_iota(jnp.int32, (blk, c), 1)

    st_s[...] = jnp.zeros_like(st_s)

    def chunk(off, d, k_s, e_s, o_s):
        fwd = d == 0
        q = qh_s[pl.ds(off, c), :]
        k = k_s[pl.ds(off, c), :]
        e = e_s[pl.ds(off, c), :]
        vb = v_s[pl.ds(off, c), :].astype(BF16)
        e_end = e_s[pl.ds(off + (c - 1 if fwd else 0), 1), :]
        st = st_s[d]
        qi = (q * jnp.exp(e)).astype(BF16)
        ki = (k * jnp.exp(e_end - e)).astype(BF16)
        o = lax.dot_general(qi, st.astype(BF16), nt, preferred_element_type=F32)
        st_s[d] = st * jnp.exp(e_end) + lax.dot_general(vb, ki, tn, preferred_element_type=F32)
        code = codes[d]
        att = jnp.zeros((c, c), F32)
        for li, w in enumerate(levels):
            refs = []
            for a in range(0, c, 2 * w):
                r_row = e_s[pl.ds(off + (a + w - 1 if fwd else a + w), 1), :]
                refs.append(jnp.broadcast_to(r_row, (2 * w, HG_DIM)))
            ref = refs[0] if len(refs) == 1 else jnp.concatenate(refs, axis=0)
            qs = (q * jnp.exp(jnp.minimum(e - ref, 0.0))).astype(BF16)
            ks = (k * jnp.exp(jnp.minimum(ref - e, 0.0))).astype(BF16)
            a_w = lax.dot_general(qs, ks, nt, preferred_element_type=F32)
            att = jnp.where(code == li + 1, a_w, att)
        slabs = []
        for i in range(c // blk):
            qb = q[i * blk:(i + 1) * blk, :]
            eb = e[i * blk:(i + 1) * blk, :]
            slab = jnp.zeros((blk, c), F32)
            for s in range(blk):
                kr = k_s[pl.ds(off + i * blk + s, 1), :]
                er = e_s[pl.ds(off + i * blk + s, 1), :]
                p = qb * kr * jnp.exp(jnp.minimum(eb - er, 0.0))
                colv = jnp.sum(p, axis=-1, keepdims=True)
                slab = jnp.where(lane8 == (i * blk + s), colv, slab)
            slabs.append(slab)
        att = jnp.where(code == diag_code, jnp.concatenate(slabs, axis=0), att)
        o = o + jnp.dot(att.astype(BF16), vb, preferred_element_type=F32)
        o_s[pl.ds(off, c), :] = o

    def scan_body(i, carry):
        off_f = pl.multiple_of(i * c, c)
        ib = jnp.where(i < ctx_chunks, ctx_chunks - 1 - i, n_chunks + ctx_chunks - 1 - i)
        off_b = pl.multiple_of(ib * c, c)
        chunk(off_f, 0, kf_s, ef_s, of_s)
        chunk(off_b, 1, kb_s, eb_s, ob_s)
        return carry

    lax.fori_loop(0, n_chunks, scan_body, 0)

    def finish(g_ref, o_ref, base, n):
        step = 256
        for t0 in range(0, n, step):
            src = slice(t0, t0 + step)
            dst = slice(base + t0, base + t0 + step)
            o = of_s[dst, :] + ob_s[dst, :]
            o = o * lax.rsqrt(jnp.mean(o * o, axis=-1, keepdims=True) + EPS) * nw_ref[...]
            o_ref[0, src, :] = (o * _silu(g_ref[0, src, :])).astype(o_ref.dtype)

    finish(gc_ref, oc_ref, 0, n_ctx)
    finish(gl_ref, ol_ref, n_ctx, n_lat)


def _hgrn2(pl3, pc3, lbp, norm_w):
    b, t, _ = pl3.shape
    ctx = pc3.shape[1]
    hd = HG_DIM
    n = t + ctx

    def col(stream):
        return lambda i, h: (i, 0, _HG0 + stream * HG_HEADS + h)

    lat_specs = [pl.BlockSpec((1, t, hd), col(s)) for s in range(5)]
    ctx_specs = [pl.BlockSpec((1, ctx, hd), col(s)) for s in range(5)]
    big = lambda: pltpu.VMEM((n, hd), F32)
    return pl.pallas_call(
        _hg_kernel,
        grid=(b, HG_HEADS),
        in_specs=lat_specs + ctx_specs + [
            pl.BlockSpec((2, 3, hd), lambda i, h: (0, 0, h)),
            pl.BlockSpec((1, hd), lambda i, h: (0, 0)),
        ],
        out_specs=[
            pl.BlockSpec((1, t, hd), lambda i, h: (i, 0, h)),
            pl.BlockSpec((1, ctx, hd), lambda i, h: (i, 0, h)),
        ],
        out_shape=[
            jax.ShapeDtypeStruct((b, t, HG_WIDTH), BF16),
            jax.ShapeDtypeStruct((b, ctx, HG_WIDTH), BF16),
        ],
        scratch_shapes=[big() for _ in range(8)] + [pltpu.VMEM((2, hd, hd), F32)],
        compiler_params=_cparams(("arbitrary", "arbitrary")),
        name="hgrn2",
    )(*([pl3] * 5 + [pc3] * 5 + [lbp, norm_w.reshape(1, hd)]))


def _outproj_kernel(oa_ref, ob_ref, oc_ref, wa_ref, wb_ref, wc_ref, x_ref, g_ref, o_ref):
    dot = functools.partial(jnp.dot, preferred_element_type=F32)
    y = dot(oa_ref[...], wa_ref[...]) + dot(ob_ref[...], wb_ref[...]) + dot(oc_ref[...], wc_ref[...])
    o_ref[...] = x_ref[...] + g_ref[0] * y


def _outproj(oa, ob, oc, w, x2d, mod, rows_per_mod, mod_row0, tm):
    m, d = x2d.shape
    tn = 512
    mrow = lambda i: mod_row0 + (i * tm) // rows_per_mod
    nb = NA_WIDTH // HG_WIDTH
    return pl.pallas_call(
        _outproj_kernel,
        grid=(m // tm, d // tn),
        in_specs=[
            pl.BlockSpec((tm, NA_WIDTH), lambda i, j: (i, 0)),
            pl.BlockSpec((tm, HG_WIDTH), lambda i, j: (i, 0)),
            pl.BlockSpec((tm, GM_WIDTH), lambda i, j: (i, 0)),
            pl.BlockSpec((NA_WIDTH, tn), lambda i, j: (0, j)),
            pl.BlockSpec((HG_WIDTH, tn), lambda i, j: (nb, j)),
            pl.BlockSpec((GM_WIDTH, tn), lambda i, j: (nb + 1, j)),
            pl.BlockSpec((tm, tn), lambda i, j: (i, j)),
            pl.BlockSpec((1, 1, tn), lambda i, j: (mrow(i), 0, 2 * (d // tn) + j)),
        ],
        out_specs=pl.BlockSpec((tm, tn), lambda i, j: (i, j)),
        out_shape=jax.ShapeDtypeStruct((m, d), F32),
        compiler_params=_cparams(("arbitrary", "arbitrary")),
        name="outproj",
    )(oa, ob, oc, w, w, w, x2d, mod)


def _mlp_kernel(x_ref, nw_ref, sh_ref, sc_ref, g_ref, w1_ref, w2_ref, fnw_ref, o_ref, h_ref, acc_ref,
                *, final_norm):
    j = pl.program_id(1)

    @pl.when(j == 0)
    def _():
        h_ref[...] = _norm_modulate(x_ref[...], nw_ref[...], sh_ref[0], sc_ref[0]).astype(BF16)
        acc_ref[...] = jnp.zeros_like(acc_ref)

    a = jnp.maximum(jnp.dot(h_ref[...], w1_ref[...], preferred_element_type=F32), 0.0)
    acc_ref[...] += jnp.dot((a * a).astype(BF16), w2_ref[...], preferred_element_type=F32)

    @pl.when(j == pl.num_programs(1) - 1)
    def _():
        y = x_ref[...] + g_ref[0] * acc_ref[...]
        if final_norm:
            y = y * lax.rsqrt(jnp.mean(y * y, axis=-1, keepdims=True) + EPS) * fnw_ref[...]
        o_ref[...] = y


def _mlp(x2d, nw, mod, w1, w2, fnw, rows_per_mod, mod_row0, tm, final_norm):
    m, d = x2d.shape
    hid = w1.shape[1]
    th = 512
    mrow = lambda i: mod_row0 + (i * tm) // rows_per_mod
    return pl.pallas_call(
        functools.partial(_mlp_kernel, final_norm=final_norm),
        grid=(m // tm, hid // th),
        in_specs=[
            pl.BlockSpec((tm, d), lambda i, j: (i, 0)),
            pl.BlockSpec((1, d), lambda i, j: (0, 0)),
            pl.BlockSpec((1, 1, d), lambda i, j: (mrow(i), 0, 3)),
            pl.BlockSpec((1, 1, d), lambda i, j: (mrow(i), 0, 4)),
            pl.BlockSpec((1, 1, d), lambda i, j: (mrow(i), 0, 5)),
            pl.BlockSpec((d, th), lambda i, j: (0, j)),
            pl.BlockSpec((th, d), lambda i, j: (j, 0)),
            pl.BlockSpec((1, d), lambda i, j: (0, 0)),
        ],
        out_specs=pl.BlockSpec((tm, d), lambda i, j: (i, 0)),
        out_shape=jax.ShapeDtypeStruct((m, d), F32),
        scratch_shapes=[pltpu.VMEM((tm, d), BF16), pltpu.VMEM((tm, d), F32)],
        compiler_params=_cparams(("arbitrary", "arbitrary")),
        name="mlp",
    )(x2d, nw, mod, mod, mod, w1, w2, fnw)


def kernel(x, c, ctx, c_ctx, ada_w, ada_b, norm1_w, norm2_w, w_in, na_rpb, hg_lb_logits, hg_norm_w,
           gm_ln_w, gm_ws, gm_bs, w_out, mlp_w1, mlp_w2, final_norm_w):
    bsz, seq, d = x.shape
    n_ctx = ctx.shape[1]
    depth = ada_w.shape[0]
    mod_rows = 16
    assert bsz < mod_rows and d == D_MODEL and seq % 512 == 0 and n_ctx % 256 == 0

    lb = jnp.cumsum(jax.nn.softmax(hg_lb_logits.astype(F32), axis=0), axis=0)
    lb = lb - lb[:1]
    lbp = jnp.stack([jnp.log(lb), jnp.log1p(-lb), 1.0 - lb], axis=2)

    cond = jnp.zeros((mod_rows, d), F32).at[:bsz].set(c).at[bsz].set(c_ctx)
    mods = _ada(cond, ada_w, ada_b)

    xl = x.reshape(bsz * seq, d)
    xc = ctx.reshape(bsz * n_ctx, d)
    tm_l, tm_c = 512, 256
    for l in range(depth):
        need_ctx = l < depth - 1
        mod = mods[l].reshape(mod_rows, 1, 6 * d)
        nw1 = norm1_w[l].reshape(1, d)
        nw2 = norm2_w[l].reshape(1, d)
        w_in_b = w_in[l].astype(BF16)
        w_out_b = w_out[l].astype(BF16)
        w1_b = mlp_w1[l].astype(BF16)
        w2_b = mlp_w2[l].astype(BF16)
        fnw = final_norm_w.reshape(1, d)

        p_l = _inproj(xl, nw1, mod, w_in_b, seq, 0, tm_l).reshape(bsz, seq, IN_WIDTH)
        p_c = _inproj(xc, nw1, mod, w_in_b, bsz * n_ctx, bsz, tm_c).reshape(bsz, n_ctx, IN_WIDTH)

        oa_l = _na_latent(p_l, p_c, _na_bias_table(na_rpb[l]))
        ob_l, ob_c = _hgrn2(p_l, p_c, lbp[l], hg_norm_w[l])
        oc_l = _chunk_gmlp(p_l, gm_ln_w[l], gm_ws[l], gm_bs[l])
        flat = lambda a: a.reshape(-1, a.shape[-1])
        xl = _outproj(flat(oa_l), flat(ob_l), flat(oc_l), w_out_b, xl, mod, seq, 0, tm_l)
        xl = _mlp(xl, nw2, mod, w1_b, w2_b, fnw, seq, 0, tm_l, final_norm=not need_ctx)
        if need_ctx:
            oa_c = _ctx_attention(p_c)
            oc_c = _chunk_gmlp(p_c, gm_ln_w[l], gm_ws[l], gm_bs[l])
            xc = _outproj(flat(oa_c), flat(ob_c), flat(oc_c), w_out_b, xc, mod, bsz * n_ctx, bsz, tm_c)
            xc = _mlp(xc, nw2, mod, w1_b, w2_b, fnw, bsz * n_ctx, bsz, tm_c, final_norm=False)
    return xl.reshape(bsz, seq, d)
```

```python
import functools

import numpy as np
import jax
import jax.numpy as jnp
from jax import lax
from jax.experimental import pallas as pl
from jax.experimental.pallas import tpu as pltpu

F32 = jnp.float32
BF16 = jnp.bfloat16

D_MODEL = 2048
DEPTH = 2
GRID_W = 64
EPS = 1e-6

NA_HEAD_DIM = 128
NA_HEADS = 8
NA_WIDTH = NA_HEADS * NA_HEAD_DIM
NA_WIN_ROWS = 8
NA_WIN_COLS = 16

HG_HEADS = 4
HG_DIM = 128
HG_WIDTH = HG_HEADS * HG_DIM
HG_CHUNK = 64

GM_GROUPS = 4
GM_DIM = 128
GM_WIDTH = GM_GROUPS * GM_DIM
GM_CHUNK = 128

IN_WIDTH = 3 * NA_WIDTH + 5 * HG_WIDTH + 2 * GM_WIDTH
MLP_HIDDEN = 4 * D_MODEL
LANE = 128
LOG2E = 1.4426950408889634

_QA, _KA, _VA = 0, NA_HEADS, 2 * NA_HEADS
_HG0 = 3 * NA_HEADS
_GM_U = (3 * NA_WIDTH + 5 * HG_WIDTH) // GM_WIDTH
_GM_V = _GM_U + 1

VMEM_LIMIT_V7X = 56 * 1024 * 1024


def _cparams(sem):
    return pltpu.CompilerParams(dimension_semantics=sem, vmem_limit_bytes=VMEM_LIMIT_V7X)


def _silu(x):
    return x / (1.0 + jnp.exp(-x))


def _gelu_tanh(x):
    return 0.5 * x * (1.0 + jnp.tanh(0.7978845608028654 * (x + 0.044715 * (x * x * x))))


def _norm_modulate(x, nw, sh, sc):
    ms = jnp.mean(x * x, axis=-1, keepdims=True)
    y = x * lax.rsqrt(ms + EPS) * nw
    return y * (1.0 + sc) + sh


def _ada_kernel(c_ref, w_ref, b_ref, o_ref):
    s = _silu(c_ref[...]).astype(BF16)
    o_ref[0] = jnp.dot(s, w_ref[0].astype(BF16), preferred_element_type=F32) + b_ref[0]


def _ada(cond, ada_w, ada_b):
    depth, d, n = ada_w.shape
    tn = 1536
    return pl.pallas_call(
        _ada_kernel,
        grid=(depth, n // tn),
        in_specs=[
            pl.BlockSpec((cond.shape[0], d), lambda l, j: (0, 0)),
            pl.BlockSpec((1, d, tn), lambda l, j: (l, 0, j)),
            pl.BlockSpec((1, 1, tn), lambda l, j: (l, 0, j)),
        ],
        out_specs=pl.BlockSpec((1, cond.shape[0], tn), lambda l, j: (l, 0, j)),
        out_shape=jax.ShapeDtypeStruct((depth, cond.shape[0], n), F32),
        compiler_params=_cparams(("arbitrary", "arbitrary")),
        name="ada",
    )(cond, ada_w, ada_b.reshape(depth, 1, n))


def _inproj_kernel(x_ref, nw_ref, sh_ref, sc_ref, w_ref, o_ref, h_ref):
    @pl.when(pl.program_id(1) == 0)
    def _():
        h_ref[...] = _norm_modulate(x_ref[...], nw_ref[...], sh_ref[0], sc_ref[0]).astype(BF16)

    o_ref[...] = jnp.dot(h_ref[...], w_ref[...], preferred_element_type=F32)


def _inproj(x2d, nw, mod, w, rows_per_mod, mod_row0, tm):
    m, d = x2d.shape
    n = w.shape[1]
    tn = 512
    mrow = lambda i: mod_row0 + (i * tm) // rows_per_mod
    return pl.pallas_call(
        _inproj_kernel,
        grid=(m // tm, n // tn),
        in_specs=[
            pl.BlockSpec((tm, d), lambda i, j: (i, 0)),
            pl.BlockSpec((1, d), lambda i, j: (0, 0)),
            pl.BlockSpec((1, 1, d), lambda i, j: (mrow(i), 0, 0)),
            pl.BlockSpec((1, 1, d), lambda i, j: (mrow(i), 0, 1)),
            pl.BlockSpec((d, tn), lambda i, j: (0, j)),
        ],
        out_specs=pl.BlockSpec((tm, tn), lambda i, j: (i, j)),
        out_shape=jax.ShapeDtypeStruct((m, n), F32),
        scratch_shapes=[pltpu.VMEM((tm, d), BF16)],
        compiler_params=_cparams(("arbitrary", "arbitrary")),
        name="inproj",
    )(x2d, nw, mod, mod, w)


NA_GROUP_ROWS = 4
NA_GROUP_TOK = NA_GROUP_ROWS * GRID_W
NA_WIN_BLOCKS = 3
NA_GROUPS_PER_STEP = 4


def _na_bias_table(rpb, rows):
    n_groups = rows // NA_GROUP_ROWS
    nk = NA_WIN_BLOCKS * NA_GROUP_TOK
    kk = np.arange(nk)[:, None]
    qq = np.arange(NA_GROUP_TOK)[None, :]
    kcol, qcol = kk % GRID_W, qq % GRID_W
    wstart = np.clip(qcol - NA_WIN_COLS // 2, 0, GRID_W - NA_WIN_COLS)
    valid_col = (kcol >= wstart) & (kcol < wstart + NA_WIN_COLS)
    dcol = np.clip(kcol - qcol, 1 - NA_WIN_COLS, NA_WIN_COLS - 1) + NA_WIN_COLS - 1
    drows, valids = [], []
    for g in (0, 1, n_groups - 1):
        krow = int(np.clip(g - 1, 0, n_groups - NA_WIN_BLOCKS)) * NA_GROUP_ROWS + kk // GRID_W
        qrow = g * NA_GROUP_ROWS + qq // GRID_W
        r0 = np.clip(qrow - NA_WIN_ROWS // 2, 0, rows - NA_WIN_ROWS)
        valids.append(valid_col & (krow >= r0) & (krow < r0 + NA_WIN_ROWS))
        drows.append(np.clip(krow - qrow + NA_WIN_ROWS - 1, 0, 2 * NA_WIN_ROWS - 2))
    drow = np.stack(drows)
    dcol = np.broadcast_to(dcol, drow.shape)
    return jnp.where(np.stack(valids), rpb.astype(F32)[:, drow, dcol], -1e30)


def _na_kernel(q_ref, k_ref, v_ref, kc_ref, vc_ref, bias_ref, o_ref, qb_ref, kb_ref, vt_ref):
    n_groups = q_ref.shape[1] // NA_GROUP_TOK
    gt = NA_GROUP_TOK
    scale = NA_HEAD_DIM ** -0.5
    nt = (((1,), (1,)), ((), ()))
    dot = functools.partial(jnp.dot, preferred_element_type=F32)
    qb_ref[...] = q_ref[0].astype(BF16)
    kb_ref[...] = k_ref[0].astype(BF16)
    for b in range(n_groups):
        vt_ref[b] = v_ref[0, b * gt:(b + 1) * gt, :].T.astype(BF16)
    kc = kc_ref[0].astype(BF16)
    vct = vc_ref[0].T.astype(BF16)

    def group(g):
        blk = jnp.clip(g - 1, 0, n_groups - NA_WIN_BLOCKS)
        kind = (g > 0).astype(jnp.int32) + (g == n_groups - 1).astype(jnp.int32)
        qg = qb_ref[pl.ds(pl.multiple_of(g * gt, gt), gt), :]
        kw = kb_ref[pl.ds(pl.multiple_of(blk * gt, gt), NA_WIN_BLOCKS * gt), :]
        s_lat = lax.dot_general(kw, qg, nt, preferred_element_type=F32) * scale + bias_ref[0, kind]
        s_ctx = lax.dot_general(kc, qg, nt, preferred_element_type=F32) * scale
        yield
        m = jnp.maximum(jnp.max(s_lat, axis=0, keepdims=True), jnp.max(s_ctx, axis=0, keepdims=True))
        p_lat = jnp.exp(s_lat - m)
        p_ctx = jnp.exp(s_ctx - m)
        l = jnp.sum(p_lat, axis=0, keepdims=True) + jnp.sum(p_ctx, axis=0, keepdims=True)
        yield
        ot = dot(vct, p_ctx.astype(BF16))
        for j in range(NA_WIN_BLOCKS):
            ot = ot + dot(vt_ref[blk + j], p_lat[j * gt:(j + 1) * gt, :].astype(BF16))
        yield
        o_ref[0, pl.ds(pl.multiple_of(g * gt, gt), gt), :] = (ot / l).T.astype(o_ref.dtype)
        yield

    def body(i, carry):
        groups = [group(i * NA_GROUPS_PER_STEP + u) for u in range(NA_GROUPS_PER_STEP)]
        for _ in range(4):
            for grp in groups:
                next(grp)
        return carry

    lax.fori_loop(0, n_groups // NA_GROUPS_PER_STEP, body, 0)


def _na_latent(pl3, pc3, bias):
    b, t, _ = pl3.shape
    ctx = pc3.shape[1]
    hd = NA_HEAD_DIM
    return pl.pallas_call(
        _na_kernel,
        grid=(NA_HEADS, b),
        in_specs=[
            pl.BlockSpec((1, t, hd), lambda h, i: (i, 0, _QA + h)),
            pl.BlockSpec((1, t, hd), lambda h, i: (i, 0, _KA + h)),
            pl.BlockSpec((1, t, hd), lambda h, i: (i, 0, _VA + h)),
            pl.BlockSpec((1, ctx, hd), lambda h, i: (i, 0, _KA + h)),
            pl.BlockSpec((1, ctx, hd), lambda h, i: (i, 0, _VA + h)),
            pl.BlockSpec((1,) + bias.shape[1:], lambda h, i: (h, 0, 0, 0)),
        ],
        out_specs=pl.BlockSpec((1, t, hd), lambda h, i: (i, 0, h)),
        out_shape=jax.ShapeDtypeStruct((b, t, NA_WIDTH), BF16),
        scratch_shapes=[
            pltpu.VMEM((t, hd), BF16),
            pltpu.VMEM((t, hd), BF16),
            pltpu.VMEM((t // NA_GROUP_TOK, hd, NA_GROUP_TOK), BF16),
        ],
        compiler_params=_cparams(("arbitrary", "arbitrary")),
        name="na_latent",
    )(pl3, pl3, pl3, pc3, pc3, bias)


def _ctx_attn_kernel(q_ref, k_ref, v_ref, o_ref):
    scale = NA_HEAD_DIM ** -0.5
    q = q_ref[0].astype(BF16)
    k = k_ref[0].astype(BF16)
    v = v_ref[0].astype(BF16)
    s = lax.dot_general(q, k, (((1,), (1,)), ((), ())), preferred_element_type=F32) * scale
    p = jnp.exp(s - jnp.max(s, axis=-1, keepdims=True))
    l = jnp.sum(p, axis=-1, keepdims=True)
    o = jnp.dot(p.astype(BF16), v, preferred_element_type=F32)
    o_ref[0] = (o / l).astype(o_ref.dtype)


def _ctx_attention(pc3):
    b, ctx, _ = pc3.shape
    hd = NA_HEAD_DIM
    return pl.pallas_call(
        _ctx_attn_kernel,
        grid=(b, NA_HEADS),
        in_specs=[
            pl.BlockSpec((1, ctx, hd), lambda i, h: (i, 0, _QA + h)),
            pl.BlockSpec((1, ctx, hd), lambda i, h: (i, 0, _KA + h)),
            pl.BlockSpec((1, ctx, hd), lambda i, h: (i, 0, _VA + h)),
        ],
        out_specs=pl.BlockSpec((1, ctx, hd), lambda i, h: (i, 0, h)),
        out_shape=jax.ShapeDtypeStruct((b, ctx, NA_WIDTH), BF16),
        compiler_params=_cparams(("arbitrary", "arbitrary")),
        name="ctx_attention",
    )(pc3, pc3, pc3)


def _gm_kernel(u_ref, v_ref, lnw_ref, ws_ref, bs_ref, o_ref):
    uf = _gelu_tanh(u_ref[0])
    vf = _gelu_tanh(v_ref[0])
    for g in range(GM_GROUPS):
        sl = slice(g * GM_DIM, (g + 1) * GM_DIM)
        vg = vf[:, sl]
        mu = jnp.mean(vg, axis=-1, keepdims=True)
        dv = vg - mu
        var = jnp.mean(dv * dv, axis=-1, keepdims=True)
        vn = dv * lax.rsqrt(var + EPS) * lnw_ref[:, sl]
        mixed = jnp.dot(ws_ref[g].astype(BF16), vn.astype(BF16), preferred_element_type=F32) + bs_ref[g]
        o_ref[0, :, sl] = (uf[:, sl] * mixed).astype(o_ref.dtype)


def _chunk_gmlp(p3, ln_w, ws, bs):
    b, t, _ = p3.shape
    return pl.pallas_call(
        _gm_kernel,
        grid=(b, t // GM_CHUNK),
        in_specs=[
            pl.BlockSpec((1, GM_CHUNK, GM_WIDTH), lambda i, c: (i, c, _GM_U)),
            pl.BlockSpec((1, GM_CHUNK, GM_WIDTH), lambda i, c: (i, c, _GM_V)),
            pl.BlockSpec((1, GM_WIDTH), lambda i, c: (0, 0)),
            pl.BlockSpec((GM_GROUPS, GM_CHUNK, GM_CHUNK), lambda i, c: (0, 0, 0)),
            pl.BlockSpec((GM_GROUPS, GM_CHUNK, 1), lambda i, c: (0, 0, 0)),
        ],
        out_specs=pl.BlockSpec((1, GM_CHUNK, GM_WIDTH), lambda i, c: (i, c, 0)),
        out_shape=jax.ShapeDtypeStruct((b, t, GM_WIDTH), BF16),
        compiler_params=_cparams(("arbitrary", "arbitrary")),
        name="chunk_gmlp",
    )(p3, p3, ln_w.reshape(1, GM_WIDTH), ws, bs.reshape(GM_GROUPS, GM_CHUNK, 1))


HG_PAIR = 2
HG_LEVELS = (32, 16, 8)
HG_DIAG = 8
HG_PREP_ROWS = 256


def _split3_dot(tri, g):
    g0 = g.astype(BF16)
    r1 = g - g0.astype(F32)
    g1 = r1.astype(BF16)
    g2 = (r1 - g1.astype(F32)).astype(BF16)
    dot = functools.partial(jnp.dot, preferred_element_type=F32)
    return dot(tri, g0) + dot(tri, g1) + dot(tri, g2)


def _hg_kernel(ql_ref, ffl_ref, fbl_ref, il_ref, gl_ref,
               qc_ref, ffc_ref, fbc_ref, ic_ref, gc_ref,
               lbp_ref, nw_ref, tri_ref, code_ref, lsum_ref, ol_ref, oc_ref,
               qh_s, kf_s, kb_s, ef_s, eb_s, of_s, ob_s, st_s):
    c = HG_CHUNK
    hd = HG_DIM
    n_ctx = qc_ref.shape[1]
    n_lat = ql_ref.shape[1]
    n_chunks = (n_ctx + n_lat) // c
    scale = hd ** -0.5
    nt = (((1,), (1,)), ((), ()))
    tn = (((0,), (0,)), ((), ()))

    def prep(q_ref, ff_ref, fb_ref, base, n):
        step = HG_PREP_ROWS
        for t0 in range(0, n, step):
            src = slice(t0, t0 + step)
            dst = slice(base + t0, base + t0 + step)
            qh = _silu(q_ref[0, src, :]) * scale
            for h in range(HG_PAIR):
                qh_s[h, dst, :] = qh[:, h * hd:(h + 1) * hd]
            for d, (f_ref, k_s, e_s) in enumerate(((ff_ref, kf_s, ef_s), (fb_ref, kb_s, eb_s))):
                x = f_ref[0, src, :]
                sp = jnp.maximum(-x, 0.0) + jnp.log(1.0 + jnp.exp(-jnp.abs(x)))
                log_lb = lbp_ref[d, 0:1, :]
                y = lbp_ref[d, 1:2, :] - sp
                mx = jnp.maximum(log_lb, y)
                log_f = mx + jnp.log(1.0 + jnp.exp(-jnp.abs(log_lb - y)))
                one_minus_f = lbp_ref[d, 2:3, :] * jnp.exp(-(sp + x))
                e = _split3_dot(tri_ref[d], log_f) * LOG2E
                for h in range(HG_PAIR):
                    e_s[h, dst, :] = e[:, h * hd:(h + 1) * hd]
                    k_s[h, dst, :] = one_minus_f[:, h * hd:(h + 1) * hd]

    prep(qc_ref, ffc_ref, fbc_ref, 0, n_ctx)
    prep(ql_ref, ffl_ref, fbl_ref, n_ctx, n_lat)

    diag_code = len(HG_LEVELS) + 1

    st_s[...] = jnp.zeros_like(st_s)

    def chunk(off, v, h, d, out):
        fwd = d == 0
        k_s, e_s = (kf_s, ef_s) if fwd else (kb_s, eb_s)
        q = qh_s[h, pl.ds(off, c), :]
        k = k_s[h, pl.ds(off, c), :]
        e = e_s[h, pl.ds(off, c), :]
        e_end = e_s[h, pl.ds(off + (c - 1 if fwd else 0), 1), :]
        st = st_s[h, d]
        qi = (q * jnp.exp2(e)).astype(BF16)
        ki = (k * jnp.exp2(e_end - e)).astype(BF16)
        o = lax.dot_general(qi, st.astype(BF16), nt, preferred_element_type=F32)
        yield
        st_new = st * jnp.exp2(e_end) + lax.dot_general(v, ki, tn, preferred_element_type=F32)
        yield
        code = code_ref[d]
        att = None
        for li, w in enumerate(HG_LEVELS):
            zeros = jnp.zeros((w, hd), F32)
            q_parts, k_parts = [], []
            for a in range(0, c, 2 * w):
                lo, hi = slice(a, a + w), slice(a + w, a + 2 * w)
                if fwd:
                    ref = e_s[h, pl.ds(off + a + w - 1, 1), :]
                    q_parts += [zeros, q[hi] * jnp.exp2(e[hi] - ref)]
                    k_parts += [k[lo] * jnp.exp2(ref - e[lo]), zeros]
                else:
                    ref = e_s[h, pl.ds(off + a + w, 1), :]
                    q_parts += [q[lo] * jnp.exp2(e[lo] - ref), zeros]
                    k_parts += [zeros, k[hi] * jnp.exp2(ref - e[hi])]
            qs = jnp.concatenate(q_parts, axis=0).astype(BF16)
            ks = jnp.concatenate(k_parts, axis=0).astype(BF16)
            a_w = lax.dot_general(qs, ks, nt, preferred_element_type=F32)
            att = a_w if att is None else jnp.where(code == li + 1, a_w, att)
            yield
        prods = []
        for i in range(c // HG_DIAG):
            bs = slice(i * HG_DIAG, (i + 1) * HG_DIAG)
            row_prods = []
            for s in range(HG_DIAG):
                kr = k_s[h, pl.ds(off + i * HG_DIAG + s, 1), :]
                er = e_s[h, pl.ds(off + i * HG_DIAG + s, 1), :]
                decay = jnp.exp2(jnp.minimum(e[bs] - er, 0.0))
                row_prods.append((q[bs] * kr * decay).astype(BF16))
            prods.append(jnp.concatenate(row_prods, axis=1))
        diag = jnp.dot(jnp.concatenate(prods, axis=0), lsum_ref[...], preferred_element_type=F32)
        yield
        att = jnp.where(code == diag_code, diag, att)
        out.append((o + jnp.dot(att.astype(BF16), v, preferred_element_type=F32), st_new))
        yield

    n_stages = len(HG_LEVELS) + 4

    def scan(v_ref, base, n):
        def body(i, carry):
            rf = pl.multiple_of(i * c, c)
            rb = pl.multiple_of((n - 1 - i) * c, c)
            vf = v_ref[0, pl.ds(rf, c), :].astype(BF16)
            vb = v_ref[0, pl.ds(rb, c), :].astype(BF16)
            chains = []
            for h in range(HG_PAIR):
                hs = slice(h * hd, (h + 1) * hd)
                for d, r, v in ((0, rf, vf), (1, rb, vb)):
                    out = []
                    chains.append((h, d, r, out, chunk(base + r, v[:, hs], h, d, out)))
            for _ in range(n_stages):
                for chain in chains:
                    next(chain[-1])
            for h, d, r, out, _ in chains:
                o, st_new = out[0]
                (of_s if d == 0 else ob_s)[h, pl.ds(base + r, c), :] = o
                st_s[h, d] = st_new
            return carry

        lax.fori_loop(0, n, body, 0)

    scan(ic_ref, 0, n_ctx // c)
    scan(il_ref, n_ctx, n_lat // c)

    def finish(g_ref, o_ref, base, n):
        step = 256
        for t0 in range(0, n, step):
            src = slice(t0, t0 + step)
            dst = slice(base + t0, base + t0 + step)
            for h in range(HG_PAIR):
                hs = slice(h * hd, (h + 1) * hd)
                o = of_s[h, dst, :] + ob_s[h, dst, :]
                o = o * lax.rsqrt(jnp.mean(o * o, axis=-1, keepdims=True) + EPS) * nw_ref[...]
                o_ref[0, src, hs] = (o * _silu(g_ref[0, src, hs])).astype(o_ref.dtype)

    finish(gc_ref, oc_ref, 0, n_ctx)
    finish(gl_ref, ol_ref, n_ctx, n_lat)


def _hg_constants():
    c = HG_CHUNK
    row, col = np.arange(c)[:, None], np.arange(c)[None, :]
    blocks = np.eye(HG_PREP_ROWS // c)
    tri = np.stack([np.kron(blocks, col <= row), np.kron(blocks, col >= row)]).astype(np.float32)
    same8 = (row // HG_DIAG) == (col // HG_DIAG)
    diag_code = len(HG_LEVELS) + 1
    code_f = np.where(same8 & (col <= row), diag_code, 0)
    code_b = np.where(same8 & (col >= row), diag_code, 0)
    for li, w in enumerate(HG_LEVELS):
        same = (row // (2 * w)) == (col // (2 * w))
        t_hi, s_hi = (row % (2 * w)) >= w, (col % (2 * w)) >= w
        code_f = np.where(same & t_hi & ~s_hi, li + 1, code_f)
        code_b = np.where(same & ~t_hi & s_hi, li + 1, code_b)
    codes = np.stack([code_f, code_b]).astype(np.int32)
    lane_sum = (np.arange(HG_DIAG * HG_DIM)[:, None] // HG_DIM == col % HG_DIAG).astype(np.float32)
    return jnp.asarray(tri, BF16), jnp.asarray(codes), jnp.asarray(lane_sum, BF16)


def _hgrn2(pl3, pc3, lbp, norm_w):
    tri, codes, lane_sum = _hg_constants()
    b, t, _ = pl3.shape
    ctx = pc3.shape[1]
    hd = HG_DIM
    pw = HG_PAIR * hd
    n = t + ctx
    first = _HG0 * LANE // pw

    def col(stream):
        return lambda i, j: (i, 0, first + stream * (HG_HEADS // HG_PAIR) + j)

    lat_specs = [pl.BlockSpec((1, t, pw), col(s)) for s in range(5)]
    ctx_specs = [pl.BlockSpec((1, ctx, pw), col(s)) for s in range(5)]
    big = lambda: pltpu.VMEM((HG_PAIR, n, hd), F32)
    return pl.pallas_call(
        _hg_kernel,
        grid=(b, HG_HEADS // HG_PAIR),
        in_specs=lat_specs + ctx_specs + [
            pl.BlockSpec((2, 3, pw), lambda i, j: (0, 0, j)),
            pl.BlockSpec((1, hd), lambda i, j: (0, 0)),
            pl.BlockSpec(tri.shape, lambda i, j: (0, 0, 0)),
            pl.BlockSpec(codes.shape, lambda i, j: (0, 0, 0)),
            pl.BlockSpec(lane_sum.shape, lambda i, j: (0, 0)),
        ],
        out_specs=[
            pl.BlockSpec((1, t, pw), lambda i, j: (i, 0, j)),
            pl.BlockSpec((1, ctx, pw), lambda i, j: (i, 0, j)),
        ],
        out_shape=[
            jax.ShapeDtypeStruct((b, t, HG_WIDTH), BF16),
            jax.ShapeDtypeStruct((b, ctx, HG_WIDTH), BF16),
        ],
        scratch_shapes=[big() for _ in range(7)] + [pltpu.VMEM((HG_PAIR, 2, hd, hd), F32)],
        compiler_params=_cparams(("arbitrary", "arbitrary")),
        name="hgrn2",
    )(*([pl3] * 5 + [pc3] * 5 + [lbp, norm_w.reshape(1, hd), tri, codes, lane_sum]))


def _outproj_kernel(oa_ref, ob_ref, oc_ref, wa_ref, wb_ref, wc_ref, x_ref, g_ref, o_ref):
    dot = functools.partial(jnp.dot, preferred_element_type=F32)
    y = dot(oa_ref[...], wa_ref[...]) + dot(ob_ref[...], wb_ref[...]) + dot(oc_ref[...], wc_ref[...])
    o_ref[...] = x_ref[...] + g_ref[0] * y


def _outproj(oa, ob, oc, w, x2d, mod, rows_per_mod, mod_row0, tm):
    m, d = x2d.shape
    tn = 512
    mrow = lambda i: mod_row0 + (i * tm) // rows_per_mod
    nb = NA_WIDTH // HG_WIDTH
    return pl.pallas_call(
        _outproj_kernel,
        grid=(m // tm, d // tn),
        in_specs=[
            pl.BlockSpec((tm, NA_WIDTH), lambda i, j: (i, 0)),
            pl.BlockSpec((tm, HG_WIDTH), lambda i, j: (i, 0)),
            pl.BlockSpec((tm, GM_WIDTH), lambda i, j: (i, 0)),
            pl.BlockSpec((NA_WIDTH, tn), lambda i, j: (0, j)),
            pl.BlockSpec((HG_WIDTH, tn), lambda i, j: (nb, j)),
            pl.BlockSpec((GM_WIDTH, tn), lambda i, j: (nb + 1, j)),
            pl.BlockSpec((tm, tn), lambda i, j: (i, j)),
            pl.BlockSpec((1, 1, tn), lambda i, j: (mrow(i), 0, 2 * (d // tn) + j)),
        ],
        out_specs=pl.BlockSpec((tm, tn), lambda i, j: (i, j)),
        out_shape=jax.ShapeDtypeStruct((m, d), F32),
        compiler_params=_cparams(("arbitrary", "arbitrary")),
        name="outproj",
    )(oa, ob, oc, w, w, w, x2d, mod)


def _mlp_kernel(x_ref, nw_ref, sh_ref, sc_ref, g_ref, w1_ref, w2_ref, fnw_ref, o_ref, h_ref, acc_ref,
                *, final_norm):
    j = pl.program_id(1)

    @pl.when(j == 0)
    def _():
        h_ref[...] = _norm_modulate(x_ref[...], nw_ref[...], sh_ref[0], sc_ref[0]).astype(BF16)
        acc_ref[...] = jnp.zeros_like(acc_ref)

    a = jnp.maximum(jnp.dot(h_ref[...], w1_ref[...], preferred_element_type=F32), 0.0)
    acc_ref[...] += jnp.dot((a * a).astype(BF16), w2_ref[...], preferred_element_type=F32)

    @pl.when(j == pl.num_programs(1) - 1)
    def _():
        y = x_ref[...] + g_ref[0] * acc_ref[...]
        if final_norm:
            y = y * lax.rsqrt(jnp.mean(y * y, axis=-1, keepdims=True) + EPS) * fnw_ref[...]
        o_ref[...] = y


def _mlp(x2d, nw, mod, w1, w2, fnw, rows_per_mod, mod_row0, tm, final_norm):
    m, d = x2d.shape
    hid = w1.shape[1]
    th = 512
    mrow = lambda i: mod_row0 + (i * tm) // rows_per_mod
    return pl.pallas_call(
        functools.partial(_mlp_kernel, final_norm=final_norm),
        grid=(m // tm, hid // th),
        in_specs=[
            pl.BlockSpec((tm, d), lambda i, j: (i, 0)),
            pl.BlockSpec((1, d), lambda i, j: (0, 0)),
            pl.BlockSpec((1, 1, d), lambda i, j: (mrow(i), 0, 3)),
            pl.BlockSpec((1, 1, d), lambda i, j: (mrow(i), 0, 4)),
            pl.BlockSpec((1, 1, d), lambda i, j: (mrow(i), 0, 5)),
            pl.BlockSpec((d, th), lambda i, j: (0, j)),
            pl.BlockSpec((th, d), lambda i, j: (j, 0)),
            pl.BlockSpec((1, d), lambda i, j: (0, 0)),
        ],
        out_specs=pl.BlockSpec((tm, d), lambda i, j: (i, 0)),
        out_shape=jax.ShapeDtypeStruct((m, d), F32),
        scratch_shapes=[pltpu.VMEM((tm, d), BF16), pltpu.VMEM((tm, d), F32)],
        compiler_params=_cparams(("arbitrary", "arbitrary")),
        name="mlp",
    )(x2d, nw, mod, mod, mod, w1, w2, fnw)


def kernel(x, c, ctx, c_ctx, ada_w, ada_b, norm1_w, norm2_w, w_in, na_rpb, hg_lb_logits, hg_norm_w,
           gm_ln_w, gm_ws, gm_bs, w_out, mlp_w1, mlp_w2, final_norm_w):
    bsz, seq, d = x.shape
    n_ctx = ctx.shape[1]
    depth = ada_w.shape[0]
    mod_rows = 16
    assert bsz < mod_rows and d == D_MODEL and seq % 512 == 0 and n_ctx % 256 == 0
    assert seq // NA_GROUP_TOK >= NA_WIN_BLOCKS and seq // GRID_W >= 2 * NA_WIN_ROWS

    lb = jnp.cumsum(jax.nn.softmax(hg_lb_logits.astype(F32), axis=0), axis=0)
    lb = lb - lb[:1]
    lbp = jnp.stack([jnp.log(lb), jnp.log1p(-lb), 1.0 - lb], axis=2)

    cond = jnp.zeros((mod_rows, d), F32).at[:bsz].set(c).at[bsz].set(c_ctx)
    mods = _ada(cond, ada_w, ada_b)

    xl = x.reshape(bsz * seq, d)
    xc = ctx.reshape(bsz * n_ctx, d)
    tm_l, tm_c = 512, 256
    for l in range(depth):
        need_ctx = l < depth - 1
        mod = mods[l].reshape(mod_rows, 1, 6 * d)
        nw1 = norm1_w[l].reshape(1, d)
        nw2 = norm2_w[l].reshape(1, d)
        w_in_b = w_in[l].astype(BF16)
        w_out_b = w_out[l].astype(BF16)
        w1_b = mlp_w1[l].astype(BF16)
        w2_b = mlp_w2[l].astype(BF16)
        fnw = final_norm_w.reshape(1, d)

        p_l = _inproj(xl, nw1, mod, w_in_b, seq, 0, tm_l).reshape(bsz, seq, IN_WIDTH)
        p_c = _inproj(xc, nw1, mod, w_in_b, bsz * n_ctx, bsz, tm_c).reshape(bsz, n_ctx, IN_WIDTH)

        oa_l = _na_latent(p_l, p_c, _na_bias_table(na_rpb[l], seq // GRID_W))
        ob_l, ob_c = _hgrn2(p_l, p_c, lbp[l], hg_norm_w[l])
        oc_l = _chunk_gmlp(p_l, gm_ln_w[l], gm_ws[l], gm_bs[l])
        flat = lambda a: a.reshape(-1, a.shape[-1])
        xl = _outproj(flat(oa_l), flat(ob_l), flat(oc_l), w_out_b, xl, mod, seq, 0, tm_l)
        xl = _mlp(xl, nw2, mod, w1_b, w2_b, fnw, seq, 0, tm_l, final_norm=not need_ctx)
        if need_ctx:
            oa_c = _ctx_attention(p_c)
            oc_c = _chunk_gmlp(p_c, gm_ln_w[l], gm_ws[l], gm_bs[l])
            xc = _outproj(flat(oa_c), flat(ob_c), flat(oc_c), w_out_b, xc, mod, bsz * n_ctx, bsz, tm_c)
            xc = _mlp(xc, nw2, mod, w1_b, w2_b, fnw, bsz * n_ctx, bsz, tm_c, final_norm=False)
    return xl.reshape(bsz, seq, d)
```

```python
import functools

import numpy as np
import jax
import jax.numpy as jnp
from jax import lax
from jax.experimental import pallas as pl
from jax.experimental.pallas import tpu as pltpu

F32 = jnp.float32
BF16 = jnp.bfloat16

D_MODEL = 2048
DEPTH = 2
GRID_W = 64
EPS = 1e-6

NA_HEAD_DIM = 128
NA_HEADS = 8
NA_WIDTH = NA_HEADS * NA_HEAD_DIM
NA_WIN_ROWS = 8
NA_WIN_COLS = 16

HG_HEADS = 4
HG_DIM = 128
HG_WIDTH = HG_HEADS * HG_DIM
HG_CHUNK = 64

GM_GROUPS = 4
GM_DIM = 128
GM_WIDTH = GM_GROUPS * GM_DIM
GM_CHUNK = 128

IN_WIDTH = 3 * NA_WIDTH + 5 * HG_WIDTH + 2 * GM_WIDTH
MLP_HIDDEN = 4 * D_MODEL
LANE = 128
LOG2E = 1.4426950408889634

_QA, _KA, _VA = 0, NA_HEADS, 2 * NA_HEADS
_HG0 = 3 * NA_HEADS
_GM_U = (3 * NA_WIDTH + 5 * HG_WIDTH) // GM_WIDTH
_GM_V = _GM_U + 1

VMEM_LIMIT_V7X = 56 * 1024 * 1024
INPROJ_TN = 1664
MLP_TH = 1024
OUTPROJ_TN = 2048


def _cparams(sem):
    return pltpu.CompilerParams(dimension_semantics=sem, vmem_limit_bytes=VMEM_LIMIT_V7X)


def _silu(x):
    return x / (1.0 + jnp.exp(-x))


def _gelu_tanh(x):
    return 0.5 * x * (1.0 + jnp.tanh(0.7978845608028654 * (x + 0.044715 * (x * x * x))))


def _norm_modulate(x, nw, sh, sc):
    ms = jnp.mean(x * x, axis=-1, keepdims=True)
    y = x * lax.rsqrt(ms + EPS) * nw
    return y * (1.0 + sc) + sh


def _ada_kernel(c_ref, w_ref, b_ref, o_ref):
    s = _silu(c_ref[...]).astype(BF16)
    o_ref[0] = jnp.dot(s, w_ref[0].astype(BF16), preferred_element_type=F32) + b_ref[0]


def _ada(cond, ada_w, ada_b):
    depth, d, n = ada_w.shape
    tn = 1536
    return pl.pallas_call(
        _ada_kernel,
        grid=(depth, n // tn),
        in_specs=[
            pl.BlockSpec((cond.shape[0], d), lambda l, j: (0, 0)),
            pl.BlockSpec((1, d, tn), lambda l, j: (l, 0, j)),
            pl.BlockSpec((1, 1, tn), lambda l, j: (l, 0, j)),
        ],
        out_specs=pl.BlockSpec((1, cond.shape[0], tn), lambda l, j: (l, 0, j)),
        out_shape=jax.ShapeDtypeStruct((depth, cond.shape[0], n), F32),
        compiler_params=_cparams(("arbitrary", "arbitrary")),
        name="ada",
    )(cond, ada_w, ada_b.reshape(depth, 1, n))


def _inproj_kernel(x_ref, nw_ref, sh_ref, sc_ref, w_ref, o_ref, h_ref):
    @pl.when(pl.program_id(1) == 0)
    def _():
        h_ref[...] = _norm_modulate(x_ref[...], nw_ref[...], sh_ref[0], sc_ref[0]).astype(BF16)

    o_ref[...] = jnp.dot(h_ref[...], w_ref[...], preferred_element_type=F32)


def _inproj(x2d, nw, mod, w, rows_per_mod, mod_row0, tm):
    m, d = x2d.shape
    n = w.shape[1]
    tn = INPROJ_TN
    mrow = lambda i: mod_row0 + (i * tm) // rows_per_mod
    return pl.pallas_call(
        _inproj_kernel,
        grid=(m // tm, n // tn),
        in_specs=[
            pl.BlockSpec((tm, d), lambda i, j: (i, 0)),
            pl.BlockSpec((1, d), lambda i, j: (0, 0)),
            pl.BlockSpec((1, 1, d), lambda i, j: (mrow(i), 0, 0)),
            pl.BlockSpec((1, 1, d), lambda i, j: (mrow(i), 0, 1)),
            pl.BlockSpec((d, tn), lambda i, j: (0, j)),
        ],
        out_specs=pl.BlockSpec((tm, tn), lambda i, j: (i, j)),
        out_shape=jax.ShapeDtypeStruct((m, n), F32),
        scratch_shapes=[pltpu.VMEM((tm, d), BF16)],
        compiler_params=_cparams(("arbitrary", "arbitrary")),
        name="inproj",
    )(x2d, nw, mod, mod, w)


NA_GROUP_ROWS = 4
NA_GROUP_TOK = NA_GROUP_ROWS * GRID_W
NA_WIN_BLOCKS = 3
NA_GROUPS_PER_STEP = 4


def _na_bias_table(rpb, rows):
    n_groups = rows // NA_GROUP_ROWS
    h = rpb.shape[0]
    kcol = np.arange(GRID_W)[:, None]
    qcol = np.arange(GRID_W)[None, :]
    wstart = np.clip(qcol - NA_WIN_COLS // 2, 0, GRID_W - NA_WIN_COLS)
    valid_col = (kcol >= wstart) & (kcol < wstart + NA_WIN_COLS)
    dcol = np.clip(kcol - qcol, 1 - NA_WIN_COLS, NA_WIN_COLS - 1) + NA_WIN_COLS - 1
    slabs = jnp.where(valid_col, rpb.astype(F32)[:, :, dcol], -1e30)
    masked = 2 * NA_WIN_ROWS - 1
    slabs = jnp.concatenate([slabs, jnp.full((h, 1, GRID_W, GRID_W), -1e30, F32)], axis=1)
    krel = np.arange(NA_WIN_BLOCKS * NA_GROUP_ROWS)[:, None]
    qrel = np.arange(NA_GROUP_ROWS)[None, :]
    idx = []
    for g in (0, 1, n_groups - 1):
        krow = int(np.clip(g - 1, 0, n_groups - NA_WIN_BLOCKS)) * NA_GROUP_ROWS + krel
        qrow = g * NA_GROUP_ROWS + qrel
        r0 = np.clip(qrow - NA_WIN_ROWS // 2, 0, rows - NA_WIN_ROWS)
        in_window = (krow >= r0) & (krow < r0 + NA_WIN_ROWS)
        idx.append(np.where(in_window, krow - qrow + NA_WIN_ROWS - 1, masked))
    table = slabs[:, np.stack(idx)]
    table = table.transpose(0, 1, 2, 4, 3, 5)
    return table.reshape(h, 3, NA_WIN_BLOCKS * NA_GROUP_TOK, NA_GROUP_TOK)


def _na_kernel(q_ref, k_ref, v_ref, kc_ref, vc_ref, bias_ref, o_ref, qb_ref, kb_ref, vt_ref):
    n_groups = q_ref.shape[1] // NA_GROUP_TOK
    gt = NA_GROUP_TOK
    scale = NA_HEAD_DIM ** -0.5
    nt = (((1,), (1,)), ((), ()))
    dot = functools.partial(jnp.dot, preferred_element_type=F32)
    qb_ref[...] = q_ref[0].astype(BF16)
    kb_ref[...] = k_ref[0].astype(BF16)
    for b in range(n_groups):
        vt_ref[b] = v_ref[0, b * gt:(b + 1) * gt, :].T.astype(BF16)
    kc = kc_ref[0].astype(BF16)
    vct = vc_ref[0].T.astype(BF16)

    def group(g):
        blk = jnp.clip(g - 1, 0, n_groups - NA_WIN_BLOCKS)
        kind = (g > 0).astype(jnp.int32) + (g == n_groups - 1).astype(jnp.int32)
        qg = qb_ref[pl.ds(pl.multiple_of(g * gt, gt), gt), :]
        kw = kb_ref[pl.ds(pl.multiple_of(blk * gt, gt), NA_WIN_BLOCKS * gt), :]
        s_lat = lax.dot_general(kw, qg, nt, preferred_element_type=F32) * scale + bias_ref[0, kind]
        s_ctx = lax.dot_general(kc, qg, nt, preferred_element_type=F32) * scale
        yield
        m = jnp.maximum(jnp.max(s_lat, axis=0, keepdims=True), jnp.max(s_ctx, axis=0, keepdims=True))
        p_lat = jnp.exp(s_lat - m)
        p_ctx = jnp.exp(s_ctx - m)
        l = jnp.sum(p_lat, axis=0, keepdims=True) + jnp.sum(p_ctx, axis=0, keepdims=True)
        yield
        ot = dot(vct, p_ctx.astype(BF16))
        for j in range(NA_WIN_BLOCKS):
            ot = ot + dot(vt_ref[blk + j], p_lat[j * gt:(j + 1) * gt, :].astype(BF16))
        yield
        o_ref[0, pl.ds(pl.multiple_of(g * gt, gt), gt), :] = (ot / l).T.astype(o_ref.dtype)
        yield

    def body(i, carry):
        groups = [group(i * NA_GROUPS_PER_STEP + u) for u in range(NA_GROUPS_PER_STEP)]
        for _ in range(4):
            for grp in groups:
                next(grp)
        return carry

    lax.fori_loop(0, n_groups // NA_GROUPS_PER_STEP, body, 0)


def _na_latent(pl3, pc3, bias):
    b, t, _ = pl3.shape
    ctx = pc3.shape[1]
    hd = NA_HEAD_DIM
    return pl.pallas_call(
        _na_kernel,
        grid=(NA_HEADS, b),
        in_specs=[
            pl.BlockSpec((1, t, hd), lambda h, i: (i, 0, _QA + h)),
            pl.BlockSpec((1, t, hd), lambda h, i: (i, 0, _KA + h)),
            pl.BlockSpec((1, t, hd), lambda h, i: (i, 0, _VA + h)),
            pl.BlockSpec((1, ctx, hd), lambda h, i: (i, 0, _KA + h)),
            pl.BlockSpec((1, ctx, hd), lambda h, i: (i, 0, _VA + h)),
            pl.BlockSpec((1,) + bias.shape[1:], lambda h, i: (h, 0, 0, 0)),
        ],
        out_specs=pl.BlockSpec((1, t, hd), lambda h, i: (i, 0, h)),
        out_shape=jax.ShapeDtypeStruct((b, t, NA_WIDTH), BF16),
        scratch_shapes=[
            pltpu.VMEM((t, hd), BF16),
            pltpu.VMEM((t, hd), BF16),
            pltpu.VMEM((t // NA_GROUP_TOK, hd, NA_GROUP_TOK), BF16),
        ],
        compiler_params=_cparams(("arbitrary", "arbitrary")),
        name="na_latent",
    )(pl3, pl3, pl3, pc3, pc3, bias)


def _ctx_attn_kernel(q_ref, k_ref, v_ref, o_ref):
    scale = NA_HEAD_DIM ** -0.5
    q = q_ref[0].astype(BF16)
    k = k_ref[0].astype(BF16)
    v = v_ref[0].astype(BF16)
    s = lax.dot_general(q, k, (((1,), (1,)), ((), ())), preferred_element_type=F32) * scale
    p = jnp.exp(s - jnp.max(s, axis=-1, keepdims=True))
    l = jnp.sum(p, axis=-1, keepdims=True)
    o = jnp.dot(p.astype(BF16), v, preferred_element_type=F32)
    o_ref[0] = (o / l).astype(o_ref.dtype)


def _ctx_attention(pc3):
    b, ctx, _ = pc3.shape
    hd = NA_HEAD_DIM
    return pl.pallas_call(
        _ctx_attn_kernel,
        grid=(b, NA_HEADS),
        in_specs=[
            pl.BlockSpec((1, ctx, hd), lambda i, h: (i, 0, _QA + h)),
            pl.BlockSpec((1, ctx, hd), lambda i, h: (i, 0, _KA + h)),
            pl.BlockSpec((1, ctx, hd), lambda i, h: (i, 0, _VA + h)),
        ],
        out_specs=pl.BlockSpec((1, ctx, hd), lambda i, h: (i, 0, h)),
        out_shape=jax.ShapeDtypeStruct((b, ctx, NA_WIDTH), BF16),
        compiler_params=_cparams(("arbitrary", "arbitrary")),
        name="ctx_attention",
    )(pc3, pc3, pc3)


def _gm_kernel(u_ref, v_ref, lnw_ref, ws_ref, bs_ref, o_ref):
    uf = _gelu_tanh(u_ref[0])
    vf = _gelu_tanh(v_ref[0])
    for g in range(GM_GROUPS):
        sl = slice(g * GM_DIM, (g + 1) * GM_DIM)
        vg = vf[:, sl]
        mu = jnp.mean(vg, axis=-1, keepdims=True)
        dv = vg - mu
        var = jnp.mean(dv * dv, axis=-1, keepdims=True)
        vn = dv * lax.rsqrt(var + EPS) * lnw_ref[:, sl]
        mixed = jnp.dot(ws_ref[g].astype(BF16), vn.astype(BF16), preferred_element_type=F32) + bs_ref[g]
        o_ref[0, :, sl] = (uf[:, sl] * mixed).astype(o_ref.dtype)


def _chunk_gmlp(p3, ln_w, ws, bs):
    b, t, _ = p3.shape
    return pl.pallas_call(
        _gm_kernel,
        grid=(b, t // GM_CHUNK),
        in_specs=[
            pl.BlockSpec((1, GM_CHUNK, GM_WIDTH), lambda i, c: (i, c, _GM_U)),
            pl.BlockSpec((1, GM_CHUNK, GM_WIDTH), lambda i, c: (i, c, _GM_V)),
            pl.BlockSpec((1, GM_WIDTH), lambda i, c: (0, 0)),
            pl.BlockSpec((GM_GROUPS, GM_CHUNK, GM_CHUNK), lambda i, c: (0, 0, 0)),
            pl.BlockSpec((GM_GROUPS, GM_CHUNK, 1), lambda i, c: (0, 0, 0)),
        ],
        out_specs=pl.BlockSpec((1, GM_CHUNK, GM_WIDTH), lambda i, c: (i, c, 0)),
        out_shape=jax.ShapeDtypeStruct((b, t, GM_WIDTH), BF16),
        compiler_params=_cparams(("arbitrary", "arbitrary")),
        name="chunk_gmlp",
    )(p3, p3, ln_w.reshape(1, GM_WIDTH), ws, bs.reshape(GM_GROUPS, GM_CHUNK, 1))


HG_PAIR = 2
HG_LEVELS = (32, 16, 8)
HG_DIAG = 8
HG_PREP_ROWS = 256


def _split3_dot(tri, g):
    g0 = g.astype(BF16)
    r1 = g - g0.astype(F32)
    g1 = r1.astype(BF16)
    g2 = (r1 - g1.astype(F32)).astype(BF16)
    dot = functools.partial(jnp.dot, preferred_element_type=F32)
    return dot(tri, g0) + dot(tri, g1) + dot(tri, g2)


def _hg_kernel(ql_ref, ffl_ref, fbl_ref, il_ref, gl_ref,
               qc_ref, ffc_ref, fbc_ref, ic_ref, gc_ref,
               lbp_ref, nw_ref, tri_ref, code_ref, lsum_ref, ol_ref, oc_ref,
               qh_s, kf_s, kb_s, ef_s, eb_s, of_s, ob_s, st_s):
    c = HG_CHUNK
    hd = HG_DIM
    n_ctx = qc_ref.shape[1]
    n_lat = ql_ref.shape[1]
    n_chunks = (n_ctx + n_lat) // c
    scale = hd ** -0.5
    nt = (((1,), (1,)), ((), ()))
    tn = (((0,), (0,)), ((), ()))

    def prep(q_ref, ff_ref, fb_ref, base, n):
        step = HG_PREP_ROWS
        for t0 in range(0, n, step):
            src = slice(t0, t0 + step)
            dst = slice(base + t0, base + t0 + step)
            qh = _silu(q_ref[0, src, :]) * scale
            for h in range(HG_PAIR):
                qh_s[h, dst, :] = qh[:, h * hd:(h + 1) * hd]
            for d, (f_ref, k_s, e_s) in enumerate(((ff_ref, kf_s, ef_s), (fb_ref, kb_s, eb_s))):
                x = f_ref[0, src, :]
                sp = jnp.maximum(-x, 0.0) + jnp.log(1.0 + jnp.exp(-jnp.abs(x)))
                log_lb = lbp_ref[d, 0:1, :]
                y = lbp_ref[d, 1:2, :] - sp
                mx = jnp.maximum(log_lb, y)
                log_f = mx + jnp.log(1.0 + jnp.exp(-jnp.abs(log_lb - y)))
                one_minus_f = lbp_ref[d, 2:3, :] * jnp.exp(-(sp + x))
                e = _split3_dot(tri_ref[d], log_f) * LOG2E
                for h in range(HG_PAIR):
                    e_s[h, dst, :] = e[:, h * hd:(h + 1) * hd]
                    k_s[h, dst, :] = one_minus_f[:, h * hd:(h + 1) * hd]

    prep(qc_ref, ffc_ref, fbc_ref, 0, n_ctx)
    prep(ql_ref, ffl_ref, fbl_ref, n_ctx, n_lat)

    diag_code = len(HG_LEVELS) + 1

    st_s[...] = jnp.zeros_like(st_s)

    def chunk(off, v, h, d, out):
        fwd = d == 0
        k_s, e_s = (kf_s, ef_s) if fwd else (kb_s, eb_s)
        q = qh_s[h, pl.ds(off, c), :]
        k = k_s[h, pl.ds(off, c), :]
        e = e_s[h, pl.ds(off, c), :]
        e_end = e_s[h, pl.ds(off + (c - 1 if fwd else 0), 1), :]
        st = st_s[h, d]
        qi = (q * jnp.exp2(e)).astype(BF16)
        ki = (k * jnp.exp2(e_end - e)).astype(BF16)
        o = lax.dot_general(qi, st.astype(BF16), nt, preferred_element_type=F32)
        yield
        st_new = st * jnp.exp2(e_end) + lax.dot_general(v, ki, tn, preferred_element_type=F32)
        yield
        code = code_ref[d]
        att = None
        for li, w in enumerate(HG_LEVELS):
            zeros = jnp.zeros((w, hd), F32)
            q_parts, k_parts = [], []
            for a in range(0, c, 2 * w):
                lo, hi = slice(a, a + w), slice(a + w, a + 2 * w)
                if fwd:
                    ref = e_s[h, pl.ds(off + a + w - 1, 1), :]
                    q_parts += [zeros, q[hi] * jnp.exp2(e[hi] - ref)]
                    k_parts += [k[lo] * jnp.exp2(ref - e[lo]), zeros]
                else:
                    ref = e_s[h, pl.ds(off + a + w, 1), :]
                    q_parts += [q[lo] * jnp.exp2(e[lo] - ref), zeros]
                    k_parts += [zeros, k[hi] * jnp.exp2(ref - e[hi])]
            qs = jnp.concatenate(q_parts, axis=0).astype(BF16)
            ks = jnp.concatenate(k_parts, axis=0).astype(BF16)
            a_w = lax.dot_general(qs, ks, nt, preferred_element_type=F32)
            att = a_w if att is None else jnp.where(code == li + 1, a_w, att)
            yield
        prods = []
        for i in range(c // HG_DIAG):
            bs = slice(i * HG_DIAG, (i + 1) * HG_DIAG)
            row_prods = []
            for s in range(HG_DIAG):
                kr = k_s[h, pl.ds(off + i * HG_DIAG + s, 1), :]
                er = e_s[h, pl.ds(off + i * HG_DIAG + s, 1), :]
                decay = jnp.exp2(jnp.minimum(e[bs] - er, 0.0))
                row_prods.append((q[bs] * kr * decay).astype(BF16))
            prods.append(jnp.concatenate(row_prods, axis=1))
        diag = jnp.dot(jnp.concatenate(prods, axis=0), lsum_ref[...], preferred_element_type=F32)
        yield
        att = jnp.where(code == diag_code, diag, att)
        out.append((o + jnp.dot(att.astype(BF16), v, preferred_element_type=F32), st_new))
        yield

    n_stages = len(HG_LEVELS) + 4

    def scan(v_ref, base, n):
        def body(i, carry):
            rf = pl.multiple_of(i * c, c)
            rb = pl.multiple_of((n - 1 - i) * c, c)
            vf = v_ref[0, pl.ds(rf, c), :].astype(BF16)
            vb = v_ref[0, pl.ds(rb, c), :].astype(BF16)
            chains = []
            for h in range(HG_PAIR):
                hs = slice(h * hd, (h + 1) * hd)
                for d, r, v in ((0, rf, vf), (1, rb, vb)):
                    out = []
                    chains.append((h, d, r, out, chunk(base + r, v[:, hs], h, d, out)))
            for _ in range(n_stages):
                for chain in chains:
                    next(chain[-1])
            for h, d, r, out, _ in chains:
                o, st_new = out[0]
                (of_s if d == 0 else ob_s)[h, pl.ds(base + r, c), :] = o
                st_s[h, d] = st_new
            return carry

        lax.fori_loop(0, n, body, 0)

    scan(ic_ref, 0, n_ctx // c)
    scan(il_ref, n_ctx, n_lat // c)

    def finish(g_ref, o_ref, base, n):
        step = 256
        for t0 in range(0, n, step):
            src = slice(t0, t0 + step)
            dst = slice(base + t0, base + t0 + step)
            for h in range(HG_PAIR):
                hs = slice(h * hd, (h + 1) * hd)
                o = of_s[h, dst, :] + ob_s[h, dst, :]
                o = o * lax.rsqrt(jnp.mean(o * o, axis=-1, keepdims=True) + EPS) * nw_ref[...]
                o_ref[0, src, hs] = (o * _silu(g_ref[0, src, hs])).astype(o_ref.dtype)

    finish(gc_ref, oc_ref, 0, n_ctx)
    finish(gl_ref, ol_ref, n_ctx, n_lat)


def _hg_constants():
    c = HG_CHUNK
    row, col = np.arange(c)[:, None], np.arange(c)[None, :]
    blocks = np.eye(HG_PREP_ROWS // c)
    tri = np.stack([np.kron(blocks, col <= row), np.kron(blocks, col >= row)]).astype(np.float32)
    same8 = (row // HG_DIAG) == (col // HG_DIAG)
    diag_code = len(HG_LEVELS) + 1
    code_f = np.where(same8 & (col <= row), diag_code, 0)
    code_b = np.where(same8 & (col >= row), diag_code, 0)
    for li, w in enumerate(HG_LEVELS):
        same = (row // (2 * w)) == (col // (2 * w))
        t_hi, s_hi = (row % (2 * w)) >= w, (col % (2 * w)) >= w
        code_f = np.where(same & t_hi & ~s_hi, li + 1, code_f)
        code_b = np.where(same & ~t_hi & s_hi, li + 1, code_b)
    codes = np.stack([code_f, code_b]).astype(np.int32)
    lane_sum = (np.arange(HG_DIAG * HG_DIM)[:, None] // HG_DIM == col % HG_DIAG).astype(np.float32)
    return jnp.asarray(tri, BF16), jnp.asarray(codes), jnp.asarray(lane_sum, BF16)


def _hgrn2(pl3, pc3, lbp, norm_w):
    tri, codes, lane_sum = _hg_constants()
    b, t, _ = pl3.shape
    ctx = pc3.shape[1]
    hd = HG_DIM
    pw = HG_PAIR * hd
    n = t + ctx
    first = _HG0 * LANE // pw

    def col(stream):
        return lambda i, j: (i, 0, first + stream * (HG_HEADS // HG_PAIR) + j)

    lat_specs = [pl.BlockSpec((1, t, pw), col(s)) for s in range(5)]
    ctx_specs = [pl.BlockSpec((1, ctx, pw), col(s)) for s in range(5)]
    big = lambda: pltpu.VMEM((HG_PAIR, n, hd), F32)
    return pl.pallas_call(
        _hg_kernel,
        grid=(b, HG_HEADS // HG_PAIR),
        in_specs=lat_specs + ctx_specs + [
            pl.BlockSpec((2, 3, pw), lambda i, j: (0, 0, j)),
            pl.BlockSpec((1, hd), lambda i, j: (0, 0)),
            pl.BlockSpec(tri.shape, lambda i, j: (0, 0, 0)),
            pl.BlockSpec(codes.shape, lambda i, j: (0, 0, 0)),
            pl.BlockSpec(lane_sum.shape, lambda i, j: (0, 0)),
        ],
        out_specs=[
            pl.BlockSpec((1, t, pw), lambda i, j: (i, 0, j)),
            pl.BlockSpec((1, ctx, pw), lambda i, j: (i, 0, j)),
        ],
        out_shape=[
            jax.ShapeDtypeStruct((b, t, HG_WIDTH), BF16),
            jax.ShapeDtypeStruct((b, ctx, HG_WIDTH), BF16),
        ],
        scratch_shapes=[big() for _ in range(7)] + [pltpu.VMEM((HG_PAIR, 2, hd, hd), F32)],
        compiler_params=_cparams(("arbitrary", "arbitrary")),
        name="hgrn2",
    )(*([pl3] * 5 + [pc3] * 5 + [lbp, norm_w.reshape(1, hd), tri, codes, lane_sum]))


def _outproj_kernel(oa_ref, ob_ref, oc_ref, wa_ref, wb_ref, wc_ref, x_ref, g_ref, o_ref):
    dot = functools.partial(jnp.dot, preferred_element_type=F32)
    y = dot(oa_ref[...], wa_ref[...]) + dot(ob_ref[...], wb_ref[...]) + dot(oc_ref[...], wc_ref[...])
    o_ref[...] = x_ref[...] + g_ref[0] * y


def _outproj(oa, ob, oc, w, x2d, mod, rows_per_mod, mod_row0, tm):
    m, d = x2d.shape
    tn = OUTPROJ_TN
    mrow = lambda i: mod_row0 + (i * tm) // rows_per_mod
    nb = NA_WIDTH // HG_WIDTH
    return pl.pallas_call(
        _outproj_kernel,
        grid=(m // tm, d // tn),
        in_specs=[
            pl.BlockSpec((tm, NA_WIDTH), lambda i, j: (i, 0)),
            pl.BlockSpec((tm, HG_WIDTH), lambda i, j: (i, 0)),
            pl.BlockSpec((tm, GM_WIDTH), lambda i, j: (i, 0)),
            pl.BlockSpec((NA_WIDTH, tn), lambda i, j: (0, j)),
            pl.BlockSpec((HG_WIDTH, tn), lambda i, j: (nb, j)),
            pl.BlockSpec((GM_WIDTH, tn), lambda i, j: (nb + 1, j)),
            pl.BlockSpec((tm, tn), lambda i, j: (i, j)),
            pl.BlockSpec((1, 1, tn), lambda i, j: (mrow(i), 0, 2 * (d // tn) + j)),
        ],
        out_specs=pl.BlockSpec((tm, tn), lambda i, j: (i, j)),
        out_shape=jax.ShapeDtypeStruct((m, d), F32),
        compiler_params=_cparams(("arbitrary", "arbitrary")),
        name="outproj",
    )(oa, ob, oc, w, w, w, x2d, mod)


def _mlp_kernel(x_ref, nw_ref, sh_ref, sc_ref, g_ref, w1_ref, w2_ref, fnw_ref, o_ref, h_ref, acc_ref,
                *, final_norm):
    j = pl.program_id(1)

    @pl.when(j == 0)
    def _():
        h_ref[...] = _norm_modulate(x_ref[...], nw_ref[...], sh_ref[0], sc_ref[0]).astype(BF16)
        acc_ref[...] = jnp.zeros_like(acc_ref)

    a = jnp.maximum(jnp.dot(h_ref[...], w1_ref[...], preferred_element_type=F32), 0.0)
    acc_ref[...] += jnp.dot((a * a).astype(BF16), w2_ref[...], preferred_element_type=F32)

    @pl.when(j == pl.num_programs(1) - 1)
    def _():
        y = x_ref[...] + g_ref[0] * acc_ref[...]
        if final_norm:
            y = y * lax.rsqrt(jnp.mean(y * y, axis=-1, keepdims=True) + EPS) * fnw_ref[...]
        o_ref[...] = y


def _mlp(x2d, nw, mod, w1, w2, fnw, rows_per_mod, mod_row0, tm, final_norm):
    m, d = x2d.shape
    hid = w1.shape[1]
    th = MLP_TH
    mrow = lambda i: mod_row0 + (i * tm) // rows_per_mod
    return pl.pallas_call(
        functools.partial(_mlp_kernel, final_norm=final_norm),
        grid=(m // tm, hid // th),
        in_specs=[
            pl.BlockSpec((tm, d), lambda i, j: (i, 0)),
            pl.BlockSpec((1, d), lambda i, j: (0, 0)),
            pl.BlockSpec((1, 1, d), lambda i, j: (mrow(i), 0, 3)),
            pl.BlockSpec((1, 1, d), lambda i, j: (mrow(i), 0, 4)),
            pl.BlockSpec((1, 1, d), lambda i, j: (mrow(i), 0, 5)),
            pl.BlockSpec((d, th), lambda i, j: (0, j)),
            pl.BlockSpec((th, d), lambda i, j: (j, 0)),
            pl.BlockSpec((1, d), lambda i, j: (0, 0)),
        ],
        out_specs=pl.BlockSpec((tm, d), lambda i, j: (i, 0)),
        out_shape=jax.ShapeDtypeStruct((m, d), F32),
        scratch_shapes=[pltpu.VMEM((tm, d), BF16), pltpu.VMEM((tm, d), F32)],
        compiler_params=_cparams(("arbitrary", "arbitrary")),
        name="mlp",
    )(x2d, nw, mod, mod, mod, w1, w2, fnw)


def kernel(x, c, ctx, c_ctx, ada_w, ada_b, norm1_w, norm2_w, w_in, na_rpb, hg_lb_logits, hg_norm_w,
           gm_ln_w, gm_ws, gm_bs, w_out, mlp_w1, mlp_w2, final_norm_w):
    bsz, seq, d = x.shape
    n_ctx = ctx.shape[1]
    depth = ada_w.shape[0]
    mod_rows = 16
    assert bsz < mod_rows and d == D_MODEL and seq % 512 == 0 and n_ctx % 256 == 0
    assert seq // NA_GROUP_TOK >= NA_WIN_BLOCKS and seq // GRID_W >= 2 * NA_WIN_ROWS

    lb = jnp.cumsum(jax.nn.softmax(hg_lb_logits.astype(F32), axis=0), axis=0)
    lb = lb - lb[:1]
    lbp = jnp.stack([jnp.log(lb), jnp.log1p(-lb), 1.0 - lb], axis=2)

    cond = jnp.zeros((mod_rows, d), F32).at[:bsz].set(c).at[bsz].set(c_ctx)
    mods = _ada(cond, ada_w, ada_b)

    xl = x.reshape(bsz * seq, d)
    xc = ctx.reshape(bsz * n_ctx, d)
    tm_l, tm_c = 512, 256
    for l in range(depth):
        need_ctx = l < depth - 1
        mod = mods[l].reshape(mod_rows, 1, 6 * d)
        nw1 = norm1_w[l].reshape(1, d)
        nw2 = norm2_w[l].reshape(1, d)
        w_in_b = w_in[l].astype(BF16)
        w_out_b = w_out[l].astype(BF16)
        w1_b = mlp_w1[l].astype(BF16)
        w2_b = mlp_w2[l].astype(BF16)
        fnw = final_norm_w.reshape(1, d)

        p_l = _inproj(xl, nw1, mod, w_in_b, seq, 0, tm_l).reshape(bsz, seq, IN_WIDTH)
        p_c = _inproj(xc, nw1, mod, w_in_b, bsz * n_ctx, bsz, tm_c).reshape(bsz, n_ctx, IN_WIDTH)

        oa_l = _na_latent(p_l, p_c, _na_bias_table(na_rpb[l], seq // GRID_W))
        ob_l, ob_c = _hgrn2(p_l, p_c, lbp[l], hg_norm_w[l])
        oc_l = _chunk_gmlp(p_l, gm_ln_w[l], gm_ws[l], gm_bs[l])
        flat = lambda a: a.reshape(-1, a.shape[-1])
        xl = _outproj(flat(oa_l), flat(ob_l), flat(oc_l), w_out_b, xl, mod, seq, 0, tm_l)
        xl = _mlp(xl, nw2, mod, w1_b, w2_b, fnw, seq, 0, tm_l, final_norm=not need_ctx)
        if need_ctx:
            oa_c = _ctx_attention(p_c)
            oc_c = _chunk_gmlp(p_c, gm_ln_w[l], gm_ws[l], gm_bs[l])
            xc = _outproj(flat(oa_c), flat(ob_c), flat(oc_c), w_out_b, xc, mod, bsz * n_ctx, bsz, tm_c)
            xc = _mlp(xc, nw2, mod, w1_b, w2_b, fnw, bsz * n_ctx, bsz, tm_c, final_norm=False)
    return xl.reshape(bsz, seq, d)
```

```python
import functools

import numpy as np
import jax
import jax.numpy as jnp
from jax import lax
from jax.experimental import pallas as pl
from jax.experimental.pallas import tpu as pltpu

F32 = jnp.float32
BF16 = jnp.bfloat16

D_MODEL = 2048
DEPTH = 2
GRID_W = 64
EPS = 1e-6

NA_HEAD_DIM = 128
NA_HEADS = 8
NA_WIDTH = NA_HEADS * NA_HEAD_DIM
NA_WIN_ROWS = 8
NA_WIN_COLS = 16

HG_HEADS = 4
HG_DIM = 128
HG_WIDTH = HG_HEADS * HG_DIM
HG_CHUNK = 64

GM_GROUPS = 4
GM_DIM = 128
GM_WIDTH = GM_GROUPS * GM_DIM
GM_CHUNK = 128

IN_WIDTH = 3 * NA_WIDTH + 5 * HG_WIDTH + 2 * GM_WIDTH
MLP_HIDDEN = 4 * D_MODEL
LANE = 128
LOG2E = 1.4426950408889634

_QA, _KA, _VA = 0, NA_HEADS, 2 * NA_HEADS
_HG0 = 3 * NA_HEADS
_GM_U = (3 * NA_WIDTH + 5 * HG_WIDTH) // GM_WIDTH
_GM_V = _GM_U + 1

VMEM_LIMIT_V7X = 56 * 1024 * 1024
INPROJ_TN = 1664
MLP_TH = 1024
OUTPROJ_TN = 2048
NORM_CHUNK_ROWS = 32
NORM_UNROLL = 4
MOD_ROWS = 16


def _cparams(sem):
    return pltpu.CompilerParams(dimension_semantics=sem, vmem_limit_bytes=VMEM_LIMIT_V7X)


def _silu(x):
    return x / (1.0 + jnp.exp(-x))


def _gelu_tanh(x):
    return 0.5 * x * (1.0 + jnp.tanh(0.7978845608028654 * (x + 0.044715 * (x * x * x))))


def _norm_modulate_store(x_ref, nw_ref, sh_ref, sc_ref, h_ref):
    gain = nw_ref[...] * (1.0 + sc_ref[0])
    shift = sh_ref[0]

    def body(i, carry):
        rows = pl.ds(pl.multiple_of(i * NORM_CHUNK_ROWS, NORM_CHUNK_ROWS), NORM_CHUNK_ROWS)
        x = x_ref[rows, :]
        rs = lax.rsqrt(jnp.mean(x * x, axis=-1, keepdims=True) + EPS)
        h_ref[rows, :] = (x * rs * gain + shift).astype(h_ref.dtype)
        return carry

    lax.fori_loop(0, x_ref.shape[0] // NORM_CHUNK_ROWS, body, 0, unroll=NORM_UNROLL)


def _ada_kernel(c_ref, w_ref, b_ref, o_ref):
    s = _silu(c_ref[...]).astype(BF16)
    o_ref[0] = jnp.dot(s, w_ref[0].astype(BF16), preferred_element_type=F32) + b_ref[0]


def _ada(cond, ada_w, ada_b):
    depth, d, n = ada_w.shape
    tn = 1536
    return pl.pallas_call(
        _ada_kernel,
        grid=(depth, n // tn),
        in_specs=[
            pl.BlockSpec((cond.shape[0], d), lambda l, j: (0, 0)),
            pl.BlockSpec((1, d, tn), lambda l, j: (l, 0, j)),
            pl.BlockSpec((1, 1, tn), lambda l, j: (l, 0, j)),
        ],
        out_specs=pl.BlockSpec((1, cond.shape[0], tn), lambda l, j: (l, 0, j)),
        out_shape=jax.ShapeDtypeStruct((depth, cond.shape[0], n), F32),
        compiler_params=_cparams(("arbitrary", "arbitrary")),
        name="ada",
    )(cond, ada_w, ada_b.reshape(depth, 1, n))


def _inproj_kernel(x_ref, nw_ref, sh_ref, sc_ref, w_ref, o_ref, h_ref):
    @pl.when(pl.program_id(1) == 0)
    def _():
        _norm_modulate_store(x_ref, nw_ref, sh_ref, sc_ref, h_ref)

    o_ref[...] = jnp.dot(h_ref[...], w_ref[...], preferred_element_type=F32)


def _inproj(x2d, nw, mod, w, layer, rows_per_mod, mod_row0, tm):
    m, d = x2d.shape
    n = w.shape[2]
    tn = INPROJ_TN
    mrow = lambda i: layer * MOD_ROWS + mod_row0 + (i * tm) // rows_per_mod
    return pl.pallas_call(
        _inproj_kernel,
        grid=(m // tm, n // tn),
        in_specs=[
            pl.BlockSpec((tm, d), lambda i, j: (i, 0)),
            pl.BlockSpec((None, 1, d), lambda i, j: (layer, 0, 0)),
            pl.BlockSpec((1, 1, d), lambda i, j: (mrow(i), 0, 0)),
            pl.BlockSpec((1, 1, d), lambda i, j: (mrow(i), 0, 1)),
            pl.BlockSpec((None, d, tn), lambda i, j: (layer, 0, j)),
        ],
        out_specs=pl.BlockSpec((tm, tn), lambda i, j: (i, j)),
        out_shape=jax.ShapeDtypeStruct((m, n), F32),
        scratch_shapes=[pltpu.VMEM((tm, d), BF16)],
        compiler_params=_cparams(("arbitrary", "arbitrary")),
        name="inproj",
    )(x2d, nw, mod, mod, w)


NA_GROUP_ROWS = 4
NA_GROUP_TOK = NA_GROUP_ROWS * GRID_W
NA_WIN_BLOCKS = 3
NA_GROUPS_PER_STEP = 4


def _na_bias_table(rpb, rows):
    n_groups = rows // NA_GROUP_ROWS
    h = rpb.shape[0]
    kcol = np.arange(GRID_W)[:, None]
    qcol = np.arange(GRID_W)[None, :]
    wstart = np.clip(qcol - NA_WIN_COLS // 2, 0, GRID_W - NA_WIN_COLS)
    valid_col = (kcol >= wstart) & (kcol < wstart + NA_WIN_COLS)
    dcol = np.clip(kcol - qcol, 1 - NA_WIN_COLS, NA_WIN_COLS - 1) + NA_WIN_COLS - 1
    slabs = jnp.where(valid_col, rpb.astype(F32)[:, :, dcol], -1e30)
    masked = 2 * NA_WIN_ROWS - 1
    slabs = jnp.concatenate([slabs, jnp.full((h, 1, GRID_W, GRID_W), -1e30, F32)], axis=1)
    krel = np.arange(NA_WIN_BLOCKS * NA_GROUP_ROWS)[:, None]
    qrel = np.arange(NA_GROUP_ROWS)[None, :]
    idx = []
    for g in (0, 1, n_groups - 1):
        krow = int(np.clip(g - 1, 0, n_groups - NA_WIN_BLOCKS)) * NA_GROUP_ROWS + krel
        qrow = g * NA_GROUP_ROWS + qrel
        r0 = np.clip(qrow - NA_WIN_ROWS // 2, 0, rows - NA_WIN_ROWS)
        in_window = (krow >= r0) & (krow < r0 + NA_WIN_ROWS)
        idx.append(np.where(in_window, krow - qrow + NA_WIN_ROWS - 1, masked))
    table = slabs[:, np.stack(idx)]
    table = table.transpose(0, 1, 2, 4, 3, 5)
    return table.reshape(h, 3, NA_WIN_BLOCKS * NA_GROUP_TOK, NA_GROUP_TOK)


def _na_kernel(q_ref, k_ref, v_ref, kc_ref, vc_ref, bias_ref, o_ref, qb_ref, kb_ref, vt_ref):
    n_groups = q_ref.shape[0] // NA_GROUP_TOK
    gt = NA_GROUP_TOK
    scale = NA_HEAD_DIM ** -0.5
    nt = (((1,), (1,)), ((), ()))
    dot = functools.partial(jnp.dot, preferred_element_type=F32)
    qb_ref[...] = q_ref[...].astype(BF16)
    kb_ref[...] = k_ref[...].astype(BF16)
    for b in range(n_groups):
        vt_ref[b] = v_ref[b * gt:(b + 1) * gt, :].T.astype(BF16)
    kc = kc_ref[...].astype(BF16)
    vct = vc_ref[...].T.astype(BF16)

    def group(g):
        blk = jnp.clip(g - 1, 0, n_groups - NA_WIN_BLOCKS)
        kind = jnp.where(g == 0, 0, jnp.where(g == n_groups - 1, 2, 1))
        qg = qb_ref[pl.ds(pl.multiple_of(g * gt, gt), gt), :]
        kw = kb_ref[pl.ds(pl.multiple_of(blk * gt, gt), NA_WIN_BLOCKS * gt), :]
        s_lat = lax.dot_general(kw, qg, nt, preferred_element_type=F32) * scale + bias_ref[0, kind]
        s_ctx = lax.dot_general(kc, qg, nt, preferred_element_type=F32) * scale
        yield
        m = jnp.maximum(jnp.max(s_lat, axis=0, keepdims=True), jnp.max(s_ctx, axis=0, keepdims=True))
        p_lat = jnp.exp(s_lat - m)
        p_ctx = jnp.exp(s_ctx - m)
        l = jnp.sum(p_lat, axis=0, keepdims=True) + jnp.sum(p_ctx, axis=0, keepdims=True)
        yield
        ot = dot(vct, p_ctx.astype(BF16))
        for j in range(NA_WIN_BLOCKS):
            ot = ot + dot(vt_ref[blk + j], p_lat[j * gt:(j + 1) * gt, :].astype(BF16))
        yield
        o_ref[pl.ds(pl.multiple_of(g * gt, gt), gt), :] = (ot / l).T.astype(o_ref.dtype)
        yield

    def body(i, carry):
        groups = [group(i * NA_GROUPS_PER_STEP + u) for u in range(NA_GROUPS_PER_STEP)]
        for _ in range(4):
            for grp in groups:
                next(grp)
        return carry

    lax.fori_loop(0, n_groups // NA_GROUPS_PER_STEP, body, 0)


def _na_latent(p_l, p_c, bias, b):
    t = p_l.shape[0] // b
    ctx = p_c.shape[0] // b
    hd = NA_HEAD_DIM
    return pl.pallas_call(
        _na_kernel,
        grid=(NA_HEADS, b),
        in_specs=[
            pl.BlockSpec((t, hd), lambda h, i: (i, _QA + h)),
            pl.BlockSpec((t, hd), lambda h, i: (i, _KA + h)),
            pl.BlockSpec((t, hd), lambda h, i: (i, _VA + h)),
            pl.BlockSpec((ctx, hd), lambda h, i: (i, _KA + h)),
            pl.BlockSpec((ctx, hd), lambda h, i: (i, _VA + h)),
            pl.BlockSpec((1,) + bias.shape[1:], lambda h, i: (h, 0, 0, 0)),
        ],
        out_specs=pl.BlockSpec((t, hd), lambda h, i: (i, h)),
        out_shape=jax.ShapeDtypeStruct((b * t, NA_WIDTH), BF16),
        scratch_shapes=[
            pltpu.VMEM((t, hd), BF16),
            pltpu.VMEM((t, hd), BF16),
            pltpu.VMEM((t // NA_GROUP_TOK, hd, NA_GROUP_TOK), BF16),
        ],
        compiler_params=_cparams(("arbitrary", "arbitrary")),
        name="na_latent",
    )(p_l, p_l, p_l, p_c, p_c, bias)


def _ctx_attn_kernel(q_ref, k_ref, v_ref, o_ref):
    scale = NA_HEAD_DIM ** -0.5
    q = q_ref[...].astype(BF16)
    k = k_ref[...].astype(BF16)
    v = v_ref[...].astype(BF16)
    s = lax.dot_general(q, k, (((1,), (1,)), ((), ())), preferred_element_type=F32) * scale
    p = jnp.exp(s - jnp.max(s, axis=-1, keepdims=True))
    l = jnp.sum(p, axis=-1, keepdims=True)
    o = jnp.dot(p.astype(BF16), v, preferred_element_type=F32)
    o_ref[...] = (o / l).astype(o_ref.dtype)


def _ctx_attention(p_c, b):
    ctx = p_c.shape[0] // b
    hd = NA_HEAD_DIM
    return pl.pallas_call(
        _ctx_attn_kernel,
        grid=(b, NA_HEADS),
        in_specs=[
            pl.BlockSpec((ctx, hd), lambda i, h: (i, _QA + h)),
            pl.BlockSpec((ctx, hd), lambda i, h: (i, _KA + h)),
            pl.BlockSpec((ctx, hd), lambda i, h: (i, _VA + h)),
        ],
        out_specs=pl.BlockSpec((ctx, hd), lambda i, h: (i, h)),
        out_shape=jax.ShapeDtypeStruct((b * ctx, NA_WIDTH), BF16),
        compiler_params=_cparams(("arbitrary", "arbitrary")),
        name="ctx_attention",
    )(p_c, p_c, p_c)


def _gm_kernel(u_ref, v_ref, lnw_ref, ws_ref, bs_ref, o_ref):
    uf = _gelu_tanh(u_ref[...])
    vf = _gelu_tanh(v_ref[...])
    for g in range(GM_GROUPS):
        sl = slice(g * GM_DIM, (g + 1) * GM_DIM)
        vg = vf[:, sl]
        mu = jnp.mean(vg, axis=-1, keepdims=True)
        dv = vg - mu
        var = jnp.mean(dv * dv, axis=-1, keepdims=True)
        vn = dv * lax.rsqrt(var + EPS) * lnw_ref[:, sl]
        mixed = jnp.dot(ws_ref[g].astype(BF16), vn.astype(BF16), preferred_element_type=F32) + bs_ref[g]
        o_ref[:, sl] = (uf[:, sl] * mixed).astype(o_ref.dtype)


def _chunk_gmlp(p2, ln_w, ws, bs):
    m = p2.shape[0]
    return pl.pallas_call(
        _gm_kernel,
        grid=(m // GM_CHUNK,),
        in_specs=[
            pl.BlockSpec((GM_CHUNK, GM_WIDTH), lambda c: (c, _GM_U)),
            pl.BlockSpec((GM_CHUNK, GM_WIDTH), lambda c: (c, _GM_V)),
            pl.BlockSpec((1, GM_WIDTH), lambda c: (0, 0)),
            pl.BlockSpec((GM_GROUPS, GM_CHUNK, GM_CHUNK), lambda c: (0, 0, 0)),
            pl.BlockSpec((GM_GROUPS, GM_CHUNK, 1), lambda c: (0, 0, 0)),
        ],
        out_specs=pl.BlockSpec((GM_CHUNK, GM_WIDTH), lambda c: (c, 0)),
        out_shape=jax.ShapeDtypeStruct((m, GM_WIDTH), BF16),
        compiler_params=_cparams(("arbitrary",)),
        name="chunk_gmlp",
    )(p2, p2, ln_w.reshape(1, GM_WIDTH), ws, bs.reshape(GM_GROUPS, GM_CHUNK, 1))


HG_PAIR = 2
HG_LEVELS = (32, 16, 8)
HG_DIAG = 8
HG_PREP_ROWS = 256


def _split3_dot(tri, g):
    g0 = g.astype(BF16)
    r1 = g - g0.astype(F32)
    g1 = r1.astype(BF16)
    g2 = (r1 - g1.astype(F32)).astype(BF16)
    dot = functools.partial(jnp.dot, preferred_element_type=F32)
    return dot(tri, g0) + dot(tri, g1) + dot(tri, g2)


def _hg_kernel(ql_ref, ffl_ref, fbl_ref, il_ref, gl_ref,
               qc_ref, ffc_ref, fbc_ref, ic_ref, gc_ref,
               lbp_ref, nw_ref, tri_ref, code_ref, lsum_ref, ol_ref, oc_ref,
               qh_s, kf_s, kb_s, ef_s, eb_s, of_s, ob_s, st_s):
    c = HG_CHUNK
    hd = HG_DIM
    n_ctx = qc_ref.shape[0]
    n_lat = ql_ref.shape[0]
    scale = hd ** -0.5
    nt = (((1,), (1,)), ((), ()))
    tn = (((0,), (0,)), ((), ()))

    def prep(q_ref, ff_ref, fb_ref, base, n):
        step = HG_PREP_ROWS
        for t0 in range(0, n, step):
            src = slice(t0, t0 + step)
            dst = slice(base + t0, base + t0 + step)
            qh = _silu(q_ref[src, :]) * scale
            for h in range(HG_PAIR):
                qh_s[h, dst, :] = qh[:, h * hd:(h + 1) * hd]
            for d, (f_ref, k_s, e_s) in enumerate(((ff_ref, kf_s, ef_s), (fb_ref, kb_s, eb_s))):
                x = f_ref[src, :]
                sp = jnp.maximum(-x, 0.0) + jnp.log(1.0 + jnp.exp(-jnp.abs(x)))
                log_lb = lbp_ref[d, 0:1, :]
                y = lbp_ref[d, 1:2, :] - sp
                mx = jnp.maximum(log_lb, y)
                log_f = mx + jnp.log(1.0 + jnp.exp(-jnp.abs(log_lb - y)))
                one_minus_f = lbp_ref[d, 2:3, :] * jnp.exp(-(sp + x))
                e = _split3_dot(tri_ref[d], log_f) * LOG2E
                for h in range(HG_PAIR):
                    e_s[h, dst, :] = e[:, h * hd:(h + 1) * hd]
                    k_s[h, dst, :] = one_minus_f[:, h * hd:(h + 1) * hd]

    prep(qc_ref, ffc_ref, fbc_ref, 0, n_ctx)
    prep(ql_ref, ffl_ref, fbl_ref, n_ctx, n_lat)

    diag_code = len(HG_LEVELS) + 1

    st_s[...] = jnp.zeros_like(st_s)

    def chunk(off, v, h, d, out):
        fwd = d == 0
        k_s, e_s = (kf_s, ef_s) if fwd else (kb_s, eb_s)
        q = qh_s[h, pl.ds(off, c), :]
        k = k_s[h, pl.ds(off, c), :]
        e = e_s[h, pl.ds(off, c), :]
        e_end = e_s[h, pl.ds(off + (c - 1 if fwd else 0), 1), :]
        st = st_s[h, d]
        qi = (q * jnp.exp2(e)).astype(BF16)
        ki = (k * jnp.exp2(e_end - e)).astype(BF16)
        o = lax.dot_general(qi, st.astype(BF16), nt, preferred_element_type=F32)
        yield
        st_new = st * jnp.exp2(e_end) + lax.dot_general(v, ki, tn, preferred_element_type=F32)
        yield
        code = code_ref[d]
        att = None
        for li, w in enumerate(HG_LEVELS):
            zeros = jnp.zeros((w, hd), F32)
            q_parts, k_parts = [], []
            for a in range(0, c, 2 * w):
                lo, hi = slice(a, a + w), slice(a + w, a + 2 * w)
                if fwd:
                    ref = e_s[h, pl.ds(off + a + w - 1, 1), :]
                    q_parts += [zeros, q[hi] * jnp.exp2(e[hi] - ref)]
                    k_parts += [k[lo] * jnp.exp2(ref - e[lo]), zeros]
                else:
                    ref = e_s[h, pl.ds(off + a + w, 1), :]
                    q_parts += [q[lo] * jnp.exp2(e[lo] - ref), zeros]
                    k_parts += [zeros, k[hi] * jnp.exp2(ref - e[hi])]
            qs = jnp.concatenate(q_parts, axis=0).astype(BF16)
            ks = jnp.concatenate(k_parts, axis=0).astype(BF16)
            a_w = lax.dot_general(qs, ks, nt, preferred_element_type=F32)
            att = a_w if att is None else jnp.where(code == li + 1, a_w, att)
            yield
        prods = []
        for i in range(c // HG_DIAG):
            bs = slice(i * HG_DIAG, (i + 1) * HG_DIAG)
            row_prods = []
            for s in range(HG_DIAG):
                kr = k_s[h, pl.ds(off + i * HG_DIAG + s, 1), :]
                er = e_s[h, pl.ds(off + i * HG_DIAG + s, 1), :]
                decay = jnp.exp2(jnp.minimum(e[bs] - er, 0.0))
                row_prods.append((q[bs] * kr * decay).astype(BF16))
            prods.append(jnp.concatenate(row_prods, axis=1))
        diag = jnp.dot(jnp.concatenate(prods, axis=0), lsum_ref[...], preferred_element_type=F32)
        yield
        att = jnp.where(code == diag_code, diag, att)
        out.append((o + jnp.dot(att.astype(BF16), v, preferred_element_type=F32), st_new))
        yield

    n_stages = len(HG_LEVELS) + 4

    def scan(v_ref, base, n):
        def body(i, carry):
            rf = pl.multiple_of(i * c, c)
            rb = pl.multiple_of((n - 1 - i) * c, c)
            vf = v_ref[pl.ds(rf, c), :].astype(BF16)
            vb = v_ref[pl.ds(rb, c), :].astype(BF16)
            chains = []
            for h in range(HG_PAIR):
                hs = slice(h * hd, (h + 1) * hd)
                for d, r, v in ((0, rf, vf), (1, rb, vb)):
                    out = []
                    chains.append((h, d, r, out, chunk(base + r, v[:, hs], h, d, out)))
            for _ in range(n_stages):
                for chain in chains:
                    next(chain[-1])
            for h, d, r, out, _ in chains:
                o, st_new = out[0]
                (of_s if d == 0 else ob_s)[h, pl.ds(base + r, c), :] = o
                st_s[h, d] = st_new
            return carry

        lax.fori_loop(0, n, body, 0)

    scan(ic_ref, 0, n_ctx // c)
    scan(il_ref, n_ctx, n_lat // c)

    def finish(g_ref, o_ref, base, n):
        step = 256
        for t0 in range(0, n, step):
            src = slice(t0, t0 + step)
            dst = slice(base + t0, base + t0 + step)
            for h in range(HG_PAIR):
                hs = slice(h * hd, (h + 1) * hd)
                o = of_s[h, dst, :] + ob_s[h, dst, :]
                o = o * lax.rsqrt(jnp.mean(o * o, axis=-1, keepdims=True) + EPS) * nw_ref[...]
                o_ref[src, hs] = (o * _silu(g_ref[src, hs])).astype(o_ref.dtype)

    finish(gc_ref, oc_ref, 0, n_ctx)
    finish(gl_ref, ol_ref, n_ctx, n_lat)


def _hg_constants():
    c = HG_CHUNK
    row, col = np.arange(c)[:, None], np.arange(c)[None, :]
    blocks = np.eye(HG_PREP_ROWS // c)
    tri = np.stack([np.kron(blocks, col <= row), np.kron(blocks, col >= row)]).astype(np.float32)
    same8 = (row // HG_DIAG) == (col // HG_DIAG)
    diag_code = len(HG_LEVELS) + 1
    code_f = np.where(same8 & (col <= row), diag_code, 0)
    code_b = np.where(same8 & (col >= row), diag_code, 0)
    for li, w in enumerate(HG_LEVELS):
        same = (row // (2 * w)) == (col // (2 * w))
        t_hi, s_hi = (row % (2 * w)) >= w, (col % (2 * w)) >= w
        code_f = np.where(same & t_hi & ~s_hi, li + 1, code_f)
        code_b = np.where(same & ~t_hi & s_hi, li + 1, code_b)
    codes = np.stack([code_f, code_b]).astype(np.int32)
    lane_sum = (np.arange(HG_DIAG * HG_DIM)[:, None] // HG_DIM == col % HG_DIAG).astype(np.float32)
    return jnp.asarray(tri, BF16), jnp.asarray(codes), jnp.asarray(lane_sum, BF16)


def _hgrn2(p_l, p_c, lbp, norm_w, b):
    tri, codes, lane_sum = _hg_constants()
    t = p_l.shape[0] // b
    ctx = p_c.shape[0] // b
    hd = HG_DIM
    pw = HG_PAIR * hd
    n = t + ctx
    first = _HG0 * LANE // pw

    def col(stream):
        return lambda i, j: (i, first + stream * (HG_HEADS // HG_PAIR) + j)

    lat_specs = [pl.BlockSpec((t, pw), col(s)) for s in range(5)]
    ctx_specs = [pl.BlockSpec((ctx, pw), col(s)) for s in range(5)]
    big = lambda: pltpu.VMEM((HG_PAIR, n, hd), F32)
    return pl.pallas_call(
        _hg_kernel,
        grid=(b, HG_HEADS // HG_PAIR),
        in_specs=lat_specs + ctx_specs + [
            pl.BlockSpec((2, 3, pw), lambda i, j: (0, 0, j)),
            pl.BlockSpec((1, hd), lambda i, j: (0, 0)),
            pl.BlockSpec(tri.shape, lambda i, j: (0, 0, 0)),
            pl.BlockSpec(codes.shape, lambda i, j: (0, 0, 0)),
            pl.BlockSpec(lane_sum.shape, lambda i, j: (0, 0)),
        ],
        out_specs=[
            pl.BlockSpec((t, pw), lambda i, j: (i, j)),
            pl.BlockSpec((ctx, pw), lambda i, j: (i, j)),
        ],
        out_shape=[
            jax.ShapeDtypeStruct((b * t, HG_WIDTH), BF16),
            jax.ShapeDtypeStruct((b * ctx, HG_WIDTH), BF16),
        ],
        scratch_shapes=[big() for _ in range(7)] + [pltpu.VMEM((HG_PAIR, 2, hd, hd), F32)],
        compiler_params=_cparams(("arbitrary", "arbitrary")),
        name="hgrn2",
    )(*([p_l] * 5 + [p_c] * 5 + [lbp, norm_w.reshape(1, hd), tri, codes, lane_sum]))


def _outproj_kernel(oa_ref, ob_ref, oc_ref, wa_ref, wb_ref, wc_ref, x_ref, g_ref, o_ref):
    dot = functools.partial(jnp.dot, preferred_element_type=F32)
    y = dot(oa_ref[...], wa_ref[...]) + dot(ob_ref[...], wb_ref[...]) + dot(oc_ref[...], wc_ref[...])
    o_ref[...] = x_ref[...] + g_ref[0] * y


def _outproj(oa, ob, oc, w, x2d, mod, layer, rows_per_mod, mod_row0, tm):
    m, d = x2d.shape
    tn = OUTPROJ_TN
    mrow = lambda i: layer * MOD_ROWS + mod_row0 + (i * tm) // rows_per_mod
    nb = NA_WIDTH // HG_WIDTH
    return pl.pallas_call(
        _outproj_kernel,
        grid=(m // tm, d // tn),
        in_specs=[
            pl.BlockSpec((tm, NA_WIDTH), lambda i, j: (i, 0)),
            pl.BlockSpec((tm, HG_WIDTH), lambda i, j: (i, 0)),
            pl.BlockSpec((tm, GM_WIDTH), lambda i, j: (i, 0)),
            pl.BlockSpec((None, NA_WIDTH, tn), lambda i, j: (layer, 0, j)),
            pl.BlockSpec((None, HG_WIDTH, tn), lambda i, j: (layer, nb, j)),
            pl.BlockSpec((None, GM_WIDTH, tn), lambda i, j: (layer, nb + 1, j)),
            pl.BlockSpec((tm, tn), lambda i, j: (i, j)),
            pl.BlockSpec((1, 1, tn), lambda i, j: (mrow(i), 0, 2 * (d // tn) + j)),
        ],
        out_specs=pl.BlockSpec((tm, tn), lambda i, j: (i, j)),
        out_shape=jax.ShapeDtypeStruct((m, d), F32),
        compiler_params=_cparams(("arbitrary", "arbitrary")),
        name="outproj",
    )(oa, ob, oc, w, w, w, x2d, mod)


def _mlp_kernel(x_ref, nw_ref, sh_ref, sc_ref, g_ref, w1_ref, w2_ref, fnw_ref, o_ref, h_ref, acc_ref,
                *, final_norm):
    j = pl.program_id(1)

    @pl.when(j == 0)
    def _():
        _norm_modulate_store(x_ref, nw_ref, sh_ref, sc_ref, h_ref)
        acc_ref[...] = jnp.zeros_like(acc_ref)

    a = jnp.maximum(jnp.dot(h_ref[...], w1_ref[...], preferred_element_type=F32), 0.0)
    acc_ref[...] += jnp.dot((a * a).astype(BF16), w2_ref[...], preferred_element_type=F32)

    @pl.when(j == pl.num_programs(1) - 1)
    def _():
        y = x_ref[...] + g_ref[0] * acc_ref[...]
        if final_norm:
            y = y * lax.rsqrt(jnp.mean(y * y, axis=-1, keepdims=True) + EPS) * fnw_ref[...]
        o_ref[...] = y


def _mlp(x2d, nw, mod, w1, w2, fnw, layer, rows_per_mod, mod_row0, tm, final_norm):
    m, d = x2d.shape
    hid = w1.shape[2]
    th = MLP_TH
    mrow = lambda i: layer * MOD_ROWS + mod_row0 + (i * tm) // rows_per_mod
    return pl.pallas_call(
        functools.partial(_mlp_kernel, final_norm=final_norm),
        grid=(m // tm, hid // th),
        in_specs=[
            pl.BlockSpec((tm, d), lambda i, j: (i, 0)),
            pl.BlockSpec((None, 1, d), lambda i, j: (layer, 0, 0)),
            pl.BlockSpec((1, 1, d), lambda i, j: (mrow(i), 0, 3)),
            pl.BlockSpec((1, 1, d), lambda i, j: (mrow(i), 0, 4)),
            pl.BlockSpec((1, 1, d), lambda i, j: (mrow(i), 0, 5)),
            pl.BlockSpec((None, d, th), lambda i, j: (layer, 0, j)),
            pl.BlockSpec((None, th, d), lambda i, j: (layer, j, 0)),
            pl.BlockSpec((1, d), lambda i, j: (0, 0)),
        ],
        out_specs=pl.BlockSpec((tm, d), lambda i, j: (i, 0)),
        out_shape=jax.ShapeDtypeStruct((m, d), F32),
        scratch_shapes=[pltpu.VMEM((tm, d), BF16), pltpu.VMEM((tm, d), F32)],
        compiler_params=_cparams(("arbitrary", "arbitrary")),
        name="mlp",
    )(x2d, nw, mod, mod, mod, w1, w2, fnw)


def kernel(x, c, ctx, c_ctx, ada_w, ada_b, norm1_w, norm2_w, w_in, na_rpb, hg_lb_logits, hg_norm_w,
           gm_ln_w, gm_ws, gm_bs, w_out, mlp_w1, mlp_w2, final_norm_w):
    bsz, seq, d = x.shape
    n_ctx = ctx.shape[1]
    depth = ada_w.shape[0]
    assert bsz < MOD_ROWS and d == D_MODEL and seq % 512 == 0 and n_ctx % 256 == 0
    assert seq // NA_GROUP_TOK >= NA_WIN_BLOCKS and seq // GRID_W >= 2 * NA_WIN_ROWS

    lb = jnp.cumsum(jax.nn.softmax(hg_lb_logits.astype(F32), axis=0), axis=0)
    lb = lb - lb[:1]
    lbp = jnp.stack([jnp.log(lb), jnp.log1p(-lb), 1.0 - lb], axis=2)

    cond = jnp.zeros((MOD_ROWS, d), F32).at[:bsz].set(c).at[bsz].set(c_ctx)
    mod = _ada(cond, ada_w, ada_b).reshape(depth * MOD_ROWS, 1, 6 * d)

    w_in_b, w_out_b = w_in.astype(BF16), w_out.astype(BF16)
    w1_b, w2_b = mlp_w1.astype(BF16), mlp_w2.astype(BF16)
    nw1 = norm1_w.reshape(depth, 1, d)
    nw2 = norm2_w.reshape(depth, 1, d)
    fnw = final_norm_w.reshape(1, d)

    xl = x.reshape(bsz * seq, d)
    xc = ctx.reshape(bsz * n_ctx, d)
    n_c = bsz * n_ctx
    tm_l, tm_c = 512, 256
    for l in range(depth):
        need_ctx = l < depth - 1
        p_l = _inproj(xl, nw1, mod, w_in_b, l, seq, 0, tm_l)
        p_c = _inproj(xc, nw1, mod, w_in_b, l, n_c, bsz, tm_c)

        oa_l = _na_latent(p_l, p_c, _na_bias_table(na_rpb[l], seq // GRID_W), bsz)
        ob_l, ob_c = _hgrn2(p_l, p_c, lbp[l], hg_norm_w[l], bsz)
        oc_l = _chunk_gmlp(p_l, gm_ln_w[l], gm_ws[l], gm_bs[l])
        xl = _outproj(oa_l, ob_l, oc_l, w_out_b, xl, mod, l, seq, 0, tm_l)
        xl = _mlp(xl, nw2, mod, w1_b, w2_b, fnw, l, seq, 0, tm_l, final_norm=not need_ctx)
        if need_ctx:
            oa_c = _ctx_attention(p_c, bsz)
            oc_c = _chunk_gmlp(p_c, gm_ln_w[l], gm_ws[l], gm_bs[l])
            xc = _outproj(oa_c, ob_c, oc_c, w_out_b, xc, mod, l, n_c, bsz, tm_c)
            xc = _mlp(xc, nw2, mod, w1_b, w2_b, fnw, l, n_c, bsz, tm_c, final_norm=False)
    return xl.reshape(bsz, seq, d)
```

```python
import functools

import numpy as np
import jax
import jax.numpy as jnp
from jax import lax
from jax.experimental import pallas as pl
from jax.experimental.pallas import tpu as pltpu

F32 = jnp.float32
BF16 = jnp.bfloat16

D_MODEL = 2048
DEPTH = 2
GRID_W = 64
EPS = 1e-6

NA_HEAD_DIM = 128
NA_HEADS = 8
NA_WIDTH = NA_HEADS * NA_HEAD_DIM
NA_WIN_ROWS = 8
NA_WIN_COLS = 16

HG_HEADS = 4
HG_DIM = 128
HG_WIDTH = HG_HEADS * HG_DIM
HG_CHUNK = 64

GM_GROUPS = 4
GM_DIM = 128
GM_WIDTH = GM_GROUPS * GM_DIM
GM_CHUNK = 128
GM_BLOCK = 4 * GM_CHUNK

IN_WIDTH = 3 * NA_WIDTH + 5 * HG_WIDTH + 2 * GM_WIDTH
MLP_HIDDEN = 4 * D_MODEL
LANE = 128
LOG2E = 1.4426950408889634

_QA, _KA, _VA = 0, NA_HEADS, 2 * NA_HEADS
_HG0 = 3 * NA_HEADS
_GM_U = (3 * NA_WIDTH + 5 * HG_WIDTH) // GM_WIDTH
_GM_V = _GM_U + 1

VMEM_LIMIT_V7X = 56 * 1024 * 1024
TM_LATENT = 512
TM_CONTEXT = 512
TM_INPROJ = 1024
INPROJ_TN = 1664
MLP_TH = 1024
OUTPROJ_TN = 2048
NORM_CHUNK_ROWS = 32
NORM_UNROLL = 4
MOD_ROWS = 16


def _cparams(sem):
    return pltpu.CompilerParams(dimension_semantics=sem, vmem_limit_bytes=VMEM_LIMIT_V7X)


def _silu(x):
    return x / (1.0 + jnp.exp(-x))


def _gelu_tanh(x):
    return 0.5 * x * (1.0 + jnp.tanh(0.7978845608028654 * (x + 0.044715 * (x * x * x))))


def _norm_modulate_store(x_ref, nw_ref, sh_ref, sc_ref, h_ref):
    gain = nw_ref[...] * (1.0 + sc_ref[0])
    shift = sh_ref[0]

    def body(i, carry):
        rows = pl.ds(pl.multiple_of(i * NORM_CHUNK_ROWS, NORM_CHUNK_ROWS), NORM_CHUNK_ROWS)
        x = x_ref[rows, :]
        rs = lax.rsqrt(jnp.mean(x * x, axis=-1, keepdims=True) + EPS)
        h_ref[rows, :] = (x * rs * gain + shift).astype(h_ref.dtype)
        return carry

    lax.fori_loop(0, x_ref.shape[0] // NORM_CHUNK_ROWS, body, 0, unroll=NORM_UNROLL)


def _ada_kernel(c_ref, w_ref, b_ref, o_ref):
    s = _silu(c_ref[...]).astype(BF16)
    o_ref[0] = jnp.dot(s, w_ref[0].astype(BF16), preferred_element_type=F32) + b_ref[0]


def _ada(cond, ada_w, ada_b):
    depth, d, n = ada_w.shape
    tn = 1536
    return pl.pallas_call(
        _ada_kernel,
        grid=(depth, n // tn),
        in_specs=[
            pl.BlockSpec((cond.shape[0], d), lambda l, j: (0, 0)),
            pl.BlockSpec((1, d, tn), lambda l, j: (l, 0, j)),
            pl.BlockSpec((1, 1, tn), lambda l, j: (l, 0, j)),
        ],
        out_specs=pl.BlockSpec((1, cond.shape[0], tn), lambda l, j: (l, 0, j)),
        out_shape=jax.ShapeDtypeStruct((depth, cond.shape[0], n), F32),
        compiler_params=_cparams(("arbitrary", "arbitrary")),
        name="ada",
    )(cond, ada_w, ada_b.reshape(depth, 1, n))


def _inproj_kernel(x_ref, nw_ref, sh_ref, sc_ref, w_ref, o_ref, h_ref):
    @pl.when(pl.program_id(1) == 0)
    def _():
        _norm_modulate_store(x_ref, nw_ref, sh_ref, sc_ref, h_ref)

    o_ref[...] = jnp.dot(h_ref[...], w_ref[...], preferred_element_type=F32)


def _inproj(x2d, nw, mod, w, layer, rows_per_mod, mod_row0, tm):
    m, d = x2d.shape
    n = w.shape[2]
    tn = INPROJ_TN
    mrow = lambda i: layer * MOD_ROWS + mod_row0 + (i * tm) // rows_per_mod
    return pl.pallas_call(
        _inproj_kernel,
        grid=(m // tm, n // tn),
        in_specs=[
            pl.BlockSpec((tm, d), lambda i, j: (i, 0)),
            pl.BlockSpec((None, 1, d), lambda i, j: (layer, 0, 0)),
            pl.BlockSpec((1, 1, d), lambda i, j: (mrow(i), 0, 0)),
            pl.BlockSpec((1, 1, d), lambda i, j: (mrow(i), 0, 1)),
            pl.BlockSpec((None, d, tn), lambda i, j: (layer, 0, j)),
        ],
        out_specs=pl.BlockSpec((tm, tn), lambda i, j: (i, j)),
        out_shape=jax.ShapeDtypeStruct((m, n), F32),
        scratch_shapes=[pltpu.VMEM((tm, d), BF16)],
        compiler_params=_cparams(("arbitrary", "arbitrary")),
        name="inproj",
    )(x2d, nw, mod, mod, w)


NA_GROUP_ROWS = 4
NA_GROUP_TOK = NA_GROUP_ROWS * GRID_W
NA_WIN_BLOCKS = 3
NA_GROUPS_PER_STEP = 4


NA_MASKED_SLAB = 2 * NA_WIN_ROWS - 1


def _na_bias_slabs(rpb):
    h = rpb.shape[0]
    kcol = np.arange(GRID_W)[:, None]
    qcol = np.arange(GRID_W)[None, :]
    wstart = np.clip(qcol - NA_WIN_COLS // 2, 0, GRID_W - NA_WIN_COLS)
    valid_col = (kcol >= wstart) & (kcol < wstart + NA_WIN_COLS)
    pad = GRID_W - NA_WIN_COLS
    padded = jnp.pad(rpb.astype(F32), ((0, 0), (0, 0), (pad, pad)))
    m = jnp.tile(padded, (1, 1, GRID_W + 1))[:, :, :GRID_W * 2 * GRID_W]
    m = m.reshape(h, rpb.shape[1], GRID_W, 2 * GRID_W)[..., :GRID_W]
    slabs = jnp.where(valid_col, m[..., ::-1], -1e30)
    slabs = jnp.concatenate([slabs, jnp.full((h, 1, GRID_W, GRID_W), -1e30, F32)], axis=1)
    return jnp.concatenate([slabs, slabs], axis=-1)


def _na_slab_index(rows):
    n_groups = rows // NA_GROUP_ROWS
    krel = np.arange(NA_WIN_BLOCKS * NA_GROUP_ROWS)[:, None]
    qrel = np.arange(NA_GROUP_ROWS)[None, :]
    idx = []
    for g in (0, 1, n_groups - 1):
        krow = int(np.clip(g - 1, 0, n_groups - NA_WIN_BLOCKS)) * NA_GROUP_ROWS + krel
        qrow = g * NA_GROUP_ROWS + qrel
        r0 = np.clip(qrow - NA_WIN_ROWS // 2, 0, rows - NA_WIN_ROWS)
        in_window = (krow >= r0) & (krow < r0 + NA_WIN_ROWS)
        idx.append(np.where(in_window, krow - qrow + NA_WIN_ROWS - 1, NA_MASKED_SLAB))
    return np.stack(idx).tolist()


def _na_kernel(q_ref, k_ref, v_ref, kc_ref, vc_ref, slab_ref, o_ref, qb_ref, kb_ref, vt_ref, bias_ref,
               *, slab_index):
    n_groups = q_ref.shape[0] // NA_GROUP_TOK
    gt = NA_GROUP_TOK
    scale = NA_HEAD_DIM ** -0.5
    nt = (((1,), (1,)), ((), ()))
    dot = functools.partial(jnp.dot, preferred_element_type=F32)

    @pl.when(pl.program_id(1) == 0)
    def _():
        left = lax.broadcasted_iota(jnp.int32, (GRID_W, 2 * GRID_W), 1) < GRID_W
        for kind, per_key_row in enumerate(slab_index):
            for j, per_query_row in enumerate(per_key_row):
                for a in range(0, NA_GROUP_ROWS, 2):
                    tile = jnp.where(left, slab_ref[0, per_query_row[a]], slab_ref[0, per_query_row[a + 1]])
                    bias_ref[kind, j * GRID_W:(j + 1) * GRID_W, a * GRID_W:(a + 2) * GRID_W] = tile

    qb_ref[...] = q_ref[...].astype(BF16)
    kb_ref[...] = k_ref[...].astype(BF16)
    for b in range(n_groups):
        vt_ref[b] = v_ref[b * gt:(b + 1) * gt, :].T.astype(BF16)
    kc = kc_ref[...].astype(BF16)
    vct = vc_ref[...].T.astype(BF16)

    def group(g):
        blk = jnp.clip(g - 1, 0, n_groups - NA_WIN_BLOCKS)
        kind = jnp.where(g == 0, 0, jnp.where(g == n_groups - 1, 2, 1))
        qg = qb_ref[pl.ds(pl.multiple_of(g * gt, gt), gt), :]
        kw = kb_ref[pl.ds(pl.multiple_of(blk * gt, gt), NA_WIN_BLOCKS * gt), :]
        s_lat = lax.dot_general(kw, qg, nt, preferred_element_type=F32) * scale + bias_ref[kind]
        s_ctx = lax.dot_general(kc, qg, nt, preferred_element_type=F32) * scale
        yield
        m = jnp.maximum(jnp.max(s_lat, axis=0, keepdims=True), jnp.max(s_ctx, axis=0, keepdims=True))
        p_lat = jnp.exp(s_lat - m)
        p_ctx = jnp.exp(s_ctx - m)
        l = jnp.sum(p_lat, axis=0, keepdims=True) + jnp.sum(p_ctx, axis=0, keepdims=True)
        yield
        ot = dot(vct, p_ctx.astype(BF16))
        for j in range(NA_WIN_BLOCKS):
            ot = ot + dot(vt_ref[blk + j], p_lat[j * gt:(j + 1) * gt, :].astype(BF16))
        yield
        o_ref[pl.ds(pl.multiple_of(g * gt, gt), gt), :] = (ot / l).T.astype(o_ref.dtype)
        yield

    def body(i, carry):
        groups = [group(i * NA_GROUPS_PER_STEP + u) for u in range(NA_GROUPS_PER_STEP)]
        for _ in range(4):
            for grp in groups:
                next(grp)
        return carry

    lax.fori_loop(0, n_groups // NA_GROUPS_PER_STEP, body, 0)


def _na_latent(p_l, p_c, slabs, b):
    t = p_l.shape[0] // b
    ctx = p_c.shape[0] // b
    hd = NA_HEAD_DIM
    return pl.pallas_call(
        functools.partial(_na_kernel, slab_index=_na_slab_index(t // GRID_W)),
        grid=(NA_HEADS, b),
        in_specs=[
            pl.BlockSpec((t, hd), lambda h, i: (i, _QA + h)),
            pl.BlockSpec((t, hd), lambda h, i: (i, _KA + h)),
            pl.BlockSpec((t, hd), lambda h, i: (i, _VA + h)),
            pl.BlockSpec((ctx, hd), lambda h, i: (i, _KA + h)),
            pl.BlockSpec((ctx, hd), lambda h, i: (i, _VA + h)),
            pl.BlockSpec((1,) + slabs.shape[1:], lambda h, i: (h, 0, 0, 0)),
        ],
        out_specs=pl.BlockSpec((t, hd), lambda h, i: (i, h)),
        out_shape=jax.ShapeDtypeStruct((b * t, NA_WIDTH), BF16),
        scratch_shapes=[
            pltpu.VMEM((t, hd), BF16),
            pltpu.VMEM((t, hd), BF16),
            pltpu.VMEM((t // NA_GROUP_TOK, hd, NA_GROUP_TOK), BF16),
            pltpu.VMEM((3, NA_WIN_BLOCKS * NA_GROUP_TOK, NA_GROUP_TOK), F32),
        ],
        compiler_params=_cparams(("arbitrary", "arbitrary")),
        name="na_latent",
    )(p_l, p_l, p_l, p_c, p_c, slabs)


def _ctx_attn_kernel(q_ref, k_ref, v_ref, o_ref):
    scale = NA_HEAD_DIM ** -0.5
    q = q_ref[...].astype(BF16)
    k = k_ref[...].astype(BF16)
    v = v_ref[...].astype(BF16)
    s = lax.dot_general(q, k, (((1,), (1,)), ((), ())), preferred_element_type=F32) * scale
    p = jnp.exp(s - jnp.max(s, axis=-1, keepdims=True))
    l = jnp.sum(p, axis=-1, keepdims=True)
    o = jnp.dot(p.astype(BF16), v, preferred_element_type=F32)
    o_ref[...] = (o / l).astype(o_ref.dtype)


def _ctx_attention(p_c, b):
    ctx = p_c.shape[0] // b
    hd = NA_HEAD_DIM
    return pl.pallas_call(
        _ctx_attn_kernel,
        grid=(b, NA_HEADS),
        in_specs=[
            pl.BlockSpec((ctx, hd), lambda i, h: (i, _QA + h)),
            pl.BlockSpec((ctx, hd), lambda i, h: (i, _KA + h)),
            pl.BlockSpec((ctx, hd), lambda i, h: (i, _VA + h)),
        ],
        out_specs=pl.BlockSpec((ctx, hd), lambda i, h: (i, h)),
        out_shape=jax.ShapeDtypeStruct((b * ctx, NA_WIDTH), BF16),
        compiler_params=_cparams(("arbitrary", "arbitrary")),
        name="ctx_attention",
    )(p_c, p_c, p_c)


def _gm_kernel(u_ref, v_ref, lnw_ref, ws_ref, bs_ref, o_ref):
    for ck in range(u_ref.shape[0] // GM_CHUNK):
        rows = slice(ck * GM_CHUNK, (ck + 1) * GM_CHUNK)
        uf = _gelu_tanh(u_ref[rows, :])
        vf = _gelu_tanh(v_ref[rows, :])
        for g in range(GM_GROUPS):
            sl = slice(g * GM_DIM, (g + 1) * GM_DIM)
            vg = vf[:, sl]
            mu = jnp.mean(vg, axis=-1, keepdims=True)
            dv = vg - mu
            var = jnp.mean(dv * dv, axis=-1, keepdims=True)
            vn = dv * lax.rsqrt(var + EPS) * lnw_ref[:, sl]
            mixed = jnp.dot(ws_ref[g].astype(BF16), vn.astype(BF16), preferred_element_type=F32) + bs_ref[g]
            o_ref[rows, sl] = (uf[:, sl] * mixed).astype(o_ref.dtype)


def _chunk_gmlp(p2, ln_w, ws, bs):
    m = p2.shape[0]
    blk = GM_BLOCK if m % GM_BLOCK == 0 else GM_CHUNK
    return pl.pallas_call(
        _gm_kernel,
        grid=(m // blk,),
        in_specs=[
            pl.BlockSpec((blk, GM_WIDTH), lambda c: (c, _GM_U)),
            pl.BlockSpec((blk, GM_WIDTH), lambda c: (c, _GM_V)),
            pl.BlockSpec((1, GM_WIDTH), lambda c: (0, 0)),
            pl.BlockSpec((GM_GROUPS, GM_CHUNK, GM_CHUNK), lambda c: (0, 0, 0)),
            pl.BlockSpec((GM_GROUPS, GM_CHUNK, 1), lambda c: (0, 0, 0)),
        ],
        out_specs=pl.BlockSpec((blk, GM_WIDTH), lambda c: (c, 0)),
        out_shape=jax.ShapeDtypeStruct((m, GM_WIDTH), BF16),
        compiler_params=_cparams(("arbitrary",)),
        name="chunk_gmlp",
    )(p2, p2, ln_w.reshape(1, GM_WIDTH), ws, bs.reshape(GM_GROUPS, GM_CHUNK, 1))


HG_PAIR = 2
HG_LEVELS = (32, 16, 8)
HG_DIAG = 8
HG_PREP_ROWS = 256


def _split3_dot(tri, g):
    g0 = g.astype(BF16)
    r1 = g - g0.astype(F32)
    g1 = r1.astype(BF16)
    g2 = (r1 - g1.astype(F32)).astype(BF16)
    dot = functools.partial(jnp.dot, preferred_element_type=F32)
    return dot(tri, g0) + dot(tri, g1) + dot(tri, g2)


def _hg_kernel(ql_ref, ffl_ref, fbl_ref, il_ref, gl_ref,
               qc_ref, ffc_ref, fbc_ref, ic_ref, gc_ref,
               lbp_ref, nw_ref, tri_ref, code_ref, lsum_ref, ol_ref, oc_ref,
               qh_s, kf_s, kb_s, ef_s, eb_s, of_s, ob_s, st_s):
    c = HG_CHUNK
    hd = HG_DIM
    n_ctx = qc_ref.shape[0]
    n_lat = ql_ref.shape[0]
    scale = hd ** -0.5
    nt = (((1,), (1,)), ((), ()))
    tn = (((0,), (0,)), ((), ()))

    def prep(q_ref, ff_ref, fb_ref, base, n):
        step = HG_PREP_ROWS
        for t0 in range(0, n, step):
            src = slice(t0, t0 + step)
            dst = slice(base + t0, base + t0 + step)
            qh = _silu(q_ref[src, :]) * scale
            for h in range(HG_PAIR):
                qh_s[h, dst, :] = qh[:, h * hd:(h + 1) * hd]
            for d, (f_ref, k_s, e_s) in enumerate(((ff_ref, kf_s, ef_s), (fb_ref, kb_s, eb_s))):
                x = f_ref[src, :]
                sp = jnp.maximum(-x, 0.0) + jnp.log(1.0 + jnp.exp(-jnp.abs(x)))
                log_lb = lbp_ref[d, 0:1, :]
                y = lbp_ref[d, 1:2, :] - sp
                mx = jnp.maximum(log_lb, y)
                log_f = mx + jnp.log(1.0 + jnp.exp(-jnp.abs(log_lb - y)))
                one_minus_f = lbp_ref[d, 2:3, :] * jnp.exp(-(sp + x))
                e = _split3_dot(tri_ref[d], log_f) * LOG2E
                for h in range(HG_PAIR):
                    e_s[h, dst, :] = e[:, h * hd:(h + 1) * hd]
                    k_s[h, dst, :] = one_minus_f[:, h * hd:(h + 1) * hd]

    prep(qc_ref, ffc_ref, fbc_ref, 0, n_ctx)
    prep(ql_ref, ffl_ref, fbl_ref, n_ctx, n_lat)

    diag_code = len(HG_LEVELS) + 1

    st_s[...] = jnp.zeros_like(st_s)

    def chunk(off, v, h, d, out):
        fwd = d == 0
        k_s, e_s = (kf_s, ef_s) if fwd else (kb_s, eb_s)
        q = qh_s[h, pl.ds(off, c), :]
        k = k_s[h, pl.ds(off, c), :]
        e = e_s[h, pl.ds(off, c), :]
        code = code_ref[d]
        prods = []
        for i in range(c // HG_DIAG):
            bs = slice(i * HG_DIAG, (i + 1) * HG_DIAG)
            row_prods = []
            for s in range(HG_DIAG):
                kr = k_s[h, pl.ds(off + i * HG_DIAG + s, 1), :]
                er = e_s[h, pl.ds(off + i * HG_DIAG + s, 1), :]
                decay = jnp.exp2(jnp.minimum(e[bs] - er, 0.0))
                row_prods.append((q[bs] * kr * decay).astype(BF16))
            prods.append(jnp.concatenate(row_prods, axis=1))
        diag = jnp.dot(jnp.concatenate(prods, axis=0), lsum_ref[...], preferred_element_type=F32)
        yield
        att = None
        for li, w in enumerate(HG_LEVELS):
            zeros = jnp.zeros((w, hd), F32)
            q_parts, k_parts = [], []
            for a in range(0, c, 2 * w):
                lo, hi = slice(a, a + w), slice(a + w, a + 2 * w)
                if fwd:
                    ref = e_s[h, pl.ds(off + a + w - 1, 1), :]
                    q_parts += [zeros, q[hi] * jnp.exp2(e[hi] - ref)]
                    k_parts += [k[lo] * jnp.exp2(ref - e[lo]), zeros]
                else:
                    ref = e_s[h, pl.ds(off + a + w, 1), :]
                    q_parts += [q[lo] * jnp.exp2(e[lo] - ref), zeros]
                    k_parts += [zeros, k[hi] * jnp.exp2(ref - e[hi])]
            qs = jnp.concatenate(q_parts, axis=0).astype(BF16)
            ks = jnp.concatenate(k_parts, axis=0).astype(BF16)
            a_w = lax.dot_general(qs, ks, nt, preferred_element_type=F32)
            att = a_w if att is None else jnp.where(code == li + 1, a_w, att)
            yield
        e_end = e_s[h, pl.ds(off + (c - 1 if fwd else 0), 1), :]
        st = st_s[h, d]
        qi = (q * jnp.exp2(e)).astype(BF16)
        ki = (k * jnp.exp2(e_end - e)).astype(BF16)
        o = lax.dot_general(qi, st.astype(BF16), nt, preferred_element_type=F32)
        yield
        st_new = st * jnp.exp2(e_end) + lax.dot_general(v, ki, tn, preferred_element_type=F32)
        yield
        att = jnp.where(code == diag_code, diag, att)
        out.append((o + jnp.dot(att.astype(BF16), v, preferred_element_type=F32), st_new))
        yield

    n_stages = len(HG_LEVELS) + 4

    def scan(v_ref, base, n):
        def body(i, carry):
            rf = pl.multiple_of(i * c, c)
            rb = pl.multiple_of((n - 1 - i) * c, c)
            vf = v_ref[pl.ds(rf, c), :].astype(BF16)
            vb = v_ref[pl.ds(rb, c), :].astype(BF16)
            chains = []
            for h in range(HG_PAIR):
                hs = slice(h * hd, (h + 1) * hd)
                for d, r, v in ((0, rf, vf), (1, rb, vb)):
                    out = []
                    chains.append((h, d, r, out, chunk(base + r, v[:, hs], h, d, out)))
            for _ in range(n_stages):
                for chain in chains:
                    next(chain[-1])
            for h, d, r, out, _ in chains:
                o, st_new = out[0]
                (of_s if d == 0 else ob_s)[h, pl.ds(base + r, c), :] = o
                st_s[h, d] = st_new
            return carry

        lax.fori_loop(0, n, body, 0)

    scan(ic_ref, 0, n_ctx // c)
    scan(il_ref, n_ctx, n_lat // c)

    def finish(g_ref, o_ref, base, n):
        step = 256
        for t0 in range(0, n, step):
            src = slice(t0, t0 + step)
            dst = slice(base + t0, base + t0 + step)
            for h in range(HG_PAIR):
                hs = slice(h * hd, (h + 1) * hd)
                o = of_s[h, dst, :] + ob_s[h, dst, :]
                o = o * lax.rsqrt(jnp.mean(o * o, axis=-1, keepdims=True) + EPS) * nw_ref[...]
                o_ref[src, hs] = (o * _silu(g_ref[src, hs])).astype(o_ref.dtype)

    finish(gc_ref, oc_ref, 0, n_ctx)
    finish(gl_ref, ol_ref, n_ctx, n_lat)


def _hg_constants():
    c = HG_CHUNK
    row, col = np.arange(c)[:, None], np.arange(c)[None, :]
    blocks = np.eye(HG_PREP_ROWS // c)
    tri = np.stack([np.kron(blocks, col <= row), np.kron(blocks, col >= row)]).astype(np.float32)
    same8 = (row // HG_DIAG) == (col // HG_DIAG)
    diag_code = len(HG_LEVELS) + 1
    code_f = np.where(same8 & (col <= row), diag_code, 0)
    code_b = np.where(same8 & (col >= row), diag_code, 0)
    for li, w in enumerate(HG_LEVELS):
        same = (row // (2 * w)) == (col // (2 * w))
        t_hi, s_hi = (row % (2 * w)) >= w, (col % (2 * w)) >= w
        code_f = np.where(same & t_hi & ~s_hi, li + 1, code_f)
        code_b = np.where(same & ~t_hi & s_hi, li + 1, code_b)
    codes = np.stack([code_f, code_b]).astype(np.int32)
    lane_sum = (np.arange(HG_DIAG * HG_DIM)[:, None] // HG_DIM == col % HG_DIAG).astype(np.float32)
    return jnp.asarray(tri, BF16), jnp.asarray(codes), jnp.asarray(lane_sum, BF16)


def _hgrn2(p_l, p_c, lbp, norm_w, b):
    tri, codes, lane_sum = _hg_constants()
    t = p_l.shape[0] // b
    ctx = p_c.shape[0] // b
    hd = HG_DIM
    pw = HG_PAIR * hd
    n = t + ctx
    first = _HG0 * LANE // pw

    def col(stream):
        return lambda i, j: (i, first + stream * (HG_HEADS // HG_PAIR) + j)

    lat_specs = [pl.BlockSpec((t, pw), col(s)) for s in range(5)]
    ctx_specs = [pl.BlockSpec((ctx, pw), col(s)) for s in range(5)]
    big = lambda: pltpu.VMEM((HG_PAIR, n, hd), F32)
    return pl.pallas_call(
        _hg_kernel,
        grid=(b, HG_HEADS // HG_PAIR),
        in_specs=lat_specs + ctx_specs + [
            pl.BlockSpec((2, 3, pw), lambda i, j: (0, 0, j)),
            pl.BlockSpec((1, hd), lambda i, j: (0, 0)),
            pl.BlockSpec(tri.shape, lambda i, j: (0, 0, 0)),
            pl.BlockSpec(codes.shape, lambda i, j: (0, 0, 0)),
            pl.BlockSpec(lane_sum.shape, lambda i, j: (0, 0)),
        ],
        out_specs=[
            pl.BlockSpec((t, pw), lambda i, j: (i, j)),
            pl.BlockSpec((ctx, pw), lambda i, j: (i, j)),
        ],
        out_shape=[
            jax.ShapeDtypeStruct((b * t, HG_WIDTH), BF16),
            jax.ShapeDtypeStruct((b * ctx, HG_WIDTH), BF16),
        ],
        scratch_shapes=[big() for _ in range(7)] + [pltpu.VMEM((HG_PAIR, 2, hd, hd), F32)],
        compiler_params=_cparams(("arbitrary", "arbitrary")),
        name="hgrn2",
    )(*([p_l] * 5 + [p_c] * 5 + [lbp, norm_w.reshape(1, hd), tri, codes, lane_sum]))


def _outproj_kernel(oa_ref, ob_ref, oc_ref, wa_ref, wb_ref, wc_ref, x_ref, g_ref, o_ref):
    dot = functools.partial(jnp.dot, preferred_element_type=F32)
    y = dot(oa_ref[...], wa_ref[...]) + dot(ob_ref[...], wb_ref[...]) + dot(oc_ref[...], wc_ref[...])
    o_ref[...] = x_ref[...] + g_ref[0] * y


def _outproj(oa, ob, oc, w, x2d, mod, layer, rows_per_mod, mod_row0, tm):
    m, d = x2d.shape
    tn = OUTPROJ_TN
    mrow = lambda i: layer * MOD_ROWS + mod_row0 + (i * tm) // rows_per_mod
    nb = NA_WIDTH // HG_WIDTH
    return pl.pallas_call(
        _outproj_kernel,
        grid=(m // tm, d // tn),
        in_specs=[
            pl.BlockSpec((tm, NA_WIDTH), lambda i, j: (i, 0)),
            pl.BlockSpec((tm, HG_WIDTH), lambda i, j: (i, 0)),
            pl.BlockSpec((tm, GM_WIDTH), lambda i, j: (i, 0)),
            pl.BlockSpec((None, NA_WIDTH, tn), lambda i, j: (layer, 0, j)),
            pl.BlockSpec((None, HG_WIDTH, tn), lambda i, j: (layer, nb, j)),
            pl.BlockSpec((None, GM_WIDTH, tn), lambda i, j: (layer, nb + 1, j)),
            pl.BlockSpec((tm, tn), lambda i, j: (i, j)),
            pl.BlockSpec((1, 1, tn), lambda i, j: (mrow(i), 0, 2 * (d // tn) + j)),
        ],
        out_specs=pl.BlockSpec((tm, tn), lambda i, j: (i, j)),
        out_shape=jax.ShapeDtypeStruct((m, d), F32),
        compiler_params=_cparams(("arbitrary", "arbitrary")),
        name="outproj",
    )(oa, ob, oc, w, w, w, x2d, mod)


def _mlp_kernel(x_ref, nw_ref, sh_ref, sc_ref, g_ref, w1_ref, w2_ref, fnw_ref, o_ref, h_ref, acc_ref,
                *, final_norm):
    j = pl.program_id(1)

    @pl.when(j == 0)
    def _():
        _norm_modulate_store(x_ref, nw_ref, sh_ref, sc_ref, h_ref)
        acc_ref[...] = jnp.zeros_like(acc_ref)

    a = jnp.maximum(jnp.dot(h_ref[...], w1_ref[...], preferred_element_type=F32), 0.0)
    acc_ref[...] += jnp.dot((a * a).astype(BF16), w2_ref[...], preferred_element_type=F32)

    @pl.when(j == pl.num_programs(1) - 1)
    def _():
        y = x_ref[...] + g_ref[0] * acc_ref[...]
        if final_norm:
            y = y * lax.rsqrt(jnp.mean(y * y, axis=-1, keepdims=True) + EPS) * fnw_ref[...]
        o_ref[...] = y


def _mlp(x2d, nw, mod, w1, w2, fnw, layer, rows_per_mod, mod_row0, tm, final_norm):
    m, d = x2d.shape
    hid = w1.shape[2]
    th = MLP_TH
    mrow = lambda i: layer * MOD_ROWS + mod_row0 + (i * tm) // rows_per_mod
    return pl.pallas_call(
        functools.partial(_mlp_kernel, final_norm=final_norm),
        grid=(m // tm, hid // th),
        in_specs=[
            pl.BlockSpec((tm, d), lambda i, j: (i, 0)),
            pl.BlockSpec((None, 1, d), lambda i, j: (layer, 0, 0)),
            pl.BlockSpec((1, 1, d), lambda i, j: (mrow(i), 0, 3)),
            pl.BlockSpec((1, 1, d), lambda i, j: (mrow(i), 0, 4)),
            pl.BlockSpec((1, 1, d), lambda i, j: (mrow(i), 0, 5)),
            pl.BlockSpec((None, d, th), lambda i, j: (layer, 0, j)),
            pl.BlockSpec((None, th, d), lambda i, j: (layer, j, 0)),
            pl.BlockSpec((1, d), lambda i, j: (0, 0)),
        ],
        out_specs=pl.BlockSpec((tm, d), lambda i, j: (i, 0)),
        out_shape=jax.ShapeDtypeStruct((m, d), F32),
        scratch_shapes=[pltpu.VMEM((tm, d), BF16), pltpu.VMEM((tm, d), F32)],
        compiler_params=_cparams(("arbitrary", "arbitrary")),
        name="mlp",
    )(x2d, nw, mod, mod, mod, w1, w2, fnw)


def kernel(x, c, ctx, c_ctx, ada_w, ada_b, norm1_w, norm2_w, w_in, na_rpb, hg_lb_logits, hg_norm_w,
           gm_ln_w, gm_ws, gm_bs, w_out, mlp_w1, mlp_w2, final_norm_w):
    bsz, seq, d = x.shape
    n_ctx = ctx.shape[1]
    depth = ada_w.shape[0]
    assert bsz < MOD_ROWS and d == D_MODEL and seq % 512 == 0 and n_ctx % 256 == 0
    assert seq // NA_GROUP_TOK >= NA_WIN_BLOCKS and seq // GRID_W >= 2 * NA_WIN_ROWS

    lb = jnp.cumsum(jax.nn.softmax(hg_lb_logits.astype(F32), axis=0), axis=0)
    lb = lb - lb[:1]
    lbp = jnp.stack([jnp.log(lb), jnp.log1p(-lb), 1.0 - lb], axis=2)

    cond = jnp.zeros((MOD_ROWS, d), F32).at[:bsz].set(c).at[bsz].set(c_ctx)
    mod = _ada(cond, ada_w, ada_b).reshape(depth * MOD_ROWS, 1, 6 * d)

    w_in_b, w_out_b = w_in.astype(BF16), w_out.astype(BF16)
    w1_b, w2_b = mlp_w1.astype(BF16), mlp_w2.astype(BF16)
    nw1 = norm1_w.reshape(depth, 1, d)
    nw2 = norm2_w.reshape(depth, 1, d)
    fnw = final_norm_w.reshape(1, d)

    xl = x.reshape(bsz * seq, d)
    xc = ctx.reshape(bsz * n_ctx, d)
    n_c = bsz * n_ctx
    tm_l, tm_c = TM_LATENT, min(TM_CONTEXT, n_c)
    for l in range(depth):
        need_ctx = l < depth - 1
        p_l = _inproj(xl, nw1, mod, w_in_b, l, seq, 0, TM_INPROJ)
        p_c = _inproj(xc, nw1, mod, w_in_b, l, n_c, bsz, tm_c)

        oa_l = _na_latent(p_l, p_c, _na_bias_slabs(na_rpb[l]), bsz)
        ob_l, ob_c = _hgrn2(p_l, p_c, lbp[l], hg_norm_w[l], bsz)
        oc_l = _chunk_gmlp(p_l, gm_ln_w[l], gm_ws[l], gm_bs[l])
        xl = _outproj(oa_l, ob_l, oc_l, w_out_b, xl, mod, l, seq, 0, tm_l)
        xl = _mlp(xl, nw2, mod, w1_b, w2_b, fnw, l, seq, 0, tm_l, final_norm=not need_ctx)
        if need_ctx:
            oa_c = _ctx_attention(p_c, bsz)
            oc_c = _chunk_gmlp(p_c, gm_ln_w[l], gm_ws[l], gm_bs[l])
            xc = _outproj(oa_c, ob_c, oc_c, w_out_b, xc, mod, l, n_c, bsz, tm_c)
            xc = _mlp(xc, nw2, mod, w1_b, w2_b, fnw, l, n_c, bsz, tm_c, final_norm=False)
    return xl.reshape(bsz, seq, d)
```

```python
import functools

import numpy as np
import jax
import jax.numpy as jnp
from jax import lax
from jax.experimental import pallas as pl
from jax.experimental.pallas import tpu as pltpu

F32 = jnp.float32
BF16 = jnp.bfloat16

D_MODEL = 2048
DEPTH = 2
GRID_W = 64
EPS = 1e-6

NA_HEAD_DIM = 128
NA_HEADS = 8
NA_WIDTH = NA_HEADS * NA_HEAD_DIM
NA_WIN_ROWS = 8
NA_WIN_COLS = 16

HG_HEADS = 4
HG_DIM = 128
HG_WIDTH = HG_HEADS * HG_DIM
HG_CHUNK = 64

GM_GROUPS = 4
GM_DIM = 128
GM_WIDTH = GM_GROUPS * GM_DIM
GM_CHUNK = 128
GM_BLOCK = 4 * GM_CHUNK

IN_WIDTH = 3 * NA_WIDTH + 5 * HG_WIDTH + 2 * GM_WIDTH
MLP_HIDDEN = 4 * D_MODEL
LANE = 128
LOG2E = 1.4426950408889634

_QA, _KA, _VA = 0, NA_HEADS, 2 * NA_HEADS
_HG0 = 3 * NA_HEADS
_GM_U = (3 * NA_WIDTH + 5 * HG_WIDTH) // GM_WIDTH
_GM_V = _GM_U + 1

VMEM_LIMIT_V7X = 56 * 1024 * 1024
TM_LATENT = 512
TM_CONTEXT = 512
TM_INPROJ = 1024
INPROJ_TN = 1664
MLP_TH = 1024
OUTPROJ_TN = 2048
NORM_CHUNK_ROWS = 32
NORM_UNROLL = 4
MOD_ROWS = 16


def _cparams(sem):
    return pltpu.CompilerParams(dimension_semantics=sem, vmem_limit_bytes=VMEM_LIMIT_V7X)


def _silu(x):
    return x / (1.0 + jnp.exp(-x))


def _gelu_tanh(x):
    return 0.5 * x * (1.0 + jnp.tanh(0.7978845608028654 * (x + 0.044715 * (x * x * x))))


def _norm_modulate_store(x_ref, nw_ref, sh_ref, sc_ref, h_ref):
    gain = nw_ref[...] * (1.0 + sc_ref[0])
    shift = sh_ref[0]

    def body(i, carry):
        rows = pl.ds(pl.multiple_of(i * NORM_CHUNK_ROWS, NORM_CHUNK_ROWS), NORM_CHUNK_ROWS)
        x = x_ref[rows, :]
        rs = lax.rsqrt(jnp.mean(x * x, axis=-1, keepdims=True) + EPS)
        h_ref[rows, :] = (x * rs * gain + shift).astype(h_ref.dtype)
        return carry

    lax.fori_loop(0, x_ref.shape[0] // NORM_CHUNK_ROWS, body, 0, unroll=NORM_UNROLL)


def _ada_kernel(c_ref, w_ref, b_ref, o_ref):
    s = _silu(c_ref[...]).astype(BF16)
    o_ref[0] = jnp.dot(s, w_ref[0].astype(BF16), preferred_element_type=F32) + b_ref[0]


def _ada(cond, ada_w, ada_b):
    depth, d, n = ada_w.shape
    tn = 1536
    return pl.pallas_call(
        _ada_kernel,
        grid=(depth, n // tn),
        in_specs=[
            pl.BlockSpec((cond.shape[0], d), lambda l, j: (0, 0)),
            pl.BlockSpec((1, d, tn), lambda l, j: (l, 0, j)),
            pl.BlockSpec((1, 1, tn), lambda l, j: (l, 0, j)),
        ],
        out_specs=pl.BlockSpec((1, cond.shape[0], tn), lambda l, j: (l, 0, j)),
        out_shape=jax.ShapeDtypeStruct((depth, cond.shape[0], n), F32),
        compiler_params=_cparams(("arbitrary", "arbitrary")),
        name="ada",
    )(cond, ada_w, ada_b.reshape(depth, 1, n))


def _inproj_kernel(x_ref, nw_ref, sh_ref, sc_ref, w_ref, o_ref, h_ref):
    @pl.when(pl.program_id(1) == 0)
    def _():
        _norm_modulate_store(x_ref, nw_ref, sh_ref, sc_ref, h_ref)

    o_ref[...] = jnp.dot(h_ref[...], w_ref[...], preferred_element_type=F32)


def _inproj(x2d, nw, mod, w, layer, rows_per_mod, mod_row0, tm):
    m, d = x2d.shape
    n = w.shape[2]
    tn = INPROJ_TN
    mrow = lambda i: layer * MOD_ROWS + mod_row0 + (i * tm) // rows_per_mod
    return pl.pallas_call(
        _inproj_kernel,
        grid=(m // tm, n // tn),
        in_specs=[
            pl.BlockSpec((tm, d), lambda i, j: (i, 0)),
            pl.BlockSpec((None, 1, d), lambda i, j: (layer, 0, 0)),
            pl.BlockSpec((1, 1, d), lambda i, j: (mrow(i), 0, 0)),
            pl.BlockSpec((1, 1, d), lambda i, j: (mrow(i), 0, 1)),
            pl.BlockSpec((None, d, tn), lambda i, j: (layer, 0, j)),
        ],
        out_specs=pl.BlockSpec((tm, tn), lambda i, j: (i, j)),
        out_shape=jax.ShapeDtypeStruct((m, n), F32),
        scratch_shapes=[pltpu.VMEM((tm, d), BF16)],
        compiler_params=_cparams(("arbitrary", "arbitrary")),
        name="inproj",
    )(x2d, nw, mod, mod, w)


NA_GROUP_ROWS = 4
NA_GROUP_TOK = NA_GROUP_ROWS * GRID_W
NA_WIN_BLOCKS = 3
NA_GROUPS_PER_STEP = 4
NA_LOOP_UNROLL = 2


NA_MASKED_SLAB = 2 * NA_WIN_ROWS - 1


def _na_bias_slabs(rpb):
    h = rpb.shape[0]
    kcol = np.arange(GRID_W)[:, None]
    qcol = np.arange(GRID_W)[None, :]
    wstart = np.clip(qcol - NA_WIN_COLS // 2, 0, GRID_W - NA_WIN_COLS)
    valid_col = (kcol >= wstart) & (kcol < wstart + NA_WIN_COLS)
    pad = GRID_W - NA_WIN_COLS
    padded = jnp.pad(rpb.astype(F32), ((0, 0), (0, 0), (pad, pad)))
    m = jnp.tile(padded, (1, 1, GRID_W + 1))[:, :, :GRID_W * 2 * GRID_W]
    m = m.reshape(h, rpb.shape[1], GRID_W, 2 * GRID_W)[..., :GRID_W]
    slabs = jnp.where(valid_col, m[..., ::-1], -1e30)
    slabs = jnp.concatenate([slabs, jnp.full((h, 1, GRID_W, GRID_W), -1e30, F32)], axis=1)
    return jnp.concatenate([slabs, slabs], axis=-1)


def _na_slab_index(rows):
    n_groups = rows // NA_GROUP_ROWS
    krel = np.arange(NA_WIN_BLOCKS * NA_GROUP_ROWS)[:, None]
    qrel = np.arange(NA_GROUP_ROWS)[None, :]
    idx = []
    for g in (0, 1, n_groups - 1):
        krow = int(np.clip(g - 1, 0, n_groups - NA_WIN_BLOCKS)) * NA_GROUP_ROWS + krel
        qrow = g * NA_GROUP_ROWS + qrel
        r0 = np.clip(qrow - NA_WIN_ROWS // 2, 0, rows - NA_WIN_ROWS)
        in_window = (krow >= r0) & (krow < r0 + NA_WIN_ROWS)
        idx.append(np.where(in_window, krow - qrow + NA_WIN_ROWS - 1, NA_MASKED_SLAB))
    return np.stack(idx).tolist()


def _na_kernel(q_ref, k_ref, v_ref, kc_ref, vc_ref, slab_ref, o_ref, qb_ref, kb_ref, vt_ref, bias_ref,
               *, slab_index):
    n_groups = q_ref.shape[0] // NA_GROUP_TOK
    gt = NA_GROUP_TOK
    scale = NA_HEAD_DIM ** -0.5
    nt = (((1,), (1,)), ((), ()))
    dot = functools.partial(jnp.dot, preferred_element_type=F32)

    @pl.when(pl.program_id(1) == 0)
    def _():
        left = lax.broadcasted_iota(jnp.int32, (GRID_W, 2 * GRID_W), 1) < GRID_W
        for kind, per_key_row in enumerate(slab_index):
            for j, per_query_row in enumerate(per_key_row):
                for a in range(0, NA_GROUP_ROWS, 2):
                    tile = jnp.where(left, slab_ref[0, per_query_row[a]], slab_ref[0, per_query_row[a + 1]])
                    bias_ref[kind, j * GRID_W:(j + 1) * GRID_W, a * GRID_W:(a + 2) * GRID_W] = tile

    qb_ref[...] = q_ref[...].astype(BF16)
    kb_ref[...] = k_ref[...].astype(BF16)
    for b in range(n_groups):
        vt_ref[b] = v_ref[b * gt:(b + 1) * gt, :].T.astype(BF16)
    kc = kc_ref[...].astype(BF16)
    vct = vc_ref[...].T.astype(BF16)

    def group(g):
        blk = jnp.clip(g - 1, 0, n_groups - NA_WIN_BLOCKS)
        kind = jnp.where(g == 0, 0, jnp.where(g == n_groups - 1, 2, 1))
        qg = qb_ref[pl.ds(pl.multiple_of(g * gt, gt), gt), :]
        kw = kb_ref[pl.ds(pl.multiple_of(blk * gt, gt), NA_WIN_BLOCKS * gt), :]
        s_lat = lax.dot_general(kw, qg, nt, preferred_element_type=F32) * scale + bias_ref[kind]
        s_ctx = lax.dot_general(kc, qg, nt, preferred_element_type=F32) * scale
        yield
        m = jnp.maximum(jnp.max(s_lat, axis=0, keepdims=True), jnp.max(s_ctx, axis=0, keepdims=True))
        p_lat = jnp.exp(s_lat - m)
        p_ctx = jnp.exp(s_ctx - m)
        l = jnp.sum(p_lat, axis=0, keepdims=True) + jnp.sum(p_ctx, axis=0, keepdims=True)
        yield
        ot = dot(vct, p_ctx.astype(BF16))
        for j in range(NA_WIN_BLOCKS):
            ot = ot + dot(vt_ref[blk + j], p_lat[j * gt:(j + 1) * gt, :].astype(BF16))
        yield
        o_ref[pl.ds(pl.multiple_of(g * gt, gt), gt), :] = (ot / l).T.astype(o_ref.dtype)
        yield

    def body(i, carry):
        groups = [group(i * NA_GROUPS_PER_STEP + u) for u in range(NA_GROUPS_PER_STEP)]
        for _ in range(4):
            for grp in groups:
                next(grp)
        return carry

    lax.fori_loop(0, n_groups // NA_GROUPS_PER_STEP, body, 0, unroll=NA_LOOP_UNROLL)


def _na_latent(p_l, p_c, slabs, b):
    t = p_l.shape[0] // b
    ctx = p_c.shape[0] // b
    hd = NA_HEAD_DIM
    return pl.pallas_call(
        functools.partial(_na_kernel, slab_index=_na_slab_index(t // GRID_W)),
        grid=(NA_HEADS, b),
        in_specs=[
            pl.BlockSpec((t, hd), lambda h, i: (i, _QA + h)),
            pl.BlockSpec((t, hd), lambda h, i: (i, _KA + h)),
            pl.BlockSpec((t, hd), lambda h, i: (i, _VA + h)),
            pl.BlockSpec((ctx, hd), lambda h, i: (i, _KA + h)),
            pl.BlockSpec((ctx, hd), lambda h, i: (i, _VA + h)),
            pl.BlockSpec((1,) + slabs.shape[1:], lambda h, i: (h, 0, 0, 0)),
        ],
        out_specs=pl.BlockSpec((t, hd), lambda h, i: (i, h)),
        out_shape=jax.ShapeDtypeStruct((b * t, NA_WIDTH), BF16),
        scratch_shapes=[
            pltpu.VMEM((t, hd), BF16),
            pltpu.VMEM((t, hd), BF16),
            pltpu.VMEM((t // NA_GROUP_TOK, hd, NA_GROUP_TOK), BF16),
            pltpu.VMEM((3, NA_WIN_BLOCKS * NA_GROUP_TOK, NA_GROUP_TOK), F32),
        ],
        compiler_params=_cparams(("arbitrary", "arbitrary")),
        name="na_latent",
    )(p_l, p_l, p_l, p_c, p_c, slabs)


def _ctx_attn_kernel(q_ref, k_ref, v_ref, o_ref):
    scale = NA_HEAD_DIM ** -0.5
    q = q_ref[...].astype(BF16)
    k = k_ref[...].astype(BF16)
    v = v_ref[...].astype(BF16)
    s = lax.dot_general(q, k, (((1,), (1,)), ((), ())), preferred_element_type=F32) * scale
    p = jnp.exp(s - jnp.max(s, axis=-1, keepdims=True))
    l = jnp.sum(p, axis=-1, keepdims=True)
    o = jnp.dot(p.astype(BF16), v, preferred_element_type=F32)
    o_ref[...] = (o / l).astype(o_ref.dtype)


def _ctx_attention(p_c, b):
    ctx = p_c.shape[0] // b
    hd = NA_HEAD_DIM
    return pl.pallas_call(
        _ctx_attn_kernel,
        grid=(b, NA_HEADS),
        in_specs=[
            pl.BlockSpec((ctx, hd), lambda i, h: (i, _QA + h)),
            pl.BlockSpec((ctx, hd), lambda i, h: (i, _KA + h)),
            pl.BlockSpec((ctx, hd), lambda i, h: (i, _VA + h)),
        ],
        out_specs=pl.BlockSpec((ctx, hd), lambda i, h: (i, h)),
        out_shape=jax.ShapeDtypeStruct((b * ctx, NA_WIDTH), BF16),
        compiler_params=_cparams(("arbitrary", "arbitrary")),
        name="ctx_attention",
    )(p_c, p_c, p_c)


def _gm_kernel(u_ref, v_ref, lnw_ref, ws_ref, bs_ref, o_ref):
    for ck in range(u_ref.shape[0] // GM_CHUNK):
        rows = slice(ck * GM_CHUNK, (ck + 1) * GM_CHUNK)
        uf = _gelu_tanh(u_ref[rows, :])
        vf = _gelu_tanh(v_ref[rows, :])
        for g in range(GM_GROUPS):
            sl = slice(g * GM_DIM, (g + 1) * GM_DIM)
            vg = vf[:, sl]
            mu = jnp.mean(vg, axis=-1, keepdims=True)
            dv = vg - mu
            var = jnp.mean(dv * dv, axis=-1, keepdims=True)
            vn = dv * lax.rsqrt(var + EPS) * lnw_ref[:, sl]
            mixed = jnp.dot(ws_ref[g].astype(BF16), vn.astype(BF16), preferred_element_type=F32) + bs_ref[g]
            o_ref[rows, sl] = (uf[:, sl] * mixed).astype(o_ref.dtype)


def _chunk_gmlp(p2, ln_w, ws, bs):
    m = p2.shape[0]
    blk = GM_BLOCK if m % GM_BLOCK == 0 else GM_CHUNK
    return pl.pallas_call(
        _gm_kernel,
        grid=(m // blk,),
        in_specs=[
            pl.BlockSpec((blk, GM_WIDTH), lambda c: (c, _GM_U)),
            pl.BlockSpec((blk, GM_WIDTH), lambda c: (c, _GM_V)),
            pl.BlockSpec((1, GM_WIDTH), lambda c: (0, 0)),
            pl.BlockSpec((GM_GROUPS, GM_CHUNK, GM_CHUNK), lambda c: (0, 0, 0)),
            pl.BlockSpec((GM_GROUPS, GM_CHUNK, 1), lambda c: (0, 0, 0)),
        ],
        out_specs=pl.BlockSpec((blk, GM_WIDTH), lambda c: (c, 0)),
        out_shape=jax.ShapeDtypeStruct((m, GM_WIDTH), BF16),
        compiler_params=_cparams(("arbitrary",)),
        name="chunk_gmlp",
    )(p2, p2, ln_w.reshape(1, GM_WIDTH), ws, bs.reshape(GM_GROUPS, GM_CHUNK, 1))


HG_PAIR = 2
HG_LEVELS = (32, 16, 8)
HG_DIAG = 8
HG_PREP_ROWS = 256
HG_SCAN_UNROLL = 4


def _split3_dot(tri, g):
    g0 = g.astype(BF16)
    r1 = g - g0.astype(F32)
    g1 = r1.astype(BF16)
    g2 = (r1 - g1.astype(F32)).astype(BF16)
    dot = functools.partial(jnp.dot, preferred_element_type=F32)
    return dot(tri, g0) + dot(tri, g1) + dot(tri, g2)


def _hg_kernel(ql_ref, ffl_ref, fbl_ref, il_ref, gl_ref,
               qc_ref, ffc_ref, fbc_ref, ic_ref, gc_ref,
               lbp_ref, nw_ref, tri_ref, code_ref, lsum_ref, ol_ref, oc_ref,
               qh_s, kf_s, kb_s, ef_s, eb_s, of_s, ob_s, st_s):
    c = HG_CHUNK
    hd = HG_DIM
    n_ctx = qc_ref.shape[0]
    n_lat = ql_ref.shape[0]
    scale = hd ** -0.5
    nt = (((1,), (1,)), ((), ()))
    tn = (((0,), (0,)), ((), ()))

    def prep(q_ref, ff_ref, fb_ref, base, n):
        step = HG_PREP_ROWS
        for t0 in range(0, n, step):
            src = slice(t0, t0 + step)
            dst = slice(base + t0, base + t0 + step)
            qh = _silu(q_ref[src, :]) * scale
            for h in range(HG_PAIR):
                qh_s[h, dst, :] = qh[:, h * hd:(h + 1) * hd]
            for d, (f_ref, k_s, e_s) in enumerate(((ff_ref, kf_s, ef_s), (fb_ref, kb_s, eb_s))):
                x = f_ref[src, :]
                sp = jnp.maximum(-x, 0.0) + jnp.log(1.0 + jnp.exp(-jnp.abs(x)))
                log_lb = lbp_ref[d, 0:1, :]
                y = lbp_ref[d, 1:2, :] - sp
                mx = jnp.maximum(log_lb, y)
                log_f = mx + jnp.log(1.0 + jnp.exp(-jnp.abs(log_lb - y)))
                one_minus_f = lbp_ref[d, 2:3, :] * jnp.exp(-(sp + x))
                e = _split3_dot(tri_ref[d], log_f) * LOG2E
                for h in range(HG_PAIR):
                    e_s[h, dst, :] = e[:, h * hd:(h + 1) * hd]
                    k_s[h, dst, :] = one_minus_f[:, h * hd:(h + 1) * hd]

    prep(qc_ref, ffc_ref, fbc_ref, 0, n_ctx)
    prep(ql_ref, ffl_ref, fbl_ref, n_ctx, n_lat)

    diag_code = len(HG_LEVELS) + 1

    st_s[...] = jnp.zeros_like(st_s)

    def chunk(off, v, h, d, out):
        fwd = d == 0
        k_s, e_s = (kf_s, ef_s) if fwd else (kb_s, eb_s)
        q = qh_s[h, pl.ds(off, c), :]
        k = k_s[h, pl.ds(off, c), :]
        e = e_s[h, pl.ds(off, c), :]
        code = code_ref[d]
        prods = []
        for i in range(c // HG_DIAG):
            bs = slice(i * HG_DIAG, (i + 1) * HG_DIAG)
            row_prods = []
            for s in range(HG_DIAG):
                kr = k_s[h, pl.ds(off + i * HG_DIAG + s, 1), :]
                er = e_s[h, pl.ds(off + i * HG_DIAG + s, 1), :]
                decay = jnp.exp2(jnp.minimum(e[bs] - er, 0.0))
                row_prods.append((q[bs] * kr * decay).astype(BF16))
            prods.append(jnp.concatenate(row_prods, axis=1))
        diag = jnp.dot(jnp.concatenate(prods, axis=0), lsum_ref[...], preferred_element_type=F32)
        yield
        att = None
        for li, w in enumerate(HG_LEVELS):
            zeros = jnp.zeros((w, hd), F32)
            q_parts, k_parts = [], []
            for a in range(0, c, 2 * w):
                lo, hi = slice(a, a + w), slice(a + w, a + 2 * w)
                if fwd:
                    ref = e_s[h, pl.ds(off + a + w - 1, 1), :]
                    q_parts += [zeros, q[hi] * jnp.exp2(e[hi] - ref)]
                    k_parts += [k[lo] * jnp.exp2(ref - e[lo]), zeros]
                else:
                    ref = e_s[h, pl.ds(off + a + w, 1), :]
                    q_parts += [q[lo] * jnp.exp2(e[lo] - ref), zeros]
                    k_parts += [zeros, k[hi] * jnp.exp2(ref - e[hi])]
            qs = jnp.concatenate(q_parts, axis=0).astype(BF16)
            ks = jnp.concatenate(k_parts, axis=0).astype(BF16)
            a_w = lax.dot_general(qs, ks, nt, preferred_element_type=F32)
            att = a_w if att is None else jnp.where(code == li + 1, a_w, att)
            yield
        e_end = e_s[h, pl.ds(off + (c - 1 if fwd else 0), 1), :]
        st = st_s[h, d]
        qi = (q * jnp.exp2(e)).astype(BF16)
        ki = (k * jnp.exp2(e_end - e)).astype(BF16)
        o = lax.dot_general(qi, st.astype(BF16), nt, preferred_element_type=F32)
        yield
        st_new = st * jnp.exp2(e_end) + lax.dot_general(v, ki, tn, preferred_element_type=F32)
        yield
        att = jnp.where(code == diag_code, diag, att)
        out.append((o + jnp.dot(att.astype(BF16), v, preferred_element_type=F32), st_new))
        yield

    n_stages = len(HG_LEVELS) + 4

    def scan(v_ref, base, n):
        def body(i, carry):
            rf = pl.multiple_of(i * c, c)
            rb = pl.multiple_of((n - 1 - i) * c, c)
            vf = v_ref[pl.ds(rf, c), :].astype(BF16)
            vb = v_ref[pl.ds(rb, c), :].astype(BF16)
            chains = []
            for h in range(HG_PAIR):
                hs = slice(h * hd, (h + 1) * hd)
                for d, r, v in ((0, rf, vf), (1, rb, vb)):
                    out = []
                    chains.append((h, d, r, out, chunk(base + r, v[:, hs], h, d, out)))
            for _ in range(n_stages):
                for chain in chains:
                    next(chain[-1])
            for h, d, r, out, _ in chains:
                o, st_new = out[0]
                (of_s if d == 0 else ob_s)[h, pl.ds(base + r, c), :] = o
                st_s[h, d] = st_new
            return carry

        lax.fori_loop(0, n, body, 0, unroll=HG_SCAN_UNROLL)

    scan(ic_ref, 0, n_ctx // c)
    scan(il_ref, n_ctx, n_lat // c)

    def finish(g_ref, o_ref, base, n):
        step = 256
        for t0 in range(0, n, step):
            src = slice(t0, t0 + step)
            dst = slice(base + t0, base + t0 + step)
            for h in range(HG_PAIR):
                hs = slice(h * hd, (h + 1) * hd)
                o = of_s[h, dst, :] + ob_s[h, dst, :]
                o = o * lax.rsqrt(jnp.mean(o * o, axis=-1, keepdims=True) + EPS) * nw_ref[...]
                o_ref[src, hs] = (o * _silu(g_ref[src, hs])).astype(o_ref.dtype)

    finish(gc_ref, oc_ref, 0, n_ctx)
    finish(gl_ref, ol_ref, n_ctx, n_lat)


def _hg_constants():
    c = HG_CHUNK
    row, col = np.arange(c)[:, None], np.arange(c)[None, :]
    blocks = np.eye(HG_PREP_ROWS // c)
    tri = np.stack([np.kron(blocks, col <= row), np.kron(blocks, col >= row)]).astype(np.float32)
    same8 = (row // HG_DIAG) == (col // HG_DIAG)
    diag_code = len(HG_LEVELS) + 1
    code_f = np.where(same8 & (col <= row), diag_code, 0)
    code_b = np.where(same8 & (col >= row), diag_code, 0)
    for li, w in enumerate(HG_LEVELS):
        same = (row // (2 * w)) == (col // (2 * w))
        t_hi, s_hi = (row % (2 * w)) >= w, (col % (2 * w)) >= w
        code_f = np.where(same & t_hi & ~s_hi, li + 1, code_f)
        code_b = np.where(same & ~t_hi & s_hi, li + 1, code_b)
    codes = np.stack([code_f, code_b]).astype(np.int32)
    lane_sum = (np.arange(HG_DIAG * HG_DIM)[:, None] // HG_DIM == col % HG_DIAG).astype(np.float32)
    return jnp.asarray(tri, BF16), jnp.asarray(codes), jnp.asarray(lane_sum, BF16)


def _hgrn2(p_l, p_c, lbp, norm_w, b):
    tri, codes, lane_sum = _hg_constants()
    t = p_l.shape[0] // b
    ctx = p_c.shape[0] // b
    hd = HG_DIM
    pw = HG_PAIR * hd
    n = t + ctx
    first = _HG0 * LANE // pw

    def col(stream):
        return lambda i, j: (i, first + stream * (HG_HEADS // HG_PAIR) + j)

    lat_specs = [pl.BlockSpec((t, pw), col(s)) for s in range(5)]
    ctx_specs = [pl.BlockSpec((ctx, pw), col(s)) for s in range(5)]
    big = lambda: pltpu.VMEM((HG_PAIR, n, hd), F32)
    return pl.pallas_call(
        _hg_kernel,
        grid=(b, HG_HEADS // HG_PAIR),
        in_specs=lat_specs + ctx_specs + [
            pl.BlockSpec((2, 3, pw), lambda i, j: (0, 0, j)),
            pl.BlockSpec((1, hd), lambda i, j: (0, 0)),
            pl.BlockSpec(tri.shape, lambda i, j: (0, 0, 0)),
            pl.BlockSpec(codes.shape, lambda i, j: (0, 0, 0)),
            pl.BlockSpec(lane_sum.shape, lambda i, j: (0, 0)),
        ],
        out_specs=[
            pl.BlockSpec((t, pw), lambda i, j: (i, j)),
            pl.BlockSpec((ctx, pw), lambda i, j: (i, j)),
        ],
        out_shape=[
            jax.ShapeDtypeStruct((b * t, HG_WIDTH), BF16),
            jax.ShapeDtypeStruct((b * ctx, HG_WIDTH), BF16),
        ],
        scratch_shapes=[big() for _ in range(7)] + [pltpu.VMEM((HG_PAIR, 2, hd, hd), F32)],
        compiler_params=_cparams(("arbitrary", "arbitrary")),
        name="hgrn2",
    )(*([p_l] * 5 + [p_c] * 5 + [lbp, norm_w.reshape(1, hd), tri, codes, lane_sum]))


def _outproj_kernel(oa_ref, ob_ref, oc_ref, wa_ref, wb_ref, wc_ref, x_ref, g_ref, o_ref):
    dot = functools.partial(jnp.dot, preferred_element_type=F32)
    y = dot(oa_ref[...], wa_ref[...]) + dot(ob_ref[...], wb_ref[...]) + dot(oc_ref[...], wc_ref[...])
    o_ref[...] = x_ref[...] + g_ref[0] * y


def _outproj(oa, ob, oc, w, x2d, mod, layer, rows_per_mod, mod_row0, tm):
    m, d = x2d.shape
    tn = OUTPROJ_TN
    mrow = lambda i: layer * MOD_ROWS + mod_row0 + (i * tm) // rows_per_mod
    nb = NA_WIDTH // HG_WIDTH
    return pl.pallas_call(
        _outproj_kernel,
        grid=(m // tm, d // tn),
        in_specs=[
            pl.BlockSpec((tm, NA_WIDTH), lambda i, j: (i, 0)),
            pl.BlockSpec((tm, HG_WIDTH), lambda i, j: (i, 0)),
            pl.BlockSpec((tm, GM_WIDTH), lambda i, j: (i, 0)),
            pl.BlockSpec((None, NA_WIDTH, tn), lambda i, j: (layer, 0, j)),
            pl.BlockSpec((None, HG_WIDTH, tn), lambda i, j: (layer, nb, j)),
            pl.BlockSpec((None, GM_WIDTH, tn), lambda i, j: (layer, nb + 1, j)),
            pl.BlockSpec((tm, tn), lambda i, j: (i, j)),
            pl.BlockSpec((1, 1, tn), lambda i, j: (mrow(i), 0, 2 * (d // tn) + j)),
        ],
        out_specs=pl.BlockSpec((tm, tn), lambda i, j: (i, j)),
        out_shape=jax.ShapeDtypeStruct((m, d), F32),
        compiler_params=_cparams(("arbitrary", "arbitrary")),
        name="outproj",
    )(oa, ob, oc, w, w, w, x2d, mod)


def _mlp_kernel(x_ref, nw_ref, sh_ref, sc_ref, g_ref, w1_ref, w2_ref, fnw_ref, o_ref, h_ref, acc_ref,
                *, final_norm):
    j = pl.program_id(1)

    @pl.when(j == 0)
    def _():
        _norm_modulate_store(x_ref, nw_ref, sh_ref, sc_ref, h_ref)
        acc_ref[...] = jnp.zeros_like(acc_ref)

    a = jnp.maximum(jnp.dot(h_ref[...], w1_ref[...], preferred_element_type=F32), 0.0)
    acc_ref[...] += jnp.dot((a * a).astype(BF16), w2_ref[...], preferred_element_type=F32)

    @pl.when(j == pl.num_programs(1) - 1)
    def _():
        y = x_ref[...] + g_ref[0] * acc_ref[...]
        if final_norm:
            y = y * lax.rsqrt(jnp.mean(y * y, axis=-1, keepdims=True) + EPS) * fnw_ref[...]
        o_ref[...] = y


def _mlp(x2d, nw, mod, w1, w2, fnw, layer, rows_per_mod, mod_row0, tm, final_norm):
    m, d = x2d.shape
    hid = w1.shape[2]
    th = MLP_TH
    mrow = lambda i: layer * MOD_ROWS + mod_row0 + (i * tm) // rows_per_mod
    return pl.pallas_call(
        functools.partial(_mlp_kernel, final_norm=final_norm),
        grid=(m // tm, hid // th),
        in_specs=[
            pl.BlockSpec((tm, d), lambda i, j: (i, 0)),
            pl.BlockSpec((None, 1, d), lambda i, j: (layer, 0, 0)),
            pl.BlockSpec((1, 1, d), lambda i, j: (mrow(i), 0, 3)),
            pl.BlockSpec((1, 1, d), lambda i, j: (mrow(i), 0, 4)),
            pl.BlockSpec((1, 1, d), lambda i, j: (mrow(i), 0, 5)),
            pl.BlockSpec((None, d, th), lambda i, j: (layer, 0, j)),
            pl.BlockSpec((None, th, d), lambda i, j: (layer, j, 0)),
            pl.BlockSpec((1, d), lambda i, j: (0, 0)),
        ],
        out_specs=pl.BlockSpec((tm, d), lambda i, j: (i, 0)),
        out_shape=jax.ShapeDtypeStruct((m, d), F32),
        scratch_shapes=[pltpu.VMEM((tm, d), BF16), pltpu.VMEM((tm, d), F32)],
        compiler_params=_cparams(("arbitrary", "arbitrary")),
        name="mlp",
    )(x2d, nw, mod, mod, mod, w1, w2, fnw)


def kernel(x, c, ctx, c_ctx, ada_w, ada_b, norm1_w, norm2_w, w_in, na_rpb, hg_lb_logits, hg_norm_w,
           gm_ln_w, gm_ws, gm_bs, w_out, mlp_w1, mlp_w2, final_norm_w):
    bsz, seq, d = x.shape
    n_ctx = ctx.shape[1]
    depth = ada_w.shape[0]
    assert bsz < MOD_ROWS and d == D_MODEL and seq % 512 == 0 and n_ctx % 256 == 0
    assert seq // NA_GROUP_TOK >= NA_WIN_BLOCKS and seq // GRID_W >= 2 * NA_WIN_ROWS

    lb = jnp.cumsum(jax.nn.softmax(hg_lb_logits.astype(F32), axis=0), axis=0)
    lb = lb - lb[:1]
    lbp = jnp.stack([jnp.log(lb), jnp.log1p(-lb), 1.0 - lb], axis=2)

    cond = jnp.zeros((MOD_ROWS, d), F32).at[:bsz].set(c).at[bsz].set(c_ctx)
    mod = _ada(cond, ada_w, ada_b).reshape(depth * MOD_ROWS, 1, 6 * d)

    w_in_b, w_out_b = w_in.astype(BF16), w_out.astype(BF16)
    w1_b, w2_b = mlp_w1.astype(BF16), mlp_w2.astype(BF16)
    nw1 = norm1_w.reshape(depth, 1, d)
    nw2 = norm2_w.reshape(depth, 1, d)
    fnw = final_norm_w.reshape(1, d)

    xl = x.reshape(bsz * seq, d)
    xc = ctx.reshape(bsz * n_ctx, d)
    n_c = bsz * n_ctx
    tm_l, tm_c = TM_LATENT, min(TM_CONTEXT, n_c)
    for l in range(depth):
        need_ctx = l < depth - 1
        p_l = _inproj(xl, nw1, mod, w_in_b, l, seq, 0, TM_INPROJ)
        p_c = _inproj(xc, nw1, mod, w_in_b, l, n_c, bsz, tm_c)

        oa_l = _na_latent(p_l, p_c, _na_bias_slabs(na_rpb[l]), bsz)
        ob_l, ob_c = _hgrn2(p_l, p_c, lbp[l], hg_norm_w[l], bsz)
        oc_l = _chunk_gmlp(p_l, gm_ln_w[l], gm_ws[l], gm_bs[l])
        xl = _outproj(oa_l, ob_l, oc_l, w_out_b, xl, mod, l, seq, 0, tm_l)
        xl = _mlp(xl, nw2, mod, w1_b, w2_b, fnw, l, seq, 0, tm_l, final_norm=not need_ctx)
        if need_ctx:
            oa_c = _ctx_attention(p_c, bsz)
            oc_c = _chunk_gmlp(p_c, gm_ln_w[l], gm_ws[l], gm_bs[l])
            xc = _outproj(oa_c, ob_c, oc_c, w_out_b, xc, mod, l, n_c, bsz, tm_c)
            xc = _mlp(xc, nw2, mod, w1_b, w2_b, fnw, l, n_c, bsz, tm_c, final_norm=False)
    return xl.reshape(bsz, seq, d)
```

```python
import functools

import numpy as np
import jax
import jax.numpy as jnp
from jax import lax
from jax.experimental import pallas as pl
from jax.experimental.pallas import tpu as pltpu

F32 = jnp.float32
BF16 = jnp.bfloat16

D_MODEL = 2048
DEPTH = 2
GRID_W = 64
EPS = 1e-6

NA_HEAD_DIM = 128
NA_HEADS = 8
NA_WIDTH = NA_HEADS * NA_HEAD_DIM
NA_WIN_ROWS = 8
NA_WIN_COLS = 16

HG_HEADS = 4
HG_DIM = 128
HG_WIDTH = HG_HEADS * HG_DIM
HG_CHUNK = 64

GM_GROUPS = 4
GM_DIM = 128
GM_WIDTH = GM_GROUPS * GM_DIM
GM_CHUNK = 128
GM_BLOCK = 4 * GM_CHUNK

IN_WIDTH = 3 * NA_WIDTH + 5 * HG_WIDTH + 2 * GM_WIDTH
MLP_HIDDEN = 4 * D_MODEL
LANE = 128
LOG2E = 1.4426950408889634

_QA, _KA, _VA = 0, NA_HEADS, 2 * NA_HEADS
_HG0 = 3 * NA_HEADS
_GM_U = (3 * NA_WIDTH + 5 * HG_WIDTH) // GM_WIDTH
_GM_V = _GM_U + 1

VMEM_LIMIT_V7X = 56 * 1024 * 1024
TM_LATENT = 512
TM_CONTEXT = 512
TM_INPROJ = 1024
INPROJ_TN = 1664
MLP_TH = 1024
OUTPROJ_TN = 2048
NORM_CHUNK_ROWS = 32
NORM_UNROLL = 4
MOD_ROWS = 16


def _cparams(sem):
    return pltpu.CompilerParams(dimension_semantics=sem, vmem_limit_bytes=VMEM_LIMIT_V7X)


def _silu(x):
    h = 0.5 * x
    return h + h * jnp.tanh(h)


def _gelu_tanh(x):
    return 0.5 * x * (1.0 + jnp.tanh(0.7978845608028654 * (x + 0.044715 * (x * x * x))))


def _norm_modulate_store(x_ref, nw_ref, sh_ref, sc_ref, h_ref):
    gain = nw_ref[...] * (1.0 + sc_ref[0])
    shift = sh_ref[0]

    def body(i, carry):
        rows = pl.ds(pl.multiple_of(i * NORM_CHUNK_ROWS, NORM_CHUNK_ROWS), NORM_CHUNK_ROWS)
        x = x_ref[rows, :]
        rs = lax.rsqrt(jnp.mean(x * x, axis=-1, keepdims=True) + EPS)
        h_ref[rows, :] = (x * rs * gain + shift).astype(h_ref.dtype)
        return carry

    lax.fori_loop(0, x_ref.shape[0] // NORM_CHUNK_ROWS, body, 0, unroll=NORM_UNROLL)


def _ada_kernel(c_ref, w_ref, b_ref, o_ref):
    s = _silu(c_ref[...]).astype(BF16)
    o_ref[0] = jnp.dot(s, w_ref[0].astype(BF16), preferred_element_type=F32) + b_ref[0]


def _ada(cond, ada_w, ada_b):
    depth, d, n = ada_w.shape
    tn = 1536
    return pl.pallas_call(
        _ada_kernel,
        grid=(depth, n // tn),
        in_specs=[
            pl.BlockSpec((cond.shape[0], d), lambda l, j: (0, 0)),
            pl.BlockSpec((1, d, tn), lambda l, j: (l, 0, j)),
            pl.BlockSpec((1, 1, tn), lambda l, j: (l, 0, j)),
        ],
        out_specs=pl.BlockSpec((1, cond.shape[0], tn), lambda l, j: (l, 0, j)),
        out_shape=jax.ShapeDtypeStruct((depth, cond.shape[0], n), F32),
        compiler_params=_cparams(("arbitrary", "arbitrary")),
        name="ada",
    )(cond, ada_w, ada_b.reshape(depth, 1, n))


def _inproj_kernel(x_ref, nw_ref, sh_ref, sc_ref, w_ref, o_ref, h_ref):
    @pl.when(pl.program_id(1) == 0)
    def _():
        _norm_modulate_store(x_ref, nw_ref, sh_ref, sc_ref, h_ref)

    o_ref[...] = jnp.dot(h_ref[...], w_ref[...], preferred_element_type=F32)


def _inproj(x2d, nw, mod, w, layer, rows_per_mod, mod_row0, tm):
    m, d = x2d.shape
    n = w.shape[2]
    tn = INPROJ_TN
    mrow = lambda i: layer * MOD_ROWS + mod_row0 + (i * tm) // rows_per_mod
    return pl.pallas_call(
        _inproj_kernel,
        grid=(m // tm, n // tn),
        in_specs=[
            pl.BlockSpec((tm, d), lambda i, j: (i, 0)),
            pl.BlockSpec((None, 1, d), lambda i, j: (layer, 0, 0)),
            pl.BlockSpec((1, 1, d), lambda i, j: (mrow(i), 0, 0)),
            pl.BlockSpec((1, 1, d), lambda i, j: (mrow(i), 0, 1)),
            pl.BlockSpec((None, d, tn), lambda i, j: (layer, 0, j)),
        ],
        out_specs=pl.BlockSpec((tm, tn), lambda i, j: (i, j)),
        out_shape=jax.ShapeDtypeStruct((m, n), F32),
        scratch_shapes=[pltpu.VMEM((tm, d), BF16)],
        compiler_params=_cparams(("arbitrary", "arbitrary")),
        name="inproj",
    )(x2d, nw, mod, mod, w)


NA_GROUP_ROWS = 4
NA_GROUP_TOK = NA_GROUP_ROWS * GRID_W
NA_WIN_BLOCKS = 3
NA_GROUPS_PER_STEP = 4
NA_LOOP_UNROLL = 2


NA_MASKED_SLAB = 2 * NA_WIN_ROWS - 1


def _na_bias_slabs(rpb):
    h = rpb.shape[0]
    kcol = np.arange(GRID_W)[:, None]
    qcol = np.arange(GRID_W)[None, :]
    wstart = np.clip(qcol - NA_WIN_COLS // 2, 0, GRID_W - NA_WIN_COLS)
    valid_col = (kcol >= wstart) & (kcol < wstart + NA_WIN_COLS)
    pad = GRID_W - NA_WIN_COLS
    padded = jnp.pad(rpb.astype(F32), ((0, 0), (0, 0), (pad, pad)))
    m = jnp.tile(padded, (1, 1, GRID_W + 1))[:, :, :GRID_W * 2 * GRID_W]
    m = m.reshape(h, rpb.shape[1], GRID_W, 2 * GRID_W)[..., :GRID_W]
    slabs = jnp.where(valid_col, m[..., ::-1] * LOG2E, -1e30)
    slabs = jnp.concatenate([slabs, jnp.full((h, 1, GRID_W, GRID_W), -1e30, F32)], axis=1)
    return jnp.concatenate([slabs, slabs], axis=-1)


def _na_slab_index(rows):
    n_groups = rows // NA_GROUP_ROWS
    krel = np.arange(NA_WIN_BLOCKS * NA_GROUP_ROWS)[:, None]
    qrel = np.arange(NA_GROUP_ROWS)[None, :]
    idx = []
    for g in (0, 1, n_groups - 1):
        krow = int(np.clip(g - 1, 0, n_groups - NA_WIN_BLOCKS)) * NA_GROUP_ROWS + krel
        qrow = g * NA_GROUP_ROWS + qrel
        r0 = np.clip(qrow - NA_WIN_ROWS // 2, 0, rows - NA_WIN_ROWS)
        in_window = (krow >= r0) & (krow < r0 + NA_WIN_ROWS)
        idx.append(np.where(in_window, krow - qrow + NA_WIN_ROWS - 1, NA_MASKED_SLAB))
    return np.stack(idx).tolist()


def _na_kernel(q_ref, k_ref, v_ref, kc_ref, vc_ref, slab_ref, o_ref, qb_ref, kb_ref, vt_ref, bias_ref,
               *, slab_index):
    n_groups = q_ref.shape[0] // NA_GROUP_TOK
    gt = NA_GROUP_TOK
    scale2 = NA_HEAD_DIM ** -0.5 * LOG2E
    nt = (((1,), (1,)), ((), ()))
    dot = functools.partial(jnp.dot, preferred_element_type=F32)

    @pl.when(pl.program_id(1) == 0)
    def _():
        left = lax.broadcasted_iota(jnp.int32, (GRID_W, 2 * GRID_W), 1) < GRID_W
        for kind, per_key_row in enumerate(slab_index):
            for j, per_query_row in enumerate(per_key_row):
                for a in range(0, NA_GROUP_ROWS, 2):
                    tile = jnp.where(left, slab_ref[0, per_query_row[a]], slab_ref[0, per_query_row[a + 1]])
                    bias_ref[kind, j * GRID_W:(j + 1) * GRID_W, a * GRID_W:(a + 2) * GRID_W] = tile

    qb_ref[...] = (q_ref[...] * scale2).astype(BF16)
    kb_ref[...] = k_ref[...].astype(BF16)
    for b in range(n_groups):
        vt_ref[b] = v_ref[b * gt:(b + 1) * gt, :].T.astype(BF16)
    kc = kc_ref[...].astype(BF16)
    vct = vc_ref[...].T.astype(BF16)

    def group(g):
        blk = jnp.clip(g - 1, 0, n_groups - NA_WIN_BLOCKS)
        kind = jnp.where(g == 0, 0, jnp.where(g == n_groups - 1, 2, 1))
        qg = qb_ref[pl.ds(pl.multiple_of(g * gt, gt), gt), :]
        kw = kb_ref[pl.ds(pl.multiple_of(blk * gt, gt), NA_WIN_BLOCKS * gt), :]
        s_lat = lax.dot_general(kw, qg, nt, preferred_element_type=F32) + bias_ref[kind]
        s_ctx = lax.dot_general(kc, qg, nt, preferred_element_type=F32)
        yield
        m = jnp.maximum(jnp.max(s_lat, axis=0, keepdims=True), jnp.max(s_ctx, axis=0, keepdims=True))
        p_lat = jnp.exp2(s_lat - m)
        p_ctx = jnp.exp2(s_ctx - m)
        l = jnp.sum(p_lat, axis=0, keepdims=True) + jnp.sum(p_ctx, axis=0, keepdims=True)
        yield
        ot = dot(vct, p_ctx.astype(BF16))
        for j in range(NA_WIN_BLOCKS):
            ot = ot + dot(vt_ref[blk + j], p_lat[j * gt:(j + 1) * gt, :].astype(BF16))
        yield
        o_ref[pl.ds(pl.multiple_of(g * gt, gt), gt), :] = (ot / l).T.astype(o_ref.dtype)
        yield

    def body(i, carry):
        groups = [group(i * NA_GROUPS_PER_STEP + u) for u in range(NA_GROUPS_PER_STEP)]
        for _ in range(4):
            for grp in groups:
                next(grp)
        return carry

    lax.fori_loop(0, n_groups // NA_GROUPS_PER_STEP, body, 0, unroll=NA_LOOP_UNROLL)


def _na_latent(p_l, p_c, slabs, b):
    t = p_l.shape[0] // b
    ctx = p_c.shape[0] // b
    hd = NA_HEAD_DIM
    return pl.pallas_call(
        functools.partial(_na_kernel, slab_index=_na_slab_index(t // GRID_W)),
        grid=(NA_HEADS, b),
        in_specs=[
            pl.BlockSpec((t, hd), lambda h, i: (i, _QA + h)),
            pl.BlockSpec((t, hd), lambda h, i: (i, _KA + h)),
            pl.BlockSpec((t, hd), lambda h, i: (i, _VA + h)),
            pl.BlockSpec((ctx, hd), lambda h, i: (i, _KA + h)),
            pl.BlockSpec((ctx, hd), lambda h, i: (i, _VA + h)),
            pl.BlockSpec((1,) + slabs.shape[1:], lambda h, i: (h, 0, 0, 0)),
        ],
        out_specs=pl.BlockSpec((t, hd), lambda h, i: (i, h)),
        out_shape=jax.ShapeDtypeStruct((b * t, NA_WIDTH), BF16),
        scratch_shapes=[
            pltpu.VMEM((t, hd), BF16),
            pltpu.VMEM((t, hd), BF16),
            pltpu.VMEM((t // NA_GROUP_TOK, hd, NA_GROUP_TOK), BF16),
            pltpu.VMEM((3, NA_WIN_BLOCKS * NA_GROUP_TOK, NA_GROUP_TOK), F32),
        ],
        compiler_params=_cparams(("arbitrary", "arbitrary")),
        name="na_latent",
    )(p_l, p_l, p_l, p_c, p_c, slabs)


def _ctx_attn_kernel(q_ref, k_ref, v_ref, o_ref):
    scale = NA_HEAD_DIM ** -0.5
    q = q_ref[...].astype(BF16)
    k = k_ref[...].astype(BF16)
    v = v_ref[...].astype(BF16)
    s = lax.dot_general(q, k, (((1,), (1,)), ((), ())), preferred_element_type=F32) * scale
    p = jnp.exp(s - jnp.max(s, axis=-1, keepdims=True))
    l = jnp.sum(p, axis=-1, keepdims=True)
    o = jnp.dot(p.astype(BF16), v, preferred_element_type=F32)
    o_ref[...] = (o / l).astype(o_ref.dtype)


def _ctx_attention(p_c, b):
    ctx = p_c.shape[0] // b
    hd = NA_HEAD_DIM
    return pl.pallas_call(
        _ctx_attn_kernel,
        grid=(b, NA_HEADS),
        in_specs=[
            pl.BlockSpec((ctx, hd), lambda i, h: (i, _QA + h)),
            pl.BlockSpec((ctx, hd), lambda i, h: (i, _KA + h)),
            pl.BlockSpec((ctx, hd), lambda i, h: (i, _VA + h)),
        ],
        out_specs=pl.BlockSpec((ctx, hd), lambda i, h: (i, h)),
        out_shape=jax.ShapeDtypeStruct((b * ctx, NA_WIDTH), BF16),
        compiler_params=_cparams(("arbitrary", "arbitrary")),
        name="ctx_attention",
    )(p_c, p_c, p_c)


def _gm_kernel(u_ref, v_ref, lnw_ref, ws_ref, bs_ref, o_ref):
    for ck in range(u_ref.shape[0] // GM_CHUNK):
        rows = slice(ck * GM_CHUNK, (ck + 1) * GM_CHUNK)
        uf = _gelu_tanh(u_ref[rows, :])
        vf = _gelu_tanh(v_ref[rows, :])
        for g in range(GM_GROUPS):
            sl = slice(g * GM_DIM, (g + 1) * GM_DIM)
            vg = vf[:, sl]
            mu = jnp.mean(vg, axis=-1, keepdims=True)
            dv = vg - mu
            var = jnp.mean(dv * dv, axis=-1, keepdims=True)
            vn = dv * lax.rsqrt(var + EPS) * lnw_ref[:, sl]
            mixed = jnp.dot(ws_ref[g].astype(BF16), vn.astype(BF16), preferred_element_type=F32) + bs_ref[g]
            o_ref[rows, sl] = (uf[:, sl] * mixed).astype(o_ref.dtype)


def _chunk_gmlp(p2, ln_w, ws, bs):
    m = p2.shape[0]
    blk = GM_BLOCK if m % GM_BLOCK == 0 else GM_CHUNK
    return pl.pallas_call(
        _gm_kernel,
        grid=(m // blk,),
        in_specs=[
            pl.BlockSpec((blk, GM_WIDTH), lambda c: (c, _GM_U)),
            pl.BlockSpec((blk, GM_WIDTH), lambda c: (c, _GM_V)),
            pl.BlockSpec((1, GM_WIDTH), lambda c: (0, 0)),
            pl.BlockSpec((GM_GROUPS, GM_CHUNK, GM_CHUNK), lambda c: (0, 0, 0)),
            pl.BlockSpec((GM_GROUPS, GM_CHUNK, 1), lambda c: (0, 0, 0)),
        ],
        out_specs=pl.BlockSpec((blk, GM_WIDTH), lambda c: (c, 0)),
        out_shape=jax.ShapeDtypeStruct((m, GM_WIDTH), BF16),
        compiler_params=_cparams(("arbitrary",)),
        name="chunk_gmlp",
    )(p2, p2, ln_w.reshape(1, GM_WIDTH), ws, bs.reshape(GM_GROUPS, GM_CHUNK, 1))


HG_PAIR = 2
HG_LEVELS = (32, 16, 8)
HG_DIAG = 8
HG_PREP_ROWS = 256
HG_SCAN_UNROLL = 8


def _split3_dot(tri, g):
    g0 = g.astype(BF16)
    r1 = g - g0.astype(F32)
    g1 = r1.astype(BF16)
    g2 = (r1 - g1.astype(F32)).astype(BF16)
    dot = functools.partial(jnp.dot, preferred_element_type=F32)
    return dot(tri, g0) + dot(tri, g1) + dot(tri, g2)


def _hg_kernel(ql_ref, ffl_ref, fbl_ref, il_ref, gl_ref,
               qc_ref, ffc_ref, fbc_ref, ic_ref, gc_ref,
               lbp_ref, nw_ref, tri_ref, code_ref, lsum_ref, ol_ref, oc_ref,
               qh_s, kf_s, kb_s, ef_s, eb_s, of_s, ob_s, st_s):
    c = HG_CHUNK
    hd = HG_DIM
    n_ctx = qc_ref.shape[0]
    n_lat = ql_ref.shape[0]
    scale = hd ** -0.5
    nt = (((1,), (1,)), ((), ()))
    tn = (((0,), (0,)), ((), ()))

    def prep(q_ref, ff_ref, fb_ref, base, n):
        step = HG_PREP_ROWS
        for t0 in range(0, n, step):
            src = slice(t0, t0 + step)
            dst = slice(base + t0, base + t0 + step)
            qh = _silu(q_ref[src, :]) * scale
            for h in range(HG_PAIR):
                qh_s[h, dst, :] = qh[:, h * hd:(h + 1) * hd]
            for d, (f_ref, k_s, e_s) in enumerate(((ff_ref, kf_s, ef_s), (fb_ref, kb_s, eb_s))):
                x = f_ref[src, :]
                sp = jnp.maximum(-x, 0.0) + jnp.log(1.0 + jnp.exp(-jnp.abs(x)))
                log_lb = lbp_ref[d, 0:1, :]
                y = lbp_ref[d, 1:2, :] - sp
                mx = jnp.maximum(log_lb, y)
                log_f = mx + jnp.log(1.0 + jnp.exp(-jnp.abs(log_lb - y)))
                one_minus_f = lbp_ref[d, 2:3, :] * jnp.exp(-(sp + x))
                e = _split3_dot(tri_ref[d], log_f) * LOG2E
                for h in range(HG_PAIR):
                    e_s[h, dst, :] = e[:, h * hd:(h + 1) * hd]
                    k_s[h, dst, :] = one_minus_f[:, h * hd:(h + 1) * hd]

    prep(qc_ref, ffc_ref, fbc_ref, 0, n_ctx)
    prep(ql_ref, ffl_ref, fbl_ref, n_ctx, n_lat)

    diag_code = len(HG_LEVELS) + 1

    st_s[...] = jnp.zeros_like(st_s)

    def chunk(off, v, h, d, out):
        fwd = d == 0
        k_s, e_s = (kf_s, ef_s) if fwd else (kb_s, eb_s)
        q = qh_s[h, pl.ds(off, c), :]
        k = k_s[h, pl.ds(off, c), :]
        e = e_s[h, pl.ds(off, c), :]
        code = code_ref[d]
        prods = []
        for i in range(c // HG_DIAG):
            bs = slice(i * HG_DIAG, (i + 1) * HG_DIAG)
            row_prods = []
            for s in range(HG_DIAG):
                kr = k_s[h, pl.ds(off + i * HG_DIAG + s, 1), :]
                er = e_s[h, pl.ds(off + i * HG_DIAG + s, 1), :]
                decay = jnp.exp2(jnp.minimum(e[bs] - er, 0.0))
                row_prods.append((q[bs] * kr * decay).astype(BF16))
            prods.append(jnp.concatenate(row_prods, axis=1))
        diag = jnp.dot(jnp.concatenate(prods, axis=0), lsum_ref[...], preferred_element_type=F32)
        yield
        att = None
        for li, w in enumerate(HG_LEVELS):
            zeros = jnp.zeros((w, hd), F32)
            q_parts, k_parts = [], []
            for a in range(0, c, 2 * w):
                lo, hi = slice(a, a + w), slice(a + w, a + 2 * w)
                if fwd:
                    ref = e_s[h, pl.ds(off + a + w - 1, 1), :]
                    q_parts += [zeros, q[hi] * jnp.exp2(e[hi] - ref)]
                    k_parts += [k[lo] * jnp.exp2(ref - e[lo]), zeros]
                else:
                    ref = e_s[h, pl.ds(off + a + w, 1), :]
                    q_parts += [q[lo] * jnp.exp2(e[lo] - ref), zeros]
                    k_parts += [zeros, k[hi] * jnp.exp2(ref - e[hi])]
            qs = jnp.concatenate(q_parts, axis=0).astype(BF16)
            ks = jnp.concatenate(k_parts, axis=0).astype(BF16)
            a_w = lax.dot_general(qs, ks, nt, preferred_element_type=F32)
            att = a_w if att is None else jnp.where(code == li + 1, a_w, att)
            yield
        e_end = e_s[h, pl.ds(off + (c - 1 if fwd else 0), 1), :]
        st = st_s[h, d]
        qi = (q * jnp.exp2(e)).astype(BF16)
        ki = (k * jnp.exp2(e_end - e)).astype(BF16)
        o = lax.dot_general(qi, st.astype(BF16), nt, preferred_element_type=F32)
        yield
        st_new = st * jnp.exp2(e_end) + lax.dot_general(v, ki, tn, preferred_element_type=F32)
        yield
        att = jnp.where(code == diag_code, diag, att)
        out.append((o + jnp.dot(att.astype(BF16), v, preferred_element_type=F32), st_new))
        yield

    n_stages = len(HG_LEVELS) + 4

    def scan(v_ref, base, n):
        def body(i, carry):
            rf = pl.multiple_of(i * c, c)
            rb = pl.multiple_of((n - 1 - i) * c, c)
            vf = v_ref[pl.ds(rf, c), :].astype(BF16)
            vb = v_ref[pl.ds(rb, c), :].astype(BF16)
            chains = []
            for h in range(HG_PAIR):
                hs = slice(h * hd, (h + 1) * hd)
                for d, r, v in ((0, rf, vf), (1, rb, vb)):
                    out = []
                    chains.append((h, d, r, out, chunk(base + r, v[:, hs], h, d, out)))
            for _ in range(n_stages):
                for chain in chains:
                    next(chain[-1])
            for h, d, r, out, _ in chains:
                o, st_new = out[0]
                (of_s if d == 0 else ob_s)[h, pl.ds(base + r, c), :] = o
                st_s[h, d] = st_new
            return carry

        lax.fori_loop(0, n, body, 0, unroll=HG_SCAN_UNROLL)

    scan(ic_ref, 0, n_ctx // c)
    scan(il_ref, n_ctx, n_lat // c)

    def finish(g_ref, o_ref, base, n):
        step = 256
        for t0 in range(0, n, step):
            src = slice(t0, t0 + step)
            dst = slice(base + t0, base + t0 + step)
            for h in range(HG_PAIR):
                hs = slice(h * hd, (h + 1) * hd)
                o = of_s[h, dst, :] + ob_s[h, dst, :]
                o = o * lax.rsqrt(jnp.mean(o * o, axis=-1, keepdims=True) + EPS) * nw_ref[...]
                o_ref[src, hs] = (o * _silu(g_ref[src, hs])).astype(o_ref.dtype)

    finish(gc_ref, oc_ref, 0, n_ctx)
    finish(gl_ref, ol_ref, n_ctx, n_lat)


def _hg_constants():
    c = HG_CHUNK
    row, col = np.arange(c)[:, None], np.arange(c)[None, :]
    blocks = np.eye(HG_PREP_ROWS // c)
    tri = np.stack([np.kron(blocks, col <= row), np.kron(blocks, col >= row)]).astype(np.float32)
    same8 = (row // HG_DIAG) == (col // HG_DIAG)
    diag_code = len(HG_LEVELS) + 1
    code_f = np.where(same8 & (col <= row), diag_code, 0)
    code_b = np.where(same8 & (col >= row), diag_code, 0)
    for li, w in enumerate(HG_LEVELS):
        same = (row // (2 * w)) == (col // (2 * w))
        t_hi, s_hi = (row % (2 * w)) >= w, (col % (2 * w)) >= w
        code_f = np.where(same & t_hi & ~s_hi, li + 1, code_f)
        code_b = np.where(same & ~t_hi & s_hi, li + 1, code_b)
    codes = np.stack([code_f, code_b]).astype(np.int32)
    lane_sum = (np.arange(HG_DIAG * HG_DIM)[:, None] // HG_DIM == col % HG_DIAG).astype(np.float32)
    return jnp.asarray(tri, BF16), jnp.asarray(codes), jnp.asarray(lane_sum, BF16)


def _hgrn2(p_l, p_c, lbp, norm_w, b):
    tri, codes, lane_sum = _hg_constants()
    t = p_l.shape[0] // b
    ctx = p_c.shape[0] // b
    hd = HG_DIM
    pw = HG_PAIR * hd
    n = t + ctx
    first = _HG0 * LANE // pw

    def col(stream):
        return lambda i, j: (i, first + stream * (HG_HEADS // HG_PAIR) + j)

    lat_specs = [pl.BlockSpec((t, pw), col(s)) for s in range(5)]
    ctx_specs = [pl.BlockSpec((ctx, pw), col(s)) for s in range(5)]
    big = lambda: pltpu.VMEM((HG_PAIR, n, hd), F32)
    return pl.pallas_call(
        _hg_kernel,
        grid=(b, HG_HEADS // HG_PAIR),
        in_specs=lat_specs + ctx_specs + [
            pl.BlockSpec((2, 3, pw), lambda i, j: (0, 0, j)),
            pl.BlockSpec((1, hd), lambda i, j: (0, 0)),
            pl.BlockSpec(tri.shape, lambda i, j: (0, 0, 0)),
            pl.BlockSpec(codes.shape, lambda i, j: (0, 0, 0)),
            pl.BlockSpec(lane_sum.shape, lambda i, j: (0, 0)),
        ],
        out_specs=[
            pl.BlockSpec((t, pw), lambda i, j: (i, j)),
            pl.BlockSpec((ctx, pw), lambda i, j: (i, j)),
        ],
        out_shape=[
            jax.ShapeDtypeStruct((b * t, HG_WIDTH), BF16),
            jax.ShapeDtypeStruct((b * ctx, HG_WIDTH), BF16),
        ],
        scratch_shapes=[big() for _ in range(7)] + [pltpu.VMEM((HG_PAIR, 2, hd, hd), F32)],
        compiler_params=_cparams(("arbitrary", "arbitrary")),
        name="hgrn2",
    )(*([p_l] * 5 + [p_c] * 5 + [lbp, norm_w.reshape(1, hd), tri, codes, lane_sum]))


def _outproj_kernel(oa_ref, ob_ref, oc_ref, wa_ref, wb_ref, wc_ref, x_ref, g_ref, o_ref):
    dot = functools.partial(jnp.dot, preferred_element_type=F32)
    y = dot(oa_ref[...], wa_ref[...]) + dot(ob_ref[...], wb_ref[...]) + dot(oc_ref[...], wc_ref[...])
    o_ref[...] = x_ref[...] + g_ref[0] * y


def _outproj(oa, ob, oc, w, x2d, mod, layer, rows_per_mod, mod_row0, tm):
    m, d = x2d.shape
    tn = OUTPROJ_TN
    mrow = lambda i: layer * MOD_ROWS + mod_row0 + (i * tm) // rows_per_mod
    nb = NA_WIDTH // HG_WIDTH
    return pl.pallas_call(
        _outproj_kernel,
        grid=(m // tm, d // tn),
        in_specs=[
            pl.BlockSpec((tm, NA_WIDTH), lambda i, j: (i, 0)),
            pl.BlockSpec((tm, HG_WIDTH), lambda i, j: (i, 0)),
            pl.BlockSpec((tm, GM_WIDTH), lambda i, j: (i, 0)),
            pl.BlockSpec((None, NA_WIDTH, tn), lambda i, j: (layer, 0, j)),
            pl.BlockSpec((None, HG_WIDTH, tn), lambda i, j: (layer, nb, j)),
            pl.BlockSpec((None, GM_WIDTH, tn), lambda i, j: (layer, nb + 1, j)),
            pl.BlockSpec((tm, tn), lambda i, j: (i, j)),
            pl.BlockSpec((1, 1, tn), lambda i, j: (mrow(i), 0, 2 * (d // tn) + j)),
        ],
        out_specs=pl.BlockSpec((tm, tn), lambda i, j: (i, j)),
        out_shape=jax.ShapeDtypeStruct((m, d), F32),
        compiler_params=_cparams(("arbitrary", "arbitrary")),
        name="outproj",
    )(oa, ob, oc, w, w, w, x2d, mod)


def _mlp_kernel(x_ref, nw_ref, sh_ref, sc_ref, g_ref, w1_ref, w2_ref, fnw_ref, o_ref, h_ref, acc_ref,
                *, final_norm):
    j = pl.program_id(1)

    @pl.when(j == 0)
    def _():
        _norm_modulate_store(x_ref, nw_ref, sh_ref, sc_ref, h_ref)
        acc_ref[...] = jnp.zeros_like(acc_ref)

    a = jnp.maximum(jnp.dot(h_ref[...], w1_ref[...], preferred_element_type=F32), 0.0)
    acc_ref[...] += jnp.dot((a * a).astype(BF16), w2_ref[...], preferred_element_type=F32)

    @pl.when(j == pl.num_programs(1) - 1)
    def _():
        y = x_ref[...] + g_ref[0] * acc_ref[...]
        if final_norm:
            y = y * lax.rsqrt(jnp.mean(y * y, axis=-1, keepdims=True) + EPS) * fnw_ref[...]
        o_ref[...] = y


def _mlp(x2d, nw, mod, w1, w2, fnw, layer, rows_per_mod, mod_row0, tm, final_norm):
    m, d = x2d.shape
    hid = w1.shape[2]
    th = MLP_TH
    mrow = lambda i: layer * MOD_ROWS + mod_row0 + (i * tm) // rows_per_mod
    return pl.pallas_call(
        functools.partial(_mlp_kernel, final_norm=final_norm),
        grid=(m // tm, hid // th),
        in_specs=[
            pl.BlockSpec((tm, d), lambda i, j: (i, 0)),
            pl.BlockSpec((None, 1, d), lambda i, j: (layer, 0, 0)),
            pl.BlockSpec((1, 1, d), lambda i, j: (mrow(i), 0, 3)),
            pl.BlockSpec((1, 1, d), lambda i, j: (mrow(i), 0, 4)),
            pl.BlockSpec((1, 1, d), lambda i, j: (mrow(i), 0, 5)),
            pl.BlockSpec((None, d, th), lambda i, j: (layer, 0, j)),
            pl.BlockSpec((None, th, d), lambda i, j: (layer, j, 0)),
            pl.BlockSpec((1, d), lambda i, j: (0, 0)),
        ],
        out_specs=pl.BlockSpec((tm, d), lambda i, j: (i, 0)),
        out_shape=jax.ShapeDtypeStruct((m, d), F32),
        scratch_shapes=[pltpu.VMEM((tm, d), BF16), pltpu.VMEM((tm, d), F32)],
        compiler_params=_cparams(("arbitrary", "arbitrary")),
        name="mlp",
    )(x2d, nw, mod, mod, mod, w1, w2, fnw)


def kernel(x, c, ctx, c_ctx, ada_w, ada_b, norm1_w, norm2_w, w_in, na_rpb, hg_lb_logits, hg_norm_w,
           gm_ln_w, gm_ws, gm_bs, w_out, mlp_w1, mlp_w2, final_norm_w):
    bsz, seq, d = x.shape
    n_ctx = ctx.shape[1]
    depth = ada_w.shape[0]
    assert bsz < MOD_ROWS and d == D_MODEL and seq % 512 == 0 and n_ctx % 256 == 0
    assert seq // NA_GROUP_TOK >= NA_WIN_BLOCKS and seq // GRID_W >= 2 * NA_WIN_ROWS

    lb = jnp.cumsum(jax.nn.softmax(hg_lb_logits.astype(F32), axis=0), axis=0)
    lb = lb - lb[:1]
    lbp = jnp.stack([jnp.log(lb), jnp.log1p(-lb), 1.0 - lb], axis=2)

    cond = jnp.zeros((MOD_ROWS, d), F32).at[:bsz].set(c).at[bsz].set(c_ctx)
    mod = _ada(cond, ada_w, ada_b).reshape(depth * MOD_ROWS, 1, 6 * d)

    w_in_b, w_out_b = w_in.astype(BF16), w_out.astype(BF16)
    w1_b, w2_b = mlp_w1.astype(BF16), mlp_w2.astype(BF16)
    nw1 = norm1_w.reshape(depth, 1, d)
    nw2 = norm2_w.reshape(depth, 1, d)
    fnw = final_norm_w.reshape(1, d)

    xl = x.reshape(bsz * seq, d)
    xc = ctx.reshape(bsz * n_ctx, d)
    n_c = bsz * n_ctx
    tm_l, tm_c = TM_LATENT, min(TM_CONTEXT, n_c)
    for l in range(depth):
        need_ctx = l < depth - 1
        p_l = _inproj(xl, nw1, mod, w_in_b, l, seq, 0, TM_INPROJ)
        p_c = _inproj(xc, nw1, mod, w_in_b, l, n_c, bsz, tm_c)

        oa_l = _na_latent(p_l, p_c, _na_bias_slabs(na_rpb[l]), bsz)
        ob_l, ob_c = _hgrn2(p_l, p_c, lbp[l], hg_norm_w[l], bsz)
        oc_l = _chunk_gmlp(p_l, gm_ln_w[l], gm_ws[l], gm_bs[l])
        xl = _outproj(oa_l, ob_l, oc_l, w_out_b, xl, mod, l, seq, 0, tm_l)
        xl = _mlp(xl, nw2, mod, w1_b, w2_b, fnw, l, seq, 0, tm_l, final_norm=not need_ctx)
        if need_ctx:
            oa_c = _ctx_attention(p_c, bsz)
            oc_c = _chunk_gmlp(p_c, gm_ln_w[l], gm_ws[l], gm_bs[l])
            xc = _outproj(oa_c, ob_c, oc_c, w_out_b, xc, mod, l, n_c, bsz, tm_c)
            xc = _mlp(xc, nw2, mod, w1_b, w2_b, fnw, l, n_c, bsz, tm_c, final_norm=False)
    return xl.reshape(bsz, seq, d)
```

```python
import functools

import numpy as np
import jax
import jax.numpy as jnp
from jax import lax
from jax.experimental import pallas as pl
from jax.experimental.pallas import tpu as pltpu

F32 = jnp.float32
BF16 = jnp.bfloat16

D_MODEL = 2048
DEPTH = 2
GRID_W = 64
EPS = 1e-6

NA_HEAD_DIM = 128
NA_HEADS = 8
NA_WIDTH = NA_HEADS * NA_HEAD_DIM
NA_WIN_ROWS = 8
NA_WIN_COLS = 16

HG_HEADS = 4
HG_DIM = 128
HG_WIDTH = HG_HEADS * HG_DIM
HG_CHUNK = 64

GM_GROUPS = 4
GM_DIM = 128
GM_WIDTH = GM_GROUPS * GM_DIM
GM_CHUNK = 128
GM_BLOCK = 4 * GM_CHUNK

IN_WIDTH = 3 * NA_WIDTH + 5 * HG_WIDTH + 2 * GM_WIDTH
MLP_HIDDEN = 4 * D_MODEL
LANE = 128
LOG2E = 1.4426950408889634

_QA, _KA, _VA = 0, NA_HEADS, 2 * NA_HEADS
_HG0 = 3 * NA_HEADS
_GM_U = (3 * NA_WIDTH + 5 * HG_WIDTH) // GM_WIDTH
_GM_V = _GM_U + 1

VMEM_LIMIT_V7X = 56 * 1024 * 1024
TM_LATENT = 512
TM_CONTEXT = 512
TM_INPROJ = 1024
INPROJ_TN = 1664
MLP_TH = 1024
OUTPROJ_TN = 2048
NORM_CHUNK_ROWS = 32
NORM_UNROLL = 4
MOD_ROWS = 16


def _cparams(sem):
    return pltpu.CompilerParams(dimension_semantics=sem, vmem_limit_bytes=VMEM_LIMIT_V7X)


def _silu(x):
    h = 0.5 * x
    return h + h * jnp.tanh(h)


def _gelu_tanh(x):
    return 0.5 * x * (1.0 + jnp.tanh(0.7978845608028654 * (x + 0.044715 * (x * x * x))))


def _norm_modulate_store(x_ref, nw_ref, sh_ref, sc_ref, h_ref):
    gain = nw_ref[...] * (1.0 + sc_ref[0])
    shift = sh_ref[0]

    def body(i, carry):
        rows = pl.ds(pl.multiple_of(i * NORM_CHUNK_ROWS, NORM_CHUNK_ROWS), NORM_CHUNK_ROWS)
        x = x_ref[rows, :]
        rs = lax.rsqrt(jnp.mean(x * x, axis=-1, keepdims=True) + EPS)
        h_ref[rows, :] = (x * rs * gain + shift).astype(h_ref.dtype)
        return carry

    lax.fori_loop(0, x_ref.shape[0] // NORM_CHUNK_ROWS, body, 0, unroll=NORM_UNROLL)


def _ada_kernel(c_ref, w_ref, b_ref, o_ref):
    s = _silu(c_ref[...]).astype(BF16)
    o_ref[0] = jnp.dot(s, w_ref[0].astype(BF16), preferred_element_type=F32) + b_ref[0]


def _ada(cond, ada_w, ada_b):
    depth, d, n = ada_w.shape
    tn = 1536
    return pl.pallas_call(
        _ada_kernel,
        grid=(depth, n // tn),
        in_specs=[
            pl.BlockSpec((cond.shape[0], d), lambda l, j: (0, 0)),
            pl.BlockSpec((1, d, tn), lambda l, j: (l, 0, j)),
            pl.BlockSpec((1, 1, tn), lambda l, j: (l, 0, j)),
        ],
        out_specs=pl.BlockSpec((1, cond.shape[0], tn), lambda l, j: (l, 0, j)),
        out_shape=jax.ShapeDtypeStruct((depth, cond.shape[0], n), F32),
        compiler_params=_cparams(("arbitrary", "arbitrary")),
        name="ada",
    )(cond, ada_w, ada_b.reshape(depth, 1, n))


def _inproj_kernel(x_ref, nw_ref, sh_ref, sc_ref, w_ref, o_ref, h_ref):
    @pl.when(pl.program_id(1) == 0)
    def _():
        _norm_modulate_store(x_ref, nw_ref, sh_ref, sc_ref, h_ref)

    o_ref[...] = jnp.dot(h_ref[...], w_ref[...], preferred_element_type=F32)


def _inproj(x2d, nw, mod, w, layer, rows_per_mod, mod_row0, tm):
    m, d = x2d.shape
    n = w.shape[2]
    tn = INPROJ_TN
    mrow = lambda i: layer * MOD_ROWS + mod_row0 + (i * tm) // rows_per_mod
    return pl.pallas_call(
        _inproj_kernel,
        grid=(m // tm, n // tn),
        in_specs=[
            pl.BlockSpec((tm, d), lambda i, j: (i, 0)),
            pl.BlockSpec((None, 1, d), lambda i, j: (layer, 0, 0)),
            pl.BlockSpec((1, 1, d), lambda i, j: (mrow(i), 0, 0)),
            pl.BlockSpec((1, 1, d), lambda i, j: (mrow(i), 0, 1)),
            pl.BlockSpec((None, d, tn), lambda i, j: (layer, 0, j)),
        ],
        out_specs=pl.BlockSpec((tm, tn), lambda i, j: (i, j)),
        out_shape=jax.ShapeDtypeStruct((m, n), F32),
        scratch_shapes=[pltpu.VMEM((tm, d), BF16)],
        compiler_params=_cparams(("arbitrary", "arbitrary")),
        name="inproj",
    )(x2d, nw, mod, mod, w)


NA_GROUP_ROWS = 4
NA_GROUP_TOK = NA_GROUP_ROWS * GRID_W
NA_WIN_BLOCKS = 3
NA_GROUPS_PER_STEP = 4
NA_LOOP_UNROLL = 2


NA_MASKED_SLAB = 2 * NA_WIN_ROWS - 1


def _na_bias_slabs(rpb):
    h = rpb.shape[0]
    kcol = np.arange(GRID_W)[:, None]
    qcol = np.arange(GRID_W)[None, :]
    wstart = np.clip(qcol - NA_WIN_COLS // 2, 0, GRID_W - NA_WIN_COLS)
    valid_col = (kcol >= wstart) & (kcol < wstart + NA_WIN_COLS)
    pad = GRID_W - NA_WIN_COLS
    padded = jnp.pad(rpb.astype(F32), ((0, 0), (0, 0), (pad, pad)))
    m = jnp.tile(padded, (1, 1, GRID_W + 1))[:, :, :GRID_W * 2 * GRID_W]
    m = m.reshape(h, rpb.shape[1], GRID_W, 2 * GRID_W)[..., :GRID_W]
    slabs = jnp.where(valid_col, m[..., ::-1] * LOG2E, -1e30)
    slabs = jnp.concatenate([slabs, jnp.full((h, 1, GRID_W, GRID_W), -1e30, F32)], axis=1)
    return jnp.concatenate([slabs, slabs], axis=-1)


def _na_slab_index(rows):
    n_groups = rows // NA_GROUP_ROWS
    krel = np.arange(NA_WIN_BLOCKS * NA_GROUP_ROWS)[:, None]
    qrel = np.arange(NA_GROUP_ROWS)[None, :]
    idx = []
    for g in (0, 1, n_groups - 1):
        krow = int(np.clip(g - 1, 0, n_groups - NA_WIN_BLOCKS)) * NA_GROUP_ROWS + krel
        qrow = g * NA_GROUP_ROWS + qrel
        r0 = np.clip(qrow - NA_WIN_ROWS // 2, 0, rows - NA_WIN_ROWS)
        in_window = (krow >= r0) & (krow < r0 + NA_WIN_ROWS)
        idx.append(np.where(in_window, krow - qrow + NA_WIN_ROWS - 1, NA_MASKED_SLAB))
    return np.stack(idx).tolist()


def _na_kernel(q_ref, k_ref, v_ref, kc_ref, vc_ref, slab_ref, o_ref, qb_ref, kb_ref, vt_ref, bias_ref,
               *, slab_index):
    n_groups = q_ref.shape[0] // NA_GROUP_TOK
    gt = NA_GROUP_TOK
    scale2 = NA_HEAD_DIM ** -0.5 * LOG2E
    nt = (((1,), (1,)), ((), ()))
    dot = functools.partial(jnp.dot, preferred_element_type=F32)

    @pl.when(pl.program_id(1) == 0)
    def _():
        left = lax.broadcasted_iota(jnp.int32, (GRID_W, 2 * GRID_W), 1) < GRID_W
        for kind, per_key_row in enumerate(slab_index):
            for j, per_query_row in enumerate(per_key_row):
                for a in range(0, NA_GROUP_ROWS, 2):
                    tile = jnp.where(left, slab_ref[0, per_query_row[a]], slab_ref[0, per_query_row[a + 1]])
                    bias_ref[kind, j * GRID_W:(j + 1) * GRID_W, a * GRID_W:(a + 2) * GRID_W] = tile

    qb_ref[...] = (q_ref[...] * scale2).astype(BF16)
    kb_ref[...] = k_ref[...].astype(BF16)
    for b in range(n_groups):
        vt_ref[b] = v_ref[b * gt:(b + 1) * gt, :].T.astype(BF16)
    kc = kc_ref[...].astype(BF16)
    vct = vc_ref[...].T.astype(BF16)

    def group(g):
        blk = jnp.clip(g - 1, 0, n_groups - NA_WIN_BLOCKS)
        kind = jnp.where(g == 0, 0, jnp.where(g == n_groups - 1, 2, 1))
        qg = qb_ref[pl.ds(pl.multiple_of(g * gt, gt), gt), :]
        kw = kb_ref[pl.ds(pl.multiple_of(blk * gt, gt), NA_WIN_BLOCKS * gt), :]
        s_lat = lax.dot_general(kw, qg, nt, preferred_element_type=F32) + bias_ref[kind]
        s_ctx = lax.dot_general(kc, qg, nt, preferred_element_type=F32)
        yield
        m = jnp.maximum(jnp.max(s_lat, axis=0, keepdims=True), jnp.max(s_ctx, axis=0, keepdims=True))
        p_lat = jnp.exp2(s_lat - m)
        p_ctx = jnp.exp2(s_ctx - m)
        l = jnp.sum(p_lat, axis=0, keepdims=True) + jnp.sum(p_ctx, axis=0, keepdims=True)
        yield
        ot = dot(vct, p_ctx.astype(BF16))
        for j in range(NA_WIN_BLOCKS):
            ot = ot + dot(vt_ref[blk + j], p_lat[j * gt:(j + 1) * gt, :].astype(BF16))
        yield
        o_ref[pl.ds(pl.multiple_of(g * gt, gt), gt), :] = (ot / l).T.astype(o_ref.dtype)
        yield

    def body(i, carry):
        groups = [group(i * NA_GROUPS_PER_STEP + u) for u in range(NA_GROUPS_PER_STEP)]
        for _ in range(4):
            for grp in groups:
                next(grp)
        return carry

    lax.fori_loop(0, n_groups // NA_GROUPS_PER_STEP, body, 0, unroll=NA_LOOP_UNROLL)


def _na_latent(p_l, p_c, slabs, b):
    t = p_l.shape[0] // b
    ctx = p_c.shape[0] // b
    hd = NA_HEAD_DIM
    return pl.pallas_call(
        functools.partial(_na_kernel, slab_index=_na_slab_index(t // GRID_W)),
        grid=(NA_HEADS, b),
        in_specs=[
            pl.BlockSpec((t, hd), lambda h, i: (i, _QA + h)),
            pl.BlockSpec((t, hd), lambda h, i: (i, _KA + h)),
            pl.BlockSpec((t, hd), lambda h, i: (i, _VA + h)),
            pl.BlockSpec((ctx, hd), lambda h, i: (i, _KA + h)),
            pl.BlockSpec((ctx, hd), lambda h, i: (i, _VA + h)),
            pl.BlockSpec((1,) + slabs.shape[1:], lambda h, i: (h, 0, 0, 0)),
        ],
        out_specs=pl.BlockSpec((t, hd), lambda h, i: (i, h)),
        out_shape=jax.ShapeDtypeStruct((b * t, NA_WIDTH), BF16),
        scratch_shapes=[
            pltpu.VMEM((t, hd), BF16),
            pltpu.VMEM((t, hd), BF16),
            pltpu.VMEM((t // NA_GROUP_TOK, hd, NA_GROUP_TOK), BF16),
            pltpu.VMEM((3, NA_WIN_BLOCKS * NA_GROUP_TOK, NA_GROUP_TOK), F32),
        ],
        compiler_params=_cparams(("arbitrary", "arbitrary")),
        name="na_latent",
    )(p_l, p_l, p_l, p_c, p_c, slabs)


def _ctx_attn_kernel(q_ref, k_ref, v_ref, o_ref):
    scale = NA_HEAD_DIM ** -0.5
    for h in range(NA_HEADS):
        hs = slice(h * NA_HEAD_DIM, (h + 1) * NA_HEAD_DIM)
        q = q_ref[:, hs].astype(BF16)
        k = k_ref[:, hs].astype(BF16)
        v = v_ref[:, hs].astype(BF16)
        s = lax.dot_general(q, k, (((1,), (1,)), ((), ())), preferred_element_type=F32) * scale
        p = jnp.exp(s - jnp.max(s, axis=-1, keepdims=True))
        l = jnp.sum(p, axis=-1, keepdims=True)
        o = jnp.dot(p.astype(BF16), v, preferred_element_type=F32)
        o_ref[:, hs] = (o / l).astype(o_ref.dtype)


def _ctx_attention(p_c, b):
    ctx = p_c.shape[0] // b
    return pl.pallas_call(
        _ctx_attn_kernel,
        grid=(b,),
        in_specs=[
            pl.BlockSpec((ctx, NA_WIDTH), lambda i: (i, 0)),
            pl.BlockSpec((ctx, NA_WIDTH), lambda i: (i, 1)),
            pl.BlockSpec((ctx, NA_WIDTH), lambda i: (i, 2)),
        ],
        out_specs=pl.BlockSpec((ctx, NA_WIDTH), lambda i: (i, 0)),
        out_shape=jax.ShapeDtypeStruct((b * ctx, NA_WIDTH), BF16),
        compiler_params=_cparams(("arbitrary",)),
        name="ctx_attention",
    )(p_c, p_c, p_c)


def _gm_kernel(u_ref, v_ref, lnw_ref, ws_ref, bs_ref, o_ref):
    for ck in range(u_ref.shape[0] // GM_CHUNK):
        rows = slice(ck * GM_CHUNK, (ck + 1) * GM_CHUNK)
        uf = _gelu_tanh(u_ref[rows, :])
        vf = _gelu_tanh(v_ref[rows, :])
        for g in range(GM_GROUPS):
            sl = slice(g * GM_DIM, (g + 1) * GM_DIM)
            vg = vf[:, sl]
            mu = jnp.mean(vg, axis=-1, keepdims=True)
            dv = vg - mu
            var = jnp.mean(dv * dv, axis=-1, keepdims=True)
            vn = dv * lax.rsqrt(var + EPS) * lnw_ref[:, sl]
            mixed = jnp.dot(ws_ref[g].astype(BF16), vn.astype(BF16), preferred_element_type=F32) + bs_ref[g]
            o_ref[rows, sl] = (uf[:, sl] * mixed).astype(o_ref.dtype)


def _chunk_gmlp(p2, ln_w, ws, bs):
    m = p2.shape[0]
    blk = GM_BLOCK if m % GM_BLOCK == 0 else GM_CHUNK
    return pl.pallas_call(
        _gm_kernel,
        grid=(m // blk,),
        in_specs=[
            pl.BlockSpec((blk, GM_WIDTH), lambda c: (c, _GM_U)),
            pl.BlockSpec((blk, GM_WIDTH), lambda c: (c, _GM_V)),
            pl.BlockSpec((1, GM_WIDTH), lambda c: (0, 0)),
            pl.BlockSpec((GM_GROUPS, GM_CHUNK, GM_CHUNK), lambda c: (0, 0, 0)),
            pl.BlockSpec((GM_GROUPS, GM_CHUNK, 1), lambda c: (0, 0, 0)),
        ],
        out_specs=pl.BlockSpec((blk, GM_WIDTH), lambda c: (c, 0)),
        out_shape=jax.ShapeDtypeStruct((m, GM_WIDTH), BF16),
        compiler_params=_cparams(("arbitrary",)),
        name="chunk_gmlp",
    )(p2, p2, ln_w.reshape(1, GM_WIDTH), ws, bs.reshape(GM_GROUPS, GM_CHUNK, 1))


HG_PAIR = 2
HG_LEVELS = (32, 16, 8)
HG_DIAG = 8
HG_PREP_ROWS = 256
HG_SCAN_UNROLL = 8


def _split3_dot(tri, g):
    g0 = g.astype(BF16)
    r1 = g - g0.astype(F32)
    g1 = r1.astype(BF16)
    g2 = (r1 - g1.astype(F32)).astype(BF16)
    dot = functools.partial(jnp.dot, preferred_element_type=F32)
    return dot(tri, g0) + dot(tri, g1) + dot(tri, g2)


def _hg_kernel(ql_ref, ffl_ref, fbl_ref, il_ref, gl_ref,
               qc_ref, ffc_ref, fbc_ref, ic_ref, gc_ref,
               lbp_ref, nw_ref, tri_ref, code_ref, lsum_ref, ol_ref, oc_ref,
               qh_s, kf_s, kb_s, ef_s, eb_s, of_s, ob_s, st_s):
    c = HG_CHUNK
    hd = HG_DIM
    n_ctx = qc_ref.shape[0]
    n_lat = ql_ref.shape[0]
    scale = hd ** -0.5
    nt = (((1,), (1,)), ((), ()))
    tn = (((0,), (0,)), ((), ()))

    def prep(q_ref, ff_ref, fb_ref, base, n):
        step = HG_PREP_ROWS
        for t0 in range(0, n, step):
            src = slice(t0, t0 + step)
            dst = slice(base + t0, base + t0 + step)
            qh = _silu(q_ref[src, :]) * scale
            for h in range(HG_PAIR):
                qh_s[h, dst, :] = qh[:, h * hd:(h + 1) * hd]
            for d, (f_ref, k_s, e_s) in enumerate(((ff_ref, kf_s, ef_s), (fb_ref, kb_s, eb_s))):
                x = f_ref[src, :]
                sp = jnp.maximum(-x, 0.0) + jnp.log(1.0 + jnp.exp(-jnp.abs(x)))
                log_lb = lbp_ref[d, 0:1, :]
                y = lbp_ref[d, 1:2, :] - sp
                mx = jnp.maximum(log_lb, y)
                log_f = mx + jnp.log(1.0 + jnp.exp(-jnp.abs(log_lb - y)))
                one_minus_f = lbp_ref[d, 2:3, :] * jnp.exp(-(sp + x))
                e = _split3_dot(tri_ref[d], log_f) * LOG2E
                for h in range(HG_PAIR):
                    e_s[h, dst, :] = e[:, h * hd:(h + 1) * hd]
                    k_s[h, dst, :] = one_minus_f[:, h * hd:(h + 1) * hd]

    prep(qc_ref, ffc_ref, fbc_ref, 0, n_ctx)
    prep(ql_ref, ffl_ref, fbl_ref, n_ctx, n_lat)

    diag_code = len(HG_LEVELS) + 1

    st_s[...] = jnp.zeros_like(st_s)

    def chunk(off, v, h, d, out):
        fwd = d == 0
        k_s, e_s = (kf_s, ef_s) if fwd else (kb_s, eb_s)
        q = qh_s[h, pl.ds(off, c), :]
        k = k_s[h, pl.ds(off, c), :]
        e = e_s[h, pl.ds(off, c), :]
        code = code_ref[d]
        prods = []
        for i in range(c // HG_DIAG):
            bs = slice(i * HG_DIAG, (i + 1) * HG_DIAG)
            row_prods = []
            for s in range(HG_DIAG):
                kr = k_s[h, pl.ds(off + i * HG_DIAG + s, 1), :]
                er = e_s[h, pl.ds(off + i * HG_DIAG + s, 1), :]
                decay = jnp.exp2(jnp.minimum(e[bs] - er, 0.0))
                row_prods.append((q[bs] * kr * decay).astype(BF16))
            prods.append(jnp.concatenate(row_prods, axis=1))
        diag = jnp.dot(jnp.concatenate(prods, axis=0), lsum_ref[...], preferred_element_type=F32)
        yield
        att = None
        for li, w in enumerate(HG_LEVELS):
            zeros = jnp.zeros((w, hd), F32)
            q_parts, k_parts = [], []
            for a in range(0, c, 2 * w):
                lo, hi = slice(a, a + w), slice(a + w, a + 2 * w)
                if fwd:
                    ref = e_s[h, pl.ds(off + a + w - 1, 1), :]
                    q_parts += [zeros, q[hi] * jnp.exp2(e[hi] - ref)]
                    k_parts += [k[lo] * jnp.exp2(ref - e[lo]), zeros]
                else:
                    ref = e_s[h, pl.ds(off + a + w, 1), :]
                    q_parts += [q[lo] * jnp.exp2(e[lo] - ref), zeros]
                    k_parts += [zeros, k[hi] * jnp.exp2(ref - e[hi])]
            qs = jnp.concatenate(q_parts, axis=0).astype(BF16)
            ks = jnp.concatenate(k_parts, axis=0).astype(BF16)
            a_w = lax.dot_general(qs, ks, nt, preferred_element_type=F32)
            att = a_w if att is None else jnp.where(code == li + 1, a_w, att)
            yield
        e_end = e_s[h, pl.ds(off + (c - 1 if fwd else 0), 1), :]
        st = st_s[h, d]
        qi = (q * jnp.exp2(e)).astype(BF16)
        ki = (k * jnp.exp2(e_end - e)).astype(BF16)
        o = lax.dot_general(qi, st.astype(BF16), nt, preferred_element_type=F32)
        yield
        st_new = st * jnp.exp2(e_end) + lax.dot_general(v, ki, tn, preferred_element_type=F32)
        yield
        att = jnp.where(code == diag_code, diag, att)
        out.append((o + jnp.dot(att.astype(BF16), v, preferred_element_type=F32), st_new))
        yield

    n_stages = len(HG_LEVELS) + 4

    def scan(v_ref, base, n):
        def body(i, carry):
            rf = pl.multiple_of(i * c, c)
            rb = pl.multiple_of((n - 1 - i) * c, c)
            vf = v_ref[pl.ds(rf, c), :].astype(BF16)
            vb = v_ref[pl.ds(rb, c), :].astype(BF16)
            chains = []
            for h in range(HG_PAIR):
                hs = slice(h * hd, (h + 1) * hd)
                for d, r, v in ((0, rf, vf), (1, rb, vb)):
                    out = []
                    chains.append((h, d, r, out, chunk(base + r, v[:, hs], h, d, out)))
            for _ in range(n_stages):
                for chain in chains:
                    next(chain[-1])
            for h, d, r, out, _ in chains:
                o, st_new = out[0]
                (of_s if d == 0 else ob_s)[h, pl.ds(base + r, c), :] = o
                st_s[h, d] = st_new
            return carry

        lax.fori_loop(0, n, body, 0, unroll=HG_SCAN_UNROLL)

    scan(ic_ref, 0, n_ctx // c)
    scan(il_ref, n_ctx, n_lat // c)

    def finish(g_ref, o_ref, base, n):
        step = 256
        for t0 in range(0, n, step):
            src = slice(t0, t0 + step)
            dst = slice(base + t0, base + t0 + step)
            for h in range(HG_PAIR):
                hs = slice(h * hd, (h + 1) * hd)
                o = of_s[h, dst, :] + ob_s[h, dst, :]
                o = o * lax.rsqrt(jnp.mean(o * o, axis=-1, keepdims=True) + EPS) * nw_ref[...]
                o_ref[src, hs] = (o * _silu(g_ref[src, hs])).astype(o_ref.dtype)

    finish(gc_ref, oc_ref, 0, n_ctx)
    finish(gl_ref, ol_ref, n_ctx, n_lat)


def _hg_constants():
    c = HG_CHUNK
    row, col = np.arange(c)[:, None], np.arange(c)[None, :]
    blocks = np.eye(HG_PREP_ROWS // c)
    tri = np.stack([np.kron(blocks, col <= row), np.kron(blocks, col >= row)]).astype(np.float32)
    same8 = (row // HG_DIAG) == (col // HG_DIAG)
    diag_code = len(HG_LEVELS) + 1
    code_f = np.where(same8 & (col <= row), diag_code, 0)
    code_b = np.where(same8 & (col >= row), diag_code, 0)
    for li, w in enumerate(HG_LEVELS):
        same = (row // (2 * w)) == (col // (2 * w))
        t_hi, s_hi = (row % (2 * w)) >= w, (col % (2 * w)) >= w
        code_f = np.where(same & t_hi & ~s_hi, li + 1, code_f)
        code_b = np.where(same & ~t_hi & s_hi, li + 1, code_b)
    codes = np.stack([code_f, code_b]).astype(np.int32)
    lane_sum = (np.arange(HG_DIAG * HG_DIM)[:, None] // HG_DIM == col % HG_DIAG).astype(np.float32)
    return jnp.asarray(tri, BF16), jnp.asarray(codes), jnp.asarray(lane_sum, BF16)


def _hgrn2(p_l, p_c, lbp, norm_w, b):
    tri, codes, lane_sum = _hg_constants()
    t = p_l.shape[0] // b
    ctx = p_c.shape[0] // b
    hd = HG_DIM
    pw = HG_PAIR * hd
    n = t + ctx
    first = _HG0 * LANE // pw

    def col(stream):
        return lambda i, j: (i, first + stream * (HG_HEADS // HG_PAIR) + j)

    lat_specs = [pl.BlockSpec((t, pw), col(s)) for s in range(5)]
    ctx_specs = [pl.BlockSpec((ctx, pw), col(s)) for s in range(5)]
    big = lambda: pltpu.VMEM((HG_PAIR, n, hd), F32)
    return pl.pallas_call(
        _hg_kernel,
        grid=(b, HG_HEADS // HG_PAIR),
        in_specs=lat_specs + ctx_specs + [
            pl.BlockSpec((2, 3, pw), lambda i, j: (0, 0, j)),
            pl.BlockSpec((1, hd), lambda i, j: (0, 0)),
            pl.BlockSpec(tri.shape, lambda i, j: (0, 0, 0)),
            pl.BlockSpec(codes.shape, lambda i, j: (0, 0, 0)),
            pl.BlockSpec(lane_sum.shape, lambda i, j: (0, 0)),
        ],
        out_specs=[
            pl.BlockSpec((t, pw), lambda i, j: (i, j)),
            pl.BlockSpec((ctx, pw), lambda i, j: (i, j)),
        ],
        out_shape=[
            jax.ShapeDtypeStruct((b * t, HG_WIDTH), BF16),
            jax.ShapeDtypeStruct((b * ctx, HG_WIDTH), BF16),
        ],
        scratch_shapes=[big() for _ in range(7)] + [pltpu.VMEM((HG_PAIR, 2, hd, hd), F32)],
        compiler_params=_cparams(("arbitrary", "arbitrary")),
        name="hgrn2",
    )(*([p_l] * 5 + [p_c] * 5 + [lbp, norm_w.reshape(1, hd), tri, codes, lane_sum]))


def _outproj_kernel(oa_ref, ob_ref, oc_ref, wa_ref, wb_ref, wc_ref, x_ref, g_ref, o_ref):
    dot = functools.partial(jnp.dot, preferred_element_type=F32)
    y = dot(oa_ref[...], wa_ref[...]) + dot(ob_ref[...], wb_ref[...]) + dot(oc_ref[...], wc_ref[...])
    o_ref[...] = x_ref[...] + g_ref[0] * y


def _outproj(oa, ob, oc, w, x2d, mod, layer, rows_per_mod, mod_row0, tm):
    m, d = x2d.shape
    tn = OUTPROJ_TN
    mrow = lambda i: layer * MOD_ROWS + mod_row0 + (i * tm) // rows_per_mod
    nb = NA_WIDTH // HG_WIDTH
    return pl.pallas_call(
        _outproj_kernel,
        grid=(m // tm, d // tn),
        in_specs=[
            pl.BlockSpec((tm, NA_WIDTH), lambda i, j: (i, 0)),
            pl.BlockSpec((tm, HG_WIDTH), lambda i, j: (i, 0)),
            pl.BlockSpec((tm, GM_WIDTH), lambda i, j: (i, 0)),
            pl.BlockSpec((None, NA_WIDTH, tn), lambda i, j: (layer, 0, j)),
            pl.BlockSpec((None, HG_WIDTH, tn), lambda i, j: (layer, nb, j)),
            pl.BlockSpec((None, GM_WIDTH, tn), lambda i, j: (layer, nb + 1, j)),
            pl.BlockSpec((tm, tn), lambda i, j: (i, j)),
            pl.BlockSpec((1, 1, tn), lambda i, j: (mrow(i), 0, 2 * (d // tn) + j)),
        ],
        out_specs=pl.BlockSpec((tm, tn), lambda i, j: (i, j)),
        out_shape=jax.ShapeDtypeStruct((m, d), F32),
        compiler_params=_cparams(("arbitrary", "arbitrary")),
        name="outproj",
    )(oa, ob, oc, w, w, w, x2d, mod)


def _mlp_kernel(x_ref, xn_ref, nw_ref, sh_ref, sc_ref, shn_ref, scn_ref, g_ref, w1_ref, w2_ref, fnw_ref,
                o_ref, h0_ref, h1_ref, acc_ref, *, final_norm):
    i = pl.program_id(0)
    j = pl.program_id(1)
    tm = x_ref.shape[0]

    @pl.when((i == 0) & (j == 0))
    def _():
        _norm_modulate_store(x_ref, nw_ref, sh_ref, sc_ref, h0_ref)

    @pl.when(j == 0)
    def _():
        acc_ref[...] = jnp.zeros_like(acc_ref)

    rows_per_step = tm // pl.num_programs(1)

    def step(h_cur, h_next):
        gain = nw_ref[...] * (1.0 + scn_ref[0])
        for ck in range(rows_per_step // NORM_CHUNK_ROWS):
            start = pl.multiple_of(j * rows_per_step + ck * NORM_CHUNK_ROWS, NORM_CHUNK_ROWS)
            xr = xn_ref[pl.ds(start, NORM_CHUNK_ROWS), :]
            rs = lax.rsqrt(jnp.mean(xr * xr, axis=-1, keepdims=True) + EPS)
            h_next[pl.ds(start, NORM_CHUNK_ROWS), :] = (xr * rs * gain + shn_ref[0]).astype(BF16)
        a = jnp.maximum(jnp.dot(h_cur[...], w1_ref[...], preferred_element_type=F32), 0.0)
        acc_ref[...] += jnp.dot((a * a).astype(BF16), w2_ref[...], preferred_element_type=F32)

    @pl.when(i % 2 == 0)
    def _():
        step(h0_ref, h1_ref)

    @pl.when(i % 2 == 1)
    def _():
        step(h1_ref, h0_ref)

    @pl.when(j == pl.num_programs(1) - 1)
    def _():
        y = x_ref[...] + g_ref[0] * acc_ref[...]
        if final_norm:
            y = y * lax.rsqrt(jnp.mean(y * y, axis=-1, keepdims=True) + EPS) * fnw_ref[...]
        o_ref[...] = y


def _mlp(x2d, nw, mod, w1, w2, fnw, layer, rows_per_mod, mod_row0, tm, final_norm):
    m, d = x2d.shape
    hid = w1.shape[2]
    th = MLP_TH
    mrow = lambda i: layer * MOD_ROWS + mod_row0 + (i * tm) // rows_per_mod
    nxt = lambda i: jnp.minimum(i + 1, m // tm - 1)
    assert (tm // (hid // th)) % NORM_CHUNK_ROWS == 0
    return pl.pallas_call(
        functools.partial(_mlp_kernel, final_norm=final_norm),
        grid=(m // tm, hid // th),
        in_specs=[
            pl.BlockSpec((tm, d), lambda i, j: (i, 0)),
            pl.BlockSpec((tm, d), lambda i, j: (nxt(i), 0)),
            pl.BlockSpec((None, 1, d), lambda i, j: (layer, 0, 0)),
            pl.BlockSpec((1, 1, d), lambda i, j: (mrow(i), 0, 3)),
            pl.BlockSpec((1, 1, d), lambda i, j: (mrow(i), 0, 4)),
            pl.BlockSpec((1, 1, d), lambda i, j: (mrow(nxt(i)), 0, 3)),
            pl.BlockSpec((1, 1, d), lambda i, j: (mrow(nxt(i)), 0, 4)),
            pl.BlockSpec((1, 1, d), lambda i, j: (mrow(i), 0, 5)),
            pl.BlockSpec((None, d, th), lambda i, j: (layer, 0, j)),
            pl.BlockSpec((None, th, d), lambda i, j: (layer, j, 0)),
            pl.BlockSpec((1, d), lambda i, j: (0, 0)),
        ],
        out_specs=pl.BlockSpec((tm, d), lambda i, j: (i, 0)),
        out_shape=jax.ShapeDtypeStruct((m, d), F32),
        scratch_shapes=[pltpu.VMEM((tm, d), BF16), pltpu.VMEM((tm, d), BF16), pltpu.VMEM((tm, d), F32)],
        compiler_params=_cparams(("arbitrary", "arbitrary")),
        name="mlp",
    )(x2d, x2d, nw, mod, mod, mod, mod, mod, w1, w2, fnw)


def kernel(x, c, ctx, c_ctx, ada_w, ada_b, norm1_w, norm2_w, w_in, na_rpb, hg_lb_logits, hg_norm_w,
           gm_ln_w, gm_ws, gm_bs, w_out, mlp_w1, mlp_w2, final_norm_w):
    bsz, seq, d = x.shape
    n_ctx = ctx.shape[1]
    depth = ada_w.shape[0]
    assert bsz < MOD_ROWS and d == D_MODEL and seq % 512 == 0 and n_ctx % 256 == 0
    assert seq // NA_GROUP_TOK >= NA_WIN_BLOCKS and seq // GRID_W >= 2 * NA_WIN_ROWS

    lb = jnp.cumsum(jax.nn.softmax(hg_lb_logits.astype(F32), axis=0), axis=0)
    lb = lb - lb[:1]
    lbp = jnp.stack([jnp.log(lb), jnp.log1p(-lb), 1.0 - lb], axis=2)

    cond = jnp.zeros((MOD_ROWS, d), F32).at[:bsz].set(c).at[bsz].set(c_ctx)
    mod = _ada(cond, ada_w, ada_b).reshape(depth * MOD_ROWS, 1, 6 * d)

    w_in_b, w_out_b = w_in.astype(BF16), w_out.astype(BF16)
    w1_b, w2_b = mlp_w1.astype(BF16), mlp_w2.astype(BF16)
    nw1 = norm1_w.reshape(depth, 1, d)
    nw2 = norm2_w.reshape(depth, 1, d)
    fnw = final_norm_w.reshape(1, d)

    xl = x.reshape(bsz * seq, d)
    xc = ctx.reshape(bsz * n_ctx, d)
    n_c = bsz * n_ctx
    tm_l, tm_c = TM_LATENT, min(TM_CONTEXT, n_c)
    for l in range(depth):
        need_ctx = l < depth - 1
        p_l = _inproj(xl, nw1, mod, w_in_b, l, seq, 0, TM_INPROJ)
        p_c = _inproj(xc, nw1, mod, w_in_b, l, n_c, bsz, tm_c)

        oa_l = _na_latent(p_l, p_c, _na_bias_slabs(na_rpb[l]), bsz)
        ob_l, ob_c = _hgrn2(p_l, p_c, lbp[l], hg_norm_w[l], bsz)
        oc_l = _chunk_gmlp(p_l, gm_ln_w[l], gm_ws[l], gm_bs[l])
        xl = _outproj(oa_l, ob_l, oc_l, w_out_b, xl, mod, l, seq, 0, tm_l)
        xl = _mlp(xl, nw2, mod, w1_b, w2_b, fnw, l, seq, 0, tm_l, final_norm=not need_ctx)
        if need_ctx:
            oa_c = _ctx_attention(p_c, bsz)
            oc_c = _chunk_gmlp(p_c, gm_ln_w[l], gm_ws[l], gm_bs[l])
            xc = _outproj(oa_c, ob_c, oc_c, w_out_b, xc, mod, l, n_c, bsz, tm_c)
            xc = _mlp(xc, nw2, mod, w1_b, w2_b, fnw, l, n_c, bsz, tm_c, final_norm=False)
    return xl.reshape(bsz, seq, d)
```

```python
import functools

import numpy as np
import jax
import jax.numpy as jnp
from jax import lax
from jax.experimental import pallas as pl
from jax.experimental.pallas import tpu as pltpu

F32 = jnp.float32
BF16 = jnp.bfloat16

D_MODEL = 2048
DEPTH = 2
GRID_W = 64
EPS = 1e-6

NA_HEAD_DIM = 128
NA_HEADS = 8
NA_WIDTH = NA_HEADS * NA_HEAD_DIM
NA_WIN_ROWS = 8
NA_WIN_COLS = 16

HG_HEADS = 4
HG_DIM = 128
HG_WIDTH = HG_HEADS * HG_DIM
HG_CHUNK = 64

GM_GROUPS = 4
GM_DIM = 128
GM_WIDTH = GM_GROUPS * GM_DIM
GM_CHUNK = 128
GM_BLOCK = 4 * GM_CHUNK

IN_WIDTH = 3 * NA_WIDTH + 5 * HG_WIDTH + 2 * GM_WIDTH
MLP_HIDDEN = 4 * D_MODEL
LANE = 128
LOG2E = 1.4426950408889634

_QA, _KA, _VA = 0, NA_HEADS, 2 * NA_HEADS
_HG0 = 3 * NA_HEADS
_GM_U = (3 * NA_WIDTH + 5 * HG_WIDTH) // GM_WIDTH
_GM_V = _GM_U + 1

VMEM_LIMIT_V7X = 56 * 1024 * 1024
TM_LATENT = 512
TM_CONTEXT = 512
TM_INPROJ = 512
INPROJ_TN = 3328
MLP_TH = 1024
OUTPROJ_TN = 2048
NORM_CHUNK_ROWS = 32
NORM_UNROLL = 4
MOD_ROWS = 16


def _cparams(sem):
    return pltpu.CompilerParams(dimension_semantics=sem, vmem_limit_bytes=VMEM_LIMIT_V7X)


def _silu(x):
    h = 0.5 * x
    return h + h * jnp.tanh(h)


def _gelu_tanh(x):
    return 0.5 * x * (1.0 + jnp.tanh(0.7978845608028654 * (x + 0.044715 * (x * x * x))))


def _norm_modulate_store(x_ref, nw_ref, sh_ref, sc_ref, h_ref):
    gain = nw_ref[...] * (1.0 + sc_ref[0])
    shift = sh_ref[0]

    def body(i, carry):
        rows = pl.ds(pl.multiple_of(i * NORM_CHUNK_ROWS, NORM_CHUNK_ROWS), NORM_CHUNK_ROWS)
        x = x_ref[rows, :]
        rs = lax.rsqrt(jnp.mean(x * x, axis=-1, keepdims=True) + EPS)
        h_ref[rows, :] = (x * rs * gain + shift).astype(h_ref.dtype)
        return carry

    lax.fori_loop(0, x_ref.shape[0] // NORM_CHUNK_ROWS, body, 0, unroll=NORM_UNROLL)


def _ada_kernel(c_ref, w_ref, b_ref, o_ref):
    s = _silu(c_ref[...]).astype(BF16)
    o_ref[0] = jnp.dot(s, w_ref[0].astype(BF16), preferred_element_type=F32) + b_ref[0]


def _ada(cond, ada_w, ada_b):
    depth, d, n = ada_w.shape
    tn = 1536
    return pl.pallas_call(
        _ada_kernel,
        grid=(depth, n // tn),
        in_specs=[
            pl.BlockSpec((cond.shape[0], d), lambda l, j: (0, 0)),
            pl.BlockSpec((1, d, tn), lambda l, j: (l, 0, j)),
            pl.BlockSpec((1, 1, tn), lambda l, j: (l, 0, j)),
        ],
        out_specs=pl.BlockSpec((1, cond.shape[0], tn), lambda l, j: (l, 0, j)),
        out_shape=jax.ShapeDtypeStruct((depth, cond.shape[0], n), F32),
        compiler_params=_cparams(("arbitrary", "arbitrary")),
        name="ada",
    )(cond, ada_w, ada_b.reshape(depth, 1, n))


def _inproj_kernel(x_ref, nw_ref, sh_ref, sc_ref, w_ref, o_ref, h_ref):
    @pl.when(pl.program_id(1) == 0)
    def _():
        _norm_modulate_store(x_ref, nw_ref, sh_ref, sc_ref, h_ref)

    o_ref[...] = jnp.dot(h_ref[...], w_ref[...], preferred_element_type=F32)


def _inproj(x2d, nw, mod, w, layer, rows_per_mod, mod_row0, tm):
    m, d = x2d.shape
    n = w.shape[2]
    tn = INPROJ_TN
    mrow = lambda i: layer * MOD_ROWS + mod_row0 + (i * tm) // rows_per_mod
    return pl.pallas_call(
        _inproj_kernel,
        grid=(m // tm, n // tn),
        in_specs=[
            pl.BlockSpec((tm, d), lambda i, j: (i, 0)),
            pl.BlockSpec((None, 1, d), lambda i, j: (layer, 0, 0)),
            pl.BlockSpec((1, 1, d), lambda i, j: (mrow(i), 0, 0)),
            pl.BlockSpec((1, 1, d), lambda i, j: (mrow(i), 0, 1)),
            pl.BlockSpec((None, d, tn), lambda i, j: (layer, 0, j)),
        ],
        out_specs=pl.BlockSpec((tm, tn), lambda i, j: (i, j)),
        out_shape=jax.ShapeDtypeStruct((m, n), F32),
        scratch_shapes=[pltpu.VMEM((tm, d), BF16)],
        compiler_params=_cparams(("arbitrary", "arbitrary")),
        name="inproj",
    )(x2d, nw, mod, mod, w)


NA_GROUP_ROWS = 4
NA_GROUP_TOK = NA_GROUP_ROWS * GRID_W
NA_WIN_BLOCKS = 3
NA_GROUPS_PER_STEP = 4
NA_LOOP_UNROLL = 2


NA_MASKED_SLAB = 2 * NA_WIN_ROWS - 1


def _na_bias_slabs(rpb):
    h = rpb.shape[0]
    kcol = np.arange(GRID_W)[:, None]
    qcol = np.arange(GRID_W)[None, :]
    wstart = np.clip(qcol - NA_WIN_COLS // 2, 0, GRID_W - NA_WIN_COLS)
    valid_col = (kcol >= wstart) & (kcol < wstart + NA_WIN_COLS)
    pad = GRID_W - NA_WIN_COLS
    padded = jnp.pad(rpb.astype(F32), ((0, 0), (0, 0), (pad, pad)))
    m = jnp.tile(padded, (1, 1, GRID_W + 1))[:, :, :GRID_W * 2 * GRID_W]
    m = m.reshape(h, rpb.shape[1], GRID_W, 2 * GRID_W)[..., :GRID_W]
    slabs = jnp.where(valid_col, m[..., ::-1] * LOG2E, -1e30)
    slabs = jnp.concatenate([slabs, jnp.full((h, 1, GRID_W, GRID_W), -1e30, F32)], axis=1)
    return jnp.concatenate([slabs, slabs], axis=-1)


def _na_slab_index(rows):
    n_groups = rows // NA_GROUP_ROWS
    krel = np.arange(NA_WIN_BLOCKS * NA_GROUP_ROWS)[:, None]
    qrel = np.arange(NA_GROUP_ROWS)[None, :]
    idx = []
    for g in (0, 1, n_groups - 1):
        krow = int(np.clip(g - 1, 0, n_groups - NA_WIN_BLOCKS)) * NA_GROUP_ROWS + krel
        qrow = g * NA_GROUP_ROWS + qrel
        r0 = np.clip(qrow - NA_WIN_ROWS // 2, 0, rows - NA_WIN_ROWS)
        in_window = (krow >= r0) & (krow < r0 + NA_WIN_ROWS)
        idx.append(np.where(in_window, krow - qrow + NA_WIN_ROWS - 1, NA_MASKED_SLAB))
    return np.stack(idx).tolist()


def _na_kernel(q_ref, k_ref, v_ref, kc_ref, vc_ref, slab_ref, o_ref, qb_ref, kb_ref, vt_ref, bias_ref,
               *, slab_index):
    n_groups = q_ref.shape[0] // NA_GROUP_TOK
    gt = NA_GROUP_TOK
    scale2 = NA_HEAD_DIM ** -0.5 * LOG2E
    nt = (((1,), (1,)), ((), ()))
    dot = functools.partial(jnp.dot, preferred_element_type=F32)

    @pl.when(pl.program_id(1) == 0)
    def _():
        left = lax.broadcasted_iota(jnp.int32, (GRID_W, 2 * GRID_W), 1) < GRID_W
        for kind, per_key_row in enumerate(slab_index):
            for j, per_query_row in enumerate(per_key_row):
                for a in range(0, NA_GROUP_ROWS, 2):
                    tile = jnp.where(left, slab_ref[0, per_query_row[a]], slab_ref[0, per_query_row[a + 1]])
                    bias_ref[kind, j * GRID_W:(j + 1) * GRID_W, a * GRID_W:(a + 2) * GRID_W] = tile

    qb_ref[...] = (q_ref[...] * scale2).astype(BF16)
    kb_ref[...] = k_ref[...].astype(BF16)
    for b in range(n_groups):
        vt_ref[b] = v_ref[b * gt:(b + 1) * gt, :].T.astype(BF16)
    kc = kc_ref[...].astype(BF16)
    vct = vc_ref[...].T.astype(BF16)

    def group(g):
        blk = jnp.clip(g - 1, 0, n_groups - NA_WIN_BLOCKS)
        kind = jnp.where(g == 0, 0, jnp.where(g == n_groups - 1, 2, 1))
        qg = qb_ref[pl.ds(pl.multiple_of(g * gt, gt), gt), :]
        kw = kb_ref[pl.ds(pl.multiple_of(blk * gt, gt), NA_WIN_BLOCKS * gt), :]
        s_lat = lax.dot_general(kw, qg, nt, preferred_element_type=F32) + bias_ref[kind]
        s_ctx = lax.dot_general(kc, qg, nt, preferred_element_type=F32)
        yield
        m = jnp.maximum(jnp.max(s_lat, axis=0, keepdims=True), jnp.max(s_ctx, axis=0, keepdims=True))
        p_lat = jnp.exp2(s_lat - m)
        p_ctx = jnp.exp2(s_ctx - m)
        l = jnp.sum(p_lat, axis=0, keepdims=True) + jnp.sum(p_ctx, axis=0, keepdims=True)
        yield
        ot = dot(vct, p_ctx.astype(BF16))
        for j in range(NA_WIN_BLOCKS):
            ot = ot + dot(vt_ref[blk + j], p_lat[j * gt:(j + 1) * gt, :].astype(BF16))
        yield
        o_ref[pl.ds(pl.multiple_of(g * gt, gt), gt), :] = (ot / l).T.astype(o_ref.dtype)
        yield

    def body(i, carry):
        groups = [group(i * NA_GROUPS_PER_STEP + u) for u in range(NA_GROUPS_PER_STEP)]
        for _ in range(4):
            for grp in groups:
                next(grp)
        return carry

    lax.fori_loop(0, n_groups // NA_GROUPS_PER_STEP, body, 0, unroll=NA_LOOP_UNROLL)


def _na_latent(p_l, p_c, slabs, b):
    t = p_l.shape[0] // b
    ctx = p_c.shape[0] // b
    hd = NA_HEAD_DIM
    return pl.pallas_call(
        functools.partial(_na_kernel, slab_index=_na_slab_index(t // GRID_W)),
        grid=(NA_HEADS, b),
        in_specs=[
            pl.BlockSpec((t, hd), lambda h, i: (i, _QA + h)),
            pl.BlockSpec((t, hd), lambda h, i: (i, _KA + h)),
            pl.BlockSpec((t, hd), lambda h, i: (i, _VA + h)),
            pl.BlockSpec((ctx, hd), lambda h, i: (i, _KA + h)),
            pl.BlockSpec((ctx, hd), lambda h, i: (i, _VA + h)),
            pl.BlockSpec((1,) + slabs.shape[1:], lambda h, i: (h, 0, 0, 0)),
        ],
        out_specs=pl.BlockSpec((t, hd), lambda h, i: (i, h)),
        out_shape=jax.ShapeDtypeStruct((b * t, NA_WIDTH), BF16),
        scratch_shapes=[
            pltpu.VMEM((t, hd), BF16),
            pltpu.VMEM((t, hd), BF16),
            pltpu.VMEM((t // NA_GROUP_TOK, hd, NA_GROUP_TOK), BF16),
            pltpu.VMEM((3, NA_WIN_BLOCKS * NA_GROUP_TOK, NA_GROUP_TOK), F32),
        ],
        compiler_params=_cparams(("arbitrary", "arbitrary")),
        name="na_latent",
    )(p_l, p_l, p_l, p_c, p_c, slabs)


def _ctx_attn_kernel(q_ref, k_ref, v_ref, o_ref):
    scale = NA_HEAD_DIM ** -0.5
    for h in range(NA_HEADS):
        hs = slice(h * NA_HEAD_DIM, (h + 1) * NA_HEAD_DIM)
        q = q_ref[:, hs].astype(BF16)
        k = k_ref[:, hs].astype(BF16)
        v = v_ref[:, hs].astype(BF16)
        s = lax.dot_general(q, k, (((1,), (1,)), ((), ())), preferred_element_type=F32) * scale
        p = jnp.exp(s - jnp.max(s, axis=-1, keepdims=True))
        l = jnp.sum(p, axis=-1, keepdims=True)
        o = jnp.dot(p.astype(BF16), v, preferred_element_type=F32)
        o_ref[:, hs] = (o / l).astype(o_ref.dtype)


def _ctx_attention(p_c, b):
    ctx = p_c.shape[0] // b
    return pl.pallas_call(
        _ctx_attn_kernel,
        grid=(b,),
        in_specs=[
            pl.BlockSpec((ctx, NA_WIDTH), lambda i: (i, 0)),
            pl.BlockSpec((ctx, NA_WIDTH), lambda i: (i, 1)),
            pl.BlockSpec((ctx, NA_WIDTH), lambda i: (i, 2)),
        ],
        out_specs=pl.BlockSpec((ctx, NA_WIDTH), lambda i: (i, 0)),
        out_shape=jax.ShapeDtypeStruct((b * ctx, NA_WIDTH), BF16),
        compiler_params=_cparams(("arbitrary",)),
        name="ctx_attention",
    )(p_c, p_c, p_c)


def _gm_kernel(u_ref, v_ref, lnw_ref, ws_ref, bs_ref, o_ref):
    for ck in range(u_ref.shape[0] // GM_CHUNK):
        rows = slice(ck * GM_CHUNK, (ck + 1) * GM_CHUNK)
        uf = _gelu_tanh(u_ref[rows, :])
        vf = _gelu_tanh(v_ref[rows, :])
        for g in range(GM_GROUPS):
            sl = slice(g * GM_DIM, (g + 1) * GM_DIM)
            vg = vf[:, sl]
            mu = jnp.mean(vg, axis=-1, keepdims=True)
            dv = vg - mu
            var = jnp.mean(dv * dv, axis=-1, keepdims=True)
            vn = dv * lax.rsqrt(var + EPS) * lnw_ref[:, sl]
            mixed = jnp.dot(ws_ref[g].astype(BF16), vn.astype(BF16), preferred_element_type=F32) + bs_ref[g]
            o_ref[rows, sl] = (uf[:, sl] * mixed).astype(o_ref.dtype)


def _chunk_gmlp(p2, ln_w, ws, bs):
    m = p2.shape[0]
    blk = GM_BLOCK if m % GM_BLOCK == 0 else GM_CHUNK
    return pl.pallas_call(
        _gm_kernel,
        grid=(m // blk,),
        in_specs=[
            pl.BlockSpec((blk, GM_WIDTH), lambda c: (c, _GM_U)),
            pl.BlockSpec((blk, GM_WIDTH), lambda c: (c, _GM_V)),
            pl.BlockSpec((1, GM_WIDTH), lambda c: (0, 0)),
            pl.BlockSpec((GM_GROUPS, GM_CHUNK, GM_CHUNK), lambda c: (0, 0, 0)),
            pl.BlockSpec((GM_GROUPS, GM_CHUNK, 1), lambda c: (0, 0, 0)),
        ],
        out_specs=pl.BlockSpec((blk, GM_WIDTH), lambda c: (c, 0)),
        out_shape=jax.ShapeDtypeStruct((m, GM_WIDTH), BF16),
        compiler_params=_cparams(("arbitrary",)),
        name="chunk_gmlp",
    )(p2, p2, ln_w.reshape(1, GM_WIDTH), ws, bs.reshape(GM_GROUPS, GM_CHUNK, 1))


HG_PAIR = 2
HG_LEVELS = (32, 16, 8)
HG_DIAG = 8
HG_PREP_ROWS = 256
HG_SCAN_UNROLL = 8


def _split3_dot(tri, g):
    g0 = g.astype(BF16)
    r1 = g - g0.astype(F32)
    g1 = r1.astype(BF16)
    g2 = (r1 - g1.astype(F32)).astype(BF16)
    dot = functools.partial(jnp.dot, preferred_element_type=F32)
    return dot(tri, g0) + dot(tri, g1) + dot(tri, g2)


def _hg_kernel(ql_ref, ffl_ref, fbl_ref, il_ref, gl_ref,
               qc_ref, ffc_ref, fbc_ref, ic_ref, gc_ref,
               lbp_ref, nw_ref, tri_ref, code_ref, lsum_ref, ol_ref, oc_ref,
               qh_s, kf_s, kb_s, ef_s, eb_s, of_s, ob_s, st_s):
    c = HG_CHUNK
    hd = HG_DIM
    n_ctx = qc_ref.shape[0]
    n_lat = ql_ref.shape[0]
    scale = hd ** -0.5
    nt = (((1,), (1,)), ((), ()))
    tn = (((0,), (0,)), ((), ()))

    def prep(q_ref, ff_ref, fb_ref, base, n):
        step = HG_PREP_ROWS
        for t0 in range(0, n, step):
            src = slice(t0, t0 + step)
            dst = slice(base + t0, base + t0 + step)
            qh = _silu(q_ref[src, :]) * scale
            for h in range(HG_PAIR):
                qh_s[h, dst, :] = qh[:, h * hd:(h + 1) * hd]
            for d, (f_ref, k_s, e_s) in enumerate(((ff_ref, kf_s, ef_s), (fb_ref, kb_s, eb_s))):
                x = f_ref[src, :]
                sp = jnp.maximum(-x, 0.0) + jnp.log(1.0 + jnp.exp(-jnp.abs(x)))
                log_lb = lbp_ref[d, 0:1, :]
                y = lbp_ref[d, 1:2, :] - sp
                mx = jnp.maximum(log_lb, y)
                log_f = mx + jnp.log(1.0 + jnp.exp(-jnp.abs(log_lb - y)))
                one_minus_f = lbp_ref[d, 2:3, :] * jnp.exp(-(sp + x))
                e = _split3_dot(tri_ref[d], log_f) * LOG2E
                for h in range(HG_PAIR):
                    e_s[h, dst, :] = e[:, h * hd:(h + 1) * hd]
                    k_s[h, dst, :] = one_minus_f[:, h * hd:(h + 1) * hd]

    prep(qc_ref, ffc_ref, fbc_ref, 0, n_ctx)
    prep(ql_ref, ffl_ref, fbl_ref, n_ctx, n_lat)

    diag_code = len(HG_LEVELS) + 1

    st_s[...] = jnp.zeros_like(st_s)

    def chunk(off, v, h, d, out):
        fwd = d == 0
        k_s, e_s = (kf_s, ef_s) if fwd else (kb_s, eb_s)
        q = qh_s[h, pl.ds(off, c), :]
        k = k_s[h, pl.ds(off, c), :]
        e = e_s[h, pl.ds(off, c), :]
        code = code_ref[d]
        prods = []
        for i in range(c // HG_DIAG):
            bs = slice(i * HG_DIAG, (i + 1) * HG_DIAG)
            row_prods = []
            for s in range(HG_DIAG):
                kr = k_s[h, pl.ds(off + i * HG_DIAG + s, 1), :]
                er = e_s[h, pl.ds(off + i * HG_DIAG + s, 1), :]
                decay = jnp.exp2(jnp.minimum(e[bs] - er, 0.0))
                row_prods.append((q[bs] * kr * decay).astype(BF16))
            prods.append(jnp.concatenate(row_prods, axis=1))
        diag = jnp.dot(jnp.concatenate(prods, axis=0), lsum_ref[...], preferred_element_type=F32)
        yield
        att = None
        for li, w in enumerate(HG_LEVELS):
            zeros = jnp.zeros((w, hd), F32)
            q_parts, k_parts = [], []
            for a in range(0, c, 2 * w):
                lo, hi = slice(a, a + w), slice(a + w, a + 2 * w)
                if fwd:
                    ref = e_s[h, pl.ds(off + a + w - 1, 1), :]
                    q_parts += [zeros, q[hi] * jnp.exp2(e[hi] - ref)]
                    k_parts += [k[lo] * jnp.exp2(ref - e[lo]), zeros]
                else:
                    ref = e_s[h, pl.ds(off + a + w, 1), :]
                    q_parts += [q[lo] * jnp.exp2(e[lo] - ref), zeros]
                    k_parts += [zeros, k[hi] * jnp.exp2(ref - e[hi])]
            qs = jnp.concatenate(q_parts, axis=0).astype(BF16)
            ks = jnp.concatenate(k_parts, axis=0).astype(BF16)
            a_w = lax.dot_general(qs, ks, nt, preferred_element_type=F32)
            att = a_w if att is None else jnp.where(code == li + 1, a_w, att)
            yield
        e_end = e_s[h, pl.ds(off + (c - 1 if fwd else 0), 1), :]
        st = st_s[h, d]
        qi = (q * jnp.exp2(e)).astype(BF16)
        ki = (k * jnp.exp2(e_end - e)).astype(BF16)
        o = lax.dot_general(qi, st.astype(BF16), nt, preferred_element_type=F32)
        yield
        st_new = st * jnp.exp2(e_end) + lax.dot_general(v, ki, tn, preferred_element_type=F32)
        yield
        att = jnp.where(code == diag_code, diag, att)
        out.append((o + jnp.dot(att.astype(BF16), v, preferred_element_type=F32), st_new))
        yield

    n_stages = len(HG_LEVELS) + 4

    def scan(v_ref, base, n):
        def body(i, carry):
            rf = pl.multiple_of(i * c, c)
            rb = pl.multiple_of((n - 1 - i) * c, c)
            vf = v_ref[pl.ds(rf, c), :].astype(BF16)
            vb = v_ref[pl.ds(rb, c), :].astype(BF16)
            chains = []
            for h in range(HG_PAIR):
                hs = slice(h * hd, (h + 1) * hd)
                for d, r, v in ((0, rf, vf), (1, rb, vb)):
                    out = []
                    chains.append((h, d, r, out, chunk(base + r, v[:, hs], h, d, out)))
            for _ in range(n_stages):
                for chain in chains:
                    next(chain[-1])
            for h, d, r, out, _ in chains:
                o, st_new = out[0]
                (of_s if d == 0 else ob_s)[h, pl.ds(base + r, c), :] = o
                st_s[h, d] = st_new
            return carry

        lax.fori_loop(0, n, body, 0, unroll=HG_SCAN_UNROLL)

    scan(ic_ref, 0, n_ctx // c)
    scan(il_ref, n_ctx, n_lat // c)

    def finish(g_ref, o_ref, base, n):
        step = 256
        for t0 in range(0, n, step):
            src = slice(t0, t0 + step)
            dst = slice(base + t0, base + t0 + step)
            for h in range(HG_PAIR):
                hs = slice(h * hd, (h + 1) * hd)
                o = of_s[h, dst, :] + ob_s[h, dst, :]
                o = o * lax.rsqrt(jnp.mean(o * o, axis=-1, keepdims=True) + EPS) * nw_ref[...]
                o_ref[src, hs] = (o * _silu(g_ref[src, hs])).astype(o_ref.dtype)

    finish(gc_ref, oc_ref, 0, n_ctx)
    finish(gl_ref, ol_ref, n_ctx, n_lat)


def _hg_constants():
    c = HG_CHUNK
    row, col = np.arange(c)[:, None], np.arange(c)[None, :]
    blocks = np.eye(HG_PREP_ROWS // c)
    tri = np.stack([np.kron(blocks, col <= row), np.kron(blocks, col >= row)]).astype(np.float32)
    same8 = (row // HG_DIAG) == (col // HG_DIAG)
    diag_code = len(HG_LEVELS) + 1
    code_f = np.where(same8 & (col <= row), diag_code, 0)
    code_b = np.where(same8 & (col >= row), diag_code, 0)
    for li, w in enumerate(HG_LEVELS):
        same = (row // (2 * w)) == (col // (2 * w))
        t_hi, s_hi = (row % (2 * w)) >= w, (col % (2 * w)) >= w
        code_f = np.where(same & t_hi & ~s_hi, li + 1, code_f)
        code_b = np.where(same & ~t_hi & s_hi, li + 1, code_b)
    codes = np.stack([code_f, code_b]).astype(np.int32)
    lane_sum = (np.arange(HG_DIAG * HG_DIM)[:, None] // HG_DIM == col % HG_DIAG).astype(np.float32)
    return jnp.asarray(tri, BF16), jnp.asarray(codes), jnp.asarray(lane_sum, BF16)


def _hgrn2(p_l, p_c, lbp, norm_w, b):
    tri, codes, lane_sum = _hg_constants()
    t = p_l.shape[0] // b
    ctx = p_c.shape[0] // b
    hd = HG_DIM
    pw = HG_PAIR * hd
    n = t + ctx
    first = _HG0 * LANE // pw

    def col(stream):
        return lambda i, j: (i, first + stream * (HG_HEADS // HG_PAIR) + j)

    lat_specs = [pl.BlockSpec((t, pw), col(s)) for s in range(5)]
    ctx_specs = [pl.BlockSpec((ctx, pw), col(s)) for s in range(5)]
    big = lambda: pltpu.VMEM((HG_PAIR, n, hd), F32)
    return pl.pallas_call(
        _hg_kernel,
        grid=(b, HG_HEADS // HG_PAIR),
        in_specs=lat_specs + ctx_specs + [
            pl.BlockSpec((2, 3, pw), lambda i, j: (0, 0, j)),
            pl.BlockSpec((1, hd), lambda i, j: (0, 0)),
            pl.BlockSpec(tri.shape, lambda i, j: (0, 0, 0)),
            pl.BlockSpec(codes.shape, lambda i, j: (0, 0, 0)),
            pl.BlockSpec(lane_sum.shape, lambda i, j: (0, 0)),
        ],
        out_specs=[
            pl.BlockSpec((t, pw), lambda i, j: (i, j)),
            pl.BlockSpec((ctx, pw), lambda i, j: (i, j)),
        ],
        out_shape=[
            jax.ShapeDtypeStruct((b * t, HG_WIDTH), BF16),
            jax.ShapeDtypeStruct((b * ctx, HG_WIDTH), BF16),
        ],
        scratch_shapes=[big() for _ in range(7)] + [pltpu.VMEM((HG_PAIR, 2, hd, hd), F32)],
        compiler_params=_cparams(("arbitrary", "arbitrary")),
        name="hgrn2",
    )(*([p_l] * 5 + [p_c] * 5 + [lbp, norm_w.reshape(1, hd), tri, codes, lane_sum]))


def _outproj_kernel(oa_ref, ob_ref, oc_ref, wa_ref, wb_ref, wc_ref, x_ref, g_ref, o_ref):
    dot = functools.partial(jnp.dot, preferred_element_type=F32)
    y = dot(oa_ref[...], wa_ref[...]) + dot(ob_ref[...], wb_ref[...]) + dot(oc_ref[...], wc_ref[...])
    o_ref[...] = x_ref[...] + g_ref[0] * y


def _outproj(oa, ob, oc, w, x2d, mod, layer, rows_per_mod, mod_row0, tm):
    m, d = x2d.shape
    tn = OUTPROJ_TN
    mrow = lambda i: layer * MOD_ROWS + mod_row0 + (i * tm) // rows_per_mod
    nb = NA_WIDTH // HG_WIDTH
    return pl.pallas_call(
        _outproj_kernel,
        grid=(m // tm, d // tn),
        in_specs=[
            pl.BlockSpec((tm, NA_WIDTH), lambda i, j: (i, 0)),
            pl.BlockSpec((tm, HG_WIDTH), lambda i, j: (i, 0)),
            pl.BlockSpec((tm, GM_WIDTH), lambda i, j: (i, 0)),
            pl.BlockSpec((None, NA_WIDTH, tn), lambda i, j: (layer, 0, j)),
            pl.BlockSpec((None, HG_WIDTH, tn), lambda i, j: (layer, nb, j)),
            pl.BlockSpec((None, GM_WIDTH, tn), lambda i, j: (layer, nb + 1, j)),
            pl.BlockSpec((tm, tn), lambda i, j: (i, j)),
            pl.BlockSpec((1, 1, tn), lambda i, j: (mrow(i), 0, 2 * (d // tn) + j)),
        ],
        out_specs=pl.BlockSpec((tm, tn), lambda i, j: (i, j)),
        out_shape=jax.ShapeDtypeStruct((m, d), F32),
        compiler_params=_cparams(("arbitrary", "arbitrary")),
        name="outproj",
    )(oa, ob, oc, w, w, w, x2d, mod)


def _mlp_kernel(x_ref, nw_ref, sh_ref, sc_ref, g_ref, w1_ref, w2_ref, fnw_ref, o_ref, h_ref, acc_ref,
                *, final_norm):
    j = pl.program_id(1)

    @pl.when(j == 0)
    def _():
        _norm_modulate_store(x_ref, nw_ref, sh_ref, sc_ref, h_ref)
        acc_ref[...] = jnp.zeros_like(acc_ref)

    a = jnp.maximum(jnp.dot(h_ref[...], w1_ref[...], preferred_element_type=F32), 0.0)
    acc_ref[...] += jnp.dot((a * a).astype(BF16), w2_ref[...], preferred_element_type=F32)

    @pl.when(j == pl.num_programs(1) - 1)
    def _():
        y = x_ref[...] + g_ref[0] * acc_ref[...]
        if final_norm:
            y = y * lax.rsqrt(jnp.mean(y * y, axis=-1, keepdims=True) + EPS) * fnw_ref[...]
        o_ref[...] = y


def _mlp(x2d, nw, mod, w1, w2, fnw, layer, rows_per_mod, mod_row0, tm, final_norm):
    m, d = x2d.shape
    hid = w1.shape[2]
    th = MLP_TH
    mrow = lambda i: layer * MOD_ROWS + mod_row0 + (i * tm) // rows_per_mod
    return pl.pallas_call(
        functools.partial(_mlp_kernel, final_norm=final_norm),
        grid=(m // tm, hid // th),
        in_specs=[
            pl.BlockSpec((tm, d), lambda i, j: (i, 0)),
            pl.BlockSpec((None, 1, d), lambda i, j: (layer, 0, 0)),
            pl.BlockSpec((1, 1, d), lambda i, j: (mrow(i), 0, 3)),
            pl.BlockSpec((1, 1, d), lambda i, j: (mrow(i), 0, 4)),
            pl.BlockSpec((1, 1, d), lambda i, j: (mrow(i), 0, 5)),
            pl.BlockSpec((None, d, th), lambda i, j: (layer, 0, j)),
            pl.BlockSpec((None, th, d), lambda i, j: (layer, j, 0)),
            pl.BlockSpec((1, d), lambda i, j: (0, 0)),
        ],
        out_specs=pl.BlockSpec((tm, d), lambda i, j: (i, 0)),
        out_shape=jax.ShapeDtypeStruct((m, d), F32),
        scratch_shapes=[pltpu.VMEM((tm, d), BF16), pltpu.VMEM((tm, d), F32)],
        compiler_params=_cparams(("arbitrary", "arbitrary")),
        name="mlp",
    )(x2d, nw, mod, mod, mod, w1, w2, fnw)


def kernel(x, c, ctx, c_ctx, ada_w, ada_b, norm1_w, norm2_w, w_in, na_rpb, hg_lb_logits, hg_norm_w,
           gm_ln_w, gm_ws, gm_bs, w_out, mlp_w1, mlp_w2, final_norm_w):
    bsz, seq, d = x.shape
    n_ctx = ctx.shape[1]
    depth = ada_w.shape[0]
    assert bsz < MOD_ROWS and d == D_MODEL and seq % 512 == 0 and n_ctx % 256 == 0
    assert seq // NA_GROUP_TOK >= NA_WIN_BLOCKS and seq // GRID_W >= 2 * NA_WIN_ROWS

    lb = jnp.cumsum(jax.nn.softmax(hg_lb_logits.astype(F32), axis=0), axis=0)
    lb = lb - lb[:1]
    lbp = jnp.stack([jnp.log(lb), jnp.log1p(-lb), 1.0 - lb], axis=2)

    cond = jnp.zeros((MOD_ROWS, d), F32).at[:bsz].set(c).at[bsz].set(c_ctx)
    mod = _ada(cond, ada_w, ada_b).reshape(depth * MOD_ROWS, 1, 6 * d)

    w_in_b, w_out_b = w_in.astype(BF16), w_out.astype(BF16)
    w1_b, w2_b = mlp_w1.astype(BF16), mlp_w2.astype(BF16)
    nw1 = norm1_w.reshape(depth, 1, d)
    nw2 = norm2_w.reshape(depth, 1, d)
    fnw = final_norm_w.reshape(1, d)

    xl = x.reshape(bsz * seq, d)
    xc = ctx.reshape(bsz * n_ctx, d)
    n_c = bsz * n_ctx
    tm_l, tm_c = TM_LATENT, min(TM_CONTEXT, n_c)
    for l in range(depth):
        need_ctx = l < depth - 1
        p_l = _inproj(xl, nw1, mod, w_in_b, l, seq, 0, TM_INPROJ)
        p_c = _inproj(xc, nw1, mod, w_in_b, l, n_c, bsz, tm_c)

        oa_l = _na_latent(p_l, p_c, _na_bias_slabs(na_rpb[l]), bsz)
        ob_l, ob_c = _hgrn2(p_l, p_c, lbp[l], hg_norm_w[l], bsz)
        oc_l = _chunk_gmlp(p_l, gm_ln_w[l], gm_ws[l], gm_bs[l])
        xl = _outproj(oa_l, ob_l, oc_l, w_out_b, xl, mod, l, seq, 0, tm_l)
        xl = _mlp(xl, nw2, mod, w1_b, w2_b, fnw, l, seq, 0, tm_l, final_norm=not need_ctx)
        if need_ctx:
            oa_c = _ctx_attention(p_c, bsz)
            oc_c = _chunk_gmlp(p_c, gm_ln_w[l], gm_ws[l], gm_bs[l])
            xc = _outproj(oa_c, ob_c, oc_c, w_out_b, xc, mod, l, n_c, bsz, tm_c)
            xc = _mlp(xc, nw2, mod, w1_b, w2_b, fnw, l, n_c, bsz, tm_c, final_norm=False)
    return xl.reshape(bsz, seq, d)
```

```python
import functools

import numpy as np
import jax
import jax.numpy as jnp
from jax import lax
from jax.experimental import pallas as pl
from jax.experimental.pallas import tpu as pltpu

F32 = jnp.float32
BF16 = jnp.bfloat16

D_MODEL = 2048
DEPTH = 2
GRID_W = 64
EPS = 1e-6

NA_HEAD_DIM = 128
NA_HEADS = 8
NA_WIDTH = NA_HEADS * NA_HEAD_DIM
NA_WIN_ROWS = 8
NA_WIN_COLS = 16

HG_HEADS = 4
HG_DIM = 128
HG_WIDTH = HG_HEADS * HG_DIM
HG_CHUNK = 64

GM_GROUPS = 4
GM_DIM = 128
GM_WIDTH = GM_GROUPS * GM_DIM
GM_CHUNK = 128

IN_WIDTH = 3 * NA_WIDTH + 5 * HG_WIDTH + 2 * GM_WIDTH
MLP_HIDDEN = 4 * D_MODEL
LANE = 128
LOG2E = 1.4426950408889634

_QA, _KA, _VA = 0, NA_HEADS, 2 * NA_HEADS
_HG0 = 3 * NA_HEADS
_GM_U = (3 * NA_WIDTH + 5 * HG_WIDTH) // GM_WIDTH
_GM_V = _GM_U + 1

VMEM_LIMIT_V7X = 56 * 1024 * 1024
TM_LATENT = 512
TM_CONTEXT = 512
TM_INPROJ = 512
INPROJ_TN = 3328
MLP_TH = 1024
NORM_CHUNK_ROWS = 32
NORM_UNROLL = 4
MOD_ROWS = 16


def _cparams(sem):
    return pltpu.CompilerParams(dimension_semantics=sem, vmem_limit_bytes=VMEM_LIMIT_V7X)


def _silu(x):
    h = 0.5 * x
    return h + h * jnp.tanh(h)


def _gelu_tanh(x):
    return 0.5 * x * (1.0 + jnp.tanh(0.7978845608028654 * (x + 0.044715 * (x * x * x))))


def _norm_modulate_store(x_ref, nw_ref, sh_ref, sc_ref, h_ref):
    gain = nw_ref[...] * (1.0 + sc_ref[0])
    shift = sh_ref[0]

    def body(i, carry):
        rows = pl.ds(pl.multiple_of(i * NORM_CHUNK_ROWS, NORM_CHUNK_ROWS), NORM_CHUNK_ROWS)
        x = x_ref[rows, :]
        rs = lax.rsqrt(jnp.mean(x * x, axis=-1, keepdims=True) + EPS)
        h_ref[rows, :] = (x * rs * gain + shift).astype(h_ref.dtype)
        return carry

    lax.fori_loop(0, x_ref.shape[0] // NORM_CHUNK_ROWS, body, 0, unroll=NORM_UNROLL)


def _ada_kernel(c_ref, w_ref, b_ref, o_ref):
    s = _silu(c_ref[...]).astype(BF16)
    o_ref[0] = jnp.dot(s, w_ref[0].astype(BF16), preferred_element_type=F32) + b_ref[0]


def _ada(cond, ada_w, ada_b):
    depth, d, n = ada_w.shape
    tn = 1536
    return pl.pallas_call(
        _ada_kernel,
        grid=(depth, n // tn),
        in_specs=[
            pl.BlockSpec((cond.shape[0], d), lambda l, j: (0, 0)),
            pl.BlockSpec((1, d, tn), lambda l, j: (l, 0, j)),
            pl.BlockSpec((1, 1, tn), lambda l, j: (l, 0, j)),
        ],
        out_specs=pl.BlockSpec((1, cond.shape[0], tn), lambda l, j: (l, 0, j)),
        out_shape=jax.ShapeDtypeStruct((depth, cond.shape[0], n), F32),
        compiler_params=_cparams(("arbitrary", "arbitrary")),
        name="ada",
    )(cond, ada_w, ada_b.reshape(depth, 1, n))


def _inproj_kernel(x_ref, nw_ref, sh_ref, sc_ref, w_ref, o_ref, h_ref):
    @pl.when(pl.program_id(1) == 0)
    def _():
        _norm_modulate_store(x_ref, nw_ref, sh_ref, sc_ref, h_ref)

    o_ref[...] = jnp.dot(h_ref[...], w_ref[...], preferred_element_type=F32)


def _inproj(x2d, nw, mod, w, layer, rows_per_mod, mod_row0, tm):
    m, d = x2d.shape
    n = w.shape[2]
    tn = INPROJ_TN
    mrow = lambda i: layer * MOD_ROWS + mod_row0 + (i * tm) // rows_per_mod
    return pl.pallas_call(
        _inproj_kernel,
        grid=(m // tm, n // tn),
        in_specs=[
            pl.BlockSpec((tm, d), lambda i, j: (i, 0)),
            pl.BlockSpec((None, 1, d), lambda i, j: (layer, 0, 0)),
            pl.BlockSpec((1, 1, d), lambda i, j: (mrow(i), 0, 0)),
            pl.BlockSpec((1, 1, d), lambda i, j: (mrow(i), 0, 1)),
            pl.BlockSpec((None, d, tn), lambda i, j: (layer, 0, j)),
        ],
        out_specs=pl.BlockSpec((tm, tn), lambda i, j: (i, j)),
        out_shape=jax.ShapeDtypeStruct((m, n), F32),
        scratch_shapes=[pltpu.VMEM((tm, d), BF16)],
        compiler_params=_cparams(("arbitrary", "arbitrary")),
        name="inproj",
    )(x2d, nw, mod, mod, w)


NA_GROUP_ROWS = 4
NA_GROUP_TOK = NA_GROUP_ROWS * GRID_W
NA_WIN_BLOCKS = 3
NA_GROUPS_PER_STEP = 4
NA_LOOP_UNROLL = 2


NA_MASKED_SLAB = 2 * NA_WIN_ROWS - 1


def _na_bias_slabs(rpb):
    h = rpb.shape[0]
    kcol = np.arange(GRID_W)[:, None]
    qcol = np.arange(GRID_W)[None, :]
    wstart = np.clip(qcol - NA_WIN_COLS // 2, 0, GRID_W - NA_WIN_COLS)
    valid_col = (kcol >= wstart) & (kcol < wstart + NA_WIN_COLS)
    pad = GRID_W - NA_WIN_COLS
    padded = jnp.pad(rpb.astype(F32), ((0, 0), (0, 0), (pad, pad)))
    m = jnp.tile(padded, (1, 1, GRID_W + 1))[:, :, :GRID_W * 2 * GRID_W]
    m = m.reshape(h, rpb.shape[1], GRID_W, 2 * GRID_W)[..., :GRID_W]
    slabs = jnp.where(valid_col, m[..., ::-1] * LOG2E, -1e30)
    slabs = jnp.concatenate([slabs, jnp.full((h, 1, GRID_W, GRID_W), -1e30, F32)], axis=1)
    return jnp.concatenate([slabs, slabs], axis=-1)


def _na_slab_index(rows):
    n_groups = rows // NA_GROUP_ROWS
    krel = np.arange(NA_WIN_BLOCKS * NA_GROUP_ROWS)[:, None]
    qrel = np.arange(NA_GROUP_ROWS)[None, :]
    idx = []
    for g in (0, 1, n_groups - 1):
        krow = int(np.clip(g - 1, 0, n_groups - NA_WIN_BLOCKS)) * NA_GROUP_ROWS + krel
        qrow = g * NA_GROUP_ROWS + qrel
        r0 = np.clip(qrow - NA_WIN_ROWS // 2, 0, rows - NA_WIN_ROWS)
        in_window = (krow >= r0) & (krow < r0 + NA_WIN_ROWS)
        idx.append(np.where(in_window, krow - qrow + NA_WIN_ROWS - 1, NA_MASKED_SLAB))
    return np.stack(idx).tolist()


def _na_kernel(q_ref, k_ref, v_ref, kc_ref, vc_ref, slab_ref, o_ref, qb_ref, kb_ref, vt_ref, bias_ref,
               *, slab_index):
    n_groups = q_ref.shape[0] // NA_GROUP_TOK
    gt = NA_GROUP_TOK
    scale2 = NA_HEAD_DIM ** -0.5 * LOG2E
    nt = (((1,), (1,)), ((), ()))
    dot = functools.partial(jnp.dot, preferred_element_type=F32)

    @pl.when(pl.program_id(1) == 0)
    def _():
        left = lax.broadcasted_iota(jnp.int32, (GRID_W, 2 * GRID_W), 1) < GRID_W
        for kind, per_key_row in enumerate(slab_index):
            for j, per_query_row in enumerate(per_key_row):
                for a in range(0, NA_GROUP_ROWS, 2):
                    tile = jnp.where(left, slab_ref[0, per_query_row[a]], slab_ref[0, per_query_row[a + 1]])
                    bias_ref[kind, j * GRID_W:(j + 1) * GRID_W, a * GRID_W:(a + 2) * GRID_W] = tile

    qb_ref[...] = (q_ref[...] * scale2).astype(BF16)
    kb_ref[...] = k_ref[...].astype(BF16)
    for b in range(n_groups):
        vt_ref[b] = v_ref[b * gt:(b + 1) * gt, :].T.astype(BF16)
    kc = kc_ref[...].astype(BF16)
    vct = vc_ref[...].T.astype(BF16)

    def group(g):
        blk = jnp.clip(g - 1, 0, n_groups - NA_WIN_BLOCKS)
        kind = jnp.where(g == 0, 0, jnp.where(g == n_groups - 1, 2, 1))
        qg = qb_ref[pl.ds(pl.multiple_of(g * gt, gt), gt), :]
        kw = kb_ref[pl.ds(pl.multiple_of(blk * gt, gt), NA_WIN_BLOCKS * gt), :]
        s_lat = lax.dot_general(kw, qg, nt, preferred_element_type=F32) + bias_ref[kind]
        s_ctx = lax.dot_general(kc, qg, nt, preferred_element_type=F32)
        yield
        m = jnp.maximum(jnp.max(s_lat, axis=0, keepdims=True), jnp.max(s_ctx, axis=0, keepdims=True))
        p_lat = jnp.exp2(s_lat - m)
        p_ctx = jnp.exp2(s_ctx - m)
        l = jnp.sum(p_lat, axis=0, keepdims=True) + jnp.sum(p_ctx, axis=0, keepdims=True)
        yield
        ot = dot(vct, p_ctx.astype(BF16))
        for j in range(NA_WIN_BLOCKS):
            ot = ot + dot(vt_ref[blk + j], p_lat[j * gt:(j + 1) * gt, :].astype(BF16))
        yield
        o_ref[pl.ds(pl.multiple_of(g * gt, gt), gt), :] = (ot / l).T.astype(o_ref.dtype)
        yield

    def body(i, carry):
        groups = [group(i * NA_GROUPS_PER_STEP + u) for u in range(NA_GROUPS_PER_STEP)]
        for _ in range(4):
            for grp in groups:
                next(grp)
        return carry

    lax.fori_loop(0, n_groups // NA_GROUPS_PER_STEP, body, 0, unroll=NA_LOOP_UNROLL)


def _na_latent(p_l, p_c, slabs, b):
    t = p_l.shape[0] // b
    ctx = p_c.shape[0] // b
    hd = NA_HEAD_DIM
    return pl.pallas_call(
        functools.partial(_na_kernel, slab_index=_na_slab_index(t // GRID_W)),
        grid=(NA_HEADS, b),
        in_specs=[
            pl.BlockSpec((t, hd), lambda h, i: (i, _QA + h)),
            pl.BlockSpec((t, hd), lambda h, i: (i, _KA + h)),
            pl.BlockSpec((t, hd), lambda h, i: (i, _VA + h)),
            pl.BlockSpec((ctx, hd), lambda h, i: (i, _KA + h)),
            pl.BlockSpec((ctx, hd), lambda h, i: (i, _VA + h)),
            pl.BlockSpec((1,) + slabs.shape[1:], lambda h, i: (h, 0, 0, 0)),
        ],
        out_specs=pl.BlockSpec((t, hd), lambda h, i: (i, h)),
        out_shape=jax.ShapeDtypeStruct((b * t, NA_WIDTH), BF16),
        scratch_shapes=[
            pltpu.VMEM((t, hd), BF16),
            pltpu.VMEM((t, hd), BF16),
            pltpu.VMEM((t // NA_GROUP_TOK, hd, NA_GROUP_TOK), BF16),
            pltpu.VMEM((3, NA_WIN_BLOCKS * NA_GROUP_TOK, NA_GROUP_TOK), F32),
        ],
        compiler_params=_cparams(("arbitrary", "arbitrary")),
        name="na_latent",
    )(p_l, p_l, p_l, p_c, p_c, slabs)


def _ctx_attn_kernel(q_ref, k_ref, v_ref, o_ref):
    scale = NA_HEAD_DIM ** -0.5
    for h in range(NA_HEADS):
        hs = slice(h * NA_HEAD_DIM, (h + 1) * NA_HEAD_DIM)
        q = q_ref[:, hs].astype(BF16)
        k = k_ref[:, hs].astype(BF16)
        v = v_ref[:, hs].astype(BF16)
        s = lax.dot_general(q, k, (((1,), (1,)), ((), ())), preferred_element_type=F32) * scale
        p = jnp.exp(s - jnp.max(s, axis=-1, keepdims=True))
        l = jnp.sum(p, axis=-1, keepdims=True)
        o = jnp.dot(p.astype(BF16), v, preferred_element_type=F32)
        o_ref[:, hs] = (o / l).astype(o_ref.dtype)


def _ctx_attention(p_c, b):
    ctx = p_c.shape[0] // b
    return pl.pallas_call(
        _ctx_attn_kernel,
        grid=(b,),
        in_specs=[
            pl.BlockSpec((ctx, NA_WIDTH), lambda i: (i, 0)),
            pl.BlockSpec((ctx, NA_WIDTH), lambda i: (i, 1)),
            pl.BlockSpec((ctx, NA_WIDTH), lambda i: (i, 2)),
        ],
        out_specs=pl.BlockSpec((ctx, NA_WIDTH), lambda i: (i, 0)),
        out_shape=jax.ShapeDtypeStruct((b * ctx, NA_WIDTH), BF16),
        compiler_params=_cparams(("arbitrary",)),
        name="ctx_attention",
    )(p_c, p_c, p_c)


def _gm_block(u_ref, v_ref, lnw_ref, ws_ref, bs_ref, o_ref):
    for ck in range(u_ref.shape[0] // GM_CHUNK):
        rows = slice(ck * GM_CHUNK, (ck + 1) * GM_CHUNK)
        uf = _gelu_tanh(u_ref[rows, :])
        vf = _gelu_tanh(v_ref[rows, :])
        for g in range(GM_GROUPS):
            sl = slice(g * GM_DIM, (g + 1) * GM_DIM)
            vg = vf[:, sl]
            mu = jnp.mean(vg, axis=-1, keepdims=True)
            dv = vg - mu
            var = jnp.mean(dv * dv, axis=-1, keepdims=True)
            vn = dv * lax.rsqrt(var + EPS) * lnw_ref[:, sl]
            mixed = jnp.dot(ws_ref[g].astype(BF16), vn.astype(BF16), preferred_element_type=F32) + bs_ref[g]
            o_ref[rows, sl] = (uf[:, sl] * mixed).astype(o_ref.dtype)


HG_PAIR = 2
HG_LEVELS = (32, 16, 8)
HG_DIAG = 8
HG_PREP_ROWS = 256
HG_SCAN_UNROLL = 8


def _split3_dot(tri, g):
    g0 = g.astype(BF16)
    r1 = g - g0.astype(F32)
    g1 = r1.astype(BF16)
    g2 = (r1 - g1.astype(F32)).astype(BF16)
    dot = functools.partial(jnp.dot, preferred_element_type=F32)
    return dot(tri, g0) + dot(tri, g1) + dot(tri, g2)


def _hg_kernel(ql_ref, ffl_ref, fbl_ref, il_ref, gl_ref,
               qc_ref, ffc_ref, fbc_ref, ic_ref, gc_ref,
               lbp_ref, nw_ref, tri_ref, code_ref, lsum_ref, ol_ref, oc_ref,
               qh_s, kf_s, kb_s, ef_s, eb_s, of_s, ob_s, st_s):
    c = HG_CHUNK
    hd = HG_DIM
    n_ctx = qc_ref.shape[0]
    n_lat = ql_ref.shape[0]
    scale = hd ** -0.5
    nt = (((1,), (1,)), ((), ()))
    tn = (((0,), (0,)), ((), ()))

    def prep(q_ref, ff_ref, fb_ref, base, n):
        step = HG_PREP_ROWS
        for t0 in range(0, n, step):
            src = slice(t0, t0 + step)
            dst = slice(base + t0, base + t0 + step)
            qh = _silu(q_ref[src, :]) * scale
            for h in range(HG_PAIR):
                qh_s[h, dst, :] = qh[:, h * hd:(h + 1) * hd]
            for d, (f_ref, k_s, e_s) in enumerate(((ff_ref, kf_s, ef_s), (fb_ref, kb_s, eb_s))):
                x = f_ref[src, :]
                sp = jnp.maximum(-x, 0.0) + jnp.log(1.0 + jnp.exp(-jnp.abs(x)))
                log_lb = lbp_ref[d, 0:1, :]
                y = lbp_ref[d, 1:2, :] - sp
                mx = jnp.maximum(log_lb, y)
                log_f = mx + jnp.log(1.0 + jnp.exp(-jnp.abs(log_lb - y)))
                one_minus_f = lbp_ref[d, 2:3, :] * jnp.exp(-(sp + x))
                e = _split3_dot(tri_ref[d], log_f) * LOG2E
                for h in range(HG_PAIR):
                    e_s[h, dst, :] = e[:, h * hd:(h + 1) * hd]
                    k_s[h, dst, :] = one_minus_f[:, h * hd:(h + 1) * hd]

    prep(qc_ref, ffc_ref, fbc_ref, 0, n_ctx)
    prep(ql_ref, ffl_ref, fbl_ref, n_ctx, n_lat)

    diag_code = len(HG_LEVELS) + 1

    st_s[...] = jnp.zeros_like(st_s)

    def chunk(off, v, h, d, out):
        fwd = d == 0
        k_s, e_s = (kf_s, ef_s) if fwd else (kb_s, eb_s)
        q = qh_s[h, pl.ds(off, c), :]
        k = k_s[h, pl.ds(off, c), :]
        e = e_s[h, pl.ds(off, c), :]
        code = code_ref[d]
        prods = []
        for i in range(c // HG_DIAG):
            bs = slice(i * HG_DIAG, (i + 1) * HG_DIAG)
            row_prods = []
            for s in range(HG_DIAG):
                kr = k_s[h, pl.ds(off + i * HG_DIAG + s, 1), :]
                er = e_s[h, pl.ds(off + i * HG_DIAG + s, 1), :]
                decay = jnp.exp2(jnp.minimum(e[bs] - er, 0.0))
                row_prods.append((q[bs] * kr * decay).astype(BF16))
            prods.append(jnp.concatenate(row_prods, axis=1))
        diag = jnp.dot(jnp.concatenate(prods, axis=0), lsum_ref[...], preferred_element_type=F32)
        yield
        att = None
        for li, w in enumerate(HG_LEVELS):
            zeros = jnp.zeros((w, hd), F32)
            q_parts, k_parts = [], []
            for a in range(0, c, 2 * w):
                lo, hi = slice(a, a + w), slice(a + w, a + 2 * w)
                if fwd:
                    ref = e_s[h, pl.ds(off + a + w - 1, 1), :]
                    q_parts += [zeros, q[hi] * jnp.exp2(e[hi] - ref)]
                    k_parts += [k[lo] * jnp.exp2(ref - e[lo]), zeros]
                else:
                    ref = e_s[h, pl.ds(off + a + w, 1), :]
                    q_parts += [q[lo] * jnp.exp2(e[lo] - ref), zeros]
                    k_parts += [zeros, k[hi] * jnp.exp2(ref - e[hi])]
            qs = jnp.concatenate(q_parts, axis=0).astype(BF16)
            ks = jnp.concatenate(k_parts, axis=0).astype(BF16)
            a_w = lax.dot_general(qs, ks, nt, preferred_element_type=F32)
            att = a_w if att is None else jnp.where(code == li + 1, a_w, att)
            yield
        e_end = e_s[h, pl.ds(off + (c - 1 if fwd else 0), 1), :]
        st = st_s[h, d]
        qi = (q * jnp.exp2(e)).astype(BF16)
        ki = (k * jnp.exp2(e_end - e)).astype(BF16)
        o = lax.dot_general(qi, st.astype(BF16), nt, preferred_element_type=F32)
        yield
        st_new = st * jnp.exp2(e_end) + lax.dot_general(v, ki, tn, preferred_element_type=F32)
        yield
        att = jnp.where(code == diag_code, diag, att)
        out.append((o + jnp.dot(att.astype(BF16), v, preferred_element_type=F32), st_new))
        yield

    n_stages = len(HG_LEVELS) + 4

    def scan(v_ref, base, n):
        def body(i, carry):
            rf = pl.multiple_of(i * c, c)
            rb = pl.multiple_of((n - 1 - i) * c, c)
            vf = v_ref[pl.ds(rf, c), :].astype(BF16)
            vb = v_ref[pl.ds(rb, c), :].astype(BF16)
            chains = []
            for h in range(HG_PAIR):
                hs = slice(h * hd, (h + 1) * hd)
                for d, r, v in ((0, rf, vf), (1, rb, vb)):
                    out = []
                    chains.append((h, d, r, out, chunk(base + r, v[:, hs], h, d, out)))
            for _ in range(n_stages):
                for chain in chains:
                    next(chain[-1])
            for h, d, r, out, _ in chains:
                o, st_new = out[0]
                (of_s if d == 0 else ob_s)[h, pl.ds(base + r, c), :] = o
                st_s[h, d] = st_new
            return carry

        lax.fori_loop(0, n, body, 0, unroll=HG_SCAN_UNROLL)

    scan(ic_ref, 0, n_ctx // c)
    scan(il_ref, n_ctx, n_lat // c)

    def finish(g_ref, o_ref, base, n):
        step = 256
        for t0 in range(0, n, step):
            src = slice(t0, t0 + step)
            dst = slice(base + t0, base + t0 + step)
            for h in range(HG_PAIR):
                hs = slice(h * hd, (h + 1) * hd)
                o = of_s[h, dst, :] + ob_s[h, dst, :]
                o = o * lax.rsqrt(jnp.mean(o * o, axis=-1, keepdims=True) + EPS) * nw_ref[...]
                o_ref[src, hs] = (o * _silu(g_ref[src, hs])).astype(o_ref.dtype)

    finish(gc_ref, oc_ref, 0, n_ctx)
    finish(gl_ref, ol_ref, n_ctx, n_lat)


def _hg_constants():
    c = HG_CHUNK
    row, col = np.arange(c)[:, None], np.arange(c)[None, :]
    blocks = np.eye(HG_PREP_ROWS // c)
    tri = np.stack([np.kron(blocks, col <= row), np.kron(blocks, col >= row)]).astype(np.float32)
    same8 = (row // HG_DIAG) == (col // HG_DIAG)
    diag_code = len(HG_LEVELS) + 1
    code_f = np.where(same8 & (col <= row), diag_code, 0)
    code_b = np.where(same8 & (col >= row), diag_code, 0)
    for li, w in enumerate(HG_LEVELS):
        same = (row // (2 * w)) == (col // (2 * w))
        t_hi, s_hi = (row % (2 * w)) >= w, (col % (2 * w)) >= w
        code_f = np.where(same & t_hi & ~s_hi, li + 1, code_f)
        code_b = np.where(same & ~t_hi & s_hi, li + 1, code_b)
    codes = np.stack([code_f, code_b]).astype(np.int32)
    lane_sum = (np.arange(HG_DIAG * HG_DIM)[:, None] // HG_DIM == col % HG_DIAG).astype(np.float32)
    return jnp.asarray(tri, BF16), jnp.asarray(codes), jnp.asarray(lane_sum, BF16)


def _hgrn2(p_l, p_c, lbp, norm_w, b):
    tri, codes, lane_sum = _hg_constants()
    t = p_l.shape[0] // b
    ctx = p_c.shape[0] // b
    hd = HG_DIM
    pw = HG_PAIR * hd
    n = t + ctx
    first = _HG0 * LANE // pw

    def col(stream):
        return lambda i, j: (i, first + stream * (HG_HEADS // HG_PAIR) + j)

    lat_specs = [pl.BlockSpec((t, pw), col(s)) for s in range(5)]
    ctx_specs = [pl.BlockSpec((ctx, pw), col(s)) for s in range(5)]
    big = lambda: pltpu.VMEM((HG_PAIR, n, hd), F32)
    return pl.pallas_call(
        _hg_kernel,
        grid=(b, HG_HEADS // HG_PAIR),
        in_specs=lat_specs + ctx_specs + [
            pl.BlockSpec((2, 3, pw), lambda i, j: (0, 0, j)),
            pl.BlockSpec((1, hd), lambda i, j: (0, 0)),
            pl.BlockSpec(tri.shape, lambda i, j: (0, 0, 0)),
            pl.BlockSpec(codes.shape, lambda i, j: (0, 0, 0)),
            pl.BlockSpec(lane_sum.shape, lambda i, j: (0, 0)),
        ],
        out_specs=[
            pl.BlockSpec((t, pw), lambda i, j: (i, j)),
            pl.BlockSpec((ctx, pw), lambda i, j: (i, j)),
        ],
        out_shape=[
            jax.ShapeDtypeStruct((b * t, HG_WIDTH), BF16),
            jax.ShapeDtypeStruct((b * ctx, HG_WIDTH), BF16),
        ],
        scratch_shapes=[big() for _ in range(7)] + [pltpu.VMEM((HG_PAIR, 2, hd, hd), F32)],
        compiler_params=_cparams(("arbitrary", "arbitrary")),
        name="hgrn2",
    )(*([p_l] * 5 + [p_c] * 5 + [lbp, norm_w.reshape(1, hd), tri, codes, lane_sum]))


def _outproj_kernel(oa_ref, ob_ref, u_ref, v_ref, lnw_ref, ws_ref, bs_ref, wa_ref, wb_ref, wc_ref,
                    x_ref, g_ref, o_ref, oc_ref):
    dot = functools.partial(jnp.dot, preferred_element_type=F32)
    y = dot(oa_ref[...], wa_ref[...]) + dot(ob_ref[...], wb_ref[...])
    _gm_block(u_ref, v_ref, lnw_ref, ws_ref, bs_ref, oc_ref)
    y = y + dot(oc_ref[...], wc_ref[...])
    o_ref[...] = x_ref[...] + g_ref[0] * y


def _outproj(oa, ob, p2, gm_ln_w, gm_ws, gm_bs, w, x2d, mod, layer, rows_per_mod, mod_row0, tm):
    m, d = x2d.shape
    mrow = lambda i: layer * MOD_ROWS + mod_row0 + (i * tm) // rows_per_mod
    nb = NA_WIDTH // HG_WIDTH
    assert tm % GM_CHUNK == 0
    return pl.pallas_call(
        _outproj_kernel,
        grid=(m // tm,),
        in_specs=[
            pl.BlockSpec((tm, NA_WIDTH), lambda i: (i, 0)),
            pl.BlockSpec((tm, HG_WIDTH), lambda i: (i, 0)),
            pl.BlockSpec((tm, GM_WIDTH), lambda i: (i, _GM_U)),
            pl.BlockSpec((tm, GM_WIDTH), lambda i: (i, _GM_V)),
            pl.BlockSpec((1, GM_WIDTH), lambda i: (0, 0)),
            pl.BlockSpec((GM_GROUPS, GM_CHUNK, GM_CHUNK), lambda i: (0, 0, 0)),
            pl.BlockSpec((GM_GROUPS, GM_CHUNK, 1), lambda i: (0, 0, 0)),
            pl.BlockSpec((None, NA_WIDTH, d), lambda i: (layer, 0, 0)),
            pl.BlockSpec((None, HG_WIDTH, d), lambda i: (layer, nb, 0)),
            pl.BlockSpec((None, GM_WIDTH, d), lambda i: (layer, nb + 1, 0)),
            pl.BlockSpec((tm, d), lambda i: (i, 0)),
            pl.BlockSpec((1, 1, d), lambda i: (mrow(i), 0, 2)),
        ],
        out_specs=pl.BlockSpec((tm, d), lambda i: (i, 0)),
        out_shape=jax.ShapeDtypeStruct((m, d), F32),
        scratch_shapes=[pltpu.VMEM((tm, GM_WIDTH), BF16)],
        compiler_params=_cparams(("arbitrary",)),
        name="outproj",
    )(oa, ob, p2, p2, gm_ln_w.reshape(1, GM_WIDTH), gm_ws, gm_bs.reshape(GM_GROUPS, GM_CHUNK, 1),
      w, w, w, x2d, mod)


def _mlp_kernel(x_ref, nw_ref, sh_ref, sc_ref, g_ref, w1_ref, w2_ref, fnw_ref, o_ref, h_ref, acc_ref,
                *, final_norm):
    j = pl.program_id(1)

    @pl.when(j == 0)
    def _():
        _norm_modulate_store(x_ref, nw_ref, sh_ref, sc_ref, h_ref)
        acc_ref[...] = jnp.zeros_like(acc_ref)

    a = jnp.maximum(jnp.dot(h_ref[...], w1_ref[...], preferred_element_type=F32), 0.0)
    acc_ref[...] += jnp.dot((a * a).astype(BF16), w2_ref[...], preferred_element_type=F32)

    @pl.when(j == pl.num_programs(1) - 1)
    def _():
        y = x_ref[...] + g_ref[0] * acc_ref[...]
        if final_norm:
            y = y * lax.rsqrt(jnp.mean(y * y, axis=-1, keepdims=True) + EPS) * fnw_ref[...]
        o_ref[...] = y


def _mlp(x2d, nw, mod, w1, w2, fnw, layer, rows_per_mod, mod_row0, tm, final_norm):
    m, d = x2d.shape
    hid = w1.shape[2]
    th = MLP_TH
    mrow = lambda i: layer * MOD_ROWS + mod_row0 + (i * tm) // rows_per_mod
    return pl.pallas_call(
        functools.partial(_mlp_kernel, final_norm=final_norm),
        grid=(m // tm, hid // th),
        in_specs=[
            pl.BlockSpec((tm, d), lambda i, j: (i, 0)),
            pl.BlockSpec((None, 1, d), lambda i, j: (layer, 0, 0)),
            pl.BlockSpec((1, 1, d), lambda i, j: (mrow(i), 0, 3)),
            pl.BlockSpec((1, 1, d), lambda i, j: (mrow(i), 0, 4)),
            pl.BlockSpec((1, 1, d), lambda i, j: (mrow(i), 0, 5)),
            pl.BlockSpec((None, d, th), lambda i, j: (layer, 0, j)),
            pl.BlockSpec((None, th, d), lambda i, j: (layer, j, 0)),
            pl.BlockSpec((1, d), lambda i, j: (0, 0)),
        ],
        out_specs=pl.BlockSpec((tm, d), lambda i, j: (i, 0)),
        out_shape=jax.ShapeDtypeStruct((m, d), F32),
        scratch_shapes=[pltpu.VMEM((tm, d), BF16), pltpu.VMEM((tm, d), F32)],
        compiler_params=_cparams(("arbitrary", "arbitrary")),
        name="mlp",
    )(x2d, nw, mod, mod, mod, w1, w2, fnw)


def kernel(x, c, ctx, c_ctx, ada_w, ada_b, norm1_w, norm2_w, w_in, na_rpb, hg_lb_logits, hg_norm_w,
           gm_ln_w, gm_ws, gm_bs, w_out, mlp_w1, mlp_w2, final_norm_w):
    bsz, seq, d = x.shape
    n_ctx = ctx.shape[1]
    depth = ada_w.shape[0]
    assert bsz < MOD_ROWS and d == D_MODEL and seq % 512 == 0 and n_ctx % 256 == 0
    assert seq // NA_GROUP_TOK >= NA_WIN_BLOCKS and seq // GRID_W >= 2 * NA_WIN_ROWS

    lb = jnp.cumsum(jax.nn.softmax(hg_lb_logits.astype(F32), axis=0), axis=0)
    lb = lb - lb[:1]
    lbp = jnp.stack([jnp.log(lb), jnp.log1p(-lb), 1.0 - lb], axis=2)

    cond = jnp.zeros((MOD_ROWS, d), F32).at[:bsz].set(c).at[bsz].set(c_ctx)
    mod = _ada(cond, ada_w, ada_b).reshape(depth * MOD_ROWS, 1, 6 * d)

    w_in_b, w_out_b = w_in.astype(BF16), w_out.astype(BF16)
    w1_b, w2_b = mlp_w1.astype(BF16), mlp_w2.astype(BF16)
    nw1 = norm1_w.reshape(depth, 1, d)
    nw2 = norm2_w.reshape(depth, 1, d)
    fnw = final_norm_w.reshape(1, d)

    xl = x.reshape(bsz * seq, d)
    xc = ctx.reshape(bsz * n_ctx, d)
    n_c = bsz * n_ctx
    tm_l, tm_c = TM_LATENT, min(TM_CONTEXT, n_c)
    for l in range(depth):
        need_ctx = l < depth - 1
        p_l = _inproj(xl, nw1, mod, w_in_b, l, seq, 0, TM_INPROJ)
        p_c = _inproj(xc, nw1, mod, w_in_b, l, n_c, bsz, tm_c)

        oa_l = _na_latent(p_l, p_c, _na_bias_slabs(na_rpb[l]), bsz)
        ob_l, ob_c = _hgrn2(p_l, p_c, lbp[l], hg_norm_w[l], bsz)
        gm = (gm_ln_w[l], gm_ws[l], gm_bs[l])
        xl = _outproj(oa_l, ob_l, p_l, *gm, w_out_b, xl, mod, l, seq, 0, tm_l)
        xl = _mlp(xl, nw2, mod, w1_b, w2_b, fnw, l, seq, 0, tm_l, final_norm=not need_ctx)
        if need_ctx:
            oa_c = _ctx_attention(p_c, bsz)
            xc = _outproj(oa_c, ob_c, p_c, *gm, w_out_b, xc, mod, l, n_c, bsz, tm_c)
            xc = _mlp(xc, nw2, mod, w1_b, w2_b, fnw, l, n_c, bsz, tm_c, final_norm=False)
    return xl.reshape(bsz, seq, d)
```

```python
import functools

import numpy as np
import jax
import jax.numpy as jnp
from jax import lax
from jax.experimental import pallas as pl
from jax.experimental.pallas import tpu as pltpu

F32 = jnp.float32
BF16 = jnp.bfloat16

D_MODEL = 2048
DEPTH = 2
GRID_W = 64
EPS = 1e-6

NA_HEAD_DIM = 128
NA_HEADS = 8
NA_WIDTH = NA_HEADS * NA_HEAD_DIM
NA_WIN_ROWS = 8
NA_WIN_COLS = 16

HG_HEADS = 4
HG_DIM = 128
HG_WIDTH = HG_HEADS * HG_DIM
HG_CHUNK = 64

GM_GROUPS = 4
GM_DIM = 128
GM_WIDTH = GM_GROUPS * GM_DIM
GM_CHUNK = 128

IN_WIDTH = 3 * NA_WIDTH + 5 * HG_WIDTH + 2 * GM_WIDTH
MLP_HIDDEN = 4 * D_MODEL
LANE = 128
LOG2E = 1.4426950408889634
BF16_SUBLANES = 16

_QA, _KA, _VA = 0, NA_HEADS, 2 * NA_HEADS
_HG0 = 3 * NA_HEADS
_GM_U = (3 * NA_WIDTH + 5 * HG_WIDTH) // GM_WIDTH
_GM_V = _GM_U + 1

VMEM_LIMIT_V7X = 56 * 1024 * 1024
TM_LATENT = 512
TM_CONTEXT = 512
TM_INPROJ = 512
INPROJ_TN = 3328
MLP_TH = 1024
NORM_CHUNK_ROWS = 32
NORM_UNROLL = 4
MOD_ROWS = 16


def _cparams(sem):
    return pltpu.CompilerParams(dimension_semantics=sem, vmem_limit_bytes=VMEM_LIMIT_V7X)


def _silu(x):
    h = 0.5 * x
    return h + h * jnp.tanh(h)


def _gelu_tanh(x):
    return 0.5 * x * (1.0 + jnp.tanh(0.7978845608028654 * (x + 0.044715 * (x * x * x))))


def _norm_modulate_store(x_ref, nw_ref, sh_ref, sc_ref, h_ref):
    gain = nw_ref[...] * (1.0 + sc_ref[0])
    shift = sh_ref[0]

    def body(i, carry):
        rows = pl.ds(pl.multiple_of(i * NORM_CHUNK_ROWS, NORM_CHUNK_ROWS), NORM_CHUNK_ROWS)
        x = x_ref[rows, :]
        rs = lax.rsqrt(jnp.mean(x * x, axis=-1, keepdims=True) + EPS)
        h_ref[rows, :] = (x * rs * gain + shift).astype(h_ref.dtype)
        return carry

    lax.fori_loop(0, x_ref.shape[0] // NORM_CHUNK_ROWS, body, 0, unroll=NORM_UNROLL)


def _ada_kernel(c_ref, w_ref, b_ref, o_ref):
    s = _silu(c_ref[...]).astype(BF16)
    o_ref[0] = jnp.dot(s, w_ref[0].astype(BF16), preferred_element_type=F32) + b_ref[0]


def _ada(cond, ada_w, ada_b):
    depth, d, n = ada_w.shape
    tn = 1536
    return pl.pallas_call(
        _ada_kernel,
        grid=(depth, n // tn),
        in_specs=[
            pl.BlockSpec((cond.shape[0], d), lambda l, j: (0, 0)),
            pl.BlockSpec((1, d, tn), lambda l, j: (l, 0, j)),
            pl.BlockSpec((1, 1, tn), lambda l, j: (l, 0, j)),
        ],
        out_specs=pl.BlockSpec((1, cond.shape[0], tn), lambda l, j: (l, 0, j)),
        out_shape=jax.ShapeDtypeStruct((depth, cond.shape[0], n), F32),
        compiler_params=_cparams(("arbitrary", "arbitrary")),
        name="ada",
    )(cond, ada_w, ada_b.reshape(depth, 1, n))


def _inproj_kernel(x_ref, nw_ref, sh_ref, sc_ref, w_ref, o_ref, h_ref):
    @pl.when(pl.program_id(1) == 0)
    def _():
        _norm_modulate_store(x_ref, nw_ref, sh_ref, sc_ref, h_ref)

    o_ref[...] = jnp.dot(h_ref[...], w_ref[...], preferred_element_type=F32)


def _inproj(x2d, nw, mod, w, layer, rows_per_mod, mod_row0, tm):
    m, d = x2d.shape
    n = w.shape[2]
    tn = INPROJ_TN
    mrow = lambda i: layer * MOD_ROWS + mod_row0 + (i * tm) // rows_per_mod
    return pl.pallas_call(
        _inproj_kernel,
        grid=(m // tm, n // tn),
        in_specs=[
            pl.BlockSpec((tm, d), lambda i, j: (i, 0)),
            pl.BlockSpec((None, 1, d), lambda i, j: (layer, 0, 0)),
            pl.BlockSpec((1, 1, d), lambda i, j: (mrow(i), 0, 0)),
            pl.BlockSpec((1, 1, d), lambda i, j: (mrow(i), 0, 1)),
            pl.BlockSpec((None, d, tn), lambda i, j: (layer, 0, j)),
        ],
        out_specs=pl.BlockSpec((tm, tn), lambda i, j: (i, j)),
        out_shape=jax.ShapeDtypeStruct((m, n), F32),
        scratch_shapes=[pltpu.VMEM((tm, d), BF16)],
        compiler_params=_cparams(("arbitrary", "arbitrary")),
        name="inproj",
    )(x2d, nw, mod, mod, w)


NA_GROUP_ROWS = 4
NA_GROUP_TOK = NA_GROUP_ROWS * GRID_W
NA_WIN_BLOCKS = 3
NA_GROUPS_PER_STEP = 4
NA_LOOP_UNROLL = 2


NA_MASKED_SLAB = 2 * NA_WIN_ROWS - 1


def _na_bias_slabs(rpb):
    h = rpb.shape[0]
    kcol = np.arange(GRID_W)[:, None]
    qcol = np.arange(GRID_W)[None, :]
    wstart = np.clip(qcol - NA_WIN_COLS // 2, 0, GRID_W - NA_WIN_COLS)
    valid_col = (kcol >= wstart) & (kcol < wstart + NA_WIN_COLS)
    pad = GRID_W - NA_WIN_COLS
    padded = jnp.pad(rpb.astype(F32), ((0, 0), (0, 0), (pad, pad)))
    m = jnp.tile(padded, (1, 1, GRID_W + 1))[:, :, :GRID_W * 2 * GRID_W]
    m = m.reshape(h, rpb.shape[1], GRID_W, 2 * GRID_W)[..., :GRID_W]
    slabs = jnp.where(valid_col, m[..., ::-1] * LOG2E, -1e30)
    slabs = jnp.concatenate([slabs, jnp.full((h, 1, GRID_W, GRID_W), -1e30, F32)], axis=1)
    return jnp.concatenate([slabs, slabs], axis=-1)


def _na_slab_index(rows):
    n_groups = rows // NA_GROUP_ROWS
    krel = np.arange(NA_WIN_BLOCKS * NA_GROUP_ROWS)[:, None]
    qrel = np.arange(NA_GROUP_ROWS)[None, :]
    idx = []
    for g in (0, 1, n_groups - 1):
        krow = int(np.clip(g - 1, 0, n_groups - NA_WIN_BLOCKS)) * NA_GROUP_ROWS + krel
        qrow = g * NA_GROUP_ROWS + qrel
        r0 = np.clip(qrow - NA_WIN_ROWS // 2, 0, rows - NA_WIN_ROWS)
        in_window = (krow >= r0) & (krow < r0 + NA_WIN_ROWS)
        idx.append(np.where(in_window, krow - qrow + NA_WIN_ROWS - 1, NA_MASKED_SLAB))
    return np.stack(idx).tolist()


def _na_kernel(q_ref, k_ref, v_ref, kc_ref, vc_ref, slab_ref, wo_ref, w1_ref, w2_ref,
               o_ref, wo_b_ref, w1_b_ref, w2_b_ref, qb_ref, kb_ref, vt_ref, bias_ref, *, slab_index):
    n_groups = q_ref.shape[0] // NA_GROUP_TOK
    gt = NA_GROUP_TOK
    scale2 = NA_HEAD_DIM ** -0.5 * LOG2E
    nt = (((1,), (1,)), ((), ()))
    dot = functools.partial(jnp.dot, preferred_element_type=F32)

    for src_ref, dst_ref in ((wo_ref, wo_b_ref), (w1_ref, w1_b_ref), (w2_ref, w2_b_ref)):
        dst_ref[...] = src_ref[...].astype(BF16)

    @pl.when(pl.program_id(1) == 0)
    def _():
        left = lax.broadcasted_iota(jnp.int32, (GRID_W, 2 * GRID_W), 1) < GRID_W
        for kind, per_key_row in enumerate(slab_index):
            for j, per_query_row in enumerate(per_key_row):
                for a in range(0, NA_GROUP_ROWS, 2):
                    tile = jnp.where(left, slab_ref[0, per_query_row[a]], slab_ref[0, per_query_row[a + 1]])
                    bias_ref[kind, j * GRID_W:(j + 1) * GRID_W, a * GRID_W:(a + 2) * GRID_W] = tile

    qb_ref[...] = (q_ref[...] * scale2).astype(BF16)
    kb_ref[...] = k_ref[...].astype(BF16)
    for b in range(n_groups):
        vt_ref[b] = v_ref[b * gt:(b + 1) * gt, :].T.astype(BF16)
    kc = kc_ref[...].astype(BF16)
    vct = vc_ref[...].T.astype(BF16)

    def group(g):
        blk = jnp.clip(g - 1, 0, n_groups - NA_WIN_BLOCKS)
        kind = jnp.where(g == 0, 0, jnp.where(g == n_groups - 1, 2, 1))
        qg = qb_ref[pl.ds(pl.multiple_of(g * gt, gt), gt), :]
        kw = kb_ref[pl.ds(pl.multiple_of(blk * gt, gt), NA_WIN_BLOCKS * gt), :]
        s_lat = lax.dot_general(kw, qg, nt, preferred_element_type=F32) + bias_ref[kind]
        s_ctx = lax.dot_general(kc, qg, nt, preferred_element_type=F32)
        yield
        m = jnp.maximum(jnp.max(s_lat, axis=0, keepdims=True), jnp.max(s_ctx, axis=0, keepdims=True))
        p_lat = jnp.exp2(s_lat - m)
        p_ctx = jnp.exp2(s_ctx - m)
        l = jnp.sum(p_lat, axis=0, keepdims=True) + jnp.sum(p_ctx, axis=0, keepdims=True)
        yield
        ot = dot(vct, p_ctx.astype(BF16))
        for j in range(NA_WIN_BLOCKS):
            ot = ot + dot(vt_ref[blk + j], p_lat[j * gt:(j + 1) * gt, :].astype(BF16))
        yield
        o_ref[pl.ds(pl.multiple_of(g * gt, gt), gt), :] = (ot / l).T.astype(o_ref.dtype)
        yield

    def body(i, carry):
        groups = [group(i * NA_GROUPS_PER_STEP + u) for u in range(NA_GROUPS_PER_STEP)]
        for _ in range(4):
            for grp in groups:
                next(grp)
        return carry

    lax.fori_loop(0, n_groups // NA_GROUPS_PER_STEP, body, 0, unroll=NA_LOOP_UNROLL)


def _na_latent(p_l, p_c, slabs, weights, layer, b):
    t = p_l.shape[0] // b
    ctx = p_c.shape[0] // b
    hd = NA_HEAD_DIM
    n_steps = NA_HEADS * b
    w_specs_in, w_specs_out, w_shapes = [], [], []
    for w in weights:
        rows, cols = w.shape[1] // n_steps, w.shape[2]
        assert w.shape[1] % n_steps == 0 and rows % BF16_SUBLANES == 0
        w_specs_in.append(pl.BlockSpec((None, rows, cols), lambda h, i: (layer, h * b + i, 0)))
        w_specs_out.append(pl.BlockSpec((None, rows, cols), lambda h, i: (0, h * b + i, 0)))
        w_shapes.append(jax.ShapeDtypeStruct((1,) + w.shape[1:], BF16))
    oa, *w_bf16 = pl.pallas_call(
        functools.partial(_na_kernel, slab_index=_na_slab_index(t // GRID_W)),
        grid=(NA_HEADS, b),
        in_specs=[
            pl.BlockSpec((t, hd), lambda h, i: (i, _QA + h)),
            pl.BlockSpec((t, hd), lambda h, i: (i, _KA + h)),
            pl.BlockSpec((t, hd), lambda h, i: (i, _VA + h)),
            pl.BlockSpec((ctx, hd), lambda h, i: (i, _KA + h)),
            pl.BlockSpec((ctx, hd), lambda h, i: (i, _VA + h)),
            pl.BlockSpec((1,) + slabs.shape[1:], lambda h, i: (h, 0, 0, 0)),
        ] + w_specs_in,
        out_specs=[pl.BlockSpec((t, hd), lambda h, i: (i, h))] + w_specs_out,
        out_shape=[jax.ShapeDtypeStruct((b * t, NA_WIDTH), BF16)] + w_shapes,
        scratch_shapes=[
            pltpu.VMEM((t, hd), BF16),
            pltpu.VMEM((t, hd), BF16),
            pltpu.VMEM((t // NA_GROUP_TOK, hd, NA_GROUP_TOK), BF16),
            pltpu.VMEM((3, NA_WIN_BLOCKS * NA_GROUP_TOK, NA_GROUP_TOK), F32),
        ],
        compiler_params=_cparams(("arbitrary", "arbitrary")),
        name="na_latent",
    )(p_l, p_l, p_l, p_c, p_c, slabs, *weights)
    return oa, w_bf16


def _ctx_attn_kernel(q_ref, k_ref, v_ref, o_ref):
    scale = NA_HEAD_DIM ** -0.5
    for h in range(NA_HEADS):
        hs = slice(h * NA_HEAD_DIM, (h + 1) * NA_HEAD_DIM)
        q = q_ref[:, hs].astype(BF16)
        k = k_ref[:, hs].astype(BF16)
        v = v_ref[:, hs].astype(BF16)
        s = lax.dot_general(q, k, (((1,), (1,)), ((), ())), preferred_element_type=F32) * scale
        p = jnp.exp(s - jnp.max(s, axis=-1, keepdims=True))
        l = jnp.sum(p, axis=-1, keepdims=True)
        o = jnp.dot(p.astype(BF16), v, preferred_element_type=F32)
        o_ref[:, hs] = (o / l).astype(o_ref.dtype)


def _ctx_attention(p_c, b):
    ctx = p_c.shape[0] // b
    return pl.pallas_call(
        _ctx_attn_kernel,
        grid=(b,),
        in_specs=[
            pl.BlockSpec((ctx, NA_WIDTH), lambda i: (i, 0)),
            pl.BlockSpec((ctx, NA_WIDTH), lambda i: (i, 1)),
            pl.BlockSpec((ctx, NA_WIDTH), lambda i: (i, 2)),
        ],
        out_specs=pl.BlockSpec((ctx, NA_WIDTH), lambda i: (i, 0)),
        out_shape=jax.ShapeDtypeStruct((b * ctx, NA_WIDTH), BF16),
        compiler_params=_cparams(("arbitrary",)),
        name="ctx_attention",
    )(p_c, p_c, p_c)


def _gm_block(u_ref, v_ref, lnw_ref, ws_ref, bs_ref, o_ref):
    for ck in range(u_ref.shape[0] // GM_CHUNK):
        rows = slice(ck * GM_CHUNK, (ck + 1) * GM_CHUNK)
        uf = _gelu_tanh(u_ref[rows, :])
        vf = _gelu_tanh(v_ref[rows, :])
        for g in range(GM_GROUPS):
            sl = slice(g * GM_DIM, (g + 1) * GM_DIM)
            vg = vf[:, sl]
            mu = jnp.mean(vg, axis=-1, keepdims=True)
            dv = vg - mu
            var = jnp.mean(dv * dv, axis=-1, keepdims=True)
            vn = dv * lax.rsqrt(var + EPS) * lnw_ref[:, sl]
            mixed = jnp.dot(ws_ref[g].astype(BF16), vn.astype(BF16), preferred_element_type=F32) + bs_ref[g]
            o_ref[rows, sl] = (uf[:, sl] * mixed).astype(o_ref.dtype)


HG_PAIR = 2
HG_LEVELS = (32, 16, 8)
HG_DIAG = 8
HG_PREP_ROWS = 256
HG_SCAN_UNROLL = 8


def _split3_dot(tri, g):
    g0 = g.astype(BF16)
    r1 = g - g0.astype(F32)
    g1 = r1.astype(BF16)
    g2 = (r1 - g1.astype(F32)).astype(BF16)
    dot = functools.partial(jnp.dot, preferred_element_type=F32)
    return dot(tri, g0) + dot(tri, g1) + dot(tri, g2)


def _hg_kernel(ql_ref, ffl_ref, fbl_ref, il_ref, gl_ref,
               qc_ref, ffc_ref, fbc_ref, ic_ref, gc_ref,
               lbp_ref, nw_ref, tri_ref, code_ref, lsum_ref, ol_ref, oc_ref,
               qh_s, kf_s, kb_s, ef_s, eb_s, of_s, ob_s, st_s):
    c = HG_CHUNK
    hd = HG_DIM
    n_ctx = qc_ref.shape[0]
    n_lat = ql_ref.shape[0]
    scale = hd ** -0.5
    nt = (((1,), (1,)), ((), ()))
    tn = (((0,), (0,)), ((), ()))

    def prep(q_ref, ff_ref, fb_ref, base, n):
        step = HG_PREP_ROWS
        for t0 in range(0, n, step):
            src = slice(t0, t0 + step)
            dst = slice(base + t0, base + t0 + step)
            qh = _silu(q_ref[src, :]) * scale
            for h in range(HG_PAIR):
                qh_s[h, dst, :] = qh[:, h * hd:(h + 1) * hd]
            for d, (f_ref, k_s, e_s) in enumerate(((ff_ref, kf_s, ef_s), (fb_ref, kb_s, eb_s))):
                x = f_ref[src, :]
                sp = jnp.maximum(-x, 0.0) + jnp.log(1.0 + jnp.exp(-jnp.abs(x)))
                log_lb = lbp_ref[d, 0:1, :]
                y = lbp_ref[d, 1:2, :] - sp
                mx = jnp.maximum(log_lb, y)
                log_f = mx + jnp.log(1.0 + jnp.exp(-jnp.abs(log_lb - y)))
                one_minus_f = lbp_ref[d, 2:3, :] * jnp.exp(-(sp + x))
                e = _split3_dot(tri_ref[d], log_f) * LOG2E
                for h in range(HG_PAIR):
                    e_s[h, dst, :] = e[:, h * hd:(h + 1) * hd]
                    k_s[h, dst, :] = one_minus_f[:, h * hd:(h + 1) * hd]

    prep(qc_ref, ffc_ref, fbc_ref, 0, n_ctx)
    prep(ql_ref, ffl_ref, fbl_ref, n_ctx, n_lat)

    diag_code = len(HG_LEVELS) + 1

    st_s[...] = jnp.zeros_like(st_s)

    def chunk(off, v, h, d, out):
        fwd = d == 0
        k_s, e_s = (kf_s, ef_s) if fwd else (kb_s, eb_s)
        q = qh_s[h, pl.ds(off, c), :]
        k = k_s[h, pl.ds(off, c), :]
        e = e_s[h, pl.ds(off, c), :]
        code = code_ref[d]
        prods = []
        for i in range(c // HG_DIAG):
            bs = slice(i * HG_DIAG, (i + 1) * HG_DIAG)
            row_prods = []
            for s in range(HG_DIAG):
                kr = k_s[h, pl.ds(off + i * HG_DIAG + s, 1), :]
                er = e_s[h, pl.ds(off + i * HG_DIAG + s, 1), :]
                decay = jnp.exp2(jnp.minimum(e[bs] - er, 0.0))
                row_prods.append((q[bs] * kr * decay).astype(BF16))
            prods.append(jnp.concatenate(row_prods, axis=1))
        diag = jnp.dot(jnp.concatenate(prods, axis=0), lsum_ref[...], preferred_element_type=F32)
        yield
        att = None
        for li, w in enumerate(HG_LEVELS):
            zeros = jnp.zeros((w, hd), F32)
            q_parts, k_parts = [], []
            for a in range(0, c, 2 * w):
                lo, hi = slice(a, a + w), slice(a + w, a + 2 * w)
                if fwd:
                    ref = e_s[h, pl.ds(off + a + w - 1, 1), :]
                    q_parts += [zeros, q[hi] * jnp.exp2(e[hi] - ref)]
                    k_parts += [k[lo] * jnp.exp2(ref - e[lo]), zeros]
                else:
                    ref = e_s[h, pl.ds(off + a + w, 1), :]
                    q_parts += [q[lo] * jnp.exp2(e[lo] - ref), zeros]
                    k_parts += [zeros, k[hi] * jnp.exp2(ref - e[hi])]
            qs = jnp.concatenate(q_parts, axis=0).astype(BF16)
            ks = jnp.concatenate(k_parts, axis=0).astype(BF16)
            a_w = lax.dot_general(qs, ks, nt, preferred_element_type=F32)
            att = a_w if att is None else jnp.where(code == li + 1, a_w, att)
            yield
        e_end = e_s[h, pl.ds(off + (c - 1 if fwd else 0), 1), :]
        st = st_s[h, d]
        qi = (q * jnp.exp2(e)).astype(BF16)
        ki = (k * jnp.exp2(e_end - e)).astype(BF16)
        o = lax.dot_general(qi, st.astype(BF16), nt, preferred_element_type=F32)
        yield
        st_new = st * jnp.exp2(e_end) + lax.dot_general(v, ki, tn, preferred_element_type=F32)
        yield
        att = jnp.where(code == diag_code, diag, att)
        out.append((o + jnp.dot(att.astype(BF16), v, preferred_element_type=F32), st_new))
        yield

    n_stages = len(HG_LEVELS) + 4

    def scan(v_ref, base, n):
        def body(i, carry):
            rf = pl.multiple_of(i * c, c)
            rb = pl.multiple_of((n - 1 - i) * c, c)
            vf = v_ref[pl.ds(rf, c), :].astype(BF16)
            vb = v_ref[pl.ds(rb, c), :].astype(BF16)
            chains = []
            for h in range(HG_PAIR):
                hs = slice(h * hd, (h + 1) * hd)
                for d, r, v in ((0, rf, vf), (1, rb, vb)):
                    out = []
                    chains.append((h, d, r, out, chunk(base + r, v[:, hs], h, d, out)))
            for _ in range(n_stages):
                for chain in chains:
                    next(chain[-1])
            for h, d, r, out, _ in chains:
                o, st_new = out[0]
                (of_s if d == 0 else ob_s)[h, pl.ds(base + r, c), :] = o
                st_s[h, d] = st_new
            return carry

        lax.fori_loop(0, n, body, 0, unroll=HG_SCAN_UNROLL)

    scan(ic_ref, 0, n_ctx // c)
    scan(il_ref, n_ctx, n_lat // c)

    def finish(g_ref, o_ref, base, n):
        step = 256
        for t0 in range(0, n, step):
            src = slice(t0, t0 + step)
            dst = slice(base + t0, base + t0 + step)
            for h in range(HG_PAIR):
                hs = slice(h * hd, (h + 1) * hd)
                o = of_s[h, dst, :] + ob_s[h, dst, :]
                o = o * lax.rsqrt(jnp.mean(o * o, axis=-1, keepdims=True) + EPS) * nw_ref[...]
                o_ref[src, hs] = (o * _silu(g_ref[src, hs])).astype(o_ref.dtype)

    finish(gc_ref, oc_ref, 0, n_ctx)
    finish(gl_ref, ol_ref, n_ctx, n_lat)


def _hg_constants():
    c = HG_CHUNK
    row, col = np.arange(c)[:, None], np.arange(c)[None, :]
    blocks = np.eye(HG_PREP_ROWS // c)
    tri = np.stack([np.kron(blocks, col <= row), np.kron(blocks, col >= row)]).astype(np.float32)
    same8 = (row // HG_DIAG) == (col // HG_DIAG)
    diag_code = len(HG_LEVELS) + 1
    code_f = np.where(same8 & (col <= row), diag_code, 0)
    code_b = np.where(same8 & (col >= row), diag_code, 0)
    for li, w in enumerate(HG_LEVELS):
        same = (row // (2 * w)) == (col // (2 * w))
        t_hi, s_hi = (row % (2 * w)) >= w, (col % (2 * w)) >= w
        code_f = np.where(same & t_hi & ~s_hi, li + 1, code_f)
        code_b = np.where(same & ~t_hi & s_hi, li + 1, code_b)
    codes = np.stack([code_f, code_b]).astype(np.int32)
    lane_sum = (np.arange(HG_DIAG * HG_DIM)[:, None] // HG_DIM == col % HG_DIAG).astype(np.float32)
    return jnp.asarray(tri, BF16), jnp.asarray(codes), jnp.asarray(lane_sum, BF16)


def _hgrn2(p_l, p_c, lbp, norm_w, b):
    tri, codes, lane_sum = _hg_constants()
    t = p_l.shape[0] // b
    ctx = p_c.shape[0] // b
    hd = HG_DIM
    pw = HG_PAIR * hd
    n = t + ctx
    first = _HG0 * LANE // pw

    def col(stream):
        return lambda i, j: (i, first + stream * (HG_HEADS // HG_PAIR) + j)

    lat_specs = [pl.BlockSpec((t, pw), col(s)) for s in range(5)]
    ctx_specs = [pl.BlockSpec((ctx, pw), col(s)) for s in range(5)]
    big = lambda: pltpu.VMEM((HG_PAIR, n, hd), F32)
    return pl.pallas_call(
        _hg_kernel,
        grid=(b, HG_HEADS // HG_PAIR),
        in_specs=lat_specs + ctx_specs + [
            pl.BlockSpec((2, 3, pw), lambda i, j: (0, 0, j)),
            pl.BlockSpec((1, hd), lambda i, j: (0, 0)),
            pl.BlockSpec(tri.shape, lambda i, j: (0, 0, 0)),
            pl.BlockSpec(codes.shape, lambda i, j: (0, 0, 0)),
            pl.BlockSpec(lane_sum.shape, lambda i, j: (0, 0)),
        ],
        out_specs=[
            pl.BlockSpec((t, pw), lambda i, j: (i, j)),
            pl.BlockSpec((ctx, pw), lambda i, j: (i, j)),
        ],
        out_shape=[
            jax.ShapeDtypeStruct((b * t, HG_WIDTH), BF16),
            jax.ShapeDtypeStruct((b * ctx, HG_WIDTH), BF16),
        ],
        scratch_shapes=[big() for _ in range(7)] + [pltpu.VMEM((HG_PAIR, 2, hd, hd), F32)],
        compiler_params=_cparams(("arbitrary", "arbitrary")),
        name="hgrn2",
    )(*([p_l] * 5 + [p_c] * 5 + [lbp, norm_w.reshape(1, hd), tri, codes, lane_sum]))


def _outproj_kernel(oa_ref, ob_ref, u_ref, v_ref, lnw_ref, ws_ref, bs_ref, wa_ref, wb_ref, wc_ref,
                    x_ref, g_ref, o_ref, oc_ref):
    dot = functools.partial(jnp.dot, preferred_element_type=F32)
    y = dot(oa_ref[...], wa_ref[...]) + dot(ob_ref[...], wb_ref[...])
    _gm_block(u_ref, v_ref, lnw_ref, ws_ref, bs_ref, oc_ref)
    y = y + dot(oc_ref[...], wc_ref[...])
    o_ref[...] = x_ref[...] + g_ref[0] * y


def _outproj(oa, ob, p2, gm_ln_w, gm_ws, gm_bs, w, w_layer, x2d, mod, layer, rows_per_mod, mod_row0, tm):
    m, d = x2d.shape
    mrow = lambda i: layer * MOD_ROWS + mod_row0 + (i * tm) // rows_per_mod
    nb = NA_WIDTH // HG_WIDTH
    assert tm % GM_CHUNK == 0
    return pl.pallas_call(
        _outproj_kernel,
        grid=(m // tm,),
        in_specs=[
            pl.BlockSpec((tm, NA_WIDTH), lambda i: (i, 0)),
            pl.BlockSpec((tm, HG_WIDTH), lambda i: (i, 0)),
            pl.BlockSpec((tm, GM_WIDTH), lambda i: (i, _GM_U)),
            pl.BlockSpec((tm, GM_WIDTH), lambda i: (i, _GM_V)),
            pl.BlockSpec((1, GM_WIDTH), lambda i: (0, 0)),
            pl.BlockSpec((GM_GROUPS, GM_CHUNK, GM_CHUNK), lambda i: (0, 0, 0)),
            pl.BlockSpec((GM_GROUPS, GM_CHUNK, 1), lambda i: (0, 0, 0)),
            pl.BlockSpec((None, NA_WIDTH, d), lambda i: (w_layer, 0, 0)),
            pl.BlockSpec((None, HG_WIDTH, d), lambda i: (w_layer, nb, 0)),
            pl.BlockSpec((None, GM_WIDTH, d), lambda i: (w_layer, nb + 1, 0)),
            pl.BlockSpec((tm, d), lambda i: (i, 0)),
            pl.BlockSpec((1, 1, d), lambda i: (mrow(i), 0, 2)),
        ],
        out_specs=pl.BlockSpec((tm, d), lambda i: (i, 0)),
        out_shape=jax.ShapeDtypeStruct((m, d), F32),
        scratch_shapes=[pltpu.VMEM((tm, GM_WIDTH), BF16)],
        compiler_params=_cparams(("arbitrary",)),
        name="outproj",
    )(oa, ob, p2, p2, gm_ln_w.reshape(1, GM_WIDTH), gm_ws, gm_bs.reshape(GM_GROUPS, GM_CHUNK, 1),
      w, w, w, x2d, mod)


def _mlp_kernel(x_ref, nw_ref, sh_ref, sc_ref, g_ref, w1_ref, w2_ref, fnw_ref, o_ref, h_ref, acc_ref,
                *, final_norm):
    j = pl.program_id(1)

    @pl.when(j == 0)
    def _():
        _norm_modulate_store(x_ref, nw_ref, sh_ref, sc_ref, h_ref)
        acc_ref[...] = jnp.zeros_like(acc_ref)

    a = jnp.maximum(jnp.dot(h_ref[...], w1_ref[...], preferred_element_type=F32), 0.0)
    acc_ref[...] += jnp.dot((a * a).astype(BF16), w2_ref[...], preferred_element_type=F32)

    @pl.when(j == pl.num_programs(1) - 1)
    def _():
        y = x_ref[...] + g_ref[0] * acc_ref[...]
        if final_norm:
            y = y * lax.rsqrt(jnp.mean(y * y, axis=-1, keepdims=True) + EPS) * fnw_ref[...]
        o_ref[...] = y


def _mlp(x2d, nw, mod, w1, w2, w_layer, fnw, layer, rows_per_mod, mod_row0, tm, final_norm):
    m, d = x2d.shape
    hid = w1.shape[2]
    th = MLP_TH
    mrow = lambda i: layer * MOD_ROWS + mod_row0 + (i * tm) // rows_per_mod
    return pl.pallas_call(
        functools.partial(_mlp_kernel, final_norm=final_norm),
        grid=(m // tm, hid // th),
        in_specs=[
            pl.BlockSpec((tm, d), lambda i, j: (i, 0)),
            pl.BlockSpec((None, 1, d), lambda i, j: (layer, 0, 0)),
            pl.BlockSpec((1, 1, d), lambda i, j: (mrow(i), 0, 3)),
            pl.BlockSpec((1, 1, d), lambda i, j: (mrow(i), 0, 4)),
            pl.BlockSpec((1, 1, d), lambda i, j: (mrow(i), 0, 5)),
            pl.BlockSpec((None, d, th), lambda i, j: (w_layer, 0, j)),
            pl.BlockSpec((None, th, d), lambda i, j: (w_layer, j, 0)),
            pl.BlockSpec((1, d), lambda i, j: (0, 0)),
        ],
        out_specs=pl.BlockSpec((tm, d), lambda i, j: (i, 0)),
        out_shape=jax.ShapeDtypeStruct((m, d), F32),
        scratch_shapes=[pltpu.VMEM((tm, d), BF16), pltpu.VMEM((tm, d), F32)],
        compiler_params=_cparams(("arbitrary", "arbitrary")),
        name="mlp",
    )(x2d, nw, mod, mod, mod, w1, w2, fnw)


def kernel(x, c, ctx, c_ctx, ada_w, ada_b, norm1_w, norm2_w, w_in, na_rpb, hg_lb_logits, hg_norm_w,
           gm_ln_w, gm_ws, gm_bs, w_out, mlp_w1, mlp_w2, final_norm_w):
    bsz, seq, d = x.shape
    n_ctx = ctx.shape[1]
    depth = ada_w.shape[0]
    assert bsz < MOD_ROWS and d == D_MODEL and seq % 512 == 0 and n_ctx % 256 == 0
    assert seq // NA_GROUP_TOK >= NA_WIN_BLOCKS and seq // GRID_W >= 2 * NA_WIN_ROWS

    lb = jnp.cumsum(jax.nn.softmax(hg_lb_logits.astype(F32), axis=0), axis=0)
    lb = lb - lb[:1]
    lbp = jnp.stack([jnp.log(lb), jnp.log1p(-lb), 1.0 - lb], axis=2)

    cond = jnp.zeros((MOD_ROWS, d), F32).at[:bsz].set(c).at[bsz].set(c_ctx)
    mod = _ada(cond, ada_w, ada_b).reshape(depth * MOD_ROWS, 1, 6 * d)

    w_in_b = w_in.astype(BF16)
    nw1 = norm1_w.reshape(depth, 1, d)
    nw2 = norm2_w.reshape(depth, 1, d)
    fnw = final_norm_w.reshape(1, d)

    xl = x.reshape(bsz * seq, d)
    xc = ctx.reshape(bsz * n_ctx, d)
    n_c = bsz * n_ctx
    tm_l, tm_c = TM_LATENT, min(TM_CONTEXT, n_c)
    for l in range(depth):
        need_ctx = l < depth - 1
        p_l = _inproj(xl, nw1, mod, w_in_b, l, seq, 0, TM_INPROJ)
        p_c = _inproj(xc, nw1, mod, w_in_b, l, n_c, bsz, tm_c)

        oa_l, (w_out_b, w1_b, w2_b) = _na_latent(p_l, p_c, _na_bias_slabs(na_rpb[l]),
                                                 (w_out, mlp_w1, mlp_w2), l, bsz)
        ob_l, ob_c = _hgrn2(p_l, p_c, lbp[l], hg_norm_w[l], bsz)
        gm = (gm_ln_w[l], gm_ws[l], gm_bs[l])
        xl = _outproj(oa_l, ob_l, p_l, *gm, w_out_b, 0, xl, mod, l, seq, 0, tm_l)
        xl = _mlp(xl, nw2, mod, w1_b, w2_b, 0, fnw, l, seq, 0, tm_l, final_norm=not need_ctx)
        if need_ctx:
            oa_c = _ctx_attention(p_c, bsz)
            xc = _outproj(oa_c, ob_c, p_c, *gm, w_out_b, 0, xc, mod, l, n_c, bsz, tm_c)
            xc = _mlp(xc, nw2, mod, w1_b, w2_b, 0, fnw, l, n_c, bsz, tm_c, final_norm=False)
    return xl.reshape(bsz, seq, d)
```

```python
import functools

import numpy as np
import jax
import jax.numpy as jnp
from jax import lax
from jax.experimental import pallas as pl
from jax.experimental.pallas import tpu as pltpu

F32 = jnp.float32
BF16 = jnp.bfloat16

D_MODEL = 2048
DEPTH = 2
GRID_W = 64
EPS = 1e-6

NA_HEAD_DIM = 128
NA_HEADS = 8
NA_WIDTH = NA_HEADS * NA_HEAD_DIM
NA_WIN_ROWS = 8
NA_WIN_COLS = 16

HG_HEADS = 4
HG_DIM = 128
HG_WIDTH = HG_HEADS * HG_DIM
HG_CHUNK = 64

GM_GROUPS = 4
GM_DIM = 128
GM_WIDTH = GM_GROUPS * GM_DIM
GM_CHUNK = 128

IN_WIDTH = 3 * NA_WIDTH + 5 * HG_WIDTH + 2 * GM_WIDTH
MLP_HIDDEN = 4 * D_MODEL
LANE = 128
LOG2E = 1.4426950408889634
BF16_SUBLANES = 16

_QA, _KA, _VA = 0, NA_HEADS, 2 * NA_HEADS
_HG0 = 3 * NA_HEADS
_GM_U = (3 * NA_WIDTH + 5 * HG_WIDTH) // GM_WIDTH
_GM_V = _GM_U + 1
_HG_G = (3 * NA_WIDTH + 4 * HG_WIDTH) // HG_WIDTH

VMEM_LIMIT_V7X = 56 * 1024 * 1024
TM_LATENT = 512
TM_CONTEXT = 512
TM_INPROJ = 512
INPROJ_TN = 3328
MLP_TH = 1024
NORM_CHUNK_ROWS = 32
NORM_UNROLL = 4
MOD_ROWS = 16


def _cparams(sem):
    return pltpu.CompilerParams(dimension_semantics=sem, vmem_limit_bytes=VMEM_LIMIT_V7X)


def _silu(x):
    h = 0.5 * x
    return h + h * jnp.tanh(h)


def _gelu_tanh(x):
    return 0.5 * x * (1.0 + jnp.tanh(0.7978845608028654 * (x + 0.044715 * (x * x * x))))


def _norm_modulate_store(x_ref, nw_ref, sh_ref, sc_ref, h_ref):
    gain = nw_ref[...] * (1.0 + sc_ref[0])
    shift = sh_ref[0]

    def body(i, carry):
        rows = pl.ds(pl.multiple_of(i * NORM_CHUNK_ROWS, NORM_CHUNK_ROWS), NORM_CHUNK_ROWS)
        x = x_ref[rows, :]
        rs = lax.rsqrt(jnp.mean(x * x, axis=-1, keepdims=True) + EPS)
        h_ref[rows, :] = (x * rs * gain + shift).astype(h_ref.dtype)
        return carry

    lax.fori_loop(0, x_ref.shape[0] // NORM_CHUNK_ROWS, body, 0, unroll=NORM_UNROLL)


def _ada_kernel(c_ref, w_ref, b_ref, o_ref):
    s = _silu(c_ref[...]).astype(BF16)
    o_ref[0] = jnp.dot(s, w_ref[0].astype(BF16), preferred_element_type=F32) + b_ref[0]


def _ada(cond, ada_w, ada_b):
    depth, d, n = ada_w.shape
    tn = 1536
    return pl.pallas_call(
        _ada_kernel,
        grid=(depth, n // tn),
        in_specs=[
            pl.BlockSpec((cond.shape[0], d), lambda l, j: (0, 0)),
            pl.BlockSpec((1, d, tn), lambda l, j: (l, 0, j)),
            pl.BlockSpec((1, 1, tn), lambda l, j: (l, 0, j)),
        ],
        out_specs=pl.BlockSpec((1, cond.shape[0], tn), lambda l, j: (l, 0, j)),
        out_shape=jax.ShapeDtypeStruct((depth, cond.shape[0], n), F32),
        compiler_params=_cparams(("arbitrary", "arbitrary")),
        name="ada",
    )(cond, ada_w, ada_b.reshape(depth, 1, n))


def _inproj_kernel(x_ref, nw_ref, sh_ref, sc_ref, w_ref, o_ref, h_ref):
    @pl.when(pl.program_id(1) == 0)
    def _():
        _norm_modulate_store(x_ref, nw_ref, sh_ref, sc_ref, h_ref)

    o_ref[...] = jnp.dot(h_ref[...], w_ref[...], preferred_element_type=F32)


def _inproj(x2d, nw, mod, w, layer, rows_per_mod, mod_row0, tm):
    m, d = x2d.shape
    n = w.shape[2]
    tn = INPROJ_TN
    mrow = lambda i: layer * MOD_ROWS + mod_row0 + (i * tm) // rows_per_mod
    return pl.pallas_call(
        _inproj_kernel,
        grid=(m // tm, n // tn),
        in_specs=[
            pl.BlockSpec((tm, d), lambda i, j: (i, 0)),
            pl.BlockSpec((None, 1, d), lambda i, j: (layer, 0, 0)),
            pl.BlockSpec((1, 1, d), lambda i, j: (mrow(i), 0, 0)),
            pl.BlockSpec((1, 1, d), lambda i, j: (mrow(i), 0, 1)),
            pl.BlockSpec((None, d, tn), lambda i, j: (layer, 0, j)),
        ],
        out_specs=pl.BlockSpec((tm, tn), lambda i, j: (i, j)),
        out_shape=jax.ShapeDtypeStruct((m, n), F32),
        scratch_shapes=[pltpu.VMEM((tm, d), BF16)],
        compiler_params=_cparams(("arbitrary", "arbitrary")),
        name="inproj",
    )(x2d, nw, mod, mod, w)


NA_GROUP_ROWS = 4
NA_GROUP_TOK = NA_GROUP_ROWS * GRID_W
NA_WIN_BLOCKS = 3
NA_GROUPS_PER_STEP = 4
NA_LOOP_UNROLL = 2


NA_MASKED_SLAB = 2 * NA_WIN_ROWS - 1


def _na_bias_slabs(rpb):
    h = rpb.shape[0]
    kcol = np.arange(GRID_W)[:, None]
    qcol = np.arange(GRID_W)[None, :]
    wstart = np.clip(qcol - NA_WIN_COLS // 2, 0, GRID_W - NA_WIN_COLS)
    valid_col = (kcol >= wstart) & (kcol < wstart + NA_WIN_COLS)
    pad = GRID_W - NA_WIN_COLS
    padded = jnp.pad(rpb.astype(F32), ((0, 0), (0, 0), (pad, pad)))
    m = jnp.tile(padded, (1, 1, GRID_W + 1))[:, :, :GRID_W * 2 * GRID_W]
    m = m.reshape(h, rpb.shape[1], GRID_W, 2 * GRID_W)[..., :GRID_W]
    slabs = jnp.where(valid_col, m[..., ::-1] * LOG2E, -1e30)
    slabs = jnp.concatenate([slabs, jnp.full((h, 1, GRID_W, GRID_W), -1e30, F32)], axis=1)
    return jnp.concatenate([slabs, slabs], axis=-1)


def _na_slab_index(rows):
    n_groups = rows // NA_GROUP_ROWS
    krel = np.arange(NA_WIN_BLOCKS * NA_GROUP_ROWS)[:, None]
    qrel = np.arange(NA_GROUP_ROWS)[None, :]
    idx = []
    for g in (0, 1, n_groups - 1):
        krow = int(np.clip(g - 1, 0, n_groups - NA_WIN_BLOCKS)) * NA_GROUP_ROWS + krel
        qrow = g * NA_GROUP_ROWS + qrel
        r0 = np.clip(qrow - NA_WIN_ROWS // 2, 0, rows - NA_WIN_ROWS)
        in_window = (krow >= r0) & (krow < r0 + NA_WIN_ROWS)
        idx.append(np.where(in_window, krow - qrow + NA_WIN_ROWS - 1, NA_MASKED_SLAB))
    return np.stack(idx).tolist()


def _na_kernel(q_ref, k_ref, v_ref, kc_ref, vc_ref, slab_ref, wo_ref, w1_ref, w2_ref,
               o_ref, wo_b_ref, w1_b_ref, w2_b_ref, qb_ref, kb_ref, vt_ref, bias_ref, *, slab_index):
    n_groups = q_ref.shape[0] // NA_GROUP_TOK
    gt = NA_GROUP_TOK
    scale2 = NA_HEAD_DIM ** -0.5 * LOG2E
    nt = (((1,), (1,)), ((), ()))
    dot = functools.partial(jnp.dot, preferred_element_type=F32)

    for src_ref, dst_ref in ((wo_ref, wo_b_ref), (w1_ref, w1_b_ref), (w2_ref, w2_b_ref)):
        dst_ref[...] = src_ref[...].astype(BF16)

    @pl.when(pl.program_id(1) == 0)
    def _():
        left = lax.broadcasted_iota(jnp.int32, (GRID_W, 2 * GRID_W), 1) < GRID_W
        for kind, per_key_row in enumerate(slab_index):
            for j, per_query_row in enumerate(per_key_row):
                for a in range(0, NA_GROUP_ROWS, 2):
                    tile = jnp.where(left, slab_ref[0, per_query_row[a]], slab_ref[0, per_query_row[a + 1]])
                    bias_ref[kind, j * GRID_W:(j + 1) * GRID_W, a * GRID_W:(a + 2) * GRID_W] = tile

    qb_ref[...] = (q_ref[...] * scale2).astype(BF16)
    kb_ref[...] = k_ref[...].astype(BF16)
    for b in range(n_groups):
        vt_ref[b] = v_ref[b * gt:(b + 1) * gt, :].T.astype(BF16)
    kc = kc_ref[...].astype(BF16)
    vct = vc_ref[...].T.astype(BF16)

    def group(g):
        blk = jnp.clip(g - 1, 0, n_groups - NA_WIN_BLOCKS)
        kind = jnp.where(g == 0, 0, jnp.where(g == n_groups - 1, 2, 1))
        qg = qb_ref[pl.ds(pl.multiple_of(g * gt, gt), gt), :]
        kw = kb_ref[pl.ds(pl.multiple_of(blk * gt, gt), NA_WIN_BLOCKS * gt), :]
        s_lat = lax.dot_general(kw, qg, nt, preferred_element_type=F32) + bias_ref[kind]
        s_ctx = lax.dot_general(kc, qg, nt, preferred_element_type=F32)
        yield
        m = jnp.maximum(jnp.max(s_lat, axis=0, keepdims=True), jnp.max(s_ctx, axis=0, keepdims=True))
        p_lat = jnp.exp2(s_lat - m)
        p_ctx = jnp.exp2(s_ctx - m)
        l = jnp.sum(p_lat, axis=0, keepdims=True) + jnp.sum(p_ctx, axis=0, keepdims=True)
        yield
        ot = dot(vct, p_ctx.astype(BF16))
        for j in range(NA_WIN_BLOCKS):
            ot = ot + dot(vt_ref[blk + j], p_lat[j * gt:(j + 1) * gt, :].astype(BF16))
        yield
        o_ref[pl.ds(pl.multiple_of(g * gt, gt), gt), :] = (ot / l).T.astype(o_ref.dtype)
        yield

    def body(i, carry):
        groups = [group(i * NA_GROUPS_PER_STEP + u) for u in range(NA_GROUPS_PER_STEP)]
        for _ in range(4):
            for grp in groups:
                next(grp)
        return carry

    lax.fori_loop(0, n_groups // NA_GROUPS_PER_STEP, body, 0, unroll=NA_LOOP_UNROLL)


def _na_latent(p_l, p_c, slabs, weights, layer, b):
    t = p_l.shape[0] // b
    ctx = p_c.shape[0] // b
    hd = NA_HEAD_DIM
    n_steps = NA_HEADS * b
    w_specs_in, w_specs_out, w_shapes = [], [], []
    for w in weights:
        rows, cols = w.shape[1] // n_steps, w.shape[2]
        assert w.shape[1] % n_steps == 0 and rows % BF16_SUBLANES == 0
        w_specs_in.append(pl.BlockSpec((None, rows, cols), lambda h, i: (layer, h * b + i, 0)))
        w_specs_out.append(pl.BlockSpec((None, rows, cols), lambda h, i: (0, h * b + i, 0)))
        w_shapes.append(jax.ShapeDtypeStruct((1,) + w.shape[1:], BF16))
    oa, *w_bf16 = pl.pallas_call(
        functools.partial(_na_kernel, slab_index=_na_slab_index(t // GRID_W)),
        grid=(NA_HEADS, b),
        in_specs=[
            pl.BlockSpec((t, hd), lambda h, i: (i, _QA + h)),
            pl.BlockSpec((t, hd), lambda h, i: (i, _KA + h)),
            pl.BlockSpec((t, hd), lambda h, i: (i, _VA + h)),
            pl.BlockSpec((ctx, hd), lambda h, i: (i, _KA + h)),
            pl.BlockSpec((ctx, hd), lambda h, i: (i, _VA + h)),
            pl.BlockSpec((1,) + slabs.shape[1:], lambda h, i: (h, 0, 0, 0)),
        ] + w_specs_in,
        out_specs=[pl.BlockSpec((t, hd), lambda h, i: (i, h))] + w_specs_out,
        out_shape=[jax.ShapeDtypeStruct((b * t, NA_WIDTH), BF16)] + w_shapes,
        scratch_shapes=[
            pltpu.VMEM((t, hd), BF16),
            pltpu.VMEM((t, hd), BF16),
            pltpu.VMEM((t // NA_GROUP_TOK, hd, NA_GROUP_TOK), BF16),
            pltpu.VMEM((3, NA_WIN_BLOCKS * NA_GROUP_TOK, NA_GROUP_TOK), F32),
        ],
        compiler_params=_cparams(("arbitrary", "arbitrary")),
        name="na_latent",
    )(p_l, p_l, p_l, p_c, p_c, slabs, *weights)
    return oa, w_bf16


def _ctx_attn_kernel(q_ref, k_ref, v_ref, o_ref):
    scale = NA_HEAD_DIM ** -0.5
    for h in range(NA_HEADS):
        hs = slice(h * NA_HEAD_DIM, (h + 1) * NA_HEAD_DIM)
        q = q_ref[:, hs].astype(BF16)
        k = k_ref[:, hs].astype(BF16)
        v = v_ref[:, hs].astype(BF16)
        s = lax.dot_general(q, k, (((1,), (1,)), ((), ())), preferred_element_type=F32) * scale
        p = jnp.exp(s - jnp.max(s, axis=-1, keepdims=True))
        l = jnp.sum(p, axis=-1, keepdims=True)
        o = jnp.dot(p.astype(BF16), v, preferred_element_type=F32)
        o_ref[:, hs] = (o / l).astype(o_ref.dtype)


def _ctx_attention(p_c, b):
    ctx = p_c.shape[0] // b
    return pl.pallas_call(
        _ctx_attn_kernel,
        grid=(b,),
        in_specs=[
            pl.BlockSpec((ctx, NA_WIDTH), lambda i: (i, 0)),
            pl.BlockSpec((ctx, NA_WIDTH), lambda i: (i, 1)),
            pl.BlockSpec((ctx, NA_WIDTH), lambda i: (i, 2)),
        ],
        out_specs=pl.BlockSpec((ctx, NA_WIDTH), lambda i: (i, 0)),
        out_shape=jax.ShapeDtypeStruct((b * ctx, NA_WIDTH), BF16),
        compiler_params=_cparams(("arbitrary",)),
        name="ctx_attention",
    )(p_c, p_c, p_c)


def _gm_block(u_ref, v_ref, lnw_ref, ws_ref, bs_ref, o_ref):
    for ck in range(u_ref.shape[0] // GM_CHUNK):
        rows = slice(ck * GM_CHUNK, (ck + 1) * GM_CHUNK)
        uf = _gelu_tanh(u_ref[rows, :])
        vf = _gelu_tanh(v_ref[rows, :])
        for g in range(GM_GROUPS):
            sl = slice(g * GM_DIM, (g + 1) * GM_DIM)
            vg = vf[:, sl]
            mu = jnp.mean(vg, axis=-1, keepdims=True)
            dv = vg - mu
            var = jnp.mean(dv * dv, axis=-1, keepdims=True)
            vn = dv * lax.rsqrt(var + EPS) * lnw_ref[:, sl]
            mixed = jnp.dot(ws_ref[g].astype(BF16), vn.astype(BF16), preferred_element_type=F32) + bs_ref[g]
            o_ref[rows, sl] = (uf[:, sl] * mixed).astype(o_ref.dtype)


HG_PAIR = 2
HG_LEVELS = (32, 16, 8)
HG_DIAG = 8
HG_PREP_ROWS = 256
HG_SCAN_UNROLL = 8


def _split3_dot(tri, g):
    g0 = g.astype(BF16)
    r1 = g - g0.astype(F32)
    g1 = r1.astype(BF16)
    g2 = (r1 - g1.astype(F32)).astype(BF16)
    dot = functools.partial(jnp.dot, preferred_element_type=F32)
    return dot(tri, g0) + dot(tri, g1) + dot(tri, g2)


def _hg_kernel(ql_ref, ffl_ref, fbl_ref, il_ref,
               qc_ref, ffc_ref, fbc_ref, ic_ref,
               lbp_ref, tri_ref, code_ref, lsum_ref, ol_ref, oc_ref,
               qh_s, kf_s, kb_s, ef_s, eb_s, of_s, ob_s, st_s):
    c = HG_CHUNK
    hd = HG_DIM
    n_ctx = qc_ref.shape[0]
    n_lat = ql_ref.shape[0]
    scale = hd ** -0.5
    nt = (((1,), (1,)), ((), ()))
    tn = (((0,), (0,)), ((), ()))

    def prep(q_ref, ff_ref, fb_ref, base, n):
        step = HG_PREP_ROWS
        for t0 in range(0, n, step):
            src = slice(t0, t0 + step)
            dst = slice(base + t0, base + t0 + step)
            qh = _silu(q_ref[src, :]) * scale
            for h in range(HG_PAIR):
                qh_s[h, dst, :] = qh[:, h * hd:(h + 1) * hd]
            for d, (f_ref, k_s, e_s) in enumerate(((ff_ref, kf_s, ef_s), (fb_ref, kb_s, eb_s))):
                x = f_ref[src, :]
                sp = jnp.maximum(-x, 0.0) + jnp.log(1.0 + jnp.exp(-jnp.abs(x)))
                log_lb = lbp_ref[d, 0:1, :]
                y = lbp_ref[d, 1:2, :] - sp
                mx = jnp.maximum(log_lb, y)
                log_f = mx + jnp.log(1.0 + jnp.exp(-jnp.abs(log_lb - y)))
                one_minus_f = lbp_ref[d, 2:3, :] * jnp.exp(-(sp + x))
                e = _split3_dot(tri_ref[d], log_f) * LOG2E
                for h in range(HG_PAIR):
                    e_s[h, dst, :] = e[:, h * hd:(h + 1) * hd]
                    k_s[h, dst, :] = one_minus_f[:, h * hd:(h + 1) * hd]

    prep(qc_ref, ffc_ref, fbc_ref, 0, n_ctx)
    prep(ql_ref, ffl_ref, fbl_ref, n_ctx, n_lat)

    diag_code = len(HG_LEVELS) + 1

    st_s[...] = jnp.zeros_like(st_s)

    def chunk(off, v, h, d, out):
        fwd = d == 0
        k_s, e_s = (kf_s, ef_s) if fwd else (kb_s, eb_s)
        q = qh_s[h, pl.ds(off, c), :]
        k = k_s[h, pl.ds(off, c), :]
        e = e_s[h, pl.ds(off, c), :]
        code = code_ref[d]
        prods = []
        for i in range(c // HG_DIAG):
            bs = slice(i * HG_DIAG, (i + 1) * HG_DIAG)
            row_prods = []
            for s in range(HG_DIAG):
                kr = k_s[h, pl.ds(off + i * HG_DIAG + s, 1), :]
                er = e_s[h, pl.ds(off + i * HG_DIAG + s, 1), :]
                decay = jnp.exp2(jnp.minimum(e[bs] - er, 0.0))
                row_prods.append((q[bs] * kr * decay).astype(BF16))
            prods.append(jnp.concatenate(row_prods, axis=1))
        diag = jnp.dot(jnp.concatenate(prods, axis=0), lsum_ref[...], preferred_element_type=F32)
        yield
        att = None
        for li, w in enumerate(HG_LEVELS):
            zeros = jnp.zeros((w, hd), F32)
            q_parts, k_parts = [], []
            for a in range(0, c, 2 * w):
                lo, hi = slice(a, a + w), slice(a + w, a + 2 * w)
                if fwd:
                    ref = e_s[h, pl.ds(off + a + w - 1, 1), :]
                    q_parts += [zeros, q[hi] * jnp.exp2(e[hi] - ref)]
                    k_parts += [k[lo] * jnp.exp2(ref - e[lo]), zeros]
                else:
                    ref = e_s[h, pl.ds(off + a + w, 1), :]
                    q_parts += [q[lo] * jnp.exp2(e[lo] - ref), zeros]
                    k_parts += [zeros, k[hi] * jnp.exp2(ref - e[hi])]
            qs = jnp.concatenate(q_parts, axis=0).astype(BF16)
            ks = jnp.concatenate(k_parts, axis=0).astype(BF16)
            a_w = lax.dot_general(qs, ks, nt, preferred_element_type=F32)
            att = a_w if att is None else jnp.where(code == li + 1, a_w, att)
            yield
        e_end = e_s[h, pl.ds(off + (c - 1 if fwd else 0), 1), :]
        st = st_s[h, d]
        qi = (q * jnp.exp2(e)).astype(BF16)
        ki = (k * jnp.exp2(e_end - e)).astype(BF16)
        o = lax.dot_general(qi, st.astype(BF16), nt, preferred_element_type=F32)
        yield
        st_new = st * jnp.exp2(e_end) + lax.dot_general(v, ki, tn, preferred_element_type=F32)
        yield
        att = jnp.where(code == diag_code, diag, att)
        out.append((o + jnp.dot(att.astype(BF16), v, preferred_element_type=F32), st_new))
        yield

    n_stages = len(HG_LEVELS) + 4

    def scan(v_ref, base, n):
        def body(i, carry):
            rf = pl.multiple_of(i * c, c)
            rb = pl.multiple_of((n - 1 - i) * c, c)
            vf = v_ref[pl.ds(rf, c), :].astype(BF16)
            vb = v_ref[pl.ds(rb, c), :].astype(BF16)
            chains = []
            for h in range(HG_PAIR):
                hs = slice(h * hd, (h + 1) * hd)
                for d, r, v in ((0, rf, vf), (1, rb, vb)):
                    out = []
                    chains.append((h, d, r, out, chunk(base + r, v[:, hs], h, d, out)))
            for _ in range(n_stages):
                for chain in chains:
                    next(chain[-1])
            for h, d, r, out, _ in chains:
                o, st_new = out[0]
                (of_s if d == 0 else ob_s)[h, pl.ds(base + r, c), :] = o
                st_s[h, d] = st_new
            return carry

        lax.fori_loop(0, n, body, 0, unroll=HG_SCAN_UNROLL)

    scan(ic_ref, 0, n_ctx // c)
    scan(il_ref, n_ctx, n_lat // c)

    def finish(o_ref, base, n):
        step = 256
        for t0 in range(0, n, step):
            src = slice(t0, t0 + step)
            dst = slice(base + t0, base + t0 + step)
            for h in range(HG_PAIR):
                o_ref[src, h * hd:(h + 1) * hd] = of_s[h, dst, :] + ob_s[h, dst, :]

    finish(oc_ref, 0, n_ctx)
    finish(ol_ref, n_ctx, n_lat)


def _hg_constants():
    c = HG_CHUNK
    row, col = np.arange(c)[:, None], np.arange(c)[None, :]
    blocks = np.eye(HG_PREP_ROWS // c)
    tri = np.stack([np.kron(blocks, col <= row), np.kron(blocks, col >= row)]).astype(np.float32)
    same8 = (row // HG_DIAG) == (col // HG_DIAG)
    diag_code = len(HG_LEVELS) + 1
    code_f = np.where(same8 & (col <= row), diag_code, 0)
    code_b = np.where(same8 & (col >= row), diag_code, 0)
    for li, w in enumerate(HG_LEVELS):
        same = (row // (2 * w)) == (col // (2 * w))
        t_hi, s_hi = (row % (2 * w)) >= w, (col % (2 * w)) >= w
        code_f = np.where(same & t_hi & ~s_hi, li + 1, code_f)
        code_b = np.where(same & ~t_hi & s_hi, li + 1, code_b)
    codes = np.stack([code_f, code_b]).astype(np.int32)
    lane_sum = (np.arange(HG_DIAG * HG_DIM)[:, None] // HG_DIM == col % HG_DIAG).astype(np.float32)
    return jnp.asarray(tri, BF16), jnp.asarray(codes), jnp.asarray(lane_sum, BF16)


def _hgrn2(p_l, p_c, lbp, b):
    tri, codes, lane_sum = _hg_constants()
    t = p_l.shape[0] // b
    ctx = p_c.shape[0] // b
    hd = HG_DIM
    pw = HG_PAIR * hd
    n = t + ctx
    first = _HG0 * LANE // pw

    def col(stream):
        return lambda i, j: (i, first + stream * (HG_HEADS // HG_PAIR) + j)

    lat_specs = [pl.BlockSpec((t, pw), col(s)) for s in range(4)]
    ctx_specs = [pl.BlockSpec((ctx, pw), col(s)) for s in range(4)]
    big = lambda: pltpu.VMEM((HG_PAIR, n, hd), F32)
    return pl.pallas_call(
        _hg_kernel,
        grid=(b, HG_HEADS // HG_PAIR),
        in_specs=lat_specs + ctx_specs + [
            pl.BlockSpec((2, 3, pw), lambda i, j: (0, 0, j)),
            pl.BlockSpec(tri.shape, lambda i, j: (0, 0, 0)),
            pl.BlockSpec(codes.shape, lambda i, j: (0, 0, 0)),
            pl.BlockSpec(lane_sum.shape, lambda i, j: (0, 0)),
        ],
        out_specs=[
            pl.BlockSpec((t, pw), lambda i, j: (i, j)),
            pl.BlockSpec((ctx, pw), lambda i, j: (i, j)),
        ],
        out_shape=[
            jax.ShapeDtypeStruct((b * t, HG_WIDTH), F32),
            jax.ShapeDtypeStruct((b * ctx, HG_WIDTH), F32),
        ],
        scratch_shapes=[big() for _ in range(7)] + [pltpu.VMEM((HG_PAIR, 2, hd, hd), F32)],
        compiler_params=_cparams(("arbitrary", "arbitrary")),
        name="hgrn2",
    )(*([p_l] * 4 + [p_c] * 4 + [lbp, tri, codes, lane_sum]))


def _outproj_kernel(oa_ref, os_ref, hg_ref, hnw_ref, u_ref, v_ref, lnw_ref, ws_ref, bs_ref,
                    wa_ref, wb_ref, wc_ref, x_ref, g_ref, o_ref, ob_ref, oc_ref):
    dot = functools.partial(jnp.dot, preferred_element_type=F32)
    y = dot(oa_ref[...], wa_ref[...])
    for h in range(HG_HEADS):
        hs = slice(h * HG_DIM, (h + 1) * HG_DIM)
        o = os_ref[:, hs]
        o = o * lax.rsqrt(jnp.mean(o * o, axis=-1, keepdims=True) + EPS) * hnw_ref[...]
        ob_ref[:, hs] = (o * _silu(hg_ref[:, hs])).astype(ob_ref.dtype)
    y = y + dot(ob_ref[...], wb_ref[...])
    _gm_block(u_ref, v_ref, lnw_ref, ws_ref, bs_ref, oc_ref)
    y = y + dot(oc_ref[...], wc_ref[...])
    o_ref[...] = x_ref[...] + g_ref[0] * y


def _outproj(oa, osum, p2, hg_norm_w, gm_ln_w, gm_ws, gm_bs, w, w_layer, x2d, mod, layer, rows_per_mod,
             mod_row0, tm):
    m, d = x2d.shape
    mrow = lambda i: layer * MOD_ROWS + mod_row0 + (i * tm) // rows_per_mod
    nb = NA_WIDTH // HG_WIDTH
    assert tm % GM_CHUNK == 0
    return pl.pallas_call(
        _outproj_kernel,
        grid=(m // tm,),
        in_specs=[
            pl.BlockSpec((tm, NA_WIDTH), lambda i: (i, 0)),
            pl.BlockSpec((tm, HG_WIDTH), lambda i: (i, 0)),
            pl.BlockSpec((tm, HG_WIDTH), lambda i: (i, _HG_G)),
            pl.BlockSpec((1, HG_DIM), lambda i: (0, 0)),
            pl.BlockSpec((tm, GM_WIDTH), lambda i: (i, _GM_U)),
            pl.BlockSpec((tm, GM_WIDTH), lambda i: (i, _GM_V)),
            pl.BlockSpec((1, GM_WIDTH), lambda i: (0, 0)),
            pl.BlockSpec((GM_GROUPS, GM_CHUNK, GM_CHUNK), lambda i: (0, 0, 0)),
            pl.BlockSpec((GM_GROUPS, GM_CHUNK, 1), lambda i: (0, 0, 0)),
            pl.BlockSpec((None, NA_WIDTH, d), lambda i: (w_layer, 0, 0)),
            pl.BlockSpec((None, HG_WIDTH, d), lambda i: (w_layer, nb, 0)),
            pl.BlockSpec((None, GM_WIDTH, d), lambda i: (w_layer, nb + 1, 0)),
            pl.BlockSpec((tm, d), lambda i: (i, 0)),
            pl.BlockSpec((1, 1, d), lambda i: (mrow(i), 0, 2)),
        ],
        out_specs=pl.BlockSpec((tm, d), lambda i: (i, 0)),
        out_shape=jax.ShapeDtypeStruct((m, d), F32),
        scratch_shapes=[pltpu.VMEM((tm, HG_WIDTH), BF16), pltpu.VMEM((tm, GM_WIDTH), BF16)],
        compiler_params=_cparams(("arbitrary",)),
        name="outproj",
    )(oa, osum, p2, hg_norm_w.reshape(1, HG_DIM), p2, p2, gm_ln_w.reshape(1, GM_WIDTH), gm_ws,
      gm_bs.reshape(GM_GROUPS, GM_CHUNK, 1), w, w, w, x2d, mod)


def _mlp_kernel(x_ref, nw_ref, sh_ref, sc_ref, g_ref, w1_ref, w2_ref, fnw_ref, o_ref, h_ref, acc_ref,
                *, final_norm):
    j = pl.program_id(1)

    @pl.when(j == 0)
    def _():
        _norm_modulate_store(x_ref, nw_ref, sh_ref, sc_ref, h_ref)
        acc_ref[...] = jnp.zeros_like(acc_ref)

    a = jnp.maximum(jnp.dot(h_ref[...], w1_ref[...], preferred_element_type=F32), 0.0)
    acc_ref[...] += jnp.dot((a * a).astype(BF16), w2_ref[...], preferred_element_type=F32)

    @pl.when(j == pl.num_programs(1) - 1)
    def _():
        y = x_ref[...] + g_ref[0] * acc_ref[...]
        if final_norm:
            y = y * lax.rsqrt(jnp.mean(y * y, axis=-1, keepdims=True) + EPS) * fnw_ref[...]
        o_ref[...] = y


def _mlp(x2d, nw, mod, w1, w2, w_layer, fnw, layer, rows_per_mod, mod_row0, tm, final_norm):
    m, d = x2d.shape
    hid = w1.shape[2]
    th = MLP_TH
    mrow = lambda i: layer * MOD_ROWS + mod_row0 + (i * tm) // rows_per_mod
    return pl.pallas_call(
        functools.partial(_mlp_kernel, final_norm=final_norm),
        grid=(m // tm, hid // th),
        in_specs=[
            pl.BlockSpec((tm, d), lambda i, j: (i, 0)),
            pl.BlockSpec((None, 1, d), lambda i, j: (layer, 0, 0)),
            pl.BlockSpec((1, 1, d), lambda i, j: (mrow(i), 0, 3)),
            pl.BlockSpec((1, 1, d), lambda i, j: (mrow(i), 0, 4)),
            pl.BlockSpec((1, 1, d), lambda i, j: (mrow(i), 0, 5)),
            pl.BlockSpec((None, d, th), lambda i, j: (w_layer, 0, j)),
            pl.BlockSpec((None, th, d), lambda i, j: (w_layer, j, 0)),
            pl.BlockSpec((1, d), lambda i, j: (0, 0)),
        ],
        out_specs=pl.BlockSpec((tm, d), lambda i, j: (i, 0)),
        out_shape=jax.ShapeDtypeStruct((m, d), F32),
        scratch_shapes=[pltpu.VMEM((tm, d), BF16), pltpu.VMEM((tm, d), F32)],
        compiler_params=_cparams(("arbitrary", "arbitrary")),
        name="mlp",
    )(x2d, nw, mod, mod, mod, w1, w2, fnw)


def kernel(x, c, ctx, c_ctx, ada_w, ada_b, norm1_w, norm2_w, w_in, na_rpb, hg_lb_logits, hg_norm_w,
           gm_ln_w, gm_ws, gm_bs, w_out, mlp_w1, mlp_w2, final_norm_w):
    bsz, seq, d = x.shape
    n_ctx = ctx.shape[1]
    depth = ada_w.shape[0]
    assert bsz < MOD_ROWS and d == D_MODEL and seq % 512 == 0 and n_ctx % 256 == 0
    assert seq // NA_GROUP_TOK >= NA_WIN_BLOCKS and seq // GRID_W >= 2 * NA_WIN_ROWS

    lb = jnp.cumsum(jax.nn.softmax(hg_lb_logits.astype(F32), axis=0), axis=0)
    lb = lb - lb[:1]
    lbp = jnp.stack([jnp.log(lb), jnp.log1p(-lb), 1.0 - lb], axis=2)

    cond = jnp.zeros((MOD_ROWS, d), F32).at[:bsz].set(c).at[bsz].set(c_ctx)
    mod = _ada(cond, ada_w, ada_b).reshape(depth * MOD_ROWS, 1, 6 * d)

    w_in_b = w_in.astype(BF16)
    nw1 = norm1_w.reshape(depth, 1, d)
    nw2 = norm2_w.reshape(depth, 1, d)
    fnw = final_norm_w.reshape(1, d)

    xl = x.reshape(bsz * seq, d)
    xc = ctx.reshape(bsz * n_ctx, d)
    n_c = bsz * n_ctx
    tm_l, tm_c = TM_LATENT, min(TM_CONTEXT, n_c)
    for l in range(depth):
        need_ctx = l < depth - 1
        p_l = _inproj(xl, nw1, mod, w_in_b, l, seq, 0, TM_INPROJ)
        p_c = _inproj(xc, nw1, mod, w_in_b, l, n_c, bsz, tm_c)

        oa_l, (w_out_b, w1_b, w2_b) = _na_latent(p_l, p_c, _na_bias_slabs(na_rpb[l]),
                                                 (w_out, mlp_w1, mlp_w2), l, bsz)
        os_l, os_c = _hgrn2(p_l, p_c, lbp[l], bsz)
        mix = (hg_norm_w[l], gm_ln_w[l], gm_ws[l], gm_bs[l])
        xl = _outproj(oa_l, os_l, p_l, *mix, w_out_b, 0, xl, mod, l, seq, 0, tm_l)
        xl = _mlp(xl, nw2, mod, w1_b, w2_b, 0, fnw, l, seq, 0, tm_l, final_norm=not need_ctx)
        if need_ctx:
            oa_c = _ctx_attention(p_c, bsz)
            xc = _outproj(oa_c, os_c, p_c, *mix, w_out_b, 0, xc, mod, l, n_c, bsz, tm_c)
            xc = _mlp(xc, nw2, mod, w1_b, w2_b, 0, fnw, l, n_c, bsz, tm_c, final_norm=False)
    return xl.reshape(bsz, seq, d)
```

```python
import functools

import numpy as np
import jax
import jax.numpy as jnp
from jax import lax
from jax.experimental import pallas as pl
from jax.experimental.pallas import tpu as pltpu

F32 = jnp.float32
BF16 = jnp.bfloat16

D_MODEL = 2048
DEPTH = 2
GRID_W = 64
EPS = 1e-6

NA_HEAD_DIM = 128
NA_HEADS = 8
NA_WIDTH = NA_HEADS * NA_HEAD_DIM
NA_WIN_ROWS = 8
NA_WIN_COLS = 16

HG_HEADS = 4
HG_DIM = 128
HG_WIDTH = HG_HEADS * HG_DIM
HG_CHUNK = 64

GM_GROUPS = 4
GM_DIM = 128
GM_WIDTH = GM_GROUPS * GM_DIM
GM_CHUNK = 128

IN_WIDTH = 3 * NA_WIDTH + 5 * HG_WIDTH + 2 * GM_WIDTH
MLP_HIDDEN = 4 * D_MODEL
LANE = 128
LOG2E = 1.4426950408889634
BF16_SUBLANES = 16

QKV_WIDTH = 3 * NA_WIDTH
REST_WIDTH = IN_WIDTH - QKV_WIDTH
NA_SCORE_SCALE = NA_HEAD_DIM ** -0.5 * LOG2E
_QA, _KA, _VA = 0, NA_HEADS, 2 * NA_HEADS
_HG0 = 0
_GM_U = 5 * HG_WIDTH // GM_WIDTH
_GM_V = _GM_U + 1
_HG_G = 4

VMEM_LIMIT_V7X = 56 * 1024 * 1024
TM_LATENT = 512
TM_CONTEXT = 512
MLP_TH = 1024
NORM_CHUNK_ROWS = 32
NORM_UNROLL = 4
MOD_ROWS = 16


def _cparams(sem):
    return pltpu.CompilerParams(dimension_semantics=sem, vmem_limit_bytes=VMEM_LIMIT_V7X)


def _silu(x):
    h = 0.5 * x
    return h + h * jnp.tanh(h)


def _gelu_tanh(x):
    return 0.5 * x * (1.0 + jnp.tanh(0.7978845608028654 * (x + 0.044715 * (x * x * x))))


def _norm_modulate_store(x_ref, nw_ref, sh_ref, sc_ref, h_ref):
    gain = nw_ref[...] * (1.0 + sc_ref[0])
    shift = sh_ref[0]

    def body(i, carry):
        rows = pl.ds(pl.multiple_of(i * NORM_CHUNK_ROWS, NORM_CHUNK_ROWS), NORM_CHUNK_ROWS)
        x = x_ref[rows, :]
        rs = lax.rsqrt(jnp.mean(x * x, axis=-1, keepdims=True) + EPS)
        h_ref[rows, :] = (x * rs * gain + shift).astype(h_ref.dtype)
        return carry

    lax.fori_loop(0, x_ref.shape[0] // NORM_CHUNK_ROWS, body, 0, unroll=NORM_UNROLL)


def _ada_kernel(c_ref, w_ref, b_ref, o_ref):
    s = _silu(c_ref[...]).astype(BF16)
    o_ref[0] = jnp.dot(s, w_ref[0].astype(BF16), preferred_element_type=F32) + b_ref[0]


def _ada(cond, ada_w, ada_b):
    depth, d, n = ada_w.shape
    tn = 1536
    return pl.pallas_call(
        _ada_kernel,
        grid=(depth, n // tn),
        in_specs=[
            pl.BlockSpec((cond.shape[0], d), lambda l, j: (0, 0)),
            pl.BlockSpec((1, d, tn), lambda l, j: (l, 0, j)),
            pl.BlockSpec((1, 1, tn), lambda l, j: (l, 0, j)),
        ],
        out_specs=pl.BlockSpec((1, cond.shape[0], tn), lambda l, j: (l, 0, j)),
        out_shape=jax.ShapeDtypeStruct((depth, cond.shape[0], n), F32),
        compiler_params=_cparams(("arbitrary", "arbitrary")),
        name="ada",
    )(cond, ada_w, ada_b.reshape(depth, 1, n))


def _inproj_qkv_kernel(x_ref, nw_ref, sh_ref, sc_ref, w_ref, o_ref, h_ref):
    _norm_modulate_store(x_ref, nw_ref, sh_ref, sc_ref, h_ref)
    h = h_ref[...]
    q = jnp.dot(h, w_ref[:, :NA_WIDTH], preferred_element_type=F32)
    o_ref[:, :NA_WIDTH] = (q * NA_SCORE_SCALE).astype(o_ref.dtype)
    o_ref[:, NA_WIDTH:] = jnp.dot(h, w_ref[:, NA_WIDTH:], preferred_element_type=F32).astype(o_ref.dtype)


def _inproj_rest_kernel(h_ref, w_ref, o_ref):
    o_ref[...] = jnp.dot(h_ref[...], w_ref[...], preferred_element_type=F32)


def _inproj(x2d, nw, mod, w_qkv, w_rest, layer, rows_per_mod, mod_row0, tm):
    m, d = x2d.shape
    mrow = lambda i: layer * MOD_ROWS + mod_row0 + (i * tm) // rows_per_mod
    qkv, h = pl.pallas_call(
        _inproj_qkv_kernel,
        grid=(m // tm,),
        in_specs=[
            pl.BlockSpec((tm, d), lambda i: (i, 0)),
            pl.BlockSpec((None, 1, d), lambda i: (layer, 0, 0)),
            pl.BlockSpec((1, 1, d), lambda i: (mrow(i), 0, 0)),
            pl.BlockSpec((1, 1, d), lambda i: (mrow(i), 0, 1)),
            pl.BlockSpec((None, d, QKV_WIDTH), lambda i: (layer, 0, 0)),
        ],
        out_specs=[pl.BlockSpec((tm, QKV_WIDTH), lambda i: (i, 0)), pl.BlockSpec((tm, d), lambda i: (i, 0))],
        out_shape=[jax.ShapeDtypeStruct((m, QKV_WIDTH), BF16), jax.ShapeDtypeStruct((m, d), BF16)],
        compiler_params=_cparams(("arbitrary",)),
        name="inproj_qkv",
    )(x2d, nw, mod, mod, w_qkv)
    rest = pl.pallas_call(
        _inproj_rest_kernel,
        grid=(m // tm,),
        in_specs=[
            pl.BlockSpec((tm, d), lambda i: (i, 0)),
            pl.BlockSpec((None, d, REST_WIDTH), lambda i: (layer, 0, 0)),
        ],
        out_specs=pl.BlockSpec((tm, REST_WIDTH), lambda i: (i, 0)),
        out_shape=jax.ShapeDtypeStruct((m, REST_WIDTH), F32),
        compiler_params=_cparams(("arbitrary",)),
        name="inproj_rest",
    )(h, w_rest)
    return qkv, rest


NA_GROUP_ROWS = 4
NA_GROUP_TOK = NA_GROUP_ROWS * GRID_W
NA_WIN_BLOCKS = 3
NA_GROUPS_PER_STEP = 4
NA_LOOP_UNROLL = 2


NA_MASKED_SLAB = 2 * NA_WIN_ROWS - 1


def _na_bias_slabs(rpb):
    h = rpb.shape[0]
    kcol = np.arange(GRID_W)[:, None]
    qcol = np.arange(GRID_W)[None, :]
    wstart = np.clip(qcol - NA_WIN_COLS // 2, 0, GRID_W - NA_WIN_COLS)
    valid_col = (kcol >= wstart) & (kcol < wstart + NA_WIN_COLS)
    pad = GRID_W - NA_WIN_COLS
    padded = jnp.pad(rpb.astype(F32), ((0, 0), (0, 0), (pad, pad)))
    m = jnp.tile(padded, (1, 1, GRID_W + 1))[:, :, :GRID_W * 2 * GRID_W]
    m = m.reshape(h, rpb.shape[1], GRID_W, 2 * GRID_W)[..., :GRID_W]
    slabs = jnp.where(valid_col, m[..., ::-1] * LOG2E, -1e30)
    slabs = jnp.concatenate([slabs, jnp.full((h, 1, GRID_W, GRID_W), -1e30, F32)], axis=1)
    return jnp.concatenate([slabs, slabs], axis=-1)


def _na_slab_index(rows):
    n_groups = rows // NA_GROUP_ROWS
    krel = np.arange(NA_WIN_BLOCKS * NA_GROUP_ROWS)[:, None]
    qrel = np.arange(NA_GROUP_ROWS)[None, :]
    idx = []
    for g in (0, 1, n_groups - 1):
        krow = int(np.clip(g - 1, 0, n_groups - NA_WIN_BLOCKS)) * NA_GROUP_ROWS + krel
        qrow = g * NA_GROUP_ROWS + qrel
        r0 = np.clip(qrow - NA_WIN_ROWS // 2, 0, rows - NA_WIN_ROWS)
        in_window = (krow >= r0) & (krow < r0 + NA_WIN_ROWS)
        idx.append(np.where(in_window, krow - qrow + NA_WIN_ROWS - 1, NA_MASKED_SLAB))
    return np.stack(idx).tolist()


def _na_kernel(q_ref, k_ref, v_ref, kc_ref, vc_ref, slab_ref, wo_ref, w1_ref, w2_ref,
               o_ref, wo_b_ref, w1_b_ref, w2_b_ref, vt_ref, bias_ref, *, slab_index):
    n_groups = q_ref.shape[0] // NA_GROUP_TOK
    gt = NA_GROUP_TOK
    nt = (((1,), (1,)), ((), ()))
    dot = functools.partial(jnp.dot, preferred_element_type=F32)

    for src_ref, dst_ref in ((wo_ref, wo_b_ref), (w1_ref, w1_b_ref), (w2_ref, w2_b_ref)):
        dst_ref[...] = src_ref[...].astype(BF16)

    @pl.when(pl.program_id(1) == 0)
    def _():
        left = lax.broadcasted_iota(jnp.int32, (GRID_W, 2 * GRID_W), 1) < GRID_W
        for kind, per_key_row in enumerate(slab_index):
            for j, per_query_row in enumerate(per_key_row):
                for a in range(0, NA_GROUP_ROWS, 2):
                    tile = jnp.where(left, slab_ref[0, per_query_row[a]], slab_ref[0, per_query_row[a + 1]])
                    bias_ref[kind, j * GRID_W:(j + 1) * GRID_W, a * GRID_W:(a + 2) * GRID_W] = tile

    for b in range(n_groups):
        vt_ref[b] = v_ref[b * gt:(b + 1) * gt, :].T
    kc = kc_ref[...]
    vct = vc_ref[...].T

    def group(g):
        blk = jnp.clip(g - 1, 0, n_groups - NA_WIN_BLOCKS)
        kind = jnp.where(g == 0, 0, jnp.where(g == n_groups - 1, 2, 1))
        qg = q_ref[pl.ds(pl.multiple_of(g * gt, gt), gt), :]
        kw = k_ref[pl.ds(pl.multiple_of(blk * gt, gt), NA_WIN_BLOCKS * gt), :]
        s_lat = lax.dot_general(kw, qg, nt, preferred_element_type=F32) + bias_ref[kind]
        s_ctx = lax.dot_general(kc, qg, nt, preferred_element_type=F32)
        yield
        m = jnp.maximum(jnp.max(s_lat, axis=0, keepdims=True), jnp.max(s_ctx, axis=0, keepdims=True))
        p_lat = jnp.exp2(s_lat - m)
        p_ctx = jnp.exp2(s_ctx - m)
        l = jnp.sum(p_lat, axis=0, keepdims=True) + jnp.sum(p_ctx, axis=0, keepdims=True)
        yield
        ot = dot(vct, p_ctx.astype(BF16))
        for j in range(NA_WIN_BLOCKS):
            ot = ot + dot(vt_ref[blk + j], p_lat[j * gt:(j + 1) * gt, :].astype(BF16))
        yield
        o_ref[pl.ds(pl.multiple_of(g * gt, gt), gt), :] = (ot / l).T.astype(o_ref.dtype)
        yield

    def body(i, carry):
        groups = [group(i * NA_GROUPS_PER_STEP + u) for u in range(NA_GROUPS_PER_STEP)]
        for _ in range(4):
            for grp in groups:
                next(grp)
        return carry

    lax.fori_loop(0, n_groups // NA_GROUPS_PER_STEP, body, 0, unroll=NA_LOOP_UNROLL)


def _na_latent(p_l, p_c, slabs, weights, layer, b):
    t = p_l.shape[0] // b
    ctx = p_c.shape[0] // b
    hd = NA_HEAD_DIM
    n_steps = NA_HEADS * b
    w_specs_in, w_specs_out, w_shapes = [], [], []
    for w in weights:
        rows, cols = w.shape[1] // n_steps, w.shape[2]
        assert w.shape[1] % n_steps == 0 and rows % BF16_SUBLANES == 0
        w_specs_in.append(pl.BlockSpec((None, rows, cols), lambda h, i: (layer, h * b + i, 0)))
        w_specs_out.append(pl.BlockSpec((None, rows, cols), lambda h, i: (0, h * b + i, 0)))
        w_shapes.append(jax.ShapeDtypeStruct((1,) + w.shape[1:], BF16))
    oa, *w_bf16 = pl.pallas_call(
        functools.partial(_na_kernel, slab_index=_na_slab_index(t // GRID_W)),
        grid=(NA_HEADS, b),
        in_specs=[
            pl.BlockSpec((t, hd), lambda h, i: (i, _QA + h)),
            pl.BlockSpec((t, hd), lambda h, i: (i, _KA + h)),
            pl.BlockSpec((t, hd), lambda h, i: (i, _VA + h)),
            pl.BlockSpec((ctx, hd), lambda h, i: (i, _KA + h)),
            pl.BlockSpec((ctx, hd), lambda h, i: (i, _VA + h)),
            pl.BlockSpec((1,) + slabs.shape[1:], lambda h, i: (h, 0, 0, 0)),
        ] + w_specs_in,
        out_specs=[pl.BlockSpec((t, hd), lambda h, i: (i, h))] + w_specs_out,
        out_shape=[jax.ShapeDtypeStruct((b * t, NA_WIDTH), BF16)] + w_shapes,
        scratch_shapes=[
            pltpu.VMEM((t // NA_GROUP_TOK, hd, NA_GROUP_TOK), BF16),
            pltpu.VMEM((3, NA_WIN_BLOCKS * NA_GROUP_TOK, NA_GROUP_TOK), F32),
        ],
        compiler_params=_cparams(("arbitrary", "arbitrary")),
        name="na_latent",
    )(p_l, p_l, p_l, p_c, p_c, slabs, *weights)
    return oa, w_bf16


def _ctx_attn_kernel(q_ref, k_ref, v_ref, o_ref):
    for h in range(NA_HEADS):
        hs = slice(h * NA_HEAD_DIM, (h + 1) * NA_HEAD_DIM)
        s = lax.dot_general(q_ref[:, hs], k_ref[:, hs], (((1,), (1,)), ((), ())), preferred_element_type=F32)
        p = jnp.exp2(s - jnp.max(s, axis=-1, keepdims=True))
        l = jnp.sum(p, axis=-1, keepdims=True)
        o = jnp.dot(p.astype(BF16), v_ref[:, hs], preferred_element_type=F32)
        o_ref[:, hs] = (o / l).astype(o_ref.dtype)


def _ctx_attention(p_c, b):
    ctx = p_c.shape[0] // b
    return pl.pallas_call(
        _ctx_attn_kernel,
        grid=(b,),
        in_specs=[
            pl.BlockSpec((ctx, NA_WIDTH), lambda i: (i, 0)),
            pl.BlockSpec((ctx, NA_WIDTH), lambda i: (i, 1)),
            pl.BlockSpec((ctx, NA_WIDTH), lambda i: (i, 2)),
        ],
        out_specs=pl.BlockSpec((ctx, NA_WIDTH), lambda i: (i, 0)),
        out_shape=jax.ShapeDtypeStruct((b * ctx, NA_WIDTH), BF16),
        compiler_params=_cparams(("arbitrary",)),
        name="ctx_attention",
    )(p_c, p_c, p_c)


def _gm_block(u_ref, v_ref, lnw_ref, ws_ref, bs_ref, o_ref):
    for ck in range(u_ref.shape[0] // GM_CHUNK):
        rows = slice(ck * GM_CHUNK, (ck + 1) * GM_CHUNK)
        uf = _gelu_tanh(u_ref[rows, :])
        vf = _gelu_tanh(v_ref[rows, :])
        for g in range(GM_GROUPS):
            sl = slice(g * GM_DIM, (g + 1) * GM_DIM)
            vg = vf[:, sl]
            mu = jnp.mean(vg, axis=-1, keepdims=True)
            dv = vg - mu
            var = jnp.mean(dv * dv, axis=-1, keepdims=True)
            vn = dv * lax.rsqrt(var + EPS) * lnw_ref[:, sl]
            mixed = jnp.dot(ws_ref[g].astype(BF16), vn.astype(BF16), preferred_element_type=F32) + bs_ref[g]
            o_ref[rows, sl] = (uf[:, sl] * mixed).astype(o_ref.dtype)


HG_PAIR = 2
HG_LEVELS = (32, 16, 8)
HG_DIAG = 8
HG_PREP_ROWS = 256
HG_SCAN_UNROLL = 8


def _split3_dot(tri, g):
    g0 = g.astype(BF16)
    r1 = g - g0.astype(F32)
    g1 = r1.astype(BF16)
    g2 = (r1 - g1.astype(F32)).astype(BF16)
    dot = functools.partial(jnp.dot, preferred_element_type=F32)
    return dot(tri, g0) + dot(tri, g1) + dot(tri, g2)


def _hg_kernel(ql_ref, ffl_ref, fbl_ref, il_ref,
               qc_ref, ffc_ref, fbc_ref, ic_ref,
               lbp_ref, tri_ref, code_ref, lsum_ref, ol_ref, oc_ref,
               qh_s, kf_s, kb_s, ef_s, eb_s, of_s, ob_s, st_s):
    c = HG_CHUNK
    hd = HG_DIM
    n_ctx = qc_ref.shape[0]
    n_lat = ql_ref.shape[0]
    scale = hd ** -0.5
    nt = (((1,), (1,)), ((), ()))
    tn = (((0,), (0,)), ((), ()))

    def prep(q_ref, ff_ref, fb_ref, base, n):
        step = HG_PREP_ROWS
        for t0 in range(0, n, step):
            src = slice(t0, t0 + step)
            dst = slice(base + t0, base + t0 + step)
            qh = _silu(q_ref[src, :]) * scale
            for h in range(HG_PAIR):
                qh_s[h, dst, :] = qh[:, h * hd:(h + 1) * hd]
            for d, (f_ref, k_s, e_s) in enumerate(((ff_ref, kf_s, ef_s), (fb_ref, kb_s, eb_s))):
                x = f_ref[src, :]
                sp = jnp.maximum(-x, 0.0) + jnp.log(1.0 + jnp.exp(-jnp.abs(x)))
                log_lb = lbp_ref[d, 0:1, :]
                y = lbp_ref[d, 1:2, :] - sp
                mx = jnp.maximum(log_lb, y)
                log_f = mx + jnp.log(1.0 + jnp.exp(-jnp.abs(log_lb - y)))
                one_minus_f = lbp_ref[d, 2:3, :] * jnp.exp(-(sp + x))
                e = _split3_dot(tri_ref[d], log_f) * LOG2E
                for h in range(HG_PAIR):
                    e_s[h, dst, :] = e[:, h * hd:(h + 1) * hd]
                    k_s[h, dst, :] = one_minus_f[:, h * hd:(h + 1) * hd]

    prep(qc_ref, ffc_ref, fbc_ref, 0, n_ctx)
    prep(ql_ref, ffl_ref, fbl_ref, n_ctx, n_lat)

    diag_code = len(HG_LEVELS) + 1

    st_s[...] = jnp.zeros_like(st_s)

    def chunk(off, v, h, d, out):
        fwd = d == 0
        k_s, e_s = (kf_s, ef_s) if fwd else (kb_s, eb_s)
        q = qh_s[h, pl.ds(off, c), :]
        k = k_s[h, pl.ds(off, c), :]
        e = e_s[h, pl.ds(off, c), :]
        code = code_ref[d]
        prods = []
        for i in range(c // HG_DIAG):
            bs = slice(i * HG_DIAG, (i + 1) * HG_DIAG)
            row_prods = []
            for s in range(HG_DIAG):
                kr = k_s[h, pl.ds(off + i * HG_DIAG + s, 1), :]
                er = e_s[h, pl.ds(off + i * HG_DIAG + s, 1), :]
                decay = jnp.exp2(jnp.minimum(e[bs] - er, 0.0))
                row_prods.append((q[bs] * kr * decay).astype(BF16))
            prods.append(jnp.concatenate(row_prods, axis=1))
        diag = jnp.dot(jnp.concatenate(prods, axis=0), lsum_ref[...], preferred_element_type=F32)
        yield
        att = None
        for li, w in enumerate(HG_LEVELS):
            zeros = jnp.zeros((w, hd), F32)
            q_parts, k_parts = [], []
            for a in range(0, c, 2 * w):
                lo, hi = slice(a, a + w), slice(a + w, a + 2 * w)
                if fwd:
                    ref = e_s[h, pl.ds(off + a + w - 1, 1), :]
                    q_parts += [zeros, q[hi] * jnp.exp2(e[hi] - ref)]
                    k_parts += [k[lo] * jnp.exp2(ref - e[lo]), zeros]
                else:
                    ref = e_s[h, pl.ds(off + a + w, 1), :]
                    q_parts += [q[lo] * jnp.exp2(e[lo] - ref), zeros]
                    k_parts += [zeros, k[hi] * jnp.exp2(ref - e[hi])]
            qs = jnp.concatenate(q_parts, axis=0).astype(BF16)
            ks = jnp.concatenate(k_parts, axis=0).astype(BF16)
            a_w = lax.dot_general(qs, ks, nt, preferred_element_type=F32)
            att = a_w if att is None else jnp.where(code == li + 1, a_w, att)
            yield
        e_end = e_s[h, pl.ds(off + (c - 1 if fwd else 0), 1), :]
        st = st_s[h, d]
        qi = (q * jnp.exp2(e)).astype(BF16)
        ki = (k * jnp.exp2(e_end - e)).astype(BF16)
        o = lax.dot_general(qi, st.astype(BF16), nt, preferred_element_type=F32)
        yield
        st_new = st * jnp.exp2(e_end) + lax.dot_general(v, ki, tn, preferred_element_type=F32)
        yield
        att = jnp.where(code == diag_code, diag, att)
        out.append((o + jnp.dot(att.astype(BF16), v, preferred_element_type=F32), st_new))
        yield

    n_stages = len(HG_LEVELS) + 4

    def scan(v_ref, base, n):
        def body(i, carry):
            rf = pl.multiple_of(i * c, c)
            rb = pl.multiple_of((n - 1 - i) * c, c)
            vf = v_ref[pl.ds(rf, c), :].astype(BF16)
            vb = v_ref[pl.ds(rb, c), :].astype(BF16)
            chains = []
            for h in range(HG_PAIR):
                hs = slice(h * hd, (h + 1) * hd)
                for d, r, v in ((0, rf, vf), (1, rb, vb)):
                    out = []
                    chains.append((h, d, r, out, chunk(base + r, v[:, hs], h, d, out)))
            for _ in range(n_stages):
                for chain in chains:
                    next(chain[-1])
            for h, d, r, out, _ in chains:
                o, st_new = out[0]
                (of_s if d == 0 else ob_s)[h, pl.ds(base + r, c), :] = o
                st_s[h, d] = st_new
            return carry

        lax.fori_loop(0, n, body, 0, unroll=HG_SCAN_UNROLL)

    scan(ic_ref, 0, n_ctx // c)
    scan(il_ref, n_ctx, n_lat // c)

    def finish(o_ref, base, n):
        step = 256
        for t0 in range(0, n, step):
            src = slice(t0, t0 + step)
            dst = slice(base + t0, base + t0 + step)
            for h in range(HG_PAIR):
                o_ref[src, h * hd:(h + 1) * hd] = of_s[h, dst, :] + ob_s[h, dst, :]

    finish(oc_ref, 0, n_ctx)
    finish(ol_ref, n_ctx, n_lat)


def _hg_constants():
    c = HG_CHUNK
    row, col = np.arange(c)[:, None], np.arange(c)[None, :]
    blocks = np.eye(HG_PREP_ROWS // c)
    tri = np.stack([np.kron(blocks, col <= row), np.kron(blocks, col >= row)]).astype(np.float32)
    same8 = (row // HG_DIAG) == (col // HG_DIAG)
    diag_code = len(HG_LEVELS) + 1
    code_f = np.where(same8 & (col <= row), diag_code, 0)
    code_b = np.where(same8 & (col >= row), diag_code, 0)
    for li, w in enumerate(HG_LEVELS):
        same = (row // (2 * w)) == (col // (2 * w))
        t_hi, s_hi = (row % (2 * w)) >= w, (col % (2 * w)) >= w
        code_f = np.where(same & t_hi & ~s_hi, li + 1, code_f)
        code_b = np.where(same & ~t_hi & s_hi, li + 1, code_b)
    codes = np.stack([code_f, code_b]).astype(np.int32)
    lane_sum = (np.arange(HG_DIAG * HG_DIM)[:, None] // HG_DIM == col % HG_DIAG).astype(np.float32)
    return jnp.asarray(tri, BF16), jnp.asarray(codes), jnp.asarray(lane_sum, BF16)


def _hgrn2(p_l, p_c, lbp, b):
    tri, codes, lane_sum = _hg_constants()
    t = p_l.shape[0] // b
    ctx = p_c.shape[0] // b
    hd = HG_DIM
    pw = HG_PAIR * hd
    n = t + ctx
    first = _HG0 * LANE // pw

    def col(stream):
        return lambda i, j: (i, first + stream * (HG_HEADS // HG_PAIR) + j)

    lat_specs = [pl.BlockSpec((t, pw), col(s)) for s in range(4)]
    ctx_specs = [pl.BlockSpec((ctx, pw), col(s)) for s in range(4)]
    big = lambda: pltpu.VMEM((HG_PAIR, n, hd), F32)
    return pl.pallas_call(
        _hg_kernel,
        grid=(b, HG_HEADS // HG_PAIR),
        in_specs=lat_specs + ctx_specs + [
            pl.BlockSpec((2, 3, pw), lambda i, j: (0, 0, j)),
            pl.BlockSpec(tri.shape, lambda i, j: (0, 0, 0)),
            pl.BlockSpec(codes.shape, lambda i, j: (0, 0, 0)),
            pl.BlockSpec(lane_sum.shape, lambda i, j: (0, 0)),
        ],
        out_specs=[
            pl.BlockSpec((t, pw), lambda i, j: (i, j)),
            pl.BlockSpec((ctx, pw), lambda i, j: (i, j)),
        ],
        out_shape=[
            jax.ShapeDtypeStruct((b * t, HG_WIDTH), F32),
            jax.ShapeDtypeStruct((b * ctx, HG_WIDTH), F32),
        ],
        scratch_shapes=[big() for _ in range(7)] + [pltpu.VMEM((HG_PAIR, 2, hd, hd), F32)],
        compiler_params=_cparams(("arbitrary", "arbitrary")),
        name="hgrn2",
    )(*([p_l] * 4 + [p_c] * 4 + [lbp, tri, codes, lane_sum]))


def _outproj_kernel(oa_ref, os_ref, hg_ref, hnw_ref, u_ref, v_ref, lnw_ref, ws_ref, bs_ref,
                    wa_ref, wb_ref, wc_ref, x_ref, g_ref, o_ref, ob_ref, oc_ref):
    dot = functools.partial(jnp.dot, preferred_element_type=F32)
    y = dot(oa_ref[...], wa_ref[...])
    for h in range(HG_HEADS):
        hs = slice(h * HG_DIM, (h + 1) * HG_DIM)
        o = os_ref[:, hs]
        o = o * lax.rsqrt(jnp.mean(o * o, axis=-1, keepdims=True) + EPS) * hnw_ref[...]
        ob_ref[:, hs] = (o * _silu(hg_ref[:, hs])).astype(ob_ref.dtype)
    y = y + dot(ob_ref[...], wb_ref[...])
    _gm_block(u_ref, v_ref, lnw_ref, ws_ref, bs_ref, oc_ref)
    y = y + dot(oc_ref[...], wc_ref[...])
    o_ref[...] = x_ref[...] + g_ref[0] * y


def _outproj(oa, osum, p2, hg_norm_w, gm_ln_w, gm_ws, gm_bs, w, w_layer, x2d, mod, layer, rows_per_mod,
             mod_row0, tm):
    m, d = x2d.shape
    mrow = lambda i: layer * MOD_ROWS + mod_row0 + (i * tm) // rows_per_mod
    nb = NA_WIDTH // HG_WIDTH
    assert tm % GM_CHUNK == 0
    return pl.pallas_call(
        _outproj_kernel,
        grid=(m // tm,),
        in_specs=[
            pl.BlockSpec((tm, NA_WIDTH), lambda i: (i, 0)),
            pl.BlockSpec((tm, HG_WIDTH), lambda i: (i, 0)),
            pl.BlockSpec((tm, HG_WIDTH), lambda i: (i, _HG_G)),
            pl.BlockSpec((1, HG_DIM), lambda i: (0, 0)),
            pl.BlockSpec((tm, GM_WIDTH), lambda i: (i, _GM_U)),
            pl.BlockSpec((tm, GM_WIDTH), lambda i: (i, _GM_V)),
            pl.BlockSpec((1, GM_WIDTH), lambda i: (0, 0)),
            pl.BlockSpec((GM_GROUPS, GM_CHUNK, GM_CHUNK), lambda i: (0, 0, 0)),
            pl.BlockSpec((GM_GROUPS, GM_CHUNK, 1), lambda i: (0, 0, 0)),
            pl.BlockSpec((None, NA_WIDTH, d), lambda i: (w_layer, 0, 0)),
            pl.BlockSpec((None, HG_WIDTH, d), lambda i: (w_layer, nb, 0)),
            pl.BlockSpec((None, GM_WIDTH, d), lambda i: (w_layer, nb + 1, 0)),
            pl.BlockSpec((tm, d), lambda i: (i, 0)),
            pl.BlockSpec((1, 1, d), lambda i: (mrow(i), 0, 2)),
        ],
        out_specs=pl.BlockSpec((tm, d), lambda i: (i, 0)),
        out_shape=jax.ShapeDtypeStruct((m, d), F32),
        scratch_shapes=[pltpu.VMEM((tm, HG_WIDTH), BF16), pltpu.VMEM((tm, GM_WIDTH), BF16)],
        compiler_params=_cparams(("arbitrary",)),
        name="outproj",
    )(oa, osum, p2, hg_norm_w.reshape(1, HG_DIM), p2, p2, gm_ln_w.reshape(1, GM_WIDTH), gm_ws,
      gm_bs.reshape(GM_GROUPS, GM_CHUNK, 1), w, w, w, x2d, mod)


def _mlp_kernel(x_ref, nw_ref, sh_ref, sc_ref, g_ref, w1_ref, w2_ref, fnw_ref, o_ref, h_ref, acc_ref,
                *, final_norm):
    j = pl.program_id(1)

    @pl.when(j == 0)
    def _():
        _norm_modulate_store(x_ref, nw_ref, sh_ref, sc_ref, h_ref)
        acc_ref[...] = jnp.zeros_like(acc_ref)

    a = jnp.maximum(jnp.dot(h_ref[...], w1_ref[...], preferred_element_type=F32), 0.0)
    acc_ref[...] += jnp.dot((a * a).astype(BF16), w2_ref[...], preferred_element_type=F32)

    @pl.when(j == pl.num_programs(1) - 1)
    def _():
        y = x_ref[...] + g_ref[0] * acc_ref[...]
        if final_norm:
            y = y * lax.rsqrt(jnp.mean(y * y, axis=-1, keepdims=True) + EPS) * fnw_ref[...]
        o_ref[...] = y


def _mlp(x2d, nw, mod, w1, w2, w_layer, fnw, layer, rows_per_mod, mod_row0, tm, final_norm):
    m, d = x2d.shape
    hid = w1.shape[2]
    th = MLP_TH
    mrow = lambda i: layer * MOD_ROWS + mod_row0 + (i * tm) // rows_per_mod
    return pl.pallas_call(
        functools.partial(_mlp_kernel, final_norm=final_norm),
        grid=(m // tm, hid // th),
        in_specs=[
            pl.BlockSpec((tm, d), lambda i, j: (i, 0)),
            pl.BlockSpec((None, 1, d), lambda i, j: (layer, 0, 0)),
            pl.BlockSpec((1, 1, d), lambda i, j: (mrow(i), 0, 3)),
            pl.BlockSpec((1, 1, d), lambda i, j: (mrow(i), 0, 4)),
            pl.BlockSpec((1, 1, d), lambda i, j: (mrow(i), 0, 5)),
            pl.BlockSpec((None, d, th), lambda i, j: (w_layer, 0, j)),
            pl.BlockSpec((None, th, d), lambda i, j: (w_layer, j, 0)),
            pl.BlockSpec((1, d), lambda i, j: (0, 0)),
        ],
        out_specs=pl.BlockSpec((tm, d), lambda i, j: (i, 0)),
        out_shape=jax.ShapeDtypeStruct((m, d), F32),
        scratch_shapes=[pltpu.VMEM((tm, d), BF16), pltpu.VMEM((tm, d), F32)],
        compiler_params=_cparams(("arbitrary", "arbitrary")),
        name="mlp",
    )(x2d, nw, mod, mod, mod, w1, w2, fnw)


def kernel(x, c, ctx, c_ctx, ada_w, ada_b, norm1_w, norm2_w, w_in, na_rpb, hg_lb_logits, hg_norm_w,
           gm_ln_w, gm_ws, gm_bs, w_out, mlp_w1, mlp_w2, final_norm_w):
    bsz, seq, d = x.shape
    n_ctx = ctx.shape[1]
    depth = ada_w.shape[0]
    assert bsz < MOD_ROWS and d == D_MODEL and seq % 512 == 0 and n_ctx % 256 == 0
    assert seq // NA_GROUP_TOK >= NA_WIN_BLOCKS and seq // GRID_W >= 2 * NA_WIN_ROWS

    lb = jnp.cumsum(jax.nn.softmax(hg_lb_logits.astype(F32), axis=0), axis=0)
    lb = lb - lb[:1]
    lbp = jnp.stack([jnp.log(lb), jnp.log1p(-lb), 1.0 - lb], axis=2)

    cond = jnp.zeros((MOD_ROWS, d), F32).at[:bsz].set(c).at[bsz].set(c_ctx)
    mod = _ada(cond, ada_w, ada_b).reshape(depth * MOD_ROWS, 1, 6 * d)

    w_in_b = (w_in[:, :, :QKV_WIDTH].astype(BF16), w_in[:, :, QKV_WIDTH:].astype(BF16))
    nw1 = norm1_w.reshape(depth, 1, d)
    nw2 = norm2_w.reshape(depth, 1, d)
    fnw = final_norm_w.reshape(1, d)

    xl = x.reshape(bsz * seq, d)
    xc = ctx.reshape(bsz * n_ctx, d)
    n_c = bsz * n_ctx
    tm_l, tm_c = TM_LATENT, min(TM_CONTEXT, n_c)
    for l in range(depth):
        need_ctx = l < depth - 1
        qkv_l, p_l = _inproj(xl, nw1, mod, *w_in_b, l, seq, 0, tm_l)
        qkv_c, p_c = _inproj(xc, nw1, mod, *w_in_b, l, n_c, bsz, tm_c)

        oa_l, (w_out_b, w1_b, w2_b) = _na_latent(qkv_l, qkv_c, _na_bias_slabs(na_rpb[l]),
                                                 (w_out, mlp_w1, mlp_w2), l, bsz)
        os_l, os_c = _hgrn2(p_l, p_c, lbp[l], bsz)
        mix = (hg_norm_w[l], gm_ln_w[l], gm_ws[l], gm_bs[l])
        xl = _outproj(oa_l, os_l, p_l, *mix, w_out_b, 0, xl, mod, l, seq, 0, tm_l)
        xl = _mlp(xl, nw2, mod, w1_b, w2_b, 0, fnw, l, seq, 0, tm_l, final_norm=not need_ctx)
        if need_ctx:
            oa_c = _ctx_attention(qkv_c, bsz)
            xc = _outproj(oa_c, os_c, p_c, *mix, w_out_b, 0, xc, mod, l, n_c, bsz, tm_c)
            xc = _mlp(xc, nw2, mod, w1_b, w2_b, 0, fnw, l, n_c, bsz, tm_c, final_norm=False)
    return xl.reshape(bsz, seq, d)
```

```python
import functools

import numpy as np
import jax
import jax.numpy as jnp
from jax import lax
from jax.experimental import pallas as pl
from jax.experimental.pallas import tpu as pltpu

F32 = jnp.float32
BF16 = jnp.bfloat16

D_MODEL = 2048
DEPTH = 2
GRID_W = 64
EPS = 1e-6

NA_HEAD_DIM = 128
NA_HEADS = 8
NA_WIDTH = NA_HEADS * NA_HEAD_DIM
NA_WIN_ROWS = 8
NA_WIN_COLS = 16

HG_HEADS = 4
HG_DIM = 128
HG_WIDTH = HG_HEADS * HG_DIM
HG_CHUNK = 64

GM_GROUPS = 4
GM_DIM = 128
GM_WIDTH = GM_GROUPS * GM_DIM
GM_CHUNK = 128

IN_WIDTH = 3 * NA_WIDTH + 5 * HG_WIDTH + 2 * GM_WIDTH
MLP_HIDDEN = 4 * D_MODEL
LANE = 128
LOG2E = 1.4426950408889634
BF16_SUBLANES = 16

QKV_WIDTH = 3 * NA_WIDTH
REST_WIDTH = IN_WIDTH - QKV_WIDTH
REST_MAIN = QKV_WIDTH
REST_TAIL = REST_WIDTH - REST_MAIN
NA_SCORE_SCALE = NA_HEAD_DIM ** -0.5 * LOG2E
_QA, _KA, _VA = 0, NA_HEADS, 2 * NA_HEADS
_HG0 = 0
_GM_U = 5 * HG_WIDTH // GM_WIDTH
_GM_V = _GM_U + 1
_HG_G = 4

VMEM_LIMIT_V7X = 56 * 1024 * 1024
TM_LATENT = 512
TM_CONTEXT = 512
MLP_TH = 1024
NORM_CHUNK_ROWS = 32
NORM_UNROLL = 4
MOD_ROWS = 16


def _cparams(sem):
    return pltpu.CompilerParams(dimension_semantics=sem, vmem_limit_bytes=VMEM_LIMIT_V7X)


def _silu(x):
    h = 0.5 * x
    return h + h * jnp.tanh(h)


def _gelu_tanh(x):
    return 0.5 * x * (1.0 + jnp.tanh(0.7978845608028654 * (x + 0.044715 * (x * x * x))))


def _norm_modulate_store(x_ref, nw_ref, sh_ref, sc_ref, h_ref):
    gain = nw_ref[...] * (1.0 + sc_ref[0])
    shift = sh_ref[0]

    def body(i, carry):
        rows = pl.ds(pl.multiple_of(i * NORM_CHUNK_ROWS, NORM_CHUNK_ROWS), NORM_CHUNK_ROWS)
        x = x_ref[rows, :]
        rs = lax.rsqrt(jnp.mean(x * x, axis=-1, keepdims=True) + EPS)
        h_ref[rows, :] = (x * rs * gain + shift).astype(h_ref.dtype)
        return carry

    lax.fori_loop(0, x_ref.shape[0] // NORM_CHUNK_ROWS, body, 0, unroll=NORM_UNROLL)


def _ada_kernel(c_ref, w_ref, b_ref, o_ref):
    s = _silu(c_ref[...]).astype(BF16)
    o_ref[0] = jnp.dot(s, w_ref[0].astype(BF16), preferred_element_type=F32) + b_ref[0]


def _ada(cond, ada_w, ada_b):
    depth, d, n = ada_w.shape
    tn = 1536
    return pl.pallas_call(
        _ada_kernel,
        grid=(depth, n // tn),
        in_specs=[
            pl.BlockSpec((cond.shape[0], d), lambda l, j: (0, 0)),
            pl.BlockSpec((1, d, tn), lambda l, j: (l, 0, j)),
            pl.BlockSpec((1, 1, tn), lambda l, j: (l, 0, j)),
        ],
        out_specs=pl.BlockSpec((1, cond.shape[0], tn), lambda l, j: (l, 0, j)),
        out_shape=jax.ShapeDtypeStruct((depth, cond.shape[0], n), F32),
        compiler_params=_cparams(("arbitrary", "arbitrary")),
        name="ada",
    )(cond, ada_w, ada_b.reshape(depth, 1, n))


def _inproj_qkv_kernel(x_ref, nw_ref, sh_ref, sc_ref, w_ref, o_ref, h_ref):
    _norm_modulate_store(x_ref, nw_ref, sh_ref, sc_ref, h_ref)
    h = h_ref[...]
    q = jnp.dot(h, w_ref[:, :NA_WIDTH], preferred_element_type=F32)
    o_ref[:, :NA_WIDTH] = (q * NA_SCORE_SCALE).astype(o_ref.dtype)
    o_ref[:, NA_WIDTH:] = jnp.dot(h, w_ref[:, NA_WIDTH:], preferred_element_type=F32).astype(o_ref.dtype)


def _inproj_rest_kernel(h_ref, wa_ref, wb_ref, o_ref):
    h = h_ref[...]
    o_ref[:, :REST_MAIN] = jnp.dot(h, wa_ref[...], preferred_element_type=F32)
    o_ref[:, REST_MAIN:] = jnp.dot(h, wb_ref[...], preferred_element_type=F32)


def _inproj(x2d, nw, mod, w, layer, rows_per_mod, mod_row0, tm):
    m, d = x2d.shape
    mrow = lambda i: layer * MOD_ROWS + mod_row0 + (i * tm) // rows_per_mod
    assert w.shape[2] == QKV_WIDTH + REST_MAIN + REST_TAIL and (QKV_WIDTH + REST_MAIN) % REST_TAIL == 0
    qkv, h = pl.pallas_call(
        _inproj_qkv_kernel,
        grid=(m // tm,),
        in_specs=[
            pl.BlockSpec((tm, d), lambda i: (i, 0)),
            pl.BlockSpec((None, 1, d), lambda i: (layer, 0, 0)),
            pl.BlockSpec((1, 1, d), lambda i: (mrow(i), 0, 0)),
            pl.BlockSpec((1, 1, d), lambda i: (mrow(i), 0, 1)),
            pl.BlockSpec((None, d, QKV_WIDTH), lambda i: (layer, 0, 0)),
        ],
        out_specs=[pl.BlockSpec((tm, QKV_WIDTH), lambda i: (i, 0)), pl.BlockSpec((tm, d), lambda i: (i, 0))],
        out_shape=[jax.ShapeDtypeStruct((m, QKV_WIDTH), BF16), jax.ShapeDtypeStruct((m, d), BF16)],
        compiler_params=_cparams(("arbitrary",)),
        name="inproj_qkv",
    )(x2d, nw, mod, mod, w)
    rest = pl.pallas_call(
        _inproj_rest_kernel,
        grid=(m // tm,),
        in_specs=[
            pl.BlockSpec((tm, d), lambda i: (i, 0)),
            pl.BlockSpec((None, d, REST_MAIN), lambda i: (layer, 0, 1)),
            pl.BlockSpec((None, d, REST_TAIL), lambda i: (layer, 0, (QKV_WIDTH + REST_MAIN) // REST_TAIL)),
        ],
        out_specs=pl.BlockSpec((tm, REST_WIDTH), lambda i: (i, 0)),
        out_shape=jax.ShapeDtypeStruct((m, REST_WIDTH), F32),
        compiler_params=_cparams(("arbitrary",)),
        name="inproj_rest",
    )(h, w, w)
    return qkv, rest


NA_GROUP_ROWS = 4
NA_GROUP_TOK = NA_GROUP_ROWS * GRID_W
NA_WIN_BLOCKS = 3
NA_GROUPS_PER_STEP = 4
NA_LOOP_UNROLL = 2


NA_MASKED_SLAB = 2 * NA_WIN_ROWS - 1


def _na_bias_slabs(rpb):
    h = rpb.shape[0]
    kcol = np.arange(GRID_W)[:, None]
    qcol = np.arange(GRID_W)[None, :]
    wstart = np.clip(qcol - NA_WIN_COLS // 2, 0, GRID_W - NA_WIN_COLS)
    valid_col = (kcol >= wstart) & (kcol < wstart + NA_WIN_COLS)
    pad = GRID_W - NA_WIN_COLS
    padded = jnp.pad(rpb.astype(F32), ((0, 0), (0, 0), (pad, pad)))
    m = jnp.tile(padded, (1, 1, GRID_W + 1))[:, :, :GRID_W * 2 * GRID_W]
    m = m.reshape(h, rpb.shape[1], GRID_W, 2 * GRID_W)[..., :GRID_W]
    slabs = jnp.where(valid_col, m[..., ::-1] * LOG2E, -1e30)
    slabs = jnp.concatenate([slabs, jnp.full((h, 1, GRID_W, GRID_W), -1e30, F32)], axis=1)
    return jnp.concatenate([slabs, slabs], axis=-1)


def _na_slab_index(rows):
    n_groups = rows // NA_GROUP_ROWS
    krel = np.arange(NA_WIN_BLOCKS * NA_GROUP_ROWS)[:, None]
    qrel = np.arange(NA_GROUP_ROWS)[None, :]
    idx = []
    for g in (0, 1, n_groups - 1):
        krow = int(np.clip(g - 1, 0, n_groups - NA_WIN_BLOCKS)) * NA_GROUP_ROWS + krel
        qrow = g * NA_GROUP_ROWS + qrel
        r0 = np.clip(qrow - NA_WIN_ROWS // 2, 0, rows - NA_WIN_ROWS)
        in_window = (krow >= r0) & (krow < r0 + NA_WIN_ROWS)
        idx.append(np.where(in_window, krow - qrow + NA_WIN_ROWS - 1, NA_MASKED_SLAB))
    return np.stack(idx).tolist()


def _na_kernel(q_ref, k_ref, v_ref, kc_ref, vc_ref, slab_ref, wo_ref, w1_ref, w2_ref,
               o_ref, wo_b_ref, w1_b_ref, w2_b_ref, vt_ref, bias_ref, *, slab_index):
    n_groups = q_ref.shape[0] // NA_GROUP_TOK
    gt = NA_GROUP_TOK
    nt = (((1,), (1,)), ((), ()))
    dot = functools.partial(jnp.dot, preferred_element_type=F32)

    for src_ref, dst_ref in ((wo_ref, wo_b_ref), (w1_ref, w1_b_ref), (w2_ref, w2_b_ref)):
        dst_ref[...] = src_ref[...].astype(BF16)

    @pl.when(pl.program_id(1) == 0)
    def _():
        left = lax.broadcasted_iota(jnp.int32, (GRID_W, 2 * GRID_W), 1) < GRID_W
        for kind, per_key_row in enumerate(slab_index):
            for j, per_query_row in enumerate(per_key_row):
                for a in range(0, NA_GROUP_ROWS, 2):
                    tile = jnp.where(left, slab_ref[0, per_query_row[a]], slab_ref[0, per_query_row[a + 1]])
                    bias_ref[kind, j * GRID_W:(j + 1) * GRID_W, a * GRID_W:(a + 2) * GRID_W] = tile

    for b in range(n_groups):
        vt_ref[b] = v_ref[b * gt:(b + 1) * gt, :].T
    kc = kc_ref[...]
    vct = vc_ref[...].T

    def group(g):
        blk = jnp.clip(g - 1, 0, n_groups - NA_WIN_BLOCKS)
        kind = jnp.where(g == 0, 0, jnp.where(g == n_groups - 1, 2, 1))
        qg = q_ref[pl.ds(pl.multiple_of(g * gt, gt), gt), :]
        kw = k_ref[pl.ds(pl.multiple_of(blk * gt, gt), NA_WIN_BLOCKS * gt), :]
        s_lat = lax.dot_general(kw, qg, nt, preferred_element_type=F32) + bias_ref[kind]
        s_ctx = lax.dot_general(kc, qg, nt, preferred_element_type=F32)
        yield
        m = jnp.maximum(jnp.max(s_lat, axis=0, keepdims=True), jnp.max(s_ctx, axis=0, keepdims=True))
        p_lat = jnp.exp2(s_lat - m)
        p_ctx = jnp.exp2(s_ctx - m)
        l = jnp.sum(p_lat, axis=0, keepdims=True) + jnp.sum(p_ctx, axis=0, keepdims=True)
        yield
        ot = dot(vct, p_ctx.astype(BF16))
        for j in range(NA_WIN_BLOCKS):
            ot = ot + dot(vt_ref[blk + j], p_lat[j * gt:(j + 1) * gt, :].astype(BF16))
        yield
        o_ref[pl.ds(pl.multiple_of(g * gt, gt), gt), :] = (ot / l).T.astype(o_ref.dtype)
        yield

    def body(i, carry):
        groups = [group(i * NA_GROUPS_PER_STEP + u) for u in range(NA_GROUPS_PER_STEP)]
        for _ in range(4):
            for grp in groups:
                next(grp)
        return carry

    lax.fori_loop(0, n_groups // NA_GROUPS_PER_STEP, body, 0, unroll=NA_LOOP_UNROLL)


def _na_latent(p_l, p_c, slabs, weights, layer, b):
    t = p_l.shape[0] // b
    ctx = p_c.shape[0] // b
    hd = NA_HEAD_DIM
    n_steps = NA_HEADS * b
    w_specs_in, w_specs_out, w_shapes = [], [], []
    for w in weights:
        rows, cols = w.shape[1] // n_steps, w.shape[2]
        assert w.shape[1] % n_steps == 0 and rows % BF16_SUBLANES == 0
        w_specs_in.append(pl.BlockSpec((None, rows, cols), lambda h, i: (layer, h * b + i, 0)))
        w_specs_out.append(pl.BlockSpec((None, rows, cols), lambda h, i: (0, h * b + i, 0)))
        w_shapes.append(jax.ShapeDtypeStruct((1,) + w.shape[1:], BF16))
    oa, *w_bf16 = pl.pallas_call(
        functools.partial(_na_kernel, slab_index=_na_slab_index(t // GRID_W)),
        grid=(NA_HEADS, b),
        in_specs=[
            pl.BlockSpec((t, hd), lambda h, i: (i, _QA + h)),
            pl.BlockSpec((t, hd), lambda h, i: (i, _KA + h)),
            pl.BlockSpec((t, hd), lambda h, i: (i, _VA + h)),
            pl.BlockSpec((ctx, hd), lambda h, i: (i, _KA + h)),
            pl.BlockSpec((ctx, hd), lambda h, i: (i, _VA + h)),
            pl.BlockSpec((1,) + slabs.shape[1:], lambda h, i: (h, 0, 0, 0)),
        ] + w_specs_in,
        out_specs=[pl.BlockSpec((t, hd), lambda h, i: (i, h))] + w_specs_out,
        out_shape=[jax.ShapeDtypeStruct((b * t, NA_WIDTH), BF16)] + w_shapes,
        scratch_shapes=[
            pltpu.VMEM((t // NA_GROUP_TOK, hd, NA_GROUP_TOK), BF16),
            pltpu.VMEM((3, NA_WIN_BLOCKS * NA_GROUP_TOK, NA_GROUP_TOK), F32),
        ],
        compiler_params=_cparams(("arbitrary", "arbitrary")),
        name="na_latent",
    )(p_l, p_l, p_l, p_c, p_c, slabs, *weights)
    return oa, w_bf16


def _ctx_attn_kernel(q_ref, k_ref, v_ref, o_ref):
    for h in range(NA_HEADS):
        hs = slice(h * NA_HEAD_DIM, (h + 1) * NA_HEAD_DIM)
        s = lax.dot_general(q_ref[:, hs], k_ref[:, hs], (((1,), (1,)), ((), ())), preferred_element_type=F32)
        p = jnp.exp2(s - jnp.max(s, axis=-1, keepdims=True))
        l = jnp.sum(p, axis=-1, keepdims=True)
        o = jnp.dot(p.astype(BF16), v_ref[:, hs], preferred_element_type=F32)
        o_ref[:, hs] = (o / l).astype(o_ref.dtype)


def _ctx_attention(p_c, b):
    ctx = p_c.shape[0] // b
    return pl.pallas_call(
        _ctx_attn_kernel,
        grid=(b,),
        in_specs=[
            pl.BlockSpec((ctx, NA_WIDTH), lambda i: (i, 0)),
            pl.BlockSpec((ctx, NA_WIDTH), lambda i: (i, 1)),
            pl.BlockSpec((ctx, NA_WIDTH), lambda i: (i, 2)),
        ],
        out_specs=pl.BlockSpec((ctx, NA_WIDTH), lambda i: (i, 0)),
        out_shape=jax.ShapeDtypeStruct((b * ctx, NA_WIDTH), BF16),
        compiler_params=_cparams(("arbitrary",)),
        name="ctx_attention",
    )(p_c, p_c, p_c)


def _gm_block(u_ref, v_ref, lnw_ref, ws_ref, bs_ref, o_ref):
    for ck in range(u_ref.shape[0] // GM_CHUNK):
        rows = slice(ck * GM_CHUNK, (ck + 1) * GM_CHUNK)
        uf = _gelu_tanh(u_ref[rows, :])
        vf = _gelu_tanh(v_ref[rows, :])
        for g in range(GM_GROUPS):
            sl = slice(g * GM_DIM, (g + 1) * GM_DIM)
            vg = vf[:, sl]
            mu = jnp.mean(vg, axis=-1, keepdims=True)
            dv = vg - mu
            var = jnp.mean(dv * dv, axis=-1, keepdims=True)
            vn = dv * lax.rsqrt(var + EPS) * lnw_ref[:, sl]
            mixed = jnp.dot(ws_ref[g].astype(BF16), vn.astype(BF16), preferred_element_type=F32) + bs_ref[g]
            o_ref[rows, sl] = (uf[:, sl] * mixed).astype(o_ref.dtype)


HG_PAIR = 2
HG_LEVELS = (32, 16, 8)
HG_DIAG = 8
HG_PREP_ROWS = 256
HG_SCAN_UNROLL = 8


def _split3_dot(tri, g):
    g0 = g.astype(BF16)
    r1 = g - g0.astype(F32)
    g1 = r1.astype(BF16)
    g2 = (r1 - g1.astype(F32)).astype(BF16)
    dot = functools.partial(jnp.dot, preferred_element_type=F32)
    return dot(tri, g0) + dot(tri, g1) + dot(tri, g2)


def _hg_kernel(ql_ref, ffl_ref, fbl_ref, il_ref,
               qc_ref, ffc_ref, fbc_ref, ic_ref,
               lbp_ref, tri_ref, code_ref, lsum_ref, ol_ref, oc_ref,
               qh_s, kf_s, kb_s, ef_s, eb_s, of_s, ob_s, st_s):
    c = HG_CHUNK
    hd = HG_DIM
    n_ctx = qc_ref.shape[0]
    n_lat = ql_ref.shape[0]
    scale = hd ** -0.5
    nt = (((1,), (1,)), ((), ()))
    tn = (((0,), (0,)), ((), ()))

    def prep(q_ref, ff_ref, fb_ref, base, n):
        step = HG_PREP_ROWS
        for t0 in range(0, n, step):
            src = slice(t0, t0 + step)
            dst = slice(base + t0, base + t0 + step)
            qh = _silu(q_ref[src, :]) * scale
            for h in range(HG_PAIR):
                qh_s[h, dst, :] = qh[:, h * hd:(h + 1) * hd]
            for d, (f_ref, k_s, e_s) in enumerate(((ff_ref, kf_s, ef_s), (fb_ref, kb_s, eb_s))):
                x = f_ref[src, :]
                sp = jnp.maximum(-x, 0.0) + jnp.log(1.0 + jnp.exp(-jnp.abs(x)))
                log_lb = lbp_ref[d, 0:1, :]
                y = lbp_ref[d, 1:2, :] - sp
                mx = jnp.maximum(log_lb, y)
                log_f = mx + jnp.log(1.0 + jnp.exp(-jnp.abs(log_lb - y)))
                one_minus_f = lbp_ref[d, 2:3, :] * jnp.exp(-(sp + x))
                e = _split3_dot(tri_ref[d], log_f) * LOG2E
                for h in range(HG_PAIR):
                    e_s[h, dst, :] = e[:, h * hd:(h + 1) * hd]
                    k_s[h, dst, :] = one_minus_f[:, h * hd:(h + 1) * hd]

    prep(qc_ref, ffc_ref, fbc_ref, 0, n_ctx)
    prep(ql_ref, ffl_ref, fbl_ref, n_ctx, n_lat)

    diag_code = len(HG_LEVELS) + 1

    st_s[...] = jnp.zeros_like(st_s)

    def chunk(off, v, h, d, out):
        fwd = d == 0
        k_s, e_s = (kf_s, ef_s) if fwd else (kb_s, eb_s)
        q = qh_s[h, pl.ds(off, c), :]
        k = k_s[h, pl.ds(off, c), :]
        e = e_s[h, pl.ds(off, c), :]
        code = code_ref[d]
        prods = []
        for i in range(c // HG_DIAG):
            bs = slice(i * HG_DIAG, (i + 1) * HG_DIAG)
            row_prods = []
            for s in range(HG_DIAG):
                kr = k_s[h, pl.ds(off + i * HG_DIAG + s, 1), :]
                er = e_s[h, pl.ds(off + i * HG_DIAG + s, 1), :]
                decay = jnp.exp2(jnp.minimum(e[bs] - er, 0.0))
                row_prods.append((q[bs] * kr * decay).astype(BF16))
            prods.append(jnp.concatenate(row_prods, axis=1))
        diag = jnp.dot(jnp.concatenate(prods, axis=0), lsum_ref[...], preferred_element_type=F32)
        yield
        att = None
        for li, w in enumerate(HG_LEVELS):
            zeros = jnp.zeros((w, hd), F32)
            q_parts, k_parts = [], []
            for a in range(0, c, 2 * w):
                lo, hi = slice(a, a + w), slice(a + w, a + 2 * w)
                if fwd:
                    ref = e_s[h, pl.ds(off + a + w - 1, 1), :]
                    q_parts += [zeros, q[hi] * jnp.exp2(e[hi] - ref)]
                    k_parts += [k[lo] * jnp.exp2(ref - e[lo]), zeros]
                else:
                    ref = e_s[h, pl.ds(off + a + w, 1), :]
                    q_parts += [q[lo] * jnp.exp2(e[lo] - ref), zeros]
                    k_parts += [zeros, k[hi] * jnp.exp2(ref - e[hi])]
            qs = jnp.concatenate(q_parts, axis=0).astype(BF16)
            ks = jnp.concatenate(k_parts, axis=0).astype(BF16)
            a_w = lax.dot_general(qs, ks, nt, preferred_element_type=F32)
            att = a_w if att is None else jnp.where(code == li + 1, a_w, att)
            yield
        e_end = e_s[h, pl.ds(off + (c - 1 if fwd else 0), 1), :]
        st = st_s[h, d]
        qi = (q * jnp.exp2(e)).astype(BF16)
        ki = (k * jnp.exp2(e_end - e)).astype(BF16)
        o = lax.dot_general(qi, st.astype(BF16), nt, preferred_element_type=F32)
        yield
        st_new = st * jnp.exp2(e_end) + lax.dot_general(v, ki, tn, preferred_element_type=F32)
        yield
        att = jnp.where(code == diag_code, diag, att)
        out.append((o + jnp.dot(att.astype(BF16), v, preferred_element_type=F32), st_new))
        yield

    n_stages = len(HG_LEVELS) + 4

    def scan(v_ref, base, n):
        def body(i, carry):
            rf = pl.multiple_of(i * c, c)
            rb = pl.multiple_of((n - 1 - i) * c, c)
            vf = v_ref[pl.ds(rf, c), :].astype(BF16)
            vb = v_ref[pl.ds(rb, c), :].astype(BF16)
            chains = []
            for h in range(HG_PAIR):
                hs = slice(h * hd, (h + 1) * hd)
                for d, r, v in ((0, rf, vf), (1, rb, vb)):
                    out = []
                    chains.append((h, d, r, out, chunk(base + r, v[:, hs], h, d, out)))
            for _ in range(n_stages):
                for chain in chains:
                    next(chain[-1])
            for h, d, r, out, _ in chains:
                o, st_new = out[0]
                (of_s if d == 0 else ob_s)[h, pl.ds(base + r, c), :] = o
                st_s[h, d] = st_new
            return carry

        lax.fori_loop(0, n, body, 0, unroll=HG_SCAN_UNROLL)

    scan(ic_ref, 0, n_ctx // c)
    scan(il_ref, n_ctx, n_lat // c)

    def finish(o_ref, base, n):
        step = 256
        for t0 in range(0, n, step):
            src = slice(t0, t0 + step)
            dst = slice(base + t0, base + t0 + step)
            for h in range(HG_PAIR):
                o_ref[src, h * hd:(h + 1) * hd] = of_s[h, dst, :] + ob_s[h, dst, :]

    finish(oc_ref, 0, n_ctx)
    finish(ol_ref, n_ctx, n_lat)


def _hg_constants():
    c = HG_CHUNK
    row, col = np.arange(c)[:, None], np.arange(c)[None, :]
    blocks = np.eye(HG_PREP_ROWS // c)
    tri = np.stack([np.kron(blocks, col <= row), np.kron(blocks, col >= row)]).astype(np.float32)
    same8 = (row // HG_DIAG) == (col // HG_DIAG)
    diag_code = len(HG_LEVELS) + 1
    code_f = np.where(same8 & (col <= row), diag_code, 0)
    code_b = np.where(same8 & (col >= row), diag_code, 0)
    for li, w in enumerate(HG_LEVELS):
        same = (row // (2 * w)) == (col // (2 * w))
        t_hi, s_hi = (row % (2 * w)) >= w, (col % (2 * w)) >= w
        code_f = np.where(same & t_hi & ~s_hi, li + 1, code_f)
        code_b = np.where(same & ~t_hi & s_hi, li + 1, code_b)
    codes = np.stack([code_f, code_b]).astype(np.int32)
    lane_sum = (np.arange(HG_DIAG * HG_DIM)[:, None] // HG_DIM == col % HG_DIAG).astype(np.float32)
    return jnp.asarray(tri, BF16), jnp.asarray(codes), jnp.asarray(lane_sum, BF16)


def _hgrn2(p_l, p_c, lbp, b):
    tri, codes, lane_sum = _hg_constants()
    t = p_l.shape[0] // b
    ctx = p_c.shape[0] // b
    hd = HG_DIM
    pw = HG_PAIR * hd
    n = t + ctx
    first = _HG0 * LANE // pw

    def col(stream):
        return lambda i, j: (i, first + stream * (HG_HEADS // HG_PAIR) + j)

    lat_specs = [pl.BlockSpec((t, pw), col(s)) for s in range(4)]
    ctx_specs = [pl.BlockSpec((ctx, pw), col(s)) for s in range(4)]
    big = lambda: pltpu.VMEM((HG_PAIR, n, hd), F32)
    return pl.pallas_call(
        _hg_kernel,
        grid=(b, HG_HEADS // HG_PAIR),
        in_specs=lat_specs + ctx_specs + [
            pl.BlockSpec((2, 3, pw), lambda i, j: (0, 0, j)),
            pl.BlockSpec(tri.shape, lambda i, j: (0, 0, 0)),
            pl.BlockSpec(codes.shape, lambda i, j: (0, 0, 0)),
            pl.BlockSpec(lane_sum.shape, lambda i, j: (0, 0)),
        ],
        out_specs=[
            pl.BlockSpec((t, pw), lambda i, j: (i, j)),
            pl.BlockSpec((ctx, pw), lambda i, j: (i, j)),
        ],
        out_shape=[
            jax.ShapeDtypeStruct((b * t, HG_WIDTH), F32),
            jax.ShapeDtypeStruct((b * ctx, HG_WIDTH), F32),
        ],
        scratch_shapes=[big() for _ in range(7)] + [pltpu.VMEM((HG_PAIR, 2, hd, hd), F32)],
        compiler_params=_cparams(("arbitrary", "arbitrary")),
        name="hgrn2",
    )(*([p_l] * 4 + [p_c] * 4 + [lbp, tri, codes, lane_sum]))


def _outproj_kernel(oa_ref, os_ref, hg_ref, hnw_ref, u_ref, v_ref, lnw_ref, ws_ref, bs_ref,
                    wa_ref, wb_ref, wc_ref, x_ref, g_ref, o_ref, ob_ref, oc_ref):
    dot = functools.partial(jnp.dot, preferred_element_type=F32)
    y = dot(oa_ref[...], wa_ref[...])
    for h in range(HG_HEADS):
        hs = slice(h * HG_DIM, (h + 1) * HG_DIM)
        o = os_ref[:, hs]
        o = o * lax.rsqrt(jnp.mean(o * o, axis=-1, keepdims=True) + EPS) * hnw_ref[...]
        ob_ref[:, hs] = (o * _silu(hg_ref[:, hs])).astype(ob_ref.dtype)
    y = y + dot(ob_ref[...], wb_ref[...])
    _gm_block(u_ref, v_ref, lnw_ref, ws_ref, bs_ref, oc_ref)
    y = y + dot(oc_ref[...], wc_ref[...])
    o_ref[...] = x_ref[...] + g_ref[0] * y


def _outproj(oa, osum, p2, hg_norm_w, gm_ln_w, gm_ws, gm_bs, w, w_layer, x2d, mod, layer, rows_per_mod,
             mod_row0, tm):
    m, d = x2d.shape
    mrow = lambda i: layer * MOD_ROWS + mod_row0 + (i * tm) // rows_per_mod
    nb = NA_WIDTH // HG_WIDTH
    assert tm % GM_CHUNK == 0
    return pl.pallas_call(
        _outproj_kernel,
        grid=(m // tm,),
        in_specs=[
            pl.BlockSpec((tm, NA_WIDTH), lambda i: (i, 0)),
            pl.BlockSpec((tm, HG_WIDTH), lambda i: (i, 0)),
            pl.BlockSpec((tm, HG_WIDTH), lambda i: (i, _HG_G)),
            pl.BlockSpec((1, HG_DIM), lambda i: (0, 0)),
            pl.BlockSpec((tm, GM_WIDTH), lambda i: (i, _GM_U)),
            pl.BlockSpec((tm, GM_WIDTH), lambda i: (i, _GM_V)),
            pl.BlockSpec((1, GM_WIDTH), lambda i: (0, 0)),
            pl.BlockSpec((GM_GROUPS, GM_CHUNK, GM_CHUNK), lambda i: (0, 0, 0)),
            pl.BlockSpec((GM_GROUPS, GM_CHUNK, 1), lambda i: (0, 0, 0)),
            pl.BlockSpec((None, NA_WIDTH, d), lambda i: (w_layer, 0, 0)),
            pl.BlockSpec((None, HG_WIDTH, d), lambda i: (w_layer, nb, 0)),
            pl.BlockSpec((None, GM_WIDTH, d), lambda i: (w_layer, nb + 1, 0)),
            pl.BlockSpec((tm, d), lambda i: (i, 0)),
            pl.BlockSpec((1, 1, d), lambda i: (mrow(i), 0, 2)),
        ],
        out_specs=pl.BlockSpec((tm, d), lambda i: (i, 0)),
        out_shape=jax.ShapeDtypeStruct((m, d), F32),
        scratch_shapes=[pltpu.VMEM((tm, HG_WIDTH), BF16), pltpu.VMEM((tm, GM_WIDTH), BF16)],
        compiler_params=_cparams(("arbitrary",)),
        name="outproj",
    )(oa, osum, p2, hg_norm_w.reshape(1, HG_DIM), p2, p2, gm_ln_w.reshape(1, GM_WIDTH), gm_ws,
      gm_bs.reshape(GM_GROUPS, GM_CHUNK, 1), w, w, w, x2d, mod)


def _mlp_kernel(x_ref, nw_ref, sh_ref, sc_ref, g_ref, w1_ref, w2_ref, fnw_ref, o_ref, h_ref, acc_ref,
                *, final_norm):
    j = pl.program_id(1)

    @pl.when(j == 0)
    def _():
        _norm_modulate_store(x_ref, nw_ref, sh_ref, sc_ref, h_ref)
        acc_ref[...] = jnp.zeros_like(acc_ref)

    a = jnp.maximum(jnp.dot(h_ref[...], w1_ref[...], preferred_element_type=F32), 0.0)
    acc_ref[...] += jnp.dot((a * a).astype(BF16), w2_ref[...], preferred_element_type=F32)

    @pl.when(j == pl.num_programs(1) - 1)
    def _():
        y = x_ref[...] + g_ref[0] * acc_ref[...]
        if final_norm:
            y = y * lax.rsqrt(jnp.mean(y * y, axis=-1, keepdims=True) + EPS) * fnw_ref[...]
        o_ref[...] = y


def _mlp(x2d, nw, mod, w1, w2, w_layer, fnw, layer, rows_per_mod, mod_row0, tm, final_norm):
    m, d = x2d.shape
    hid = w1.shape[2]
    th = MLP_TH
    mrow = lambda i: layer * MOD_ROWS + mod_row0 + (i * tm) // rows_per_mod
    return pl.pallas_call(
        functools.partial(_mlp_kernel, final_norm=final_norm),
        grid=(m // tm, hid // th),
        in_specs=[
            pl.BlockSpec((tm, d), lambda i, j: (i, 0)),
            pl.BlockSpec((None, 1, d), lambda i, j: (layer, 0, 0)),
            pl.BlockSpec((1, 1, d), lambda i, j: (mrow(i), 0, 3)),
            pl.BlockSpec((1, 1, d), lambda i, j: (mrow(i), 0, 4)),
            pl.BlockSpec((1, 1, d), lambda i, j: (mrow(i), 0, 5)),
            pl.BlockSpec((None, d, th), lambda i, j: (w_layer, 0, j)),
            pl.BlockSpec((None, th, d), lambda i, j: (w_layer, j, 0)),
            pl.BlockSpec((1, d), lambda i, j: (0, 0)),
        ],
        out_specs=pl.BlockSpec((tm, d), lambda i, j: (i, 0)),
        out_shape=jax.ShapeDtypeStruct((m, d), F32),
        scratch_shapes=[pltpu.VMEM((tm, d), BF16), pltpu.VMEM((tm, d), F32)],
        compiler_params=_cparams(("arbitrary", "arbitrary")),
        name="mlp",
    )(x2d, nw, mod, mod, mod, w1, w2, fnw)


def kernel(x, c, ctx, c_ctx, ada_w, ada_b, norm1_w, norm2_w, w_in, na_rpb, hg_lb_logits, hg_norm_w,
           gm_ln_w, gm_ws, gm_bs, w_out, mlp_w1, mlp_w2, final_norm_w):
    bsz, seq, d = x.shape
    n_ctx = ctx.shape[1]
    depth = ada_w.shape[0]
    assert bsz < MOD_ROWS and d == D_MODEL and seq % 512 == 0 and n_ctx % 256 == 0
    assert seq // NA_GROUP_TOK >= NA_WIN_BLOCKS and seq // GRID_W >= 2 * NA_WIN_ROWS

    lb = jnp.cumsum(jax.nn.softmax(hg_lb_logits.astype(F32), axis=0), axis=0)
    lb = lb - lb[:1]
    lbp = jnp.stack([jnp.log(lb), jnp.log1p(-lb), 1.0 - lb], axis=2)

    cond = jnp.zeros((MOD_ROWS, d), F32).at[:bsz].set(c).at[bsz].set(c_ctx)
    mod = _ada(cond, ada_w, ada_b).reshape(depth * MOD_ROWS, 1, 6 * d)

    w_in_b = w_in.astype(BF16)
    nw1 = norm1_w.reshape(depth, 1, d)
    nw2 = norm2_w.reshape(depth, 1, d)
    fnw = final_norm_w.reshape(1, d)

    xl = x.reshape(bsz * seq, d)
    xc = ctx.reshape(bsz * n_ctx, d)
    n_c = bsz * n_ctx
    tm_l, tm_c = TM_LATENT, min(TM_CONTEXT, n_c)
    for l in range(depth):
        need_ctx = l < depth - 1
        qkv_l, p_l = _inproj(xl, nw1, mod, w_in_b, l, seq, 0, tm_l)
        qkv_c, p_c = _inproj(xc, nw1, mod, w_in_b, l, n_c, bsz, tm_c)

        oa_l, (w_out_b, w1_b, w2_b) = _na_latent(qkv_l, qkv_c, _na_bias_slabs(na_rpb[l]),
                                                 (w_out, mlp_w1, mlp_w2), l, bsz)
        os_l, os_c = _hgrn2(p_l, p_c, lbp[l], bsz)
        mix = (hg_norm_w[l], gm_ln_w[l], gm_ws[l], gm_bs[l])
        xl = _outproj(oa_l, os_l, p_l, *mix, w_out_b, 0, xl, mod, l, seq, 0, tm_l)
        xl = _mlp(xl, nw2, mod, w1_b, w2_b, 0, fnw, l, seq, 0, tm_l, final_norm=not need_ctx)
        if need_ctx:
            oa_c = _ctx_attention(qkv_c, bsz)
            xc = _outproj(oa_c, os_c, p_c, *mix, w_out_b, 0, xc, mod, l, n_c, bsz, tm_c)
            xc = _mlp(xc, nw2, mod, w1_b, w2_b, 0, fnw, l, n_c, bsz, tm_c, final_norm=False)
    return xl.reshape(bsz, seq, d)
```

```python
import functools

import numpy as np
import jax
import jax.numpy as jnp
from jax import lax
from jax.experimental import pallas as pl
from jax.experimental.pallas import tpu as pltpu

F32 = jnp.float32
BF16 = jnp.bfloat16

D_MODEL = 2048
DEPTH = 2
GRID_W = 64
EPS = 1e-6

NA_HEAD_DIM = 128
NA_HEADS = 8
NA_WIDTH = NA_HEADS * NA_HEAD_DIM
NA_WIN_ROWS = 8
NA_WIN_COLS = 16

HG_HEADS = 4
HG_DIM = 128
HG_WIDTH = HG_HEADS * HG_DIM
HG_CHUNK = 64

GM_GROUPS = 4
GM_DIM = 128
GM_WIDTH = GM_GROUPS * GM_DIM
GM_CHUNK = 128

IN_WIDTH = 3 * NA_WIDTH + 5 * HG_WIDTH + 2 * GM_WIDTH
MLP_HIDDEN = 4 * D_MODEL
LANE = 128
LOG2E = 1.4426950408889634
BF16_SUBLANES = 16

QKV_WIDTH = 3 * NA_WIDTH
REST_WIDTH = IN_WIDTH - QKV_WIDTH
REST_MAIN = QKV_WIDTH
REST_TAIL = REST_WIDTH - REST_MAIN
NA_SCORE_SCALE = NA_HEAD_DIM ** -0.5 * LOG2E
_QA, _KA, _VA = 0, NA_HEADS, 2 * NA_HEADS
_HG0 = 0
_GM_U = 5 * HG_WIDTH // GM_WIDTH
_GM_V = _GM_U + 1
_HG_G = 4

VMEM_LIMIT_V7X = 56 * 1024 * 1024
TM_LATENT = 512
TM_CONTEXT = 512
MLP_TH = 1024
NORM_CHUNK_ROWS = 32
NORM_UNROLL = 4
MOD_ROWS = 16


def _cparams(sem):
    return pltpu.CompilerParams(dimension_semantics=sem, vmem_limit_bytes=VMEM_LIMIT_V7X)


def _silu(x):
    h = 0.5 * x
    return h + h * jnp.tanh(h)


def _gelu_tanh(x):
    return 0.5 * x * (1.0 + jnp.tanh(0.7978845608028654 * (x + 0.044715 * (x * x * x))))


def _norm_modulate_store(x_ref, nw_ref, sh_ref, sc_ref, h_ref):
    gain = nw_ref[...] * (1.0 + sc_ref[0])
    shift = sh_ref[0]

    def body(i, carry):
        rows = pl.ds(pl.multiple_of(i * NORM_CHUNK_ROWS, NORM_CHUNK_ROWS), NORM_CHUNK_ROWS)
        x = x_ref[rows, :]
        rs = lax.rsqrt(jnp.mean(x * x, axis=-1, keepdims=True) + EPS)
        h_ref[rows, :] = (x * rs * gain + shift).astype(h_ref.dtype)
        return carry

    lax.fori_loop(0, x_ref.shape[0] // NORM_CHUNK_ROWS, body, 0, unroll=NORM_UNROLL)


def _ada_kernel(c_ref, w_ref, b_ref, o_ref):
    s = _silu(c_ref[...]).astype(BF16)
    o_ref[0] = jnp.dot(s, w_ref[0].astype(BF16), preferred_element_type=F32) + b_ref[0]


def _ada(cond, ada_w, ada_b):
    depth, d, n = ada_w.shape
    tn = 1536
    return pl.pallas_call(
        _ada_kernel,
        grid=(depth, n // tn),
        in_specs=[
            pl.BlockSpec((cond.shape[0], d), lambda l, j: (0, 0)),
            pl.BlockSpec((1, d, tn), lambda l, j: (l, 0, j)),
            pl.BlockSpec((1, 1, tn), lambda l, j: (l, 0, j)),
        ],
        out_specs=pl.BlockSpec((1, cond.shape[0], tn), lambda l, j: (l, 0, j)),
        out_shape=jax.ShapeDtypeStruct((depth, cond.shape[0], n), F32),
        compiler_params=_cparams(("arbitrary", "arbitrary")),
        name="ada",
    )(cond, ada_w, ada_b.reshape(depth, 1, n))


def _inproj_qkv_kernel(x_ref, nw_ref, sh_ref, sc_ref, w_ref, o_ref, h_ref):
    _norm_modulate_store(x_ref, nw_ref, sh_ref, sc_ref, h_ref)
    h = h_ref[...]
    q = jnp.dot(h, w_ref[:, :NA_WIDTH], preferred_element_type=F32)
    o_ref[:, :NA_WIDTH] = (q * NA_SCORE_SCALE).astype(o_ref.dtype)
    o_ref[:, NA_WIDTH:] = jnp.dot(h, w_ref[:, NA_WIDTH:], preferred_element_type=F32).astype(o_ref.dtype)


def _inproj_rest_kernel(h_ref, wa_ref, wb_ref, o_ref):
    h = h_ref[...]
    o_ref[:, :REST_MAIN] = jnp.dot(h, wa_ref[...], preferred_element_type=F32)
    o_ref[:, REST_MAIN:] = jnp.dot(h, wb_ref[...], preferred_element_type=F32)


def _inproj(x2d, nw, mod, w, layer, rows_per_mod, mod_row0, tm):
    m, d = x2d.shape
    mrow = lambda i: layer * MOD_ROWS + mod_row0 + (i * tm) // rows_per_mod
    assert w.shape[0] == 1 and w.shape[2] == QKV_WIDTH + REST_MAIN + REST_TAIL
    assert (QKV_WIDTH + REST_MAIN) % REST_TAIL == 0
    qkv, h = pl.pallas_call(
        _inproj_qkv_kernel,
        grid=(m // tm,),
        in_specs=[
            pl.BlockSpec((tm, d), lambda i: (i, 0)),
            pl.BlockSpec((None, 1, d), lambda i: (layer, 0, 0)),
            pl.BlockSpec((1, 1, d), lambda i: (mrow(i), 0, 0)),
            pl.BlockSpec((1, 1, d), lambda i: (mrow(i), 0, 1)),
            pl.BlockSpec((None, d, QKV_WIDTH), lambda i: (0, 0, 0)),
        ],
        out_specs=[pl.BlockSpec((tm, QKV_WIDTH), lambda i: (i, 0)), pl.BlockSpec((tm, d), lambda i: (i, 0))],
        out_shape=[jax.ShapeDtypeStruct((m, QKV_WIDTH), BF16), jax.ShapeDtypeStruct((m, d), BF16)],
        compiler_params=_cparams(("arbitrary",)),
        name="inproj_qkv",
    )(x2d, nw, mod, mod, w)
    rest = pl.pallas_call(
        _inproj_rest_kernel,
        grid=(m // tm,),
        in_specs=[
            pl.BlockSpec((tm, d), lambda i: (i, 0)),
            pl.BlockSpec((None, d, REST_MAIN), lambda i: (0, 0, 1)),
            pl.BlockSpec((None, d, REST_TAIL), lambda i: (0, 0, (QKV_WIDTH + REST_MAIN) // REST_TAIL)),
        ],
        out_specs=pl.BlockSpec((tm, REST_WIDTH), lambda i: (i, 0)),
        out_shape=jax.ShapeDtypeStruct((m, REST_WIDTH), F32),
        compiler_params=_cparams(("arbitrary",)),
        name="inproj_rest",
    )(h, w, w)
    return qkv, rest


NA_GROUP_ROWS = 4
NA_GROUP_TOK = NA_GROUP_ROWS * GRID_W
NA_WIN_BLOCKS = 3
NA_GROUPS_PER_STEP = 8
NA_LOOP_UNROLL = 1


NA_MASKED_SLAB = 2 * NA_WIN_ROWS - 1


def _na_bias_slabs(rpb):
    h = rpb.shape[0]
    kcol = np.arange(GRID_W)[:, None]
    qcol = np.arange(GRID_W)[None, :]
    wstart = np.clip(qcol - NA_WIN_COLS // 2, 0, GRID_W - NA_WIN_COLS)
    valid_col = (kcol >= wstart) & (kcol < wstart + NA_WIN_COLS)
    pad = GRID_W - NA_WIN_COLS
    padded = jnp.pad(rpb.astype(F32), ((0, 0), (0, 0), (pad, pad)))
    m = jnp.tile(padded, (1, 1, GRID_W + 1))[:, :, :GRID_W * 2 * GRID_W]
    m = m.reshape(h, rpb.shape[1], GRID_W, 2 * GRID_W)[..., :GRID_W]
    slabs = jnp.where(valid_col, m[..., ::-1] * LOG2E, -1e30)
    slabs = jnp.concatenate([slabs, jnp.full((h, 1, GRID_W, GRID_W), -1e30, F32)], axis=1)
    return jnp.concatenate([slabs, slabs], axis=-1)


def _na_slab_index(rows):
    n_groups = rows // NA_GROUP_ROWS
    krel = np.arange(NA_WIN_BLOCKS * NA_GROUP_ROWS)[:, None]
    qrel = np.arange(NA_GROUP_ROWS)[None, :]
    idx = []
    for g in (0, 1, n_groups - 1):
        krow = int(np.clip(g - 1, 0, n_groups - NA_WIN_BLOCKS)) * NA_GROUP_ROWS + krel
        qrow = g * NA_GROUP_ROWS + qrel
        r0 = np.clip(qrow - NA_WIN_ROWS // 2, 0, rows - NA_WIN_ROWS)
        in_window = (krow >= r0) & (krow < r0 + NA_WIN_ROWS)
        idx.append(np.where(in_window, krow - qrow + NA_WIN_ROWS - 1, NA_MASKED_SLAB))
    return np.stack(idx).tolist()


def _na_kernel(q_ref, k_ref, v_ref, kc_ref, vc_ref, slab_ref, *refs, n_weights, slab_index):
    w_refs, o_ref, w_bf16_refs = refs[:n_weights], refs[n_weights], refs[n_weights + 1:2 * n_weights + 1]
    vt_ref, bias_ref = refs[2 * n_weights + 1:]
    n_groups = q_ref.shape[0] // NA_GROUP_TOK
    gt = NA_GROUP_TOK
    nt = (((1,), (1,)), ((), ()))
    dot = functools.partial(jnp.dot, preferred_element_type=F32)

    for src_ref, dst_ref in zip(w_refs, w_bf16_refs):
        dst_ref[...] = src_ref[...].astype(BF16)

    @pl.when(pl.program_id(1) == 0)
    def _():
        left = lax.broadcasted_iota(jnp.int32, (GRID_W, 2 * GRID_W), 1) < GRID_W
        for kind, per_key_row in enumerate(slab_index):
            for j, per_query_row in enumerate(per_key_row):
                for a in range(0, NA_GROUP_ROWS, 2):
                    tile = jnp.where(left, slab_ref[0, per_query_row[a]], slab_ref[0, per_query_row[a + 1]])
                    bias_ref[kind, j * GRID_W:(j + 1) * GRID_W, a * GRID_W:(a + 2) * GRID_W] = tile

    for b in range(n_groups):
        vt_ref[b] = v_ref[b * gt:(b + 1) * gt, :].T
    kc = kc_ref[...]
    vct = vc_ref[...].T

    def group(g):
        blk = jnp.clip(g - 1, 0, n_groups - NA_WIN_BLOCKS)
        kind = jnp.where(g == 0, 0, jnp.where(g == n_groups - 1, 2, 1))
        qg = q_ref[pl.ds(pl.multiple_of(g * gt, gt), gt), :]
        kw = k_ref[pl.ds(pl.multiple_of(blk * gt, gt), NA_WIN_BLOCKS * gt), :]
        s_lat = lax.dot_general(kw, qg, nt, preferred_element_type=F32) + bias_ref[kind]
        s_ctx = lax.dot_general(kc, qg, nt, preferred_element_type=F32)
        yield
        m = jnp.maximum(jnp.max(s_lat, axis=0, keepdims=True), jnp.max(s_ctx, axis=0, keepdims=True))
        p_lat = jnp.exp2(s_lat - m)
        p_ctx = jnp.exp2(s_ctx - m)
        l = jnp.sum(p_lat, axis=0, keepdims=True) + jnp.sum(p_ctx, axis=0, keepdims=True)
        yield
        ot = dot(vct, p_ctx.astype(BF16))
        for j in range(NA_WIN_BLOCKS):
            ot = ot + dot(vt_ref[blk + j], p_lat[j * gt:(j + 1) * gt, :].astype(BF16))
        yield
        o_ref[pl.ds(pl.multiple_of(g * gt, gt), gt), :] = (ot / l).T.astype(o_ref.dtype)
        yield

    def body(i, carry):
        groups = [group(i * NA_GROUPS_PER_STEP + u) for u in range(NA_GROUPS_PER_STEP)]
        for _ in range(4):
            for grp in groups:
                next(grp)
        return carry

    lax.fori_loop(0, n_groups // NA_GROUPS_PER_STEP, body, 0, unroll=NA_LOOP_UNROLL)


def _na_latent(p_l, p_c, slabs, weights, b):
    t = p_l.shape[0] // b
    ctx = p_c.shape[0] // b
    hd = NA_HEAD_DIM
    n_steps = NA_HEADS * b
    w_specs_in, w_specs_out, w_shapes = [], [], []
    for w, layer in weights:
        rows, cols = w.shape[1] // n_steps, w.shape[2]
        assert w.shape[1] % n_steps == 0 and rows % BF16_SUBLANES == 0
        w_specs_in.append(pl.BlockSpec((None, rows, cols), lambda h, i, layer=layer: (layer, h * b + i, 0)))
        w_specs_out.append(pl.BlockSpec((None, rows, cols), lambda h, i: (0, h * b + i, 0)))
        w_shapes.append(jax.ShapeDtypeStruct((1,) + w.shape[1:], BF16))
    oa, *w_bf16 = pl.pallas_call(
        functools.partial(_na_kernel, n_weights=len(weights), slab_index=_na_slab_index(t // GRID_W)),
        grid=(NA_HEADS, b),
        in_specs=[
            pl.BlockSpec((t, hd), lambda h, i: (i, _QA + h)),
            pl.BlockSpec((t, hd), lambda h, i: (i, _KA + h)),
            pl.BlockSpec((t, hd), lambda h, i: (i, _VA + h)),
            pl.BlockSpec((ctx, hd), lambda h, i: (i, _KA + h)),
            pl.BlockSpec((ctx, hd), lambda h, i: (i, _VA + h)),
            pl.BlockSpec((1,) + slabs.shape[1:], lambda h, i: (h, 0, 0, 0)),
        ] + w_specs_in,
        out_specs=[pl.BlockSpec((t, hd), lambda h, i: (i, h))] + w_specs_out,
        out_shape=[jax.ShapeDtypeStruct((b * t, NA_WIDTH), BF16)] + w_shapes,
        scratch_shapes=[
            pltpu.VMEM((t // NA_GROUP_TOK, hd, NA_GROUP_TOK), BF16),
            pltpu.VMEM((3, NA_WIN_BLOCKS * NA_GROUP_TOK, NA_GROUP_TOK), F32),
        ],
        compiler_params=_cparams(("arbitrary", "arbitrary")),
        name="na_latent",
    )(p_l, p_l, p_l, p_c, p_c, slabs, *(w for w, _ in weights))
    return oa, w_bf16


def _ctx_attn_kernel(q_ref, k_ref, v_ref, o_ref):
    for h in range(NA_HEADS):
        hs = slice(h * NA_HEAD_DIM, (h + 1) * NA_HEAD_DIM)
        s = lax.dot_general(q_ref[:, hs], k_ref[:, hs], (((1,), (1,)), ((), ())), preferred_element_type=F32)
        p = jnp.exp2(s - jnp.max(s, axis=-1, keepdims=True))
        l = jnp.sum(p, axis=-1, keepdims=True)
        o = jnp.dot(p.astype(BF16), v_ref[:, hs], preferred_element_type=F32)
        o_ref[:, hs] = (o / l).astype(o_ref.dtype)


def _ctx_attention(p_c, b):
    ctx = p_c.shape[0] // b
    return pl.pallas_call(
        _ctx_attn_kernel,
        grid=(b,),
        in_specs=[
            pl.BlockSpec((ctx, NA_WIDTH), lambda i: (i, 0)),
            pl.BlockSpec((ctx, NA_WIDTH), lambda i: (i, 1)),
            pl.BlockSpec((ctx, NA_WIDTH), lambda i: (i, 2)),
        ],
        out_specs=pl.BlockSpec((ctx, NA_WIDTH), lambda i: (i, 0)),
        out_shape=jax.ShapeDtypeStruct((b * ctx, NA_WIDTH), BF16),
        compiler_params=_cparams(("arbitrary",)),
        name="ctx_attention",
    )(p_c, p_c, p_c)


def _gm_block(u_ref, v_ref, lnw_ref, ws_ref, bs_ref, o_ref):
    for ck in range(u_ref.shape[0] // GM_CHUNK):
        rows = slice(ck * GM_CHUNK, (ck + 1) * GM_CHUNK)
        uf = _gelu_tanh(u_ref[rows, :])
        vf = _gelu_tanh(v_ref[rows, :])
        for g in range(GM_GROUPS):
            sl = slice(g * GM_DIM, (g + 1) * GM_DIM)
            vg = vf[:, sl]
            mu = jnp.mean(vg, axis=-1, keepdims=True)
            dv = vg - mu
            var = jnp.mean(dv * dv, axis=-1, keepdims=True)
            vn = dv * lax.rsqrt(var + EPS) * lnw_ref[:, sl]
            mixed = jnp.dot(ws_ref[g].astype(BF16), vn.astype(BF16), preferred_element_type=F32) + bs_ref[g]
            o_ref[rows, sl] = (uf[:, sl] * mixed).astype(o_ref.dtype)


HG_PAIR = 2
HG_LEVELS = (32, 16, 8)
HG_DIAG = 8
HG_PREP_ROWS = 256
HG_SCAN_UNROLL = 8


def _split3_dot(tri, g):
    g0 = g.astype(BF16)
    r1 = g - g0.astype(F32)
    g1 = r1.astype(BF16)
    g2 = (r1 - g1.astype(F32)).astype(BF16)
    dot = functools.partial(jnp.dot, preferred_element_type=F32)
    return dot(tri, g0) + dot(tri, g1) + dot(tri, g2)


def _hg_kernel(ql_ref, ffl_ref, fbl_ref, il_ref,
               qc_ref, ffc_ref, fbc_ref, ic_ref,
               lbp_ref, tri_ref, code_ref, lsum_ref, ol_ref, oc_ref,
               qh_s, kf_s, kb_s, ef_s, eb_s, of_s, ob_s, st_s):
    c = HG_CHUNK
    hd = HG_DIM
    n_ctx = qc_ref.shape[0]
    n_lat = ql_ref.shape[0]
    scale = hd ** -0.5
    nt = (((1,), (1,)), ((), ()))
    tn = (((0,), (0,)), ((), ()))

    def prep(q_ref, ff_ref, fb_ref, base, n):
        step = HG_PREP_ROWS
        for t0 in range(0, n, step):
            src = slice(t0, t0 + step)
            dst = slice(base + t0, base + t0 + step)
            qh = _silu(q_ref[src, :]) * scale
            for h in range(HG_PAIR):
                qh_s[h, dst, :] = qh[:, h * hd:(h + 1) * hd]
            for d, (f_ref, k_s, e_s) in enumerate(((ff_ref, kf_s, ef_s), (fb_ref, kb_s, eb_s))):
                x = f_ref[src, :]
                sp = jnp.maximum(-x, 0.0) + jnp.log(1.0 + jnp.exp(-jnp.abs(x)))
                log_lb = lbp_ref[d, 0:1, :]
                y = lbp_ref[d, 1:2, :] - sp
                mx = jnp.maximum(log_lb, y)
                log_f = mx + jnp.log(1.0 + jnp.exp(-jnp.abs(log_lb - y)))
                one_minus_f = lbp_ref[d, 2:3, :] * jnp.exp(-(sp + x))
                e = _split3_dot(tri_ref[d], log_f) * LOG2E
                for h in range(HG_PAIR):
                    e_s[h, dst, :] = e[:, h * hd:(h + 1) * hd]
                    k_s[h, dst, :] = one_minus_f[:, h * hd:(h + 1) * hd]

    prep(qc_ref, ffc_ref, fbc_ref, 0, n_ctx)
    prep(ql_ref, ffl_ref, fbl_ref, n_ctx, n_lat)

    diag_code = len(HG_LEVELS) + 1

    st_s[...] = jnp.zeros_like(st_s)

    def chunk(off, v, h, d, out):
        fwd = d == 0
        k_s, e_s = (kf_s, ef_s) if fwd else (kb_s, eb_s)
        q = qh_s[h, pl.ds(off, c), :]
        k = k_s[h, pl.ds(off, c), :]
        e = e_s[h, pl.ds(off, c), :]
        code = code_ref[d]
        prods = []
        for i in range(c // HG_DIAG):
            bs = slice(i * HG_DIAG, (i + 1) * HG_DIAG)
            row_prods = []
            for s in range(HG_DIAG):
                kr = k_s[h, pl.ds(off + i * HG_DIAG + s, 1), :]
                er = e_s[h, pl.ds(off + i * HG_DIAG + s, 1), :]
                decay = jnp.exp2(jnp.minimum(e[bs] - er, 0.0))
                row_prods.append((q[bs] * kr * decay).astype(BF16))
            prods.append(jnp.concatenate(row_prods, axis=1))
        diag = jnp.dot(jnp.concatenate(prods, axis=0), lsum_ref[...], preferred_element_type=F32)
        yield
        att = None
        for li, w in enumerate(HG_LEVELS):
            zeros = jnp.zeros((w, hd), F32)
            q_parts, k_parts = [], []
            for a in range(0, c, 2 * w):
                lo, hi = slice(a, a + w), slice(a + w, a + 2 * w)
                if fwd:
                    ref = e_s[h, pl.ds(off + a + w - 1, 1), :]
                    q_parts += [zeros, q[hi] * jnp.exp2(e[hi] - ref)]
                    k_parts += [k[lo] * jnp.exp2(ref - e[lo]), zeros]
                else:
                    ref = e_s[h, pl.ds(off + a + w, 1), :]
                    q_parts += [q[lo] * jnp.exp2(e[lo] - ref), zeros]
                    k_parts += [zeros, k[hi] * jnp.exp2(ref - e[hi])]
            qs = jnp.concatenate(q_parts, axis=0).astype(BF16)
            ks = jnp.concatenate(k_parts, axis=0).astype(BF16)
            a_w = lax.dot_general(qs, ks, nt, preferred_element_type=F32)
            att = a_w if att is None else jnp.where(code == li + 1, a_w, att)
            yield
        e_end = e_s[h, pl.ds(off + (c - 1 if fwd else 0), 1), :]
        st = st_s[h, d]
        qi = (q * jnp.exp2(e)).astype(BF16)
        ki = (k * jnp.exp2(e_end - e)).astype(BF16)
        o = lax.dot_general(qi, st.astype(BF16), nt, preferred_element_type=F32)
        yield
        st_new = st * jnp.exp2(e_end) + lax.dot_general(v, ki, tn, preferred_element_type=F32)
        yield
        att = jnp.where(code == diag_code, diag, att)
        out.append((o + jnp.dot(att.astype(BF16), v, preferred_element_type=F32), st_new))
        yield

    n_stages = len(HG_LEVELS) + 4

    def scan(v_ref, base, n):
        def body(i, carry):
            rf = pl.multiple_of(i * c, c)
            rb = pl.multiple_of((n - 1 - i) * c, c)
            vf = v_ref[pl.ds(rf, c), :].astype(BF16)
            vb = v_ref[pl.ds(rb, c), :].astype(BF16)
            chains = []
            for h in range(HG_PAIR):
                hs = slice(h * hd, (h + 1) * hd)
                for d, r, v in ((0, rf, vf), (1, rb, vb)):
                    out = []
                    chains.append((h, d, r, out, chunk(base + r, v[:, hs], h, d, out)))
            for _ in range(n_stages):
                for chain in chains:
                    next(chain[-1])
            for h, d, r, out, _ in chains:
                o, st_new = out[0]
                (of_s if d == 0 else ob_s)[h, pl.ds(base + r, c), :] = o
                st_s[h, d] = st_new
            return carry

        lax.fori_loop(0, n, body, 0, unroll=HG_SCAN_UNROLL)

    scan(ic_ref, 0, n_ctx // c)
    scan(il_ref, n_ctx, n_lat // c)

    def finish(o_ref, base, n):
        step = 256
        for t0 in range(0, n, step):
            src = slice(t0, t0 + step)
            dst = slice(base + t0, base + t0 + step)
            for h in range(HG_PAIR):
                o_ref[src, h * hd:(h + 1) * hd] = of_s[h, dst, :] + ob_s[h, dst, :]

    finish(oc_ref, 0, n_ctx)
    finish(ol_ref, n_ctx, n_lat)


def _hg_constants():
    c = HG_CHUNK
    row, col = np.arange(c)[:, None], np.arange(c)[None, :]
    blocks = np.eye(HG_PREP_ROWS // c)
    tri = np.stack([np.kron(blocks, col <= row), np.kron(blocks, col >= row)]).astype(np.float32)
    same8 = (row // HG_DIAG) == (col // HG_DIAG)
    diag_code = len(HG_LEVELS) + 1
    code_f = np.where(same8 & (col <= row), diag_code, 0)
    code_b = np.where(same8 & (col >= row), diag_code, 0)
    for li, w in enumerate(HG_LEVELS):
        same = (row // (2 * w)) == (col // (2 * w))
        t_hi, s_hi = (row % (2 * w)) >= w, (col % (2 * w)) >= w
        code_f = np.where(same & t_hi & ~s_hi, li + 1, code_f)
        code_b = np.where(same & ~t_hi & s_hi, li + 1, code_b)
    codes = np.stack([code_f, code_b]).astype(np.int32)
    lane_sum = (np.arange(HG_DIAG * HG_DIM)[:, None] // HG_DIM == col % HG_DIAG).astype(np.float32)
    return jnp.asarray(tri, BF16), jnp.asarray(codes), jnp.asarray(lane_sum, BF16)


def _hgrn2(p_l, p_c, lbp, b):
    tri, codes, lane_sum = _hg_constants()
    t = p_l.shape[0] // b
    ctx = p_c.shape[0] // b
    hd = HG_DIM
    pw = HG_PAIR * hd
    n = t + ctx
    first = _HG0 * LANE // pw

    def col(stream):
        return lambda i, j: (i, first + stream * (HG_HEADS // HG_PAIR) + j)

    lat_specs = [pl.BlockSpec((t, pw), col(s)) for s in range(4)]
    ctx_specs = [pl.BlockSpec((ctx, pw), col(s)) for s in range(4)]
    big = lambda: pltpu.VMEM((HG_PAIR, n, hd), F32)
    return pl.pallas_call(
        _hg_kernel,
        grid=(b, HG_HEADS // HG_PAIR),
        in_specs=lat_specs + ctx_specs + [
            pl.BlockSpec((2, 3, pw), lambda i, j: (0, 0, j)),
            pl.BlockSpec(tri.shape, lambda i, j: (0, 0, 0)),
            pl.BlockSpec(codes.shape, lambda i, j: (0, 0, 0)),
            pl.BlockSpec(lane_sum.shape, lambda i, j: (0, 0)),
        ],
        out_specs=[
            pl.BlockSpec((t, pw), lambda i, j: (i, j)),
            pl.BlockSpec((ctx, pw), lambda i, j: (i, j)),
        ],
        out_shape=[
            jax.ShapeDtypeStruct((b * t, HG_WIDTH), F32),
            jax.ShapeDtypeStruct((b * ctx, HG_WIDTH), F32),
        ],
        scratch_shapes=[big() for _ in range(7)] + [pltpu.VMEM((HG_PAIR, 2, hd, hd), F32)],
        compiler_params=_cparams(("arbitrary", "arbitrary")),
        name="hgrn2",
    )(*([p_l] * 4 + [p_c] * 4 + [lbp, tri, codes, lane_sum]))


def _outproj_kernel(oa_ref, os_ref, hg_ref, hnw_ref, u_ref, v_ref, lnw_ref, ws_ref, bs_ref,
                    wa_ref, wb_ref, wc_ref, x_ref, g_ref, o_ref, ob_ref, oc_ref):
    dot = functools.partial(jnp.dot, preferred_element_type=F32)
    y = dot(oa_ref[...], wa_ref[...])
    for h in range(HG_HEADS):
        hs = slice(h * HG_DIM, (h + 1) * HG_DIM)
        o = os_ref[:, hs]
        o = o * lax.rsqrt(jnp.mean(o * o, axis=-1, keepdims=True) + EPS) * hnw_ref[...]
        ob_ref[:, hs] = (o * _silu(hg_ref[:, hs])).astype(ob_ref.dtype)
    y = y + dot(ob_ref[...], wb_ref[...])
    _gm_block(u_ref, v_ref, lnw_ref, ws_ref, bs_ref, oc_ref)
    y = y + dot(oc_ref[...], wc_ref[...])
    o_ref[...] = x_ref[...] + g_ref[0] * y


def _outproj(oa, osum, p2, hg_norm_w, gm_ln_w, gm_ws, gm_bs, w, w_layer, x2d, mod, layer, rows_per_mod,
             mod_row0, tm):
    m, d = x2d.shape
    mrow = lambda i: layer * MOD_ROWS + mod_row0 + (i * tm) // rows_per_mod
    nb = NA_WIDTH // HG_WIDTH
    assert tm % GM_CHUNK == 0
    return pl.pallas_call(
        _outproj_kernel,
        grid=(m // tm,),
        in_specs=[
            pl.BlockSpec((tm, NA_WIDTH), lambda i: (i, 0)),
            pl.BlockSpec((tm, HG_WIDTH), lambda i: (i, 0)),
            pl.BlockSpec((tm, HG_WIDTH), lambda i: (i, _HG_G)),
            pl.BlockSpec((1, HG_DIM), lambda i: (0, 0)),
            pl.BlockSpec((tm, GM_WIDTH), lambda i: (i, _GM_U)),
            pl.BlockSpec((tm, GM_WIDTH), lambda i: (i, _GM_V)),
            pl.BlockSpec((1, GM_WIDTH), lambda i: (0, 0)),
            pl.BlockSpec((GM_GROUPS, GM_CHUNK, GM_CHUNK), lambda i: (0, 0, 0)),
            pl.BlockSpec((GM_GROUPS, GM_CHUNK, 1), lambda i: (0, 0, 0)),
            pl.BlockSpec((None, NA_WIDTH, d), lambda i: (w_layer, 0, 0)),
            pl.BlockSpec((None, HG_WIDTH, d), lambda i: (w_layer, nb, 0)),
            pl.BlockSpec((None, GM_WIDTH, d), lambda i: (w_layer, nb + 1, 0)),
            pl.BlockSpec((tm, d), lambda i: (i, 0)),
            pl.BlockSpec((1, 1, d), lambda i: (mrow(i), 0, 2)),
        ],
        out_specs=pl.BlockSpec((tm, d), lambda i: (i, 0)),
        out_shape=jax.ShapeDtypeStruct((m, d), F32),
        scratch_shapes=[pltpu.VMEM((tm, HG_WIDTH), BF16), pltpu.VMEM((tm, GM_WIDTH), BF16)],
        compiler_params=_cparams(("arbitrary",)),
        name="outproj",
    )(oa, osum, p2, hg_norm_w.reshape(1, HG_DIM), p2, p2, gm_ln_w.reshape(1, GM_WIDTH), gm_ws,
      gm_bs.reshape(GM_GROUPS, GM_CHUNK, 1), w, w, w, x2d, mod)


def _mlp_kernel(x_ref, nw_ref, sh_ref, sc_ref, g_ref, w1_ref, w2_ref, fnw_ref, o_ref, h_ref, acc_ref,
                *, final_norm):
    j = pl.program_id(1)

    @pl.when(j == 0)
    def _():
        _norm_modulate_store(x_ref, nw_ref, sh_ref, sc_ref, h_ref)
        acc_ref[...] = jnp.zeros_like(acc_ref)

    a = jnp.maximum(jnp.dot(h_ref[...], w1_ref[...], preferred_element_type=F32), 0.0)
    acc_ref[...] += jnp.dot((a * a).astype(BF16), w2_ref[...], preferred_element_type=F32)

    @pl.when(j == pl.num_programs(1) - 1)
    def _():
        y = x_ref[...] + g_ref[0] * acc_ref[...]
        if final_norm:
            y = y * lax.rsqrt(jnp.mean(y * y, axis=-1, keepdims=True) + EPS) * fnw_ref[...]
        o_ref[...] = y


def _mlp(x2d, nw, mod, w1, w2, w_layer, fnw, layer, rows_per_mod, mod_row0, tm, final_norm):
    m, d = x2d.shape
    hid = w1.shape[2]
    th = MLP_TH
    mrow = lambda i: layer * MOD_ROWS + mod_row0 + (i * tm) // rows_per_mod
    return pl.pallas_call(
        functools.partial(_mlp_kernel, final_norm=final_norm),
        grid=(m // tm, hid // th),
        in_specs=[
            pl.BlockSpec((tm, d), lambda i, j: (i, 0)),
            pl.BlockSpec((None, 1, d), lambda i, j: (layer, 0, 0)),
            pl.BlockSpec((1, 1, d), lambda i, j: (mrow(i), 0, 3)),
            pl.BlockSpec((1, 1, d), lambda i, j: (mrow(i), 0, 4)),
            pl.BlockSpec((1, 1, d), lambda i, j: (mrow(i), 0, 5)),
            pl.BlockSpec((None, d, th), lambda i, j: (w_layer, 0, j)),
            pl.BlockSpec((None, th, d), lambda i, j: (w_layer, j, 0)),
            pl.BlockSpec((1, d), lambda i, j: (0, 0)),
        ],
        out_specs=pl.BlockSpec((tm, d), lambda i, j: (i, 0)),
        out_shape=jax.ShapeDtypeStruct((m, d), F32),
        scratch_shapes=[pltpu.VMEM((tm, d), BF16), pltpu.VMEM((tm, d), F32)],
        compiler_params=_cparams(("arbitrary", "arbitrary")),
        name="mlp",
    )(x2d, nw, mod, mod, mod, w1, w2, fnw)


def kernel(x, c, ctx, c_ctx, ada_w, ada_b, norm1_w, norm2_w, w_in, na_rpb, hg_lb_logits, hg_norm_w,
           gm_ln_w, gm_ws, gm_bs, w_out, mlp_w1, mlp_w2, final_norm_w):
    bsz, seq, d = x.shape
    n_ctx = ctx.shape[1]
    depth = ada_w.shape[0]
    assert bsz < MOD_ROWS and d == D_MODEL and seq % 512 == 0 and n_ctx % 256 == 0
    assert seq // NA_GROUP_TOK >= NA_WIN_BLOCKS and seq // GRID_W >= 2 * NA_WIN_ROWS

    lb = jnp.cumsum(jax.nn.softmax(hg_lb_logits.astype(F32), axis=0), axis=0)
    lb = lb - lb[:1]
    lbp = jnp.stack([jnp.log(lb), jnp.log1p(-lb), 1.0 - lb], axis=2)

    cond = jnp.zeros((MOD_ROWS, d), F32).at[:bsz].set(c).at[bsz].set(c_ctx)
    mod = _ada(cond, ada_w, ada_b).reshape(depth * MOD_ROWS, 1, 6 * d)

    w_in_b = w_in[:1].astype(BF16)
    nw1 = norm1_w.reshape(depth, 1, d)
    nw2 = norm2_w.reshape(depth, 1, d)
    fnw = final_norm_w.reshape(1, d)

    xl = x.reshape(bsz * seq, d)
    xc = ctx.reshape(bsz * n_ctx, d)
    n_c = bsz * n_ctx
    tm_l, tm_c = TM_LATENT, min(TM_CONTEXT, n_c)
    for l in range(depth):
        need_ctx = l < depth - 1
        qkv_l, p_l = _inproj(xl, nw1, mod, w_in_b, l, seq, 0, tm_l)
        qkv_c, p_c = _inproj(xc, nw1, mod, w_in_b, l, n_c, bsz, tm_c)

        to_cast = [(w_out, l), (mlp_w1, l), (mlp_w2, l)] + ([(w_in, l + 1)] if need_ctx else [])
        oa_l, (w_out_b, w1_b, w2_b, *w_in_next) = _na_latent(qkv_l, qkv_c, _na_bias_slabs(na_rpb[l]),
                                                             to_cast, bsz)
        w_in_b = w_in_next[0] if w_in_next else None
        os_l, os_c = _hgrn2(p_l, p_c, lbp[l], bsz)
        mix = (hg_norm_w[l], gm_ln_w[l], gm_ws[l], gm_bs[l])
        xl = _outproj(oa_l, os_l, p_l, *mix, w_out_b, 0, xl, mod, l, seq, 0, tm_l)
        xl = _mlp(xl, nw2, mod, w1_b, w2_b, 0, fnw, l, seq, 0, tm_l, final_norm=not need_ctx)
        if need_ctx:
            oa_c = _ctx_attention(qkv_c, bsz)
            xc = _outproj(oa_c, os_c, p_c, *mix, w_out_b, 0, xc, mod, l, n_c, bsz, tm_c)
            xc = _mlp(xc, nw2, mod, w1_b, w2_b, 0, fnw, l, n_c, bsz, tm_c, final_norm=False)
    return xl.reshape(bsz, seq, d)
```

```python
import functools

import numpy as np
import jax
import jax.numpy as jnp
from jax import lax
from jax.experimental import pallas as pl
from jax.experimental.pallas import tpu as pltpu

F32 = jnp.float32
BF16 = jnp.bfloat16

D_MODEL = 2048
DEPTH = 2
GRID_W = 64
EPS = 1e-6

NA_HEAD_DIM = 128
NA_HEADS = 8
NA_WIDTH = NA_HEADS * NA_HEAD_DIM
NA_WIN_ROWS = 8
NA_WIN_COLS = 16

HG_HEADS = 4
HG_DIM = 128
HG_WIDTH = HG_HEADS * HG_DIM
HG_CHUNK = 64

GM_GROUPS = 4
GM_DIM = 128
GM_WIDTH = GM_GROUPS * GM_DIM
GM_CHUNK = 128

IN_WIDTH = 3 * NA_WIDTH + 5 * HG_WIDTH + 2 * GM_WIDTH
MLP_HIDDEN = 4 * D_MODEL
LANE = 128
LOG2E = 1.4426950408889634
BF16_SUBLANES = 16

QKV_WIDTH = 3 * NA_WIDTH
REST_WIDTH = IN_WIDTH - QKV_WIDTH
REST_MAIN = QKV_WIDTH
REST_TAIL = REST_WIDTH - REST_MAIN
NA_SCORE_SCALE = NA_HEAD_DIM ** -0.5 * LOG2E
_QA, _KA, _VA = 0, NA_HEADS, 2 * NA_HEADS
_HG0 = 0
_GM_U = 5 * HG_WIDTH // GM_WIDTH
_GM_V = _GM_U + 1
_HG_G = 4

VMEM_LIMIT_V7X = 56 * 1024 * 1024
VMEM_LIMIT_MLP_V7X = 60 * 1024 * 1024
TM_LATENT = 512
TM_CONTEXT = 512
MLP_TH = 2048
MLP_TH_INNER = 512
NORM_CHUNK_ROWS = 32
NORM_UNROLL = 4
MOD_ROWS = 16


def _cparams(sem, vmem_limit=VMEM_LIMIT_V7X):
    return pltpu.CompilerParams(dimension_semantics=sem, vmem_limit_bytes=vmem_limit)


def _silu(x):
    h = 0.5 * x
    return h + h * jnp.tanh(h)


def _gelu_tanh(x):
    return 0.5 * x * (1.0 + jnp.tanh(0.7978845608028654 * (x + 0.044715 * (x * x * x))))


def _norm_modulate_store(x_ref, nw_ref, sh_ref, sc_ref, h_ref):
    gain = nw_ref[...] * (1.0 + sc_ref[0])
    shift = sh_ref[0]

    def body(i, carry):
        rows = pl.ds(pl.multiple_of(i * NORM_CHUNK_ROWS, NORM_CHUNK_ROWS), NORM_CHUNK_ROWS)
        x = x_ref[rows, :]
        rs = lax.rsqrt(jnp.mean(x * x, axis=-1, keepdims=True) + EPS)
        h_ref[rows, :] = (x * rs * gain + shift).astype(h_ref.dtype)
        return carry

    lax.fori_loop(0, x_ref.shape[0] // NORM_CHUNK_ROWS, body, 0, unroll=NORM_UNROLL)


def _ada_kernel(c_ref, w_ref, b_ref, o_ref):
    s = _silu(c_ref[...]).astype(BF16)
    o_ref[0] = jnp.dot(s, w_ref[0].astype(BF16), preferred_element_type=F32) + b_ref[0]


def _ada(cond, ada_w, ada_b):
    depth, d, n = ada_w.shape
    tn = 1536
    return pl.pallas_call(
        _ada_kernel,
        grid=(depth, n // tn),
        in_specs=[
            pl.BlockSpec((cond.shape[0], d), lambda l, j: (0, 0)),
            pl.BlockSpec((1, d, tn), lambda l, j: (l, 0, j)),
            pl.BlockSpec((1, 1, tn), lambda l, j: (l, 0, j)),
        ],
        out_specs=pl.BlockSpec((1, cond.shape[0], tn), lambda l, j: (l, 0, j)),
        out_shape=jax.ShapeDtypeStruct((depth, cond.shape[0], n), F32),
        compiler_params=_cparams(("arbitrary", "arbitrary")),
        name="ada",
    )(cond, ada_w, ada_b.reshape(depth, 1, n))


def _inproj_qkv_kernel(x_ref, nw_ref, sh_ref, sc_ref, w_ref, o_ref, h_ref):
    _norm_modulate_store(x_ref, nw_ref, sh_ref, sc_ref, h_ref)
    h = h_ref[...]
    q = jnp.dot(h, w_ref[:, :NA_WIDTH], preferred_element_type=F32)
    o_ref[:, :NA_WIDTH] = (q * NA_SCORE_SCALE).astype(o_ref.dtype)
    o_ref[:, NA_WIDTH:] = jnp.dot(h, w_ref[:, NA_WIDTH:], preferred_element_type=F32).astype(o_ref.dtype)


def _inproj_rest_kernel(h_ref, wa_ref, wb_ref, o_ref):
    h = h_ref[...]
    o_ref[:, :REST_MAIN] = jnp.dot(h, wa_ref[...], preferred_element_type=F32)
    o_ref[:, REST_MAIN:] = jnp.dot(h, wb_ref[...], preferred_element_type=F32)


def _inproj(x2d, nw, mod, w, layer, rows_per_mod, mod_row0, tm):
    m, d = x2d.shape
    mrow = lambda i: layer * MOD_ROWS + mod_row0 + (i * tm) // rows_per_mod
    assert w.shape[0] == 1 and w.shape[2] == QKV_WIDTH + REST_MAIN + REST_TAIL
    assert (QKV_WIDTH + REST_MAIN) % REST_TAIL == 0
    qkv, h = pl.pallas_call(
        _inproj_qkv_kernel,
        grid=(m // tm,),
        in_specs=[
            pl.BlockSpec((tm, d), lambda i: (i, 0)),
            pl.BlockSpec((None, 1, d), lambda i: (layer, 0, 0)),
            pl.BlockSpec((1, 1, d), lambda i: (mrow(i), 0, 0)),
            pl.BlockSpec((1, 1, d), lambda i: (mrow(i), 0, 1)),
            pl.BlockSpec((None, d, QKV_WIDTH), lambda i: (0, 0, 0)),
        ],
        out_specs=[pl.BlockSpec((tm, QKV_WIDTH), lambda i: (i, 0)), pl.BlockSpec((tm, d), lambda i: (i, 0))],
        out_shape=[jax.ShapeDtypeStruct((m, QKV_WIDTH), BF16), jax.ShapeDtypeStruct((m, d), BF16)],
        compiler_params=_cparams(("arbitrary",)),
        name="inproj_qkv",
    )(x2d, nw, mod, mod, w)
    rest = pl.pallas_call(
        _inproj_rest_kernel,
        grid=(m // tm,),
        in_specs=[
            pl.BlockSpec((tm, d), lambda i: (i, 0)),
            pl.BlockSpec((None, d, REST_MAIN), lambda i: (0, 0, 1)),
            pl.BlockSpec((None, d, REST_TAIL), lambda i: (0, 0, (QKV_WIDTH + REST_MAIN) // REST_TAIL)),
        ],
        out_specs=pl.BlockSpec((tm, REST_WIDTH), lambda i: (i, 0)),
        out_shape=jax.ShapeDtypeStruct((m, REST_WIDTH), F32),
        compiler_params=_cparams(("arbitrary",)),
        name="inproj_rest",
    )(h, w, w)
    return qkv, rest


NA_GROUP_ROWS = 4
NA_GROUP_TOK = NA_GROUP_ROWS * GRID_W
NA_WIN_BLOCKS = 3
NA_GROUPS_PER_STEP = 8
NA_LOOP_UNROLL = 1


NA_MASKED_SLAB = 2 * NA_WIN_ROWS - 1


def _na_bias_slabs(rpb):
    h = rpb.shape[0]
    kcol = np.arange(GRID_W)[:, None]
    qcol = np.arange(GRID_W)[None, :]
    wstart = np.clip(qcol - NA_WIN_COLS // 2, 0, GRID_W - NA_WIN_COLS)
    valid_col = (kcol >= wstart) & (kcol < wstart + NA_WIN_COLS)
    pad = GRID_W - NA_WIN_COLS
    padded = jnp.pad(rpb.astype(F32), ((0, 0), (0, 0), (pad, pad)))
    m = jnp.tile(padded, (1, 1, GRID_W + 1))[:, :, :GRID_W * 2 * GRID_W]
    m = m.reshape(h, rpb.shape[1], GRID_W, 2 * GRID_W)[..., :GRID_W]
    slabs = jnp.where(valid_col, m[..., ::-1] * LOG2E, -1e30)
    slabs = jnp.concatenate([slabs, jnp.full((h, 1, GRID_W, GRID_W), -1e30, F32)], axis=1)
    return jnp.concatenate([slabs, slabs], axis=-1)


def _na_slab_index(rows):
    n_groups = rows // NA_GROUP_ROWS
    krel = np.arange(NA_WIN_BLOCKS * NA_GROUP_ROWS)[:, None]
    qrel = np.arange(NA_GROUP_ROWS)[None, :]
    idx = []
    for g in (0, 1, n_groups - 1):
        krow = int(np.clip(g - 1, 0, n_groups - NA_WIN_BLOCKS)) * NA_GROUP_ROWS + krel
        qrow = g * NA_GROUP_ROWS + qrel
        r0 = np.clip(qrow - NA_WIN_ROWS // 2, 0, rows - NA_WIN_ROWS)
        in_window = (krow >= r0) & (krow < r0 + NA_WIN_ROWS)
        idx.append(np.where(in_window, krow - qrow + NA_WIN_ROWS - 1, NA_MASKED_SLAB))
    return np.stack(idx).tolist()


def _na_kernel(q_ref, k_ref, v_ref, kc_ref, vc_ref, slab_ref, *refs, n_weights, slab_index):
    w_refs, o_ref, w_bf16_refs = refs[:n_weights], refs[n_weights], refs[n_weights + 1:2 * n_weights + 1]
    vt_ref, bias_ref = refs[2 * n_weights + 1:]
    n_groups = q_ref.shape[0] // NA_GROUP_TOK
    gt = NA_GROUP_TOK
    nt = (((1,), (1,)), ((), ()))
    dot = functools.partial(jnp.dot, preferred_element_type=F32)

    for src_ref, dst_ref in zip(w_refs, w_bf16_refs):
        dst_ref[...] = src_ref[...].astype(BF16)

    @pl.when(pl.program_id(1) == 0)
    def _():
        left = lax.broadcasted_iota(jnp.int32, (GRID_W, 2 * GRID_W), 1) < GRID_W
        for kind, per_key_row in enumerate(slab_index):
            for j, per_query_row in enumerate(per_key_row):
                for a in range(0, NA_GROUP_ROWS, 2):
                    tile = jnp.where(left, slab_ref[0, per_query_row[a]], slab_ref[0, per_query_row[a + 1]])
                    bias_ref[kind, j * GRID_W:(j + 1) * GRID_W, a * GRID_W:(a + 2) * GRID_W] = tile

    for b in range(n_groups):
        vt_ref[b] = v_ref[b * gt:(b + 1) * gt, :].T
    kc = kc_ref[...]
    vct = vc_ref[...].T

    def group(g):
        blk = jnp.clip(g - 1, 0, n_groups - NA_WIN_BLOCKS)
        kind = jnp.where(g == 0, 0, jnp.where(g == n_groups - 1, 2, 1))
        qg = q_ref[pl.ds(pl.multiple_of(g * gt, gt), gt), :]
        kw = k_ref[pl.ds(pl.multiple_of(blk * gt, gt), NA_WIN_BLOCKS * gt), :]
        s_lat = lax.dot_general(kw, qg, nt, preferred_element_type=F32) + bias_ref[kind]
        s_ctx = lax.dot_general(kc, qg, nt, preferred_element_type=F32)
        yield
        m = jnp.maximum(jnp.max(s_lat, axis=0, keepdims=True), jnp.max(s_ctx, axis=0, keepdims=True))
        p_lat = jnp.exp2(s_lat - m)
        p_ctx = jnp.exp2(s_ctx - m)
        l = jnp.sum(p_lat, axis=0, keepdims=True) + jnp.sum(p_ctx, axis=0, keepdims=True)
        yield
        ot = dot(vct, p_ctx.astype(BF16))
        for j in range(NA_WIN_BLOCKS):
            ot = ot + dot(vt_ref[blk + j], p_lat[j * gt:(j + 1) * gt, :].astype(BF16))
        yield
        o_ref[pl.ds(pl.multiple_of(g * gt, gt), gt), :] = (ot / l).T.astype(o_ref.dtype)
        yield

    def body(i, carry):
        groups = [group(i * NA_GROUPS_PER_STEP + u) for u in range(NA_GROUPS_PER_STEP)]
        for _ in range(4):
            for grp in groups:
                next(grp)
        return carry

    lax.fori_loop(0, n_groups // NA_GROUPS_PER_STEP, body, 0, unroll=NA_LOOP_UNROLL)


def _na_latent(p_l, p_c, slabs, weights, b):
    t = p_l.shape[0] // b
    ctx = p_c.shape[0] // b
    hd = NA_HEAD_DIM
    n_steps = NA_HEADS * b
    w_specs_in, w_specs_out, w_shapes = [], [], []
    for w, layer in weights:
        rows, cols = w.shape[1] // n_steps, w.shape[2]
        assert w.shape[1] % n_steps == 0 and rows % BF16_SUBLANES == 0
        w_specs_in.append(pl.BlockSpec((None, rows, cols), lambda h, i, layer=layer: (layer, h * b + i, 0)))
        w_specs_out.append(pl.BlockSpec((None, rows, cols), lambda h, i: (0, h * b + i, 0)))
        w_shapes.append(jax.ShapeDtypeStruct((1,) + w.shape[1:], BF16))
    oa, *w_bf16 = pl.pallas_call(
        functools.partial(_na_kernel, n_weights=len(weights), slab_index=_na_slab_index(t // GRID_W)),
        grid=(NA_HEADS, b),
        in_specs=[
            pl.BlockSpec((t, hd), lambda h, i: (i, _QA + h)),
            pl.BlockSpec((t, hd), lambda h, i: (i, _KA + h)),
            pl.BlockSpec((t, hd), lambda h, i: (i, _VA + h)),
            pl.BlockSpec((ctx, hd), lambda h, i: (i, _KA + h)),
            pl.BlockSpec((ctx, hd), lambda h, i: (i, _VA + h)),
            pl.BlockSpec((1,) + slabs.shape[1:], lambda h, i: (h, 0, 0, 0)),
        ] + w_specs_in,
        out_specs=[pl.BlockSpec((t, hd), lambda h, i: (i, h))] + w_specs_out,
        out_shape=[jax.ShapeDtypeStruct((b * t, NA_WIDTH), BF16)] + w_shapes,
        scratch_shapes=[
            pltpu.VMEM((t // NA_GROUP_TOK, hd, NA_GROUP_TOK), BF16),
            pltpu.VMEM((3, NA_WIN_BLOCKS * NA_GROUP_TOK, NA_GROUP_TOK), F32),
        ],
        compiler_params=_cparams(("arbitrary", "arbitrary")),
        name="na_latent",
    )(p_l, p_l, p_l, p_c, p_c, slabs, *(w for w, _ in weights))
    return oa, w_bf16


def _ctx_attn_kernel(q_ref, k_ref, v_ref, o_ref):
    for h in range(NA_HEADS):
        hs = slice(h * NA_HEAD_DIM, (h + 1) * NA_HEAD_DIM)
        s = lax.dot_general(q_ref[:, hs], k_ref[:, hs], (((1,), (1,)), ((), ())), preferred_element_type=F32)
        p = jnp.exp2(s - jnp.max(s, axis=-1, keepdims=True))
        l = jnp.sum(p, axis=-1, keepdims=True)
        o = jnp.dot(p.astype(BF16), v_ref[:, hs], preferred_element_type=F32)
        o_ref[:, hs] = (o / l).astype(o_ref.dtype)


def _ctx_attention(p_c, b):
    ctx = p_c.shape[0] // b
    return pl.pallas_call(
        _ctx_attn_kernel,
        grid=(b,),
        in_specs=[
            pl.BlockSpec((ctx, NA_WIDTH), lambda i: (i, 0)),
            pl.BlockSpec((ctx, NA_WIDTH), lambda i: (i, 1)),
            pl.BlockSpec((ctx, NA_WIDTH), lambda i: (i, 2)),
        ],
        out_specs=pl.BlockSpec((ctx, NA_WIDTH), lambda i: (i, 0)),
        out_shape=jax.ShapeDtypeStruct((b * ctx, NA_WIDTH), BF16),
        compiler_params=_cparams(("arbitrary",)),
        name="ctx_attention",
    )(p_c, p_c, p_c)


def _gm_block(u_ref, v_ref, lnw_ref, ws_ref, bs_ref, o_ref):
    for ck in range(u_ref.shape[0] // GM_CHUNK):
        rows = slice(ck * GM_CHUNK, (ck + 1) * GM_CHUNK)
        uf = _gelu_tanh(u_ref[rows, :])
        vf = _gelu_tanh(v_ref[rows, :])
        for g in range(GM_GROUPS):
            sl = slice(g * GM_DIM, (g + 1) * GM_DIM)
            vg = vf[:, sl]
            mu = jnp.mean(vg, axis=-1, keepdims=True)
            dv = vg - mu
            var = jnp.mean(dv * dv, axis=-1, keepdims=True)
            vn = dv * lax.rsqrt(var + EPS) * lnw_ref[:, sl]
            mixed = jnp.dot(ws_ref[g].astype(BF16), vn.astype(BF16), preferred_element_type=F32) + bs_ref[g]
            o_ref[rows, sl] = (uf[:, sl] * mixed).astype(o_ref.dtype)


HG_PAIR = 2
HG_LEVELS = (32, 16, 8)
HG_DIAG = 8
HG_PREP_ROWS = 256
HG_SCAN_UNROLL = 8


def _split3_dot(tri, g):
    g0 = g.astype(BF16)
    r1 = g - g0.astype(F32)
    g1 = r1.astype(BF16)
    g2 = (r1 - g1.astype(F32)).astype(BF16)
    dot = functools.partial(jnp.dot, preferred_element_type=F32)
    return dot(tri, g0) + dot(tri, g1) + dot(tri, g2)


def _hg_kernel(ql_ref, ffl_ref, fbl_ref, il_ref,
               qc_ref, ffc_ref, fbc_ref, ic_ref,
               lbp_ref, tri_ref, code_ref, lsum_ref, ol_ref, oc_ref,
               qh_s, kf_s, kb_s, ef_s, eb_s, of_s, ob_s, st_s):
    c = HG_CHUNK
    hd = HG_DIM
    n_ctx = qc_ref.shape[0]
    n_lat = ql_ref.shape[0]
    scale = hd ** -0.5
    nt = (((1,), (1,)), ((), ()))
    tn = (((0,), (0,)), ((), ()))

    def prep(q_ref, ff_ref, fb_ref, base, n):
        step = HG_PREP_ROWS
        for t0 in range(0, n, step):
            src = slice(t0, t0 + step)
            dst = slice(base + t0, base + t0 + step)
            qh = _silu(q_ref[src, :]) * scale
            for h in range(HG_PAIR):
                qh_s[h, dst, :] = qh[:, h * hd:(h + 1) * hd]
            for d, (f_ref, k_s, e_s) in enumerate(((ff_ref, kf_s, ef_s), (fb_ref, kb_s, eb_s))):
                x = f_ref[src, :]
                sp = jnp.maximum(-x, 0.0) + jnp.log(1.0 + jnp.exp(-jnp.abs(x)))
                log_lb = lbp_ref[d, 0:1, :]
                y = lbp_ref[d, 1:2, :] - sp
                mx = jnp.maximum(log_lb, y)
                log_f = mx + jnp.log(1.0 + jnp.exp(-jnp.abs(log_lb - y)))
                one_minus_f = lbp_ref[d, 2:3, :] * jnp.exp(-(sp + x))
                e = _split3_dot(tri_ref[d], log_f) * LOG2E
                for h in range(HG_PAIR):
                    e_s[h, dst, :] = e[:, h * hd:(h + 1) * hd]
                    k_s[h, dst, :] = one_minus_f[:, h * hd:(h + 1) * hd]

    prep(qc_ref, ffc_ref, fbc_ref, 0, n_ctx)
    prep(ql_ref, ffl_ref, fbl_ref, n_ctx, n_lat)

    diag_code = len(HG_LEVELS) + 1

    st_s[...] = jnp.zeros_like(st_s)

    def chunk(off, v, h, d, out):
        fwd = d == 0
        k_s, e_s = (kf_s, ef_s) if fwd else (kb_s, eb_s)
        q = qh_s[h, pl.ds(off, c), :]
        k = k_s[h, pl.ds(off, c), :]
        e = e_s[h, pl.ds(off, c), :]
        code = code_ref[d]
        prods = []
        for i in range(c // HG_DIAG):
            bs = slice(i * HG_DIAG, (i + 1) * HG_DIAG)
            row_prods = []
            for s in range(HG_DIAG):
                kr = k_s[h, pl.ds(off + i * HG_DIAG + s, 1), :]
                er = e_s[h, pl.ds(off + i * HG_DIAG + s, 1), :]
                decay = jnp.exp2(jnp.minimum(e[bs] - er, 0.0))
                row_prods.append((q[bs] * kr * decay).astype(BF16))
            prods.append(jnp.concatenate(row_prods, axis=1))
        diag = jnp.dot(jnp.concatenate(prods, axis=0), lsum_ref[...], preferred_element_type=F32)
        yield
        att = None
        for li, w in enumerate(HG_LEVELS):
            zeros = jnp.zeros((w, hd), F32)
            q_parts, k_parts = [], []
            for a in range(0, c, 2 * w):
                lo, hi = slice(a, a + w), slice(a + w, a + 2 * w)
                if fwd:
                    ref = e_s[h, pl.ds(off + a + w - 1, 1), :]
                    q_parts += [zeros, q[hi] * jnp.exp2(e[hi] - ref)]
                    k_parts += [k[lo] * jnp.exp2(ref - e[lo]), zeros]
                else:
                    ref = e_s[h, pl.ds(off + a + w, 1), :]
                    q_parts += [q[lo] * jnp.exp2(e[lo] - ref), zeros]
                    k_parts += [zeros, k[hi] * jnp.exp2(ref - e[hi])]
            qs = jnp.concatenate(q_parts, axis=0).astype(BF16)
            ks = jnp.concatenate(k_parts, axis=0).astype(BF16)
            a_w = lax.dot_general(qs, ks, nt, preferred_element_type=F32)
            att = a_w if att is None else jnp.where(code == li + 1, a_w, att)
            yield
        e_end = e_s[h, pl.ds(off + (c - 1 if fwd else 0), 1), :]
        st = st_s[h, d]
        qi = (q * jnp.exp2(e)).astype(BF16)
        ki = (k * jnp.exp2(e_end - e)).astype(BF16)
        o = lax.dot_general(qi, st.astype(BF16), nt, preferred_element_type=F32)
        yield
        st_new = st * jnp.exp2(e_end) + lax.dot_general(v, ki, tn, preferred_element_type=F32)
        yield
        att = jnp.where(code == diag_code, diag, att)
        out.append((o + jnp.dot(att.astype(BF16), v, preferred_element_type=F32), st_new))
        yield

    n_stages = len(HG_LEVELS) + 4

    def scan(v_ref, base, n):
        def body(i, carry):
            rf = pl.multiple_of(i * c, c)
            rb = pl.multiple_of((n - 1 - i) * c, c)
            vf = v_ref[pl.ds(rf, c), :].astype(BF16)
            vb = v_ref[pl.ds(rb, c), :].astype(BF16)
            chains = []
            for h in range(HG_PAIR):
                hs = slice(h * hd, (h + 1) * hd)
                for d, r, v in ((0, rf, vf), (1, rb, vb)):
                    out = []
                    chains.append((h, d, r, out, chunk(base + r, v[:, hs], h, d, out)))
            for _ in range(n_stages):
                for chain in chains:
                    next(chain[-1])
            for h, d, r, out, _ in chains:
                o, st_new = out[0]
                (of_s if d == 0 else ob_s)[h, pl.ds(base + r, c), :] = o
                st_s[h, d] = st_new
            return carry

        lax.fori_loop(0, n, body, 0, unroll=HG_SCAN_UNROLL)

    scan(ic_ref, 0, n_ctx // c)
    scan(il_ref, n_ctx, n_lat // c)

    def finish(o_ref, base, n):
        step = 256
        for t0 in range(0, n, step):
            src = slice(t0, t0 + step)
            dst = slice(base + t0, base + t0 + step)
            for h in range(HG_PAIR):
                o_ref[src, h * hd:(h + 1) * hd] = of_s[h, dst, :] + ob_s[h, dst, :]

    finish(oc_ref, 0, n_ctx)
    finish(ol_ref, n_ctx, n_lat)


def _hg_constants():
    c = HG_CHUNK
    row, col = np.arange(c)[:, None], np.arange(c)[None, :]
    blocks = np.eye(HG_PREP_ROWS // c)
    tri = np.stack([np.kron(blocks, col <= row), np.kron(blocks, col >= row)]).astype(np.float32)
    same8 = (row // HG_DIAG) == (col // HG_DIAG)
    diag_code = len(HG_LEVELS) + 1
    code_f = np.where(same8 & (col <= row), diag_code, 0)
    code_b = np.where(same8 & (col >= row), diag_code, 0)
    for li, w in enumerate(HG_LEVELS):
        same = (row // (2 * w)) == (col // (2 * w))
        t_hi, s_hi = (row % (2 * w)) >= w, (col % (2 * w)) >= w
        code_f = np.where(same & t_hi & ~s_hi, li + 1, code_f)
        code_b = np.where(same & ~t_hi & s_hi, li + 1, code_b)
    codes = np.stack([code_f, code_b]).astype(np.int32)
    lane_sum = (np.arange(HG_DIAG * HG_DIM)[:, None] // HG_DIM == col % HG_DIAG).astype(np.float32)
    return jnp.asarray(tri, BF16), jnp.asarray(codes), jnp.asarray(lane_sum, BF16)


def _hgrn2(p_l, p_c, lbp, b):
    tri, codes, lane_sum = _hg_constants()
    t = p_l.shape[0] // b
    ctx = p_c.shape[0] // b
    hd = HG_DIM
    pw = HG_PAIR * hd
    n = t + ctx
    first = _HG0 * LANE // pw

    def col(stream):
        return lambda i, j: (i, first + stream * (HG_HEADS // HG_PAIR) + j)

    lat_specs = [pl.BlockSpec((t, pw), col(s)) for s in range(4)]
    ctx_specs = [pl.BlockSpec((ctx, pw), col(s)) for s in range(4)]
    big = lambda: pltpu.VMEM((HG_PAIR, n, hd), F32)
    return pl.pallas_call(
        _hg_kernel,
        grid=(b, HG_HEADS // HG_PAIR),
        in_specs=lat_specs + ctx_specs + [
            pl.BlockSpec((2, 3, pw), lambda i, j: (0, 0, j)),
            pl.BlockSpec(tri.shape, lambda i, j: (0, 0, 0)),
            pl.BlockSpec(codes.shape, lambda i, j: (0, 0, 0)),
            pl.BlockSpec(lane_sum.shape, lambda i, j: (0, 0)),
        ],
        out_specs=[
            pl.BlockSpec((t, pw), lambda i, j: (i, j)),
            pl.BlockSpec((ctx, pw), lambda i, j: (i, j)),
        ],
        out_shape=[
            jax.ShapeDtypeStruct((b * t, HG_WIDTH), F32),
            jax.ShapeDtypeStruct((b * ctx, HG_WIDTH), F32),
        ],
        scratch_shapes=[big() for _ in range(7)] + [pltpu.VMEM((HG_PAIR, 2, hd, hd), F32)],
        compiler_params=_cparams(("arbitrary", "arbitrary")),
        name="hgrn2",
    )(*([p_l] * 4 + [p_c] * 4 + [lbp, tri, codes, lane_sum]))


def _outproj_kernel(oa_ref, os_ref, hg_ref, hnw_ref, u_ref, v_ref, lnw_ref, ws_ref, bs_ref,
                    wa_ref, wb_ref, wc_ref, x_ref, g_ref, o_ref, ob_ref, oc_ref):
    dot = functools.partial(jnp.dot, preferred_element_type=F32)
    y = dot(oa_ref[...], wa_ref[...])
    for h in range(HG_HEADS):
        hs = slice(h * HG_DIM, (h + 1) * HG_DIM)
        o = os_ref[:, hs]
        o = o * lax.rsqrt(jnp.mean(o * o, axis=-1, keepdims=True) + EPS) * hnw_ref[...]
        ob_ref[:, hs] = (o * _silu(hg_ref[:, hs])).astype(ob_ref.dtype)
    y = y + dot(ob_ref[...], wb_ref[...])
    _gm_block(u_ref, v_ref, lnw_ref, ws_ref, bs_ref, oc_ref)
    y = y + dot(oc_ref[...], wc_ref[...])
    o_ref[...] = x_ref[...] + g_ref[0] * y


def _outproj(oa, osum, p2, hg_norm_w, gm_ln_w, gm_ws, gm_bs, w, w_layer, x2d, mod, layer, rows_per_mod,
             mod_row0, tm):
    m, d = x2d.shape
    mrow = lambda i: layer * MOD_ROWS + mod_row0 + (i * tm) // rows_per_mod
    nb = NA_WIDTH // HG_WIDTH
    assert tm % GM_CHUNK == 0
    return pl.pallas_call(
        _outproj_kernel,
        grid=(m // tm,),
        in_specs=[
            pl.BlockSpec((tm, NA_WIDTH), lambda i: (i, 0)),
            pl.BlockSpec((tm, HG_WIDTH), lambda i: (i, 0)),
            pl.BlockSpec((tm, HG_WIDTH), lambda i: (i, _HG_G)),
            pl.BlockSpec((1, HG_DIM), lambda i: (0, 0)),
            pl.BlockSpec((tm, GM_WIDTH), lambda i: (i, _GM_U)),
            pl.BlockSpec((tm, GM_WIDTH), lambda i: (i, _GM_V)),
            pl.BlockSpec((1, GM_WIDTH), lambda i: (0, 0)),
            pl.BlockSpec((GM_GROUPS, GM_CHUNK, GM_CHUNK), lambda i: (0, 0, 0)),
            pl.BlockSpec((GM_GROUPS, GM_CHUNK, 1), lambda i: (0, 0, 0)),
            pl.BlockSpec((None, NA_WIDTH, d), lambda i: (w_layer, 0, 0)),
            pl.BlockSpec((None, HG_WIDTH, d), lambda i: (w_layer, nb, 0)),
            pl.BlockSpec((None, GM_WIDTH, d), lambda i: (w_layer, nb + 1, 0)),
            pl.BlockSpec((tm, d), lambda i: (i, 0)),
            pl.BlockSpec((1, 1, d), lambda i: (mrow(i), 0, 2)),
        ],
        out_specs=pl.BlockSpec((tm, d), lambda i: (i, 0)),
        out_shape=jax.ShapeDtypeStruct((m, d), F32),
        scratch_shapes=[pltpu.VMEM((tm, HG_WIDTH), BF16), pltpu.VMEM((tm, GM_WIDTH), BF16)],
        compiler_params=_cparams(("arbitrary",)),
        name="outproj",
    )(oa, osum, p2, hg_norm_w.reshape(1, HG_DIM), p2, p2, gm_ln_w.reshape(1, GM_WIDTH), gm_ws,
      gm_bs.reshape(GM_GROUPS, GM_CHUNK, 1), w, w, w, x2d, mod)


def _mlp_kernel(x_ref, nw_ref, sh_ref, sc_ref, g_ref, w1_ref, w2_ref, fnw_ref, o_ref, h_ref, *, final_norm):
    j = pl.program_id(1)

    @pl.when(j == 0)
    def _():
        _norm_modulate_store(x_ref, nw_ref, sh_ref, sc_ref, h_ref)
        o_ref[...] = jnp.zeros_like(o_ref)

    h = h_ref[...]
    for c0 in range(0, w1_ref.shape[1], MLP_TH_INNER):
        cols = slice(c0, c0 + MLP_TH_INNER)
        a = jnp.maximum(jnp.dot(h, w1_ref[:, cols], preferred_element_type=F32), 0.0)
        o_ref[...] += jnp.dot((a * a).astype(BF16), w2_ref[cols, :], preferred_element_type=F32)

    @pl.when(j == pl.num_programs(1) - 1)
    def _():
        y = x_ref[...] + g_ref[0] * o_ref[...]
        if final_norm:
            y = y * lax.rsqrt(jnp.mean(y * y, axis=-1, keepdims=True) + EPS) * fnw_ref[...]
        o_ref[...] = y


def _mlp(x2d, nw, mod, w1, w2, w_layer, fnw, layer, rows_per_mod, mod_row0, tm, final_norm):
    m, d = x2d.shape
    hid = w1.shape[2]
    th = MLP_TH
    mrow = lambda i: layer * MOD_ROWS + mod_row0 + (i * tm) // rows_per_mod
    return pl.pallas_call(
        functools.partial(_mlp_kernel, final_norm=final_norm),
        grid=(m // tm, hid // th),
        in_specs=[
            pl.BlockSpec((tm, d), lambda i, j: (i, 0)),
            pl.BlockSpec((None, 1, d), lambda i, j: (layer, 0, 0)),
            pl.BlockSpec((1, 1, d), lambda i, j: (mrow(i), 0, 3)),
            pl.BlockSpec((1, 1, d), lambda i, j: (mrow(i), 0, 4)),
            pl.BlockSpec((1, 1, d), lambda i, j: (mrow(i), 0, 5)),
            pl.BlockSpec((None, d, th), lambda i, j: (w_layer, 0, j)),
            pl.BlockSpec((None, th, d), lambda i, j: (w_layer, j, 0)),
            pl.BlockSpec((1, d), lambda i, j: (0, 0)),
        ],
        out_specs=pl.BlockSpec((tm, d), lambda i, j: (i, 0)),
        out_shape=jax.ShapeDtypeStruct((m, d), F32),
        scratch_shapes=[pltpu.VMEM((tm, d), BF16)],
        compiler_params=_cparams(("arbitrary", "arbitrary"), VMEM_LIMIT_MLP_V7X),
        name="mlp",
    )(x2d, nw, mod, mod, mod, w1, w2, fnw)


def kernel(x, c, ctx, c_ctx, ada_w, ada_b, norm1_w, norm2_w, w_in, na_rpb, hg_lb_logits, hg_norm_w,
           gm_ln_w, gm_ws, gm_bs, w_out, mlp_w1, mlp_w2, final_norm_w):
    bsz, seq, d = x.shape
    n_ctx = ctx.shape[1]
    depth = ada_w.shape[0]
    assert bsz < MOD_ROWS and d == D_MODEL and seq % 512 == 0 and n_ctx % 256 == 0
    assert seq // NA_GROUP_TOK >= NA_WIN_BLOCKS and seq // GRID_W >= 2 * NA_WIN_ROWS

    lb = jnp.cumsum(jax.nn.softmax(hg_lb_logits.astype(F32), axis=0), axis=0)
    lb = lb - lb[:1]
    lbp = jnp.stack([jnp.log(lb), jnp.log1p(-lb), 1.0 - lb], axis=2)

    cond = jnp.zeros((MOD_ROWS, d), F32).at[:bsz].set(c).at[bsz].set(c_ctx)
    mod = _ada(cond, ada_w, ada_b).reshape(depth * MOD_ROWS, 1, 6 * d)

    w_in_b = w_in[:1].astype(BF16)
    nw1 = norm1_w.reshape(depth, 1, d)
    nw2 = norm2_w.reshape(depth, 1, d)
    fnw = final_norm_w.reshape(1, d)

    xl = x.reshape(bsz * seq, d)
    xc = ctx.reshape(bsz * n_ctx, d)
    n_c = bsz * n_ctx
    tm_l, tm_c = TM_LATENT, min(TM_CONTEXT, n_c)
    for l in range(depth):
        need_ctx = l < depth - 1
        qkv_l, p_l = _inproj(xl, nw1, mod, w_in_b, l, seq, 0, tm_l)
        qkv_c, p_c = _inproj(xc, nw1, mod, w_in_b, l, n_c, bsz, tm_c)

        to_cast = [(w_out, l), (mlp_w1, l), (mlp_w2, l)] + ([(w_in, l + 1)] if need_ctx else [])
        oa_l, (w_out_b, w1_b, w2_b, *w_in_next) = _na_latent(qkv_l, qkv_c, _na_bias_slabs(na_rpb[l]),
                                                             to_cast, bsz)
        w_in_b = w_in_next[0] if w_in_next else None
        os_l, os_c = _hgrn2(p_l, p_c, lbp[l], bsz)
        mix = (hg_norm_w[l], gm_ln_w[l], gm_ws[l], gm_bs[l])
        xl = _outproj(oa_l, os_l, p_l, *mix, w_out_b, 0, xl, mod, l, seq, 0, tm_l)
        xl = _mlp(xl, nw2, mod, w1_b, w2_b, 0, fnw, l, seq, 0, tm_l, final_norm=not need_ctx)
        if need_ctx:
            oa_c = _ctx_attention(qkv_c, bsz)
            xc = _outproj(oa_c, os_c, p_c, *mix, w_out_b, 0, xc, mod, l, n_c, bsz, tm_c)
            xc = _mlp(xc, nw2, mod, w1_b, w2_b, 0, fnw, l, n_c, bsz, tm_c, final_norm=False)
    return xl.reshape(bsz, seq, d)
```

```python
import functools

import numpy as np
import jax
import jax.numpy as jnp
from jax import lax
from jax.experimental import pallas as pl
from jax.experimental.pallas import tpu as pltpu

F32 = jnp.float32
BF16 = jnp.bfloat16

D_MODEL = 2048
DEPTH = 2
GRID_W = 64
EPS = 1e-6

NA_HEAD_DIM = 128
NA_HEADS = 8
NA_WIDTH = NA_HEADS * NA_HEAD_DIM
NA_WIN_ROWS = 8
NA_WIN_COLS = 16

HG_HEADS = 4
HG_DIM = 128
HG_WIDTH = HG_HEADS * HG_DIM
HG_CHUNK = 64

GM_GROUPS = 4
GM_DIM = 128
GM_WIDTH = GM_GROUPS * GM_DIM
GM_CHUNK = 128

IN_WIDTH = 3 * NA_WIDTH + 5 * HG_WIDTH + 2 * GM_WIDTH
MLP_HIDDEN = 4 * D_MODEL
LANE = 128
LOG2E = 1.4426950408889634
BF16_SUBLANES = 16

QKV_WIDTH = 3 * NA_WIDTH
REST_WIDTH = IN_WIDTH - QKV_WIDTH
REST_MAIN = QKV_WIDTH
REST_TAIL = REST_WIDTH - REST_MAIN
NA_SCORE_SCALE = NA_HEAD_DIM ** -0.5 * LOG2E
_QA, _KA, _VA = 0, NA_HEADS, 2 * NA_HEADS
_HG0 = 0
_GM_U = 5 * HG_WIDTH // GM_WIDTH
_GM_V = _GM_U + 1
_HG_G = 4

VMEM_LIMIT_V7X = 56 * 1024 * 1024
VMEM_LIMIT_MLP_V7X = 60 * 1024 * 1024
TM_LATENT = 512
TM_CONTEXT = 512
MLP_TH = 2048
MLP_TH_INNER = 512
NORM_CHUNK_ROWS = 32
NORM_UNROLL = 4
MOD_ROWS = 16


def _cparams(sem, vmem_limit=VMEM_LIMIT_V7X):
    return pltpu.CompilerParams(dimension_semantics=sem, vmem_limit_bytes=vmem_limit)


def _silu(x):
    h = 0.5 * x
    return h + h * jnp.tanh(h)


def _gelu_tanh(x):
    return 0.5 * x * (1.0 + jnp.tanh(0.7978845608028654 * (x + 0.044715 * (x * x * x))))


def _norm_modulate_store(x_ref, nw_ref, sh_ref, sc_ref, h_ref):
    gain = nw_ref[...] * (1.0 + sc_ref[0])
    shift = sh_ref[0]

    def body(i, carry):
        rows = pl.ds(pl.multiple_of(i * NORM_CHUNK_ROWS, NORM_CHUNK_ROWS), NORM_CHUNK_ROWS)
        x = x_ref[rows, :]
        rs = lax.rsqrt(jnp.mean(x * x, axis=-1, keepdims=True) + EPS)
        h_ref[rows, :] = (x * rs * gain + shift).astype(h_ref.dtype)
        return carry

    lax.fori_loop(0, x_ref.shape[0] // NORM_CHUNK_ROWS, body, 0, unroll=NORM_UNROLL)


def _ada_kernel(c_ref, w_ref, b_ref, o_ref):
    s = _silu(c_ref[...]).astype(BF16)
    res = jnp.dot(s, w_ref[0].astype(BF16), preferred_element_type=F32) + b_ref[0]
    for r in range(o_ref.shape[0]):
        o_ref[r] = res[r:r + 1, :]


def _ada(cond, ada_w, ada_b):
    depth, d, n = ada_w.shape
    tn = 1536
    return pl.pallas_call(
        _ada_kernel,
        grid=(depth, n // tn),
        in_specs=[
            pl.BlockSpec((cond.shape[0], d), lambda l, j: (0, 0)),
            pl.BlockSpec((1, d, tn), lambda l, j: (l, 0, j)),
            pl.BlockSpec((1, 1, tn), lambda l, j: (l, 0, j)),
        ],
        out_specs=pl.BlockSpec((cond.shape[0], 1, tn), lambda l, j: (l, 0, j)),
        out_shape=jax.ShapeDtypeStruct((depth * cond.shape[0], 1, n), F32),
        compiler_params=_cparams(("arbitrary", "arbitrary")),
        name="ada",
    )(cond, ada_w, ada_b.reshape(depth, 1, n))


def _inproj_qkv_kernel(x_ref, nw_ref, sh_ref, sc_ref, w_ref, o_ref, h_ref):
    _norm_modulate_store(x_ref, nw_ref, sh_ref, sc_ref, h_ref)
    h = h_ref[...]
    q = jnp.dot(h, w_ref[:, :NA_WIDTH], preferred_element_type=F32)
    o_ref[:, :NA_WIDTH] = (q * NA_SCORE_SCALE).astype(o_ref.dtype)
    o_ref[:, NA_WIDTH:] = jnp.dot(h, w_ref[:, NA_WIDTH:], preferred_element_type=F32).astype(o_ref.dtype)


def _inproj_rest_kernel(h_ref, wa_ref, wb_ref, o_ref):
    h = h_ref[...]
    o_ref[:, :REST_MAIN] = jnp.dot(h, wa_ref[...], preferred_element_type=F32)
    o_ref[:, REST_MAIN:] = jnp.dot(h, wb_ref[...], preferred_element_type=F32)


def _inproj(x2d, nw, mod, w, layer, rows_per_mod, mod_row0, tm):
    m, d = x2d.shape
    mrow = lambda i: layer * MOD_ROWS + mod_row0 + (i * tm) // rows_per_mod
    assert w.shape[0] == 1 and w.shape[2] == QKV_WIDTH + REST_MAIN + REST_TAIL
    assert (QKV_WIDTH + REST_MAIN) % REST_TAIL == 0
    qkv, h = pl.pallas_call(
        _inproj_qkv_kernel,
        grid=(m // tm,),
        in_specs=[
            pl.BlockSpec((tm, d), lambda i: (i, 0)),
            pl.BlockSpec((None, 1, d), lambda i: (layer, 0, 0)),
            pl.BlockSpec((1, 1, d), lambda i: (mrow(i), 0, 0)),
            pl.BlockSpec((1, 1, d), lambda i: (mrow(i), 0, 1)),
            pl.BlockSpec((None, d, QKV_WIDTH), lambda i: (0, 0, 0)),
        ],
        out_specs=[pl.BlockSpec((tm, QKV_WIDTH), lambda i: (i, 0)), pl.BlockSpec((tm, d), lambda i: (i, 0))],
        out_shape=[jax.ShapeDtypeStruct((m, QKV_WIDTH), BF16), jax.ShapeDtypeStruct((m, d), BF16)],
        compiler_params=_cparams(("arbitrary",)),
        name="inproj_qkv",
    )(x2d, nw, mod, mod, w)
    rest = pl.pallas_call(
        _inproj_rest_kernel,
        grid=(m // tm,),
        in_specs=[
            pl.BlockSpec((tm, d), lambda i: (i, 0)),
            pl.BlockSpec((None, d, REST_MAIN), lambda i: (0, 0, 1)),
            pl.BlockSpec((None, d, REST_TAIL), lambda i: (0, 0, (QKV_WIDTH + REST_MAIN) // REST_TAIL)),
        ],
        out_specs=pl.BlockSpec((tm, REST_WIDTH), lambda i: (i, 0)),
        out_shape=jax.ShapeDtypeStruct((m, REST_WIDTH), F32),
        compiler_params=_cparams(("arbitrary",)),
        name="inproj_rest",
    )(h, w, w)
    return qkv, rest


NA_GROUP_ROWS = 4
NA_GROUP_TOK = NA_GROUP_ROWS * GRID_W
NA_WIN_BLOCKS = 3
NA_GROUPS_PER_STEP = 8
NA_LOOP_UNROLL = 1


NA_MASKED_SLAB = 2 * NA_WIN_ROWS - 1


def _na_bias_slabs(rpb):
    h = rpb.shape[0]
    kcol = np.arange(GRID_W)[:, None]
    qcol = np.arange(GRID_W)[None, :]
    wstart = np.clip(qcol - NA_WIN_COLS // 2, 0, GRID_W - NA_WIN_COLS)
    valid_col = (kcol >= wstart) & (kcol < wstart + NA_WIN_COLS)
    period = 2 * GRID_W + 1
    rev = rpb.astype(F32)[..., ::-1]
    gap = jnp.zeros(rpb.shape[:2] + (period - rpb.shape[2],), F32)
    ring = jnp.concatenate([rev[..., NA_WIN_COLS - 1:], gap, rev[..., :NA_WIN_COLS - 1]], axis=-1)
    m = jnp.tile(ring, (1, 1, GRID_W))[:, :, :GRID_W * 2 * GRID_W]
    m = m.reshape(h, rpb.shape[1], GRID_W, 2 * GRID_W)[..., :GRID_W]
    slabs = jnp.where(valid_col, m * LOG2E, -1e30)
    slabs = jnp.concatenate([slabs, jnp.full((h, 1, GRID_W, GRID_W), -1e30, F32)], axis=1)
    return jnp.concatenate([slabs, slabs], axis=-1)


def _na_slab_index(rows):
    n_groups = rows // NA_GROUP_ROWS
    krel = np.arange(NA_WIN_BLOCKS * NA_GROUP_ROWS)[:, None]
    qrel = np.arange(NA_GROUP_ROWS)[None, :]
    idx = []
    for g in (0, 1, n_groups - 1):
        krow = int(np.clip(g - 1, 0, n_groups - NA_WIN_BLOCKS)) * NA_GROUP_ROWS + krel
        qrow = g * NA_GROUP_ROWS + qrel
        r0 = np.clip(qrow - NA_WIN_ROWS // 2, 0, rows - NA_WIN_ROWS)
        in_window = (krow >= r0) & (krow < r0 + NA_WIN_ROWS)
        idx.append(np.where(in_window, krow - qrow + NA_WIN_ROWS - 1, NA_MASKED_SLAB))
    return np.stack(idx).tolist()


def _na_kernel(q_ref, k_ref, v_ref, kc_ref, vc_ref, slab_ref, *refs, n_weights, slab_index):
    w_refs, o_ref, w_bf16_refs = refs[:n_weights], refs[n_weights], refs[n_weights + 1:2 * n_weights + 1]
    vt_ref, bias_ref = refs[2 * n_weights + 1:]
    n_groups = q_ref.shape[0] // NA_GROUP_TOK
    gt = NA_GROUP_TOK
    nt = (((1,), (1,)), ((), ()))
    dot = functools.partial(jnp.dot, preferred_element_type=F32)

    for src_ref, dst_ref in zip(w_refs, w_bf16_refs):
        dst_ref[...] = src_ref[...].astype(BF16)

    @pl.when(pl.program_id(1) == 0)
    def _():
        left = lax.broadcasted_iota(jnp.int32, (GRID_W, 2 * GRID_W), 1) < GRID_W
        for kind, per_key_row in enumerate(slab_index):
            for j, per_query_row in enumerate(per_key_row):
                for a in range(0, NA_GROUP_ROWS, 2):
                    tile = jnp.where(left, slab_ref[0, per_query_row[a]], slab_ref[0, per_query_row[a + 1]])
                    bias_ref[kind, j * GRID_W:(j + 1) * GRID_W, a * GRID_W:(a + 2) * GRID_W] = tile

    for b in range(n_groups):
        vt_ref[b] = v_ref[b * gt:(b + 1) * gt, :].T
    kc = kc_ref[...]
    vct = vc_ref[...].T

    def group(g):
        blk = jnp.clip(g - 1, 0, n_groups - NA_WIN_BLOCKS)
        kind = jnp.where(g == 0, 0, jnp.where(g == n_groups - 1, 2, 1))
        qg = q_ref[pl.ds(pl.multiple_of(g * gt, gt), gt), :]
        kw = k_ref[pl.ds(pl.multiple_of(blk * gt, gt), NA_WIN_BLOCKS * gt), :]
        s_lat = lax.dot_general(kw, qg, nt, preferred_element_type=F32) + bias_ref[kind]
        s_ctx = lax.dot_general(kc, qg, nt, preferred_element_type=F32)
        yield
        m = jnp.maximum(jnp.max(s_lat, axis=0, keepdims=True), jnp.max(s_ctx, axis=0, keepdims=True))
        p_lat = jnp.exp2(s_lat - m)
        p_ctx = jnp.exp2(s_ctx - m)
        l = jnp.sum(p_lat, axis=0, keepdims=True) + jnp.sum(p_ctx, axis=0, keepdims=True)
        yield
        ot = dot(vct, p_ctx.astype(BF16))
        for j in range(NA_WIN_BLOCKS):
            ot = ot + dot(vt_ref[blk + j], p_lat[j * gt:(j + 1) * gt, :].astype(BF16))
        yield
        o_ref[pl.ds(pl.multiple_of(g * gt, gt), gt), :] = (ot / l).T.astype(o_ref.dtype)
        yield

    def body(i, carry):
        groups = [group(i * NA_GROUPS_PER_STEP + u) for u in range(NA_GROUPS_PER_STEP)]
        for _ in range(4):
            for grp in groups:
                next(grp)
        return carry

    lax.fori_loop(0, n_groups // NA_GROUPS_PER_STEP, body, 0, unroll=NA_LOOP_UNROLL)


def _na_latent(p_l, p_c, slabs, slab_layer, weights, b):
    t = p_l.shape[0] // b
    ctx = p_c.shape[0] // b
    hd = NA_HEAD_DIM
    n_steps = NA_HEADS * b
    w_specs_in, w_specs_out, w_shapes = [], [], []
    for w, layer in weights:
        rows, cols = w.shape[1] // n_steps, w.shape[2]
        assert w.shape[1] % n_steps == 0 and rows % BF16_SUBLANES == 0
        w_specs_in.append(pl.BlockSpec((None, rows, cols), lambda h, i, layer=layer: (layer, h * b + i, 0)))
        w_specs_out.append(pl.BlockSpec((None, rows, cols), lambda h, i: (0, h * b + i, 0)))
        w_shapes.append(jax.ShapeDtypeStruct((1,) + w.shape[1:], BF16))
    oa, *w_bf16 = pl.pallas_call(
        functools.partial(_na_kernel, n_weights=len(weights), slab_index=_na_slab_index(t // GRID_W)),
        grid=(NA_HEADS, b),
        in_specs=[
            pl.BlockSpec((t, hd), lambda h, i: (i, _QA + h)),
            pl.BlockSpec((t, hd), lambda h, i: (i, _KA + h)),
            pl.BlockSpec((t, hd), lambda h, i: (i, _VA + h)),
            pl.BlockSpec((ctx, hd), lambda h, i: (i, _KA + h)),
            pl.BlockSpec((ctx, hd), lambda h, i: (i, _VA + h)),
            pl.BlockSpec((1,) + slabs.shape[1:], lambda h, i: (slab_layer * NA_HEADS + h, 0, 0, 0)),
        ] + w_specs_in,
        out_specs=[pl.BlockSpec((t, hd), lambda h, i: (i, h))] + w_specs_out,
        out_shape=[jax.ShapeDtypeStruct((b * t, NA_WIDTH), BF16)] + w_shapes,
        scratch_shapes=[
            pltpu.VMEM((t // NA_GROUP_TOK, hd, NA_GROUP_TOK), BF16),
            pltpu.VMEM((3, NA_WIN_BLOCKS * NA_GROUP_TOK, NA_GROUP_TOK), F32),
        ],
        compiler_params=_cparams(("arbitrary", "arbitrary")),
        name="na_latent",
    )(p_l, p_l, p_l, p_c, p_c, slabs, *(w for w, _ in weights))
    return oa, w_bf16


def _ctx_attn_kernel(q_ref, k_ref, v_ref, o_ref):
    for h in range(NA_HEADS):
        hs = slice(h * NA_HEAD_DIM, (h + 1) * NA_HEAD_DIM)
        s = lax.dot_general(q_ref[:, hs], k_ref[:, hs], (((1,), (1,)), ((), ())), preferred_element_type=F32)
        p = jnp.exp2(s - jnp.max(s, axis=-1, keepdims=True))
        l = jnp.sum(p, axis=-1, keepdims=True)
        o = jnp.dot(p.astype(BF16), v_ref[:, hs], preferred_element_type=F32)
        o_ref[:, hs] = (o / l).astype(o_ref.dtype)


def _ctx_attention(p_c, b):
    ctx = p_c.shape[0] // b
    return pl.pallas_call(
        _ctx_attn_kernel,
        grid=(b,),
        in_specs=[
            pl.BlockSpec((ctx, NA_WIDTH), lambda i: (i, 0)),
            pl.BlockSpec((ctx, NA_WIDTH), lambda i: (i, 1)),
            pl.BlockSpec((ctx, NA_WIDTH), lambda i: (i, 2)),
        ],
        out_specs=pl.BlockSpec((ctx, NA_WIDTH), lambda i: (i, 0)),
        out_shape=jax.ShapeDtypeStruct((b * ctx, NA_WIDTH), BF16),
        compiler_params=_cparams(("arbitrary",)),
        name="ctx_attention",
    )(p_c, p_c, p_c)


def _gm_block(u_ref, v_ref, lnw_ref, ws_ref, bs_ref, o_ref):
    for ck in range(u_ref.shape[0] // GM_CHUNK):
        rows = slice(ck * GM_CHUNK, (ck + 1) * GM_CHUNK)
        uf = _gelu_tanh(u_ref[rows, :])
        vf = _gelu_tanh(v_ref[rows, :])
        for g in range(GM_GROUPS):
            sl = slice(g * GM_DIM, (g + 1) * GM_DIM)
            vg = vf[:, sl]
            mu = jnp.mean(vg, axis=-1, keepdims=True)
            dv = vg - mu
            var = jnp.mean(dv * dv, axis=-1, keepdims=True)
            vn = dv * lax.rsqrt(var + EPS) * lnw_ref[:, sl]
            mixed = jnp.dot(ws_ref[g].astype(BF16), vn.astype(BF16), preferred_element_type=F32) + bs_ref[g]
            o_ref[rows, sl] = (uf[:, sl] * mixed).astype(o_ref.dtype)


HG_PAIR = 2
HG_LEVELS = (32, 16, 8)
HG_DIAG = 8
HG_PREP_ROWS = 256
HG_SCAN_UNROLL = 8


def _split3_dot(tri, g):
    g0 = g.astype(BF16)
    r1 = g - g0.astype(F32)
    g1 = r1.astype(BF16)
    g2 = (r1 - g1.astype(F32)).astype(BF16)
    dot = functools.partial(jnp.dot, preferred_element_type=F32)
    return dot(tri, g0) + dot(tri, g1) + dot(tri, g2)


def _hg_kernel(ql_ref, ffl_ref, fbl_ref, il_ref,
               qc_ref, ffc_ref, fbc_ref, ic_ref,
               lbp_ref, tri_ref, code_ref, lsum_ref, ol_ref, oc_ref,
               qh_s, kf_s, kb_s, ef_s, eb_s, of_s, ob_s, st_s):
    c = HG_CHUNK
    hd = HG_DIM
    n_ctx = qc_ref.shape[0]
    n_lat = ql_ref.shape[0]
    scale = hd ** -0.5
    nt = (((1,), (1,)), ((), ()))
    tn = (((0,), (0,)), ((), ()))

    def prep(q_ref, ff_ref, fb_ref, base, n):
        step = HG_PREP_ROWS
        for t0 in range(0, n, step):
            src = slice(t0, t0 + step)
            dst = slice(base + t0, base + t0 + step)
            qh = _silu(q_ref[src, :]) * scale
            for h in range(HG_PAIR):
                qh_s[h, dst, :] = qh[:, h * hd:(h + 1) * hd]
            for d, (f_ref, k_s, e_s) in enumerate(((ff_ref, kf_s, ef_s), (fb_ref, kb_s, eb_s))):
                x = f_ref[src, :]
                sp = jnp.maximum(-x, 0.0) + jnp.log(1.0 + jnp.exp(-jnp.abs(x)))
                log_lb = lbp_ref[d, 0:1, :]
                y = lbp_ref[d, 1:2, :] - sp
                mx = jnp.maximum(log_lb, y)
                log_f = mx + jnp.log(1.0 + jnp.exp(-jnp.abs(log_lb - y)))
                one_minus_f = lbp_ref[d, 2:3, :] * jnp.exp(-(sp + x))
                e = _split3_dot(tri_ref[d], log_f) * LOG2E
                for h in range(HG_PAIR):
                    e_s[h, dst, :] = e[:, h * hd:(h + 1) * hd]
                    k_s[h, dst, :] = one_minus_f[:, h * hd:(h + 1) * hd]

    prep(qc_ref, ffc_ref, fbc_ref, 0, n_ctx)
    prep(ql_ref, ffl_ref, fbl_ref, n_ctx, n_lat)

    diag_code = len(HG_LEVELS) + 1

    st_s[...] = jnp.zeros_like(st_s)

    def chunk(off, v, h, d, out):
        fwd = d == 0
        k_s, e_s = (kf_s, ef_s) if fwd else (kb_s, eb_s)
        q = qh_s[h, pl.ds(off, c), :]
        k = k_s[h, pl.ds(off, c), :]
        e = e_s[h, pl.ds(off, c), :]
        code = code_ref[d]
        prods = []
        for i in range(c // HG_DIAG):
            bs = slice(i * HG_DIAG, (i + 1) * HG_DIAG)
            row_prods = []
            for s in range(HG_DIAG):
                kr = k_s[h, pl.ds(off + i * HG_DIAG + s, 1), :]
                er = e_s[h, pl.ds(off + i * HG_DIAG + s, 1), :]
                decay = jnp.exp2(jnp.minimum(e[bs] - er, 0.0))
                row_prods.append((q[bs] * kr * decay).astype(BF16))
            prods.append(jnp.concatenate(row_prods, axis=1))
        diag = jnp.dot(jnp.concatenate(prods, axis=0), lsum_ref[...], preferred_element_type=F32)
        yield
        att = None
        for li, w in enumerate(HG_LEVELS):
            zeros = jnp.zeros((w, hd), F32)
            q_parts, k_parts = [], []
            for a in range(0, c, 2 * w):
                lo, hi = slice(a, a + w), slice(a + w, a + 2 * w)
                if fwd:
                    ref = e_s[h, pl.ds(off + a + w - 1, 1), :]
                    q_parts += [zeros, q[hi] * jnp.exp2(e[hi] - ref)]
                    k_parts += [k[lo] * jnp.exp2(ref - e[lo]), zeros]
                else:
                    ref = e_s[h, pl.ds(off + a + w, 1), :]
                    q_parts += [q[lo] * jnp.exp2(e[lo] - ref), zeros]
                    k_parts += [zeros, k[hi] * jnp.exp2(ref - e[hi])]
            qs = jnp.concatenate(q_parts, axis=0).astype(BF16)
            ks = jnp.concatenate(k_parts, axis=0).astype(BF16)
            a_w = lax.dot_general(qs, ks, nt, preferred_element_type=F32)
            att = a_w if att is None else jnp.where(code == li + 1, a_w, att)
            yield
        e_end = e_s[h, pl.ds(off + (c - 1 if fwd else 0), 1), :]
        st = st_s[h, d]
        qi = (q * jnp.exp2(e)).astype(BF16)
        ki = (k * jnp.exp2(e_end - e)).astype(BF16)
        o = lax.dot_general(qi, st.astype(BF16), nt, preferred_element_type=F32)
        yield
        st_new = st * jnp.exp2(e_end) + lax.dot_general(v, ki, tn, preferred_element_type=F32)
        yield
        att = jnp.where(code == diag_code, diag, att)
        out.append((o + jnp.dot(att.astype(BF16), v, preferred_element_type=F32), st_new))
        yield

    n_stages = len(HG_LEVELS) + 4

    def scan(v_ref, base, n):
        def body(i, carry):
            rf = pl.multiple_of(i * c, c)
            rb = pl.multiple_of((n - 1 - i) * c, c)
            vf = v_ref[pl.ds(rf, c), :].astype(BF16)
            vb = v_ref[pl.ds(rb, c), :].astype(BF16)
            chains = []
            for h in range(HG_PAIR):
                hs = slice(h * hd, (h + 1) * hd)
                for d, r, v in ((0, rf, vf), (1, rb, vb)):
                    out = []
                    chains.append((h, d, r, out, chunk(base + r, v[:, hs], h, d, out)))
            for _ in range(n_stages):
                for chain in chains:
                    next(chain[-1])
            for h, d, r, out, _ in chains:
                o, st_new = out[0]
                (of_s if d == 0 else ob_s)[h, pl.ds(base + r, c), :] = o
                st_s[h, d] = st_new
            return carry

        lax.fori_loop(0, n, body, 0, unroll=HG_SCAN_UNROLL)

    scan(ic_ref, 0, n_ctx // c)
    scan(il_ref, n_ctx, n_lat // c)

    def finish(o_ref, base, n):
        step = 256
        for t0 in range(0, n, step):
            src = slice(t0, t0 + step)
            dst = slice(base + t0, base + t0 + step)
            for h in range(HG_PAIR):
                o_ref[src, h * hd:(h + 1) * hd] = of_s[h, dst, :] + ob_s[h, dst, :]

    finish(oc_ref, 0, n_ctx)
    finish(ol_ref, n_ctx, n_lat)


def _hg_constants():
    c = HG_CHUNK
    row, col = np.arange(c)[:, None], np.arange(c)[None, :]
    blocks = np.eye(HG_PREP_ROWS // c)
    tri = np.stack([np.kron(blocks, col <= row), np.kron(blocks, col >= row)]).astype(np.float32)
    same8 = (row // HG_DIAG) == (col // HG_DIAG)
    diag_code = len(HG_LEVELS) + 1
    code_f = np.where(same8 & (col <= row), diag_code, 0)
    code_b = np.where(same8 & (col >= row), diag_code, 0)
    for li, w in enumerate(HG_LEVELS):
        same = (row // (2 * w)) == (col // (2 * w))
        t_hi, s_hi = (row % (2 * w)) >= w, (col % (2 * w)) >= w
        code_f = np.where(same & t_hi & ~s_hi, li + 1, code_f)
        code_b = np.where(same & ~t_hi & s_hi, li + 1, code_b)
    codes = np.stack([code_f, code_b]).astype(np.int32)
    lane_sum = (np.arange(HG_DIAG * HG_DIM)[:, None] // HG_DIM == col % HG_DIAG).astype(np.float32)
    return jnp.asarray(tri, BF16), jnp.asarray(codes), jnp.asarray(lane_sum, BF16)


def _hgrn2(p_l, p_c, lbp, b):
    tri, codes, lane_sum = _hg_constants()
    t = p_l.shape[0] // b
    ctx = p_c.shape[0] // b
    hd = HG_DIM
    pw = HG_PAIR * hd
    n = t + ctx
    first = _HG0 * LANE // pw

    def col(stream):
        return lambda i, j: (i, first + stream * (HG_HEADS // HG_PAIR) + j)

    lat_specs = [pl.BlockSpec((t, pw), col(s)) for s in range(4)]
    ctx_specs = [pl.BlockSpec((ctx, pw), col(s)) for s in range(4)]
    big = lambda: pltpu.VMEM((HG_PAIR, n, hd), F32)
    return pl.pallas_call(
        _hg_kernel,
        grid=(b, HG_HEADS // HG_PAIR),
        in_specs=lat_specs + ctx_specs + [
            pl.BlockSpec((2, 3, pw), lambda i, j: (0, 0, j)),
            pl.BlockSpec(tri.shape, lambda i, j: (0, 0, 0)),
            pl.BlockSpec(codes.shape, lambda i, j: (0, 0, 0)),
            pl.BlockSpec(lane_sum.shape, lambda i, j: (0, 0)),
        ],
        out_specs=[
            pl.BlockSpec((t, pw), lambda i, j: (i, j)),
            pl.BlockSpec((ctx, pw), lambda i, j: (i, j)),
        ],
        out_shape=[
            jax.ShapeDtypeStruct((b * t, HG_WIDTH), F32),
            jax.ShapeDtypeStruct((b * ctx, HG_WIDTH), F32),
        ],
        scratch_shapes=[big() for _ in range(7)] + [pltpu.VMEM((HG_PAIR, 2, hd, hd), F32)],
        compiler_params=_cparams(("arbitrary", "arbitrary")),
        name="hgrn2",
    )(*([p_l] * 4 + [p_c] * 4 + [lbp, tri, codes, lane_sum]))


def _outproj_kernel(oa_ref, os_ref, hg_ref, hnw_ref, u_ref, v_ref, lnw_ref, ws_ref, bs_ref,
                    wa_ref, wb_ref, wc_ref, x_ref, g_ref, o_ref, ob_ref, oc_ref):
    dot = functools.partial(jnp.dot, preferred_element_type=F32)
    y = dot(oa_ref[...], wa_ref[...])
    for h in range(HG_HEADS):
        hs = slice(h * HG_DIM, (h + 1) * HG_DIM)
        o = os_ref[:, hs]
        o = o * lax.rsqrt(jnp.mean(o * o, axis=-1, keepdims=True) + EPS) * hnw_ref[...]
        ob_ref[:, hs] = (o * _silu(hg_ref[:, hs])).astype(ob_ref.dtype)
    y = y + dot(ob_ref[...], wb_ref[...])
    _gm_block(u_ref, v_ref, lnw_ref, ws_ref, bs_ref, oc_ref)
    y = y + dot(oc_ref[...], wc_ref[...])
    o_ref[...] = x_ref[...] + g_ref[0] * y


def _outproj(oa, osum, p2, hg_norm_w, gm_ln_w, gm_ws, gm_bs, w, w_layer, x2d, mod, layer, rows_per_mod,
             mod_row0, tm):
    m, d = x2d.shape
    mrow = lambda i: layer * MOD_ROWS + mod_row0 + (i * tm) // rows_per_mod
    nb = NA_WIDTH // HG_WIDTH
    assert tm % GM_CHUNK == 0
    return pl.pallas_call(
        _outproj_kernel,
        grid=(m // tm,),
        in_specs=[
            pl.BlockSpec((tm, NA_WIDTH), lambda i: (i, 0)),
            pl.BlockSpec((tm, HG_WIDTH), lambda i: (i, 0)),
            pl.BlockSpec((tm, HG_WIDTH), lambda i: (i, _HG_G)),
            pl.BlockSpec((1, HG_DIM), lambda i: (0, 0)),
            pl.BlockSpec((tm, GM_WIDTH), lambda i: (i, _GM_U)),
            pl.BlockSpec((tm, GM_WIDTH), lambda i: (i, _GM_V)),
            pl.BlockSpec((1, GM_WIDTH), lambda i: (0, 0)),
            pl.BlockSpec((GM_GROUPS, GM_CHUNK, GM_CHUNK), lambda i: (0, 0, 0)),
            pl.BlockSpec((GM_GROUPS, GM_CHUNK, 1), lambda i: (0, 0, 0)),
            pl.BlockSpec((None, NA_WIDTH, d), lambda i: (w_layer, 0, 0)),
            pl.BlockSpec((None, HG_WIDTH, d), lambda i: (w_layer, nb, 0)),
            pl.BlockSpec((None, GM_WIDTH, d), lambda i: (w_layer, nb + 1, 0)),
            pl.BlockSpec((tm, d), lambda i: (i, 0)),
            pl.BlockSpec((1, 1, d), lambda i: (mrow(i), 0, 2)),
        ],
        out_specs=pl.BlockSpec((tm, d), lambda i: (i, 0)),
        out_shape=jax.ShapeDtypeStruct((m, d), F32),
        scratch_shapes=[pltpu.VMEM((tm, HG_WIDTH), BF16), pltpu.VMEM((tm, GM_WIDTH), BF16)],
        compiler_params=_cparams(("arbitrary",)),
        name="outproj",
    )(oa, osum, p2, hg_norm_w.reshape(1, HG_DIM), p2, p2, gm_ln_w.reshape(1, GM_WIDTH), gm_ws,
      gm_bs.reshape(GM_GROUPS, GM_CHUNK, 1), w, w, w, x2d, mod)


def _mlp_kernel(x_ref, nw_ref, sh_ref, sc_ref, g_ref, w1_ref, w2_ref, fnw_ref, o_ref, h_ref, *, final_norm):
    j = pl.program_id(1)

    @pl.when(j == 0)
    def _():
        _norm_modulate_store(x_ref, nw_ref, sh_ref, sc_ref, h_ref)
        o_ref[...] = jnp.zeros_like(o_ref)

    h = h_ref[...]
    for c0 in range(0, w1_ref.shape[1], MLP_TH_INNER):
        cols = slice(c0, c0 + MLP_TH_INNER)
        a = jnp.maximum(jnp.dot(h, w1_ref[:, cols], preferred_element_type=F32), 0.0)
        o_ref[...] += jnp.dot((a * a).astype(BF16), w2_ref[cols, :], preferred_element_type=F32)

    @pl.when(j == pl.num_programs(1) - 1)
    def _():
        y = x_ref[...] + g_ref[0] * o_ref[...]
        if final_norm:
            y = y * lax.rsqrt(jnp.mean(y * y, axis=-1, keepdims=True) + EPS) * fnw_ref[...]
        o_ref[...] = y


def _mlp(x2d, nw, mod, w1, w2, w_layer, fnw, layer, rows_per_mod, mod_row0, tm, final_norm):
    m, d = x2d.shape
    hid = w1.shape[2]
    th = MLP_TH
    mrow = lambda i: layer * MOD_ROWS + mod_row0 + (i * tm) // rows_per_mod
    return pl.pallas_call(
        functools.partial(_mlp_kernel, final_norm=final_norm),
        grid=(m // tm, hid // th),
        in_specs=[
            pl.BlockSpec((tm, d), lambda i, j: (i, 0)),
            pl.BlockSpec((None, 1, d), lambda i, j: (layer, 0, 0)),
            pl.BlockSpec((1, 1, d), lambda i, j: (mrow(i), 0, 3)),
            pl.BlockSpec((1, 1, d), lambda i, j: (mrow(i), 0, 4)),
            pl.BlockSpec((1, 1, d), lambda i, j: (mrow(i), 0, 5)),
            pl.BlockSpec((None, d, th), lambda i, j: (w_layer, 0, j)),
            pl.BlockSpec((None, th, d), lambda i, j: (w_layer, j, 0)),
            pl.BlockSpec((1, d), lambda i, j: (0, 0)),
        ],
        out_specs=pl.BlockSpec((tm, d), lambda i, j: (i, 0)),
        out_shape=jax.ShapeDtypeStruct((m, d), F32),
        scratch_shapes=[pltpu.VMEM((tm, d), BF16)],
        compiler_params=_cparams(("arbitrary", "arbitrary"), VMEM_LIMIT_MLP_V7X),
        name="mlp",
    )(x2d, nw, mod, mod, mod, w1, w2, fnw)


def kernel(x, c, ctx, c_ctx, ada_w, ada_b, norm1_w, norm2_w, w_in, na_rpb, hg_lb_logits, hg_norm_w,
           gm_ln_w, gm_ws, gm_bs, w_out, mlp_w1, mlp_w2, final_norm_w):
    bsz, seq, d = x.shape
    n_ctx = ctx.shape[1]
    depth = ada_w.shape[0]
    assert bsz < MOD_ROWS and d == D_MODEL and seq % 512 == 0 and n_ctx % 256 == 0
    assert seq // NA_GROUP_TOK >= NA_WIN_BLOCKS and seq // GRID_W >= 2 * NA_WIN_ROWS

    lb = jnp.cumsum(jax.nn.softmax(hg_lb_logits.astype(F32), axis=0), axis=0)
    lb = lb - lb[:1]
    lbp = jnp.stack([jnp.log(lb), jnp.log1p(-lb), 1.0 - lb], axis=2)

    cond = jnp.zeros((MOD_ROWS, d), F32).at[:bsz].set(c).at[bsz].set(c_ctx)
    mod = _ada(cond, ada_w, ada_b)

    w_in_b = w_in[:1].astype(BF16)
    slabs = _na_bias_slabs(na_rpb.reshape((depth * NA_HEADS,) + na_rpb.shape[2:]))
    nw1 = norm1_w.reshape(depth, 1, d)
    nw2 = norm2_w.reshape(depth, 1, d)
    fnw = final_norm_w.reshape(1, d)

    xl = x.reshape(bsz * seq, d)
    xc = ctx.reshape(bsz * n_ctx, d)
    n_c = bsz * n_ctx
    tm_l, tm_c = TM_LATENT, min(TM_CONTEXT, n_c)
    for l in range(depth):
        need_ctx = l < depth - 1
        qkv_l, p_l = _inproj(xl, nw1, mod, w_in_b, l, seq, 0, tm_l)
        qkv_c, p_c = _inproj(xc, nw1, mod, w_in_b, l, n_c, bsz, tm_c)

        to_cast = [(w_out, l), (mlp_w1, l), (mlp_w2, l)] + ([(w_in, l + 1)] if need_ctx else [])
        oa_l, (w_out_b, w1_b, w2_b, *w_in_next) = _na_latent(qkv_l, qkv_c, slabs, l, to_cast, bsz)
        w_in_b = w_in_next[0] if w_in_next else None
        os_l, os_c = _hgrn2(p_l, p_c, lbp[l], bsz)
        mix = (hg_norm_w[l], gm_ln_w[l], gm_ws[l], gm_bs[l])
        xl = _outproj(oa_l, os_l, p_l, *mix, w_out_b, 0, xl, mod, l, seq, 0, tm_l)
        xl = _mlp(xl, nw2, mod, w1_b, w2_b, 0, fnw, l, seq, 0, tm_l, final_norm=not need_ctx)
        if need_ctx:
            oa_c = _ctx_attention(qkv_c, bsz)
            xc = _outproj(oa_c, os_c, p_c, *mix, w_out_b, 0, xc, mod, l, n_c, bsz, tm_c)
            xc = _mlp(xc, nw2, mod, w1_b, w2_b, 0, fnw, l, n_c, bsz, tm_c, final_norm=False)
    return xl.reshape(bsz, seq, d)
```

```python
import functools

import numpy as np
import jax
import jax.numpy as jnp
from jax import lax
from jax.experimental import pallas as pl
from jax.experimental.pallas import tpu as pltpu

F32 = jnp.float32
BF16 = jnp.bfloat16

D_MODEL = 2048
DEPTH = 2
GRID_W = 64
EPS = 1e-6

NA_HEAD_DIM = 128
NA_HEADS = 8
NA_WIDTH = NA_HEADS * NA_HEAD_DIM
NA_WIN_ROWS = 8
NA_WIN_COLS = 16

HG_HEADS = 4
HG_DIM = 128
HG_WIDTH = HG_HEADS * HG_DIM
HG_CHUNK = 64

GM_GROUPS = 4
GM_DIM = 128
GM_WIDTH = GM_GROUPS * GM_DIM
GM_CHUNK = 128

IN_WIDTH = 3 * NA_WIDTH + 5 * HG_WIDTH + 2 * GM_WIDTH
MLP_HIDDEN = 4 * D_MODEL
LANE = 128
LOG2E = 1.4426950408889634
BF16_SUBLANES = 16

QKV_WIDTH = 3 * NA_WIDTH
REST_WIDTH = IN_WIDTH - QKV_WIDTH
REST_MAIN = QKV_WIDTH
REST_TAIL = REST_WIDTH - REST_MAIN
NA_SCORE_SCALE = NA_HEAD_DIM ** -0.5 * LOG2E
_QA, _KA, _VA = 0, NA_HEADS, 2 * NA_HEADS
_HG0 = 0
_GM_U = 5 * HG_WIDTH // GM_WIDTH
_GM_V = _GM_U + 1
_HG_G = 4

VMEM_LIMIT_V7X = 56 * 1024 * 1024
VMEM_LIMIT_MLP_V7X = 60 * 1024 * 1024
TM_LATENT = 512
TM_CONTEXT = 512
MLP_TH = 2048
MLP_TH_INNER = 512
NORM_CHUNK_ROWS = 32
NORM_UNROLL = 4
MOD_ROWS = 16


def _cparams(sem, vmem_limit=VMEM_LIMIT_V7X):
    return pltpu.CompilerParams(dimension_semantics=sem, vmem_limit_bytes=vmem_limit)


def _silu(x):
    h = 0.5 * x
    return h + h * jnp.tanh(h)


def _gelu_tanh(x):
    return 0.5 * x * (1.0 + jnp.tanh(0.7978845608028654 * (x + 0.044715 * (x * x * x))))


def _norm_modulate_store(x_ref, nw_ref, sh_ref, sc_ref, h_ref):
    gain = nw_ref[...] * (1.0 + sc_ref[0])
    shift = sh_ref[0]

    def body(i, carry):
        rows = pl.ds(pl.multiple_of(i * NORM_CHUNK_ROWS, NORM_CHUNK_ROWS), NORM_CHUNK_ROWS)
        x = x_ref[rows, :]
        rs = lax.rsqrt(jnp.mean(x * x, axis=-1, keepdims=True) + EPS)
        h_ref[rows, :] = (x * rs * gain + shift).astype(h_ref.dtype)
        return carry

    lax.fori_loop(0, x_ref.shape[0] // NORM_CHUNK_ROWS, body, 0, unroll=NORM_UNROLL)


def _ada_kernel(c_ref, w_ref, b_ref, o_ref):
    s = _silu(c_ref[...]).astype(BF16)
    o_ref[0] = jnp.dot(s, w_ref[0].astype(BF16), preferred_element_type=F32) + b_ref[0]


def _ada(cond, ada_w, ada_b):
    depth, d, n = ada_w.shape
    tn = 1536
    return pl.pallas_call(
        _ada_kernel,
        grid=(depth, n // tn),
        in_specs=[
            pl.BlockSpec((cond.shape[0], d), lambda l, j: (0, 0)),
            pl.BlockSpec((1, d, tn), lambda l, j: (l, 0, j)),
            pl.BlockSpec((1, 1, tn), lambda l, j: (l, 0, j)),
        ],
        out_specs=pl.BlockSpec((1, cond.shape[0], tn), lambda l, j: (l, 0, j)),
        out_shape=jax.ShapeDtypeStruct((depth, cond.shape[0], n), F32),
        compiler_params=_cparams(("arbitrary", "arbitrary")),
        name="ada",
    )(cond, ada_w, ada_b.reshape(depth, 1, n))


def _inproj_qkv_kernel(x_ref, nw_ref, sh_ref, sc_ref, w_ref, o_ref, h_ref):
    _norm_modulate_store(x_ref, nw_ref, sh_ref, sc_ref, h_ref)
    h = h_ref[...]
    q = jnp.dot(h, w_ref[:, :NA_WIDTH], preferred_element_type=F32)
    o_ref[:, :NA_WIDTH] = (q * NA_SCORE_SCALE).astype(o_ref.dtype)
    o_ref[:, NA_WIDTH:] = jnp.dot(h, w_ref[:, NA_WIDTH:], preferred_element_type=F32).astype(o_ref.dtype)


def _inproj_rest_kernel(h_ref, wa_ref, wb_ref, o_ref):
    h = h_ref[...]
    o_ref[:, :REST_MAIN] = jnp.dot(h, wa_ref[...], preferred_element_type=F32)
    o_ref[:, REST_MAIN:] = jnp.dot(h, wb_ref[...], preferred_element_type=F32)


def _inproj(x2d, nw, mod, w, layer, rows_per_mod, mod_row0, tm):
    m, d = x2d.shape
    mrow = lambda i: layer * MOD_ROWS + mod_row0 + (i * tm) // rows_per_mod
    assert w.shape[0] == 1 and w.shape[2] == QKV_WIDTH + REST_MAIN + REST_TAIL
    assert (QKV_WIDTH + REST_MAIN) % REST_TAIL == 0
    qkv, h = pl.pallas_call(
        _inproj_qkv_kernel,
        grid=(m // tm,),
        in_specs=[
            pl.BlockSpec((tm, d), lambda i: (i, 0)),
            pl.BlockSpec((None, 1, d), lambda i: (layer, 0, 0)),
            pl.BlockSpec((1, 1, d), lambda i: (mrow(i), 0, 0)),
            pl.BlockSpec((1, 1, d), lambda i: (mrow(i), 0, 1)),
            pl.BlockSpec((None, d, QKV_WIDTH), lambda i: (0, 0, 0)),
        ],
        out_specs=[pl.BlockSpec((tm, QKV_WIDTH), lambda i: (i, 0)), pl.BlockSpec((tm, d), lambda i: (i, 0))],
        out_shape=[jax.ShapeDtypeStruct((m, QKV_WIDTH), BF16), jax.ShapeDtypeStruct((m, d), BF16)],
        compiler_params=_cparams(("arbitrary",)),
        name="inproj_qkv",
    )(x2d, nw, mod, mod, w)
    rest = pl.pallas_call(
        _inproj_rest_kernel,
        grid=(m // tm,),
        in_specs=[
            pl.BlockSpec((tm, d), lambda i: (i, 0)),
            pl.BlockSpec((None, d, REST_MAIN), lambda i: (0, 0, 1)),
            pl.BlockSpec((None, d, REST_TAIL), lambda i: (0, 0, (QKV_WIDTH + REST_MAIN) // REST_TAIL)),
        ],
        out_specs=pl.BlockSpec((tm, REST_WIDTH), lambda i: (i, 0)),
        out_shape=jax.ShapeDtypeStruct((m, REST_WIDTH), F32),
        compiler_params=_cparams(("arbitrary",)),
        name="inproj_rest",
    )(h, w, w)
    return qkv, rest


NA_GROUP_ROWS = 4
NA_GROUP_TOK = NA_GROUP_ROWS * GRID_W
NA_WIN_BLOCKS = 3
NA_GROUPS_PER_STEP = 8
NA_LOOP_UNROLL = 1


NA_MASKED_SLAB = 2 * NA_WIN_ROWS - 1


def _na_bias_slabs(rpb):
    h = rpb.shape[0]
    kcol = np.arange(GRID_W)[:, None]
    qcol = np.arange(GRID_W)[None, :]
    wstart = np.clip(qcol - NA_WIN_COLS // 2, 0, GRID_W - NA_WIN_COLS)
    valid_col = (kcol >= wstart) & (kcol < wstart + NA_WIN_COLS)
    pad = GRID_W - NA_WIN_COLS
    padded = jnp.pad(rpb.astype(F32), ((0, 0), (0, 0), (pad, pad)))
    m = jnp.tile(padded, (1, 1, GRID_W + 1))[:, :, :GRID_W * 2 * GRID_W]
    m = m.reshape(h, rpb.shape[1], GRID_W, 2 * GRID_W)[..., :GRID_W]
    slabs = jnp.where(valid_col, m[..., ::-1] * LOG2E, -1e30)
    slabs = jnp.concatenate([slabs, jnp.full((h, 1, GRID_W, GRID_W), -1e30, F32)], axis=1)
    return jnp.concatenate([slabs, slabs], axis=-1)


def _na_slab_index(rows):
    n_groups = rows // NA_GROUP_ROWS
    krel = np.arange(NA_WIN_BLOCKS * NA_GROUP_ROWS)[:, None]
    qrel = np.arange(NA_GROUP_ROWS)[None, :]
    idx = []
    for g in (0, 1, n_groups - 1):
        krow = int(np.clip(g - 1, 0, n_groups - NA_WIN_BLOCKS)) * NA_GROUP_ROWS + krel
        qrow = g * NA_GROUP_ROWS + qrel
        r0 = np.clip(qrow - NA_WIN_ROWS // 2, 0, rows - NA_WIN_ROWS)
        in_window = (krow >= r0) & (krow < r0 + NA_WIN_ROWS)
        idx.append(np.where(in_window, krow - qrow + NA_WIN_ROWS - 1, NA_MASKED_SLAB))
    return np.stack(idx).tolist()


def _na_kernel(q_ref, k_ref, v_ref, kc_ref, vc_ref, slab_ref, *refs, n_weights, slab_index):
    w_refs, o_ref, w_bf16_refs = refs[:n_weights], refs[n_weights], refs[n_weights + 1:2 * n_weights + 1]
    vt_ref, bias_ref = refs[2 * n_weights + 1:]
    n_groups = q_ref.shape[0] // NA_GROUP_TOK
    gt = NA_GROUP_TOK
    nt = (((1,), (1,)), ((), ()))
    dot = functools.partial(jnp.dot, preferred_element_type=F32)

    for src_ref, dst_ref in zip(w_refs, w_bf16_refs):
        dst_ref[...] = src_ref[...].astype(BF16)

    @pl.when(pl.program_id(1) == 0)
    def _():
        left = lax.broadcasted_iota(jnp.int32, (GRID_W, 2 * GRID_W), 1) < GRID_W
        for kind, per_key_row in enumerate(slab_index):
            for j, per_query_row in enumerate(per_key_row):
                for a in range(0, NA_GROUP_ROWS, 2):
                    tile = jnp.where(left, slab_ref[0, per_query_row[a]], slab_ref[0, per_query_row[a + 1]])
                    bias_ref[kind, j * GRID_W:(j + 1) * GRID_W, a * GRID_W:(a + 2) * GRID_W] = tile

    for b in range(n_groups):
        vt_ref[b] = v_ref[b * gt:(b + 1) * gt, :].T
    kc = kc_ref[...]
    vct = vc_ref[...].T

    def group(g):
        blk = jnp.clip(g - 1, 0, n_groups - NA_WIN_BLOCKS)
        kind = jnp.where(g == 0, 0, jnp.where(g == n_groups - 1, 2, 1))
        qg = q_ref[pl.ds(pl.multiple_of(g * gt, gt), gt), :]
        kw = k_ref[pl.ds(pl.multiple_of(blk * gt, gt), NA_WIN_BLOCKS * gt), :]
        s_lat = lax.dot_general(kw, qg, nt, preferred_element_type=F32) + bias_ref[kind]
        s_ctx = lax.dot_general(kc, qg, nt, preferred_element_type=F32)
        yield
        m = jnp.maximum(jnp.max(s_lat, axis=0, keepdims=True), jnp.max(s_ctx, axis=0, keepdims=True))
        p_lat = jnp.exp2(s_lat - m)
        p_ctx = jnp.exp2(s_ctx - m)
        l = jnp.sum(p_lat, axis=0, keepdims=True) + jnp.sum(p_ctx, axis=0, keepdims=True)
        yield
        ot = dot(vct, p_ctx.astype(BF16))
        for j in range(NA_WIN_BLOCKS):
            ot = ot + dot(vt_ref[blk + j], p_lat[j * gt:(j + 1) * gt, :].astype(BF16))
        yield
        o_ref[pl.ds(pl.multiple_of(g * gt, gt), gt), :] = (ot / l).T.astype(o_ref.dtype)
        yield

    def body(i, carry):
        groups = [group(i * NA_GROUPS_PER_STEP + u) for u in range(NA_GROUPS_PER_STEP)]
        for _ in range(4):
            for grp in groups:
                next(grp)
        return carry

    lax.fori_loop(0, n_groups // NA_GROUPS_PER_STEP, body, 0, unroll=NA_LOOP_UNROLL)


def _na_latent(p_l, p_c, slabs, weights, b):
    t = p_l.shape[0] // b
    ctx = p_c.shape[0] // b
    hd = NA_HEAD_DIM
    n_steps = NA_HEADS * b
    w_specs_in, w_specs_out, w_shapes = [], [], []
    for w, layer in weights:
        rows, cols = w.shape[1] // n_steps, w.shape[2]
        assert w.shape[1] % n_steps == 0 and rows % BF16_SUBLANES == 0
        w_specs_in.append(pl.BlockSpec((None, rows, cols), lambda h, i, layer=layer: (layer, h * b + i, 0)))
        w_specs_out.append(pl.BlockSpec((None, rows, cols), lambda h, i: (0, h * b + i, 0)))
        w_shapes.append(jax.ShapeDtypeStruct((1,) + w.shape[1:], BF16))
    oa, *w_bf16 = pl.pallas_call(
        functools.partial(_na_kernel, n_weights=len(weights), slab_index=_na_slab_index(t // GRID_W)),
        grid=(NA_HEADS, b),
        in_specs=[
            pl.BlockSpec((t, hd), lambda h, i: (i, _QA + h)),
            pl.BlockSpec((t, hd), lambda h, i: (i, _KA + h)),
            pl.BlockSpec((t, hd), lambda h, i: (i, _VA + h)),
            pl.BlockSpec((ctx, hd), lambda h, i: (i, _KA + h)),
            pl.BlockSpec((ctx, hd), lambda h, i: (i, _VA + h)),
            pl.BlockSpec((1,) + slabs.shape[1:], lambda h, i: (h, 0, 0, 0)),
        ] + w_specs_in,
        out_specs=[pl.BlockSpec((t, hd), lambda h, i: (i, h))] + w_specs_out,
        out_shape=[jax.ShapeDtypeStruct((b * t, NA_WIDTH), BF16)] + w_shapes,
        scratch_shapes=[
            pltpu.VMEM((t // NA_GROUP_TOK, hd, NA_GROUP_TOK), BF16),
            pltpu.VMEM((3, NA_WIN_BLOCKS * NA_GROUP_TOK, NA_GROUP_TOK), F32),
        ],
        compiler_params=_cparams(("arbitrary", "arbitrary")),
        name="na_latent",
    )(p_l, p_l, p_l, p_c, p_c, slabs, *(w for w, _ in weights))
    return oa, w_bf16


def _ctx_attn_kernel(q_ref, k_ref, v_ref, o_ref):
    for h in range(NA_HEADS):
        hs = slice(h * NA_HEAD_DIM, (h + 1) * NA_HEAD_DIM)
        s = lax.dot_general(q_ref[:, hs], k_ref[:, hs], (((1,), (1,)), ((), ())), preferred_element_type=F32)
        p = jnp.exp2(s - jnp.max(s, axis=-1, keepdims=True))
        l = jnp.sum(p, axis=-1, keepdims=True)
        o = jnp.dot(p.astype(BF16), v_ref[:, hs], preferred_element_type=F32)
        o_ref[:, hs] = (o / l).astype(o_ref.dtype)


def _ctx_attention(p_c, b):
    ctx = p_c.shape[0] // b
    return pl.pallas_call(
        _ctx_attn_kernel,
        grid=(b,),
        in_specs=[
            pl.BlockSpec((ctx, NA_WIDTH), lambda i: (i, 0)),
            pl.BlockSpec((ctx, NA_WIDTH), lambda i: (i, 1)),
            pl.BlockSpec((ctx, NA_WIDTH), lambda i: (i, 2)),
        ],
        out_specs=pl.BlockSpec((ctx, NA_WIDTH), lambda i: (i, 0)),
        out_shape=jax.ShapeDtypeStruct((b * ctx, NA_WIDTH), BF16),
        compiler_params=_cparams(("arbitrary",)),
        name="ctx_attention",
    )(p_c, p_c, p_c)


def _gm_block(u_ref, v_ref, lnw_ref, ws_ref, bs_ref, o_ref):
    for ck in range(u_ref.shape[0] // GM_CHUNK):
        rows = slice(ck * GM_CHUNK, (ck + 1) * GM_CHUNK)
        uf = _gelu_tanh(u_ref[rows, :])
        vf = _gelu_tanh(v_ref[rows, :])
        for g in range(GM_GROUPS):
            sl = slice(g * GM_DIM, (g + 1) * GM_DIM)
            vg = vf[:, sl]
            mu = jnp.mean(vg, axis=-1, keepdims=True)
            dv = vg - mu
            var = jnp.mean(dv * dv, axis=-1, keepdims=True)
            vn = dv * lax.rsqrt(var + EPS) * lnw_ref[:, sl]
            mixed = jnp.dot(ws_ref[g].astype(BF16), vn.astype(BF16), preferred_element_type=F32) + bs_ref[g]
            o_ref[rows, sl] = (uf[:, sl] * mixed).astype(o_ref.dtype)


HG_PAIR = 2
HG_LEVELS = (32, 16, 8)
HG_DIAG = 8
HG_PREP_ROWS = 256
HG_SCAN_UNROLL = 8


def _split3_dot(tri, g):
    g0 = g.astype(BF16)
    r1 = g - g0.astype(F32)
    g1 = r1.astype(BF16)
    g2 = (r1 - g1.astype(F32)).astype(BF16)
    dot = functools.partial(jnp.dot, preferred_element_type=F32)
    return dot(tri, g0) + dot(tri, g1) + dot(tri, g2)


def _hg_kernel(ql_ref, ffl_ref, fbl_ref, il_ref,
               qc_ref, ffc_ref, fbc_ref, ic_ref,
               lbp_ref, tri_ref, code_ref, lsum_ref, ol_ref, oc_ref,
               qh_s, kf_s, kb_s, ef_s, eb_s, of_s, ob_s, st_s):
    c = HG_CHUNK
    hd = HG_DIM
    n_ctx = qc_ref.shape[0]
    n_lat = ql_ref.shape[0]
    scale = hd ** -0.5
    nt = (((1,), (1,)), ((), ()))
    tn = (((0,), (0,)), ((), ()))

    def prep(q_ref, ff_ref, fb_ref, base, n):
        step = HG_PREP_ROWS
        for t0 in range(0, n, step):
            src = slice(t0, t0 + step)
            dst = slice(base + t0, base + t0 + step)
            qh = _silu(q_ref[src, :]) * scale
            for h in range(HG_PAIR):
                qh_s[h, dst, :] = qh[:, h * hd:(h + 1) * hd]
            for d, (f_ref, k_s, e_s) in enumerate(((ff_ref, kf_s, ef_s), (fb_ref, kb_s, eb_s))):
                x = f_ref[src, :]
                sp = jnp.maximum(-x, 0.0) + jnp.log(1.0 + jnp.exp(-jnp.abs(x)))
                lb = lbp_ref[d, 0:1, :]
                y = lbp_ref[d, 1:2, :] - sp
                ey = jnp.exp(y)
                log_f = jnp.where(lb > 0.0, jnp.log(lb + ey), y)
                one_minus_f = jnp.maximum(lbp_ref[d, 2:3, :] - ey, 0.0)
                e = _split3_dot(tri_ref[d], log_f) * LOG2E
                for h in range(HG_PAIR):
                    e_s[h, dst, :] = e[:, h * hd:(h + 1) * hd]
                    k_s[h, dst, :] = one_minus_f[:, h * hd:(h + 1) * hd]

    prep(qc_ref, ffc_ref, fbc_ref, 0, n_ctx)
    prep(ql_ref, ffl_ref, fbl_ref, n_ctx, n_lat)

    diag_code = len(HG_LEVELS) + 1

    st_s[...] = jnp.zeros_like(st_s)

    def chunk(off, v, h, d, out):
        fwd = d == 0
        k_s, e_s = (kf_s, ef_s) if fwd else (kb_s, eb_s)
        q = qh_s[h, pl.ds(off, c), :]
        k = k_s[h, pl.ds(off, c), :]
        e = e_s[h, pl.ds(off, c), :]
        code = code_ref[d]
        prods = []
        for i in range(c // HG_DIAG):
            bs = slice(i * HG_DIAG, (i + 1) * HG_DIAG)
            row_prods = []
            for s in range(HG_DIAG):
                kr = k_s[h, pl.ds(off + i * HG_DIAG + s, 1), :]
                er = e_s[h, pl.ds(off + i * HG_DIAG + s, 1), :]
                decay = jnp.exp2(jnp.minimum(e[bs] - er, 0.0))
                row_prods.append((q[bs] * kr * decay).astype(BF16))
            prods.append(jnp.concatenate(row_prods, axis=1))
        diag = jnp.dot(jnp.concatenate(prods, axis=0), lsum_ref[...], preferred_element_type=F32)
        yield
        att = None
        for li, w in enumerate(HG_LEVELS):
            zeros = jnp.zeros((w, hd), F32)
            q_parts, k_parts = [], []
            for a in range(0, c, 2 * w):
                lo, hi = slice(a, a + w), slice(a + w, a + 2 * w)
                if fwd:
                    ref = e_s[h, pl.ds(off + a + w - 1, 1), :]
                    q_parts += [zeros, q[hi] * jnp.exp2(e[hi] - ref)]
                    k_parts += [k[lo] * jnp.exp2(ref - e[lo]), zeros]
                else:
                    ref = e_s[h, pl.ds(off + a + w, 1), :]
                    q_parts += [q[lo] * jnp.exp2(e[lo] - ref), zeros]
                    k_parts += [zeros, k[hi] * jnp.exp2(ref - e[hi])]
            qs = jnp.concatenate(q_parts, axis=0).astype(BF16)
            ks = jnp.concatenate(k_parts, axis=0).astype(BF16)
            a_w = lax.dot_general(qs, ks, nt, preferred_element_type=F32)
            att = a_w if att is None else jnp.where(code == li + 1, a_w, att)
            yield
        e_end = e_s[h, pl.ds(off + (c - 1 if fwd else 0), 1), :]
        st = st_s[h, d]
        qi = (q * jnp.exp2(e)).astype(BF16)
        ki = (k * jnp.exp2(e_end - e)).astype(BF16)
        o = lax.dot_general(qi, st.astype(BF16), nt, preferred_element_type=F32)
        yield
        st_new = st * jnp.exp2(e_end) + lax.dot_general(v, ki, tn, preferred_element_type=F32)
        yield
        att = jnp.where(code == diag_code, diag, att)
        out.append((o + jnp.dot(att.astype(BF16), v, preferred_element_type=F32), st_new))
        yield

    n_stages = len(HG_LEVELS) + 4

    def scan(v_ref, base, n):
        def body(i, carry):
            rf = pl.multiple_of(i * c, c)
            rb = pl.multiple_of((n - 1 - i) * c, c)
            vf = v_ref[pl.ds(rf, c), :].astype(BF16)
            vb = v_ref[pl.ds(rb, c), :].astype(BF16)
            chains = []
            for h in range(HG_PAIR):
                hs = slice(h * hd, (h + 1) * hd)
                for d, r, v in ((0, rf, vf), (1, rb, vb)):
                    out = []
                    chains.append((h, d, r, out, chunk(base + r, v[:, hs], h, d, out)))
            for _ in range(n_stages):
                for chain in chains:
                    next(chain[-1])
            for h, d, r, out, _ in chains:
                o, st_new = out[0]
                (of_s if d == 0 else ob_s)[h, pl.ds(base + r, c), :] = o
                st_s[h, d] = st_new
            return carry

        lax.fori_loop(0, n, body, 0, unroll=HG_SCAN_UNROLL)

    scan(ic_ref, 0, n_ctx // c)
    scan(il_ref, n_ctx, n_lat // c)

    def finish(o_ref, base, n):
        step = 256
        for t0 in range(0, n, step):
            src = slice(t0, t0 + step)
            dst = slice(base + t0, base + t0 + step)
            for h in range(HG_PAIR):
                o_ref[src, h * hd:(h + 1) * hd] = of_s[h, dst, :] + ob_s[h, dst, :]

    finish(oc_ref, 0, n_ctx)
    finish(ol_ref, n_ctx, n_lat)


def _hg_constants():
    c = HG_CHUNK
    row, col = np.arange(c)[:, None], np.arange(c)[None, :]
    blocks = np.eye(HG_PREP_ROWS // c)
    tri = np.stack([np.kron(blocks, col <= row), np.kron(blocks, col >= row)]).astype(np.float32)
    same8 = (row // HG_DIAG) == (col // HG_DIAG)
    diag_code = len(HG_LEVELS) + 1
    code_f = np.where(same8 & (col <= row), diag_code, 0)
    code_b = np.where(same8 & (col >= row), diag_code, 0)
    for li, w in enumerate(HG_LEVELS):
        same = (row // (2 * w)) == (col // (2 * w))
        t_hi, s_hi = (row % (2 * w)) >= w, (col % (2 * w)) >= w
        code_f = np.where(same & t_hi & ~s_hi, li + 1, code_f)
        code_b = np.where(same & ~t_hi & s_hi, li + 1, code_b)
    codes = np.stack([code_f, code_b]).astype(np.int32)
    lane_sum = (np.arange(HG_DIAG * HG_DIM)[:, None] // HG_DIM == col % HG_DIAG).astype(np.float32)
    return jnp.asarray(tri, BF16), jnp.asarray(codes), jnp.asarray(lane_sum, BF16)


def _hgrn2(p_l, p_c, lbp, b):
    tri, codes, lane_sum = _hg_constants()
    t = p_l.shape[0] // b
    ctx = p_c.shape[0] // b
    hd = HG_DIM
    pw = HG_PAIR * hd
    n = t + ctx
    first = _HG0 * LANE // pw

    def col(stream):
        return lambda i, j: (i, first + stream * (HG_HEADS // HG_PAIR) + j)

    lat_specs = [pl.BlockSpec((t, pw), col(s)) for s in range(4)]
    ctx_specs = [pl.BlockSpec((ctx, pw), col(s)) for s in range(4)]
    big = lambda: pltpu.VMEM((HG_PAIR, n, hd), F32)
    return pl.pallas_call(
        _hg_kernel,
        grid=(b, HG_HEADS // HG_PAIR),
        in_specs=lat_specs + ctx_specs + [
            pl.BlockSpec((2, 3, pw), lambda i, j: (0, 0, j)),
            pl.BlockSpec(tri.shape, lambda i, j: (0, 0, 0)),
            pl.BlockSpec(codes.shape, lambda i, j: (0, 0, 0)),
            pl.BlockSpec(lane_sum.shape, lambda i, j: (0, 0)),
        ],
        out_specs=[
            pl.BlockSpec((t, pw), lambda i, j: (i, j)),
            pl.BlockSpec((ctx, pw), lambda i, j: (i, j)),
        ],
        out_shape=[
            jax.ShapeDtypeStruct((b * t, HG_WIDTH), F32),
            jax.ShapeDtypeStruct((b * ctx, HG_WIDTH), F32),
        ],
        scratch_shapes=[big() for _ in range(7)] + [pltpu.VMEM((HG_PAIR, 2, hd, hd), F32)],
        compiler_params=_cparams(("arbitrary", "arbitrary")),
        name="hgrn2",
    )(*([p_l] * 4 + [p_c] * 4 + [lbp, tri, codes, lane_sum]))


def _outproj_kernel(oa_ref, os_ref, hg_ref, hnw_ref, u_ref, v_ref, lnw_ref, ws_ref, bs_ref,
                    wa_ref, wb_ref, wc_ref, x_ref, g_ref, o_ref, ob_ref, oc_ref):
    dot = functools.partial(jnp.dot, preferred_element_type=F32)
    y = dot(oa_ref[...], wa_ref[...])
    for h in range(HG_HEADS):
        hs = slice(h * HG_DIM, (h + 1) * HG_DIM)
        o = os_ref[:, hs]
        o = o * lax.rsqrt(jnp.mean(o * o, axis=-1, keepdims=True) + EPS) * hnw_ref[...]
        ob_ref[:, hs] = (o * _silu(hg_ref[:, hs])).astype(ob_ref.dtype)
    y = y + dot(ob_ref[...], wb_ref[...])
    _gm_block(u_ref, v_ref, lnw_ref, ws_ref, bs_ref, oc_ref)
    y = y + dot(oc_ref[...], wc_ref[...])
    o_ref[...] = x_ref[...] + g_ref[0] * y


def _outproj(oa, osum, p2, hg_norm_w, gm_ln_w, gm_ws, gm_bs, w, w_layer, x2d, mod, layer, rows_per_mod,
             mod_row0, tm):
    m, d = x2d.shape
    mrow = lambda i: layer * MOD_ROWS + mod_row0 + (i * tm) // rows_per_mod
    nb = NA_WIDTH // HG_WIDTH
    assert tm % GM_CHUNK == 0
    return pl.pallas_call(
        _outproj_kernel,
        grid=(m // tm,),
        in_specs=[
            pl.BlockSpec((tm, NA_WIDTH), lambda i: (i, 0)),
            pl.BlockSpec((tm, HG_WIDTH), lambda i: (i, 0)),
            pl.BlockSpec((tm, HG_WIDTH), lambda i: (i, _HG_G)),
            pl.BlockSpec((1, HG_DIM), lambda i: (0, 0)),
            pl.BlockSpec((tm, GM_WIDTH), lambda i: (i, _GM_U)),
            pl.BlockSpec((tm, GM_WIDTH), lambda i: (i, _GM_V)),
            pl.BlockSpec((1, GM_WIDTH), lambda i: (0, 0)),
            pl.BlockSpec((GM_GROUPS, GM_CHUNK, GM_CHUNK), lambda i: (0, 0, 0)),
            pl.BlockSpec((GM_GROUPS, GM_CHUNK, 1), lambda i: (0, 0, 0)),
            pl.BlockSpec((None, NA_WIDTH, d), lambda i: (w_layer, 0, 0)),
            pl.BlockSpec((None, HG_WIDTH, d), lambda i: (w_layer, nb, 0)),
            pl.BlockSpec((None, GM_WIDTH, d), lambda i: (w_layer, nb + 1, 0)),
            pl.BlockSpec((tm, d), lambda i: (i, 0)),
            pl.BlockSpec((1, 1, d), lambda i: (mrow(i), 0, 2)),
        ],
        out_specs=pl.BlockSpec((tm, d), lambda i: (i, 0)),
        out_shape=jax.ShapeDtypeStruct((m, d), F32),
        scratch_shapes=[pltpu.VMEM((tm, HG_WIDTH), BF16), pltpu.VMEM((tm, GM_WIDTH), BF16)],
        compiler_params=_cparams(("arbitrary",)),
        name="outproj",
    )(oa, osum, p2, hg_norm_w.reshape(1, HG_DIM), p2, p2, gm_ln_w.reshape(1, GM_WIDTH), gm_ws,
      gm_bs.reshape(GM_GROUPS, GM_CHUNK, 1), w, w, w, x2d, mod)


def _mlp_kernel(x_ref, nw_ref, sh_ref, sc_ref, g_ref, w1_ref, w2_ref, fnw_ref, o_ref, h_ref, *, final_norm):
    j = pl.program_id(1)

    @pl.when(j == 0)
    def _():
        _norm_modulate_store(x_ref, nw_ref, sh_ref, sc_ref, h_ref)
        o_ref[...] = jnp.zeros_like(o_ref)

    h = h_ref[...]
    for c0 in range(0, w1_ref.shape[1], MLP_TH_INNER):
        cols = slice(c0, c0 + MLP_TH_INNER)
        a = jnp.maximum(jnp.dot(h, w1_ref[:, cols], preferred_element_type=F32), 0.0)
        o_ref[...] += jnp.dot((a * a).astype(BF16), w2_ref[cols, :], preferred_element_type=F32)

    @pl.when(j == pl.num_programs(1) - 1)
    def _():
        y = x_ref[...] + g_ref[0] * o_ref[...]
        if final_norm:
            y = y * lax.rsqrt(jnp.mean(y * y, axis=-1, keepdims=True) + EPS) * fnw_ref[...]
        o_ref[...] = y


def _mlp(x2d, nw, mod, w1, w2, w_layer, fnw, layer, rows_per_mod, mod_row0, tm, final_norm):
    m, d = x2d.shape
    hid = w1.shape[2]
    th = MLP_TH
    mrow = lambda i: layer * MOD_ROWS + mod_row0 + (i * tm) // rows_per_mod
    return pl.pallas_call(
        functools.partial(_mlp_kernel, final_norm=final_norm),
        grid=(m // tm, hid // th),
        in_specs=[
            pl.BlockSpec((tm, d), lambda i, j: (i, 0)),
            pl.BlockSpec((None, 1, d), lambda i, j: (layer, 0, 0)),
            pl.BlockSpec((1, 1, d), lambda i, j: (mrow(i), 0, 3)),
            pl.BlockSpec((1, 1, d), lambda i, j: (mrow(i), 0, 4)),
            pl.BlockSpec((1, 1, d), lambda i, j: (mrow(i), 0, 5)),
            pl.BlockSpec((None, d, th), lambda i, j: (w_layer, 0, j)),
            pl.BlockSpec((None, th, d), lambda i, j: (w_layer, j, 0)),
            pl.BlockSpec((1, d), lambda i, j: (0, 0)),
        ],
        out_specs=pl.BlockSpec((tm, d), lambda i, j: (i, 0)),
        out_shape=jax.ShapeDtypeStruct((m, d), F32),
        scratch_shapes=[pltpu.VMEM((tm, d), BF16)],
        compiler_params=_cparams(("arbitrary", "arbitrary"), VMEM_LIMIT_MLP_V7X),
        name="mlp",
    )(x2d, nw, mod, mod, mod, w1, w2, fnw)


def kernel(x, c, ctx, c_ctx, ada_w, ada_b, norm1_w, norm2_w, w_in, na_rpb, hg_lb_logits, hg_norm_w,
           gm_ln_w, gm_ws, gm_bs, w_out, mlp_w1, mlp_w2, final_norm_w):
    bsz, seq, d = x.shape
    n_ctx = ctx.shape[1]
    depth = ada_w.shape[0]
    assert bsz < MOD_ROWS and d == D_MODEL and seq % 512 == 0 and n_ctx % 256 == 0
    assert seq // NA_GROUP_TOK >= NA_WIN_BLOCKS and seq // GRID_W >= 2 * NA_WIN_ROWS

    lb = jnp.cumsum(jax.nn.softmax(hg_lb_logits.astype(F32), axis=0), axis=0)
    lb = lb - lb[:1]
    lbp = jnp.stack([lb, jnp.log1p(-lb), 1.0 - lb], axis=2)

    cond = jnp.zeros((MOD_ROWS, d), F32).at[:bsz].set(c).at[bsz].set(c_ctx)
    mod = _ada(cond, ada_w, ada_b).reshape(depth * MOD_ROWS, 1, 6 * d)

    w_in_b = w_in[:1].astype(BF16)
    nw1 = norm1_w.reshape(depth, 1, d)
    nw2 = norm2_w.reshape(depth, 1, d)
    fnw = final_norm_w.reshape(1, d)

    xl = x.reshape(bsz * seq, d)
    xc = ctx.reshape(bsz * n_ctx, d)
    n_c = bsz * n_ctx
    tm_l, tm_c = TM_LATENT, min(TM_CONTEXT, n_c)
    for l in range(depth):
        need_ctx = l < depth - 1
        qkv_l, p_l = _inproj(xl, nw1, mod, w_in_b, l, seq, 0, tm_l)
        qkv_c, p_c = _inproj(xc, nw1, mod, w_in_b, l, n_c, bsz, tm_c)

        to_cast = [(w_out, l), (mlp_w1, l), (mlp_w2, l)] + ([(w_in, l + 1)] if need_ctx else [])
        oa_l, (w_out_b, w1_b, w2_b, *w_in_next) = _na_latent(qkv_l, qkv_c, _na_bias_slabs(na_rpb[l]),
                                                             to_cast, bsz)
        w_in_b = w_in_next[0] if w_in_next else None
        os_l, os_c = _hgrn2(p_l, p_c, lbp[l], bsz)
        mix = (hg_norm_w[l], gm_ln_w[l], gm_ws[l], gm_bs[l])
        xl = _outproj(oa_l, os_l, p_l, *mix, w_out_b, 0, xl, mod, l, seq, 0, tm_l)
        xl = _mlp(xl, nw2, mod, w1_b, w2_b, 0, fnw, l, seq, 0, tm_l, final_norm=not need_ctx)
        if need_ctx:
            oa_c = _ctx_attention(qkv_c, bsz)
            xc = _outproj(oa_c, os_c, p_c, *mix, w_out_b, 0, xc, mod, l, n_c, bsz, tm_c)
            xc = _mlp(xc, nw2, mod, w1_b, w2_b, 0, fnw, l, n_c, bsz, tm_c, final_norm=False)
    return xl.reshape(bsz, seq, d)
```

```python
import functools

import numpy as np
import jax
import jax.numpy as jnp
from jax import lax
from jax.experimental import pallas as pl
from jax.experimental.pallas import tpu as pltpu

F32 = jnp.float32
BF16 = jnp.bfloat16

D_MODEL = 2048
DEPTH = 2
GRID_W = 64
EPS = 1e-6

NA_HEAD_DIM = 128
NA_HEADS = 8
NA_WIDTH = NA_HEADS * NA_HEAD_DIM
NA_WIN_ROWS = 8
NA_WIN_COLS = 16

HG_HEADS = 4
HG_DIM = 128
HG_WIDTH = HG_HEADS * HG_DIM
HG_CHUNK = 64

GM_GROUPS = 4
GM_DIM = 128
GM_WIDTH = GM_GROUPS * GM_DIM
GM_CHUNK = 128

IN_WIDTH = 3 * NA_WIDTH + 5 * HG_WIDTH + 2 * GM_WIDTH
MLP_HIDDEN = 4 * D_MODEL
LANE = 128
LOG2E = 1.4426950408889634
BF16_SUBLANES = 16

QKV_WIDTH = 3 * NA_WIDTH
REST_WIDTH = IN_WIDTH - QKV_WIDTH
REST_MAIN = QKV_WIDTH
REST_TAIL = REST_WIDTH - REST_MAIN
NA_SCORE_SCALE = NA_HEAD_DIM ** -0.5 * LOG2E
_QA, _KA, _VA = 0, NA_HEADS, 2 * NA_HEADS
_HG0 = 0
_GM_U = 5 * HG_WIDTH // GM_WIDTH
_GM_V = _GM_U + 1
_HG_G = 4

VMEM_LIMIT_V7X = 56 * 1024 * 1024
VMEM_LIMIT_MLP_V7X = 60 * 1024 * 1024
ADA_TN = 1536
TM_LATENT = 512
TM_CONTEXT = 512
MLP_TH = 2048
MLP_TH_INNER = 512
NORM_CHUNK_ROWS = 32
NORM_UNROLL = 4
MOD_ROWS = 16


def _cparams(sem, vmem_limit=VMEM_LIMIT_V7X):
    return pltpu.CompilerParams(dimension_semantics=sem, vmem_limit_bytes=vmem_limit)


def _silu(x):
    h = 0.5 * x
    return h + h * jnp.tanh(h)


def _gelu_tanh(x):
    return 0.5 * x * (1.0 + jnp.tanh(0.7978845608028654 * (x + 0.044715 * (x * x * x))))


def _norm_modulate_store(x_ref, nw_ref, sh_ref, sc_ref, h_ref):
    gain = nw_ref[...] * (1.0 + sc_ref[0])
    shift = sh_ref[0]

    def body(i, carry):
        rows = pl.ds(pl.multiple_of(i * NORM_CHUNK_ROWS, NORM_CHUNK_ROWS), NORM_CHUNK_ROWS)
        x = x_ref[rows, :]
        rs = lax.rsqrt(jnp.mean(x * x, axis=-1, keepdims=True) + EPS)
        h_ref[rows, :] = (x * rs * gain + shift).astype(h_ref.dtype)
        return carry

    lax.fori_loop(0, x_ref.shape[0] // NORM_CHUNK_ROWS, body, 0, unroll=NORM_UNROLL)


def _ada_kernel(c_ref, w_ref, b_ref, o_ref):
    s = _silu(c_ref[...]).astype(BF16)
    o_ref[0] = jnp.dot(s, w_ref[0].astype(BF16), preferred_element_type=F32) + b_ref[0]


def _ada(cond, ada_w, ada_b):
    depth, d, n = ada_w.shape
    tn = ADA_TN
    return pl.pallas_call(
        _ada_kernel,
        grid=(depth, n // tn),
        in_specs=[
            pl.BlockSpec((cond.shape[0], d), lambda l, j: (0, 0)),
            pl.BlockSpec((1, d, tn), lambda l, j: (l, 0, j)),
            pl.BlockSpec((1, 1, tn), lambda l, j: (l, 0, j)),
        ],
        out_specs=pl.BlockSpec((1, cond.shape[0], tn), lambda l, j: (l, 0, j)),
        out_shape=jax.ShapeDtypeStruct((depth, cond.shape[0], n), F32),
        compiler_params=_cparams(("arbitrary", "arbitrary")),
        name="ada",
    )(cond, ada_w, ada_b.reshape(depth, 1, n))


def _inproj_qkv_kernel(x_ref, nw_ref, sh_ref, sc_ref, w_ref, o_ref, h_ref):
    _norm_modulate_store(x_ref, nw_ref, sh_ref, sc_ref, h_ref)
    h = h_ref[...]
    q = jnp.dot(h, w_ref[:, :NA_WIDTH], preferred_element_type=F32)
    o_ref[:, :NA_WIDTH] = (q * NA_SCORE_SCALE).astype(o_ref.dtype)
    o_ref[:, NA_WIDTH:] = jnp.dot(h, w_ref[:, NA_WIDTH:], preferred_element_type=F32).astype(o_ref.dtype)


def _inproj_rest_kernel(h_ref, wa_ref, wb_ref, o_ref):
    h = h_ref[...]
    o_ref[:, :REST_MAIN] = jnp.dot(h, wa_ref[...], preferred_element_type=F32)
    o_ref[:, REST_MAIN:] = jnp.dot(h, wb_ref[...], preferred_element_type=F32)


def _inproj(x2d, nw, mod, w, layer, rows_per_mod, mod_row0, tm):
    m, d = x2d.shape
    mrow = lambda i: layer * MOD_ROWS + mod_row0 + (i * tm) // rows_per_mod
    assert w.shape[0] == 1 and w.shape[2] == QKV_WIDTH + REST_MAIN + REST_TAIL
    assert (QKV_WIDTH + REST_MAIN) % REST_TAIL == 0
    qkv, h = pl.pallas_call(
        _inproj_qkv_kernel,
        grid=(m // tm,),
        in_specs=[
            pl.BlockSpec((tm, d), lambda i: (i, 0)),
            pl.BlockSpec((None, 1, d), lambda i: (layer, 0, 0)),
            pl.BlockSpec((1, 1, d), lambda i: (mrow(i), 0, 0)),
            pl.BlockSpec((1, 1, d), lambda i: (mrow(i), 0, 1)),
            pl.BlockSpec((None, d, QKV_WIDTH), lambda i: (0, 0, 0)),
        ],
        out_specs=[pl.BlockSpec((tm, QKV_WIDTH), lambda i: (i, 0)), pl.BlockSpec((tm, d), lambda i: (i, 0))],
        out_shape=[jax.ShapeDtypeStruct((m, QKV_WIDTH), BF16), jax.ShapeDtypeStruct((m, d), BF16)],
        compiler_params=_cparams(("arbitrary",)),
        name="inproj_qkv",
    )(x2d, nw, mod, mod, w)
    rest = pl.pallas_call(
        _inproj_rest_kernel,
        grid=(m // tm,),
        in_specs=[
            pl.BlockSpec((tm, d), lambda i: (i, 0)),
            pl.BlockSpec((None, d, REST_MAIN), lambda i: (0, 0, 1)),
            pl.BlockSpec((None, d, REST_TAIL), lambda i: (0, 0, (QKV_WIDTH + REST_MAIN) // REST_TAIL)),
        ],
        out_specs=pl.BlockSpec((tm, REST_WIDTH), lambda i: (i, 0)),
        out_shape=jax.ShapeDtypeStruct((m, REST_WIDTH), F32),
        compiler_params=_cparams(("arbitrary",)),
        name="inproj_rest",
    )(h, w, w)
    return qkv, rest


NA_GROUP_ROWS = 4
NA_GROUP_TOK = NA_GROUP_ROWS * GRID_W
NA_WIN_BLOCKS = 3
NA_GROUPS_PER_STEP = 8
NA_LOOP_UNROLL = 1


NA_MASKED_SLAB = 2 * NA_WIN_ROWS - 1


def _na_bias_slabs(rpb):
    h = rpb.shape[0]
    kcol = np.arange(GRID_W)[:, None]
    qcol = np.arange(GRID_W)[None, :]
    wstart = np.clip(qcol - NA_WIN_COLS // 2, 0, GRID_W - NA_WIN_COLS)
    valid_col = (kcol >= wstart) & (kcol < wstart + NA_WIN_COLS)
    pad = GRID_W - NA_WIN_COLS
    padded = jnp.pad(rpb.astype(F32), ((0, 0), (0, 0), (pad, pad)))
    m = jnp.tile(padded, (1, 1, GRID_W + 1))[:, :, :GRID_W * 2 * GRID_W]
    m = m.reshape(h, rpb.shape[1], GRID_W, 2 * GRID_W)[..., :GRID_W]
    slabs = jnp.where(valid_col, m[..., ::-1] * LOG2E, -1e30)
    slabs = jnp.concatenate([slabs, jnp.full((h, 1, GRID_W, GRID_W), -1e30, F32)], axis=1)
    return jnp.concatenate([slabs, slabs], axis=-1)


def _na_slab_index(rows):
    n_groups = rows // NA_GROUP_ROWS
    krel = np.arange(NA_WIN_BLOCKS * NA_GROUP_ROWS)[:, None]
    qrel = np.arange(NA_GROUP_ROWS)[None, :]
    idx = []
    for g in (0, 1, n_groups - 1):
        krow = int(np.clip(g - 1, 0, n_groups - NA_WIN_BLOCKS)) * NA_GROUP_ROWS + krel
        qrow = g * NA_GROUP_ROWS + qrel
        r0 = np.clip(qrow - NA_WIN_ROWS // 2, 0, rows - NA_WIN_ROWS)
        in_window = (krow >= r0) & (krow < r0 + NA_WIN_ROWS)
        idx.append(np.where(in_window, krow - qrow + NA_WIN_ROWS - 1, NA_MASKED_SLAB))
    return np.stack(idx).tolist()


def _na_kernel(q_ref, k_ref, v_ref, kc_ref, vc_ref, slab_ref, *refs, n_weights, slab_index):
    w_refs, o_ref, w_bf16_refs = refs[:n_weights], refs[n_weights], refs[n_weights + 1:2 * n_weights + 1]
    vt_ref, bias_ref = refs[2 * n_weights + 1:]
    n_groups = q_ref.shape[0] // NA_GROUP_TOK
    gt = NA_GROUP_TOK
    nt = (((1,), (1,)), ((), ()))
    dot = functools.partial(jnp.dot, preferred_element_type=F32)

    for src_ref, dst_ref in zip(w_refs, w_bf16_refs):
        dst_ref[...] = src_ref[...].astype(BF16)

    @pl.when(pl.program_id(1) == 0)
    def _():
        left = lax.broadcasted_iota(jnp.int32, (GRID_W, 2 * GRID_W), 1) < GRID_W
        for kind, per_key_row in enumerate(slab_index):
            for j, per_query_row in enumerate(per_key_row):
                for a in range(0, NA_GROUP_ROWS, 2):
                    tile = jnp.where(left, slab_ref[0, per_query_row[a]], slab_ref[0, per_query_row[a + 1]])
                    bias_ref[kind, j * GRID_W:(j + 1) * GRID_W, a * GRID_W:(a + 2) * GRID_W] = tile

    for b in range(n_groups):
        vt_ref[b] = v_ref[b * gt:(b + 1) * gt, :].T
    kc = kc_ref[...]
    vct = vc_ref[...].T

    def group(g):
        blk = jnp.clip(g - 1, 0, n_groups - NA_WIN_BLOCKS)
        kind = jnp.where(g == 0, 0, jnp.where(g == n_groups - 1, 2, 1))
        qg = q_ref[pl.ds(pl.multiple_of(g * gt, gt), gt), :]
        kw = k_ref[pl.ds(pl.multiple_of(blk * gt, gt), NA_WIN_BLOCKS * gt), :]
        s_lat = lax.dot_general(kw, qg, nt, preferred_element_type=F32) + bias_ref[kind]
        s_ctx = lax.dot_general(kc, qg, nt, preferred_element_type=F32)
        yield
        m = jnp.maximum(jnp.max(s_lat, axis=0, keepdims=True), jnp.max(s_ctx, axis=0, keepdims=True))
        p_lat = jnp.exp2(s_lat - m)
        p_ctx = jnp.exp2(s_ctx - m)
        l = jnp.sum(p_lat, axis=0, keepdims=True) + jnp.sum(p_ctx, axis=0, keepdims=True)
        yield
        ot = dot(vct, p_ctx.astype(BF16))
        for j in range(NA_WIN_BLOCKS):
            ot = ot + dot(vt_ref[blk + j], p_lat[j * gt:(j + 1) * gt, :].astype(BF16))
        yield
        o_ref[pl.ds(pl.multiple_of(g * gt, gt), gt), :] = (ot / l).T.astype(o_ref.dtype)
        yield

    def body(i, carry):
        groups = [group(i * NA_GROUPS_PER_STEP + u) for u in range(NA_GROUPS_PER_STEP)]
        for _ in range(4):
            for grp in groups:
                next(grp)
        return carry

    lax.fori_loop(0, n_groups // NA_GROUPS_PER_STEP, body, 0, unroll=NA_LOOP_UNROLL)


def _na_latent(p_l, p_c, slabs, weights, b):
    t = p_l.shape[0] // b
    ctx = p_c.shape[0] // b
    hd = NA_HEAD_DIM
    n_steps = NA_HEADS * b
    w_specs_in, w_specs_out, w_shapes = [], [], []
    for w, layer in weights:
        rows, cols = w.shape[1] // n_steps, w.shape[2]
        assert w.shape[1] % n_steps == 0 and rows % BF16_SUBLANES == 0
        w_specs_in.append(pl.BlockSpec((None, rows, cols), lambda h, i, layer=layer: (layer, h * b + i, 0)))
        w_specs_out.append(pl.BlockSpec((None, rows, cols), lambda h, i: (0, h * b + i, 0)))
        w_shapes.append(jax.ShapeDtypeStruct((1,) + w.shape[1:], BF16))
    oa, *w_bf16 = pl.pallas_call(
        functools.partial(_na_kernel, n_weights=len(weights), slab_index=_na_slab_index(t // GRID_W)),
        grid=(NA_HEADS, b),
        in_specs=[
            pl.BlockSpec((t, hd), lambda h, i: (i, _QA + h)),
            pl.BlockSpec((t, hd), lambda h, i: (i, _KA + h)),
            pl.BlockSpec((t, hd), lambda h, i: (i, _VA + h)),
            pl.BlockSpec((ctx, hd), lambda h, i: (i, _KA + h)),
            pl.BlockSpec((ctx, hd), lambda h, i: (i, _VA + h)),
            pl.BlockSpec((1,) + slabs.shape[1:], lambda h, i: (h, 0, 0, 0)),
        ] + w_specs_in,
        out_specs=[pl.BlockSpec((t, hd), lambda h, i: (i, h))] + w_specs_out,
        out_shape=[jax.ShapeDtypeStruct((b * t, NA_WIDTH), BF16)] + w_shapes,
        scratch_shapes=[
            pltpu.VMEM((t // NA_GROUP_TOK, hd, NA_GROUP_TOK), BF16),
            pltpu.VMEM((3, NA_WIN_BLOCKS * NA_GROUP_TOK, NA_GROUP_TOK), F32),
        ],
        compiler_params=_cparams(("arbitrary", "arbitrary")),
        name="na_latent",
    )(p_l, p_l, p_l, p_c, p_c, slabs, *(w for w, _ in weights))
    return oa, w_bf16


def _ctx_attn_kernel(q_ref, k_ref, v_ref, o_ref):
    for h in range(NA_HEADS):
        hs = slice(h * NA_HEAD_DIM, (h + 1) * NA_HEAD_DIM)
        s = lax.dot_general(q_ref[:, hs], k_ref[:, hs], (((1,), (1,)), ((), ())), preferred_element_type=F32)
        p = jnp.exp2(s - jnp.max(s, axis=-1, keepdims=True))
        l = jnp.sum(p, axis=-1, keepdims=True)
        o = jnp.dot(p.astype(BF16), v_ref[:, hs], preferred_element_type=F32)
        o_ref[:, hs] = (o / l).astype(o_ref.dtype)


def _ctx_attention(p_c, b):
    ctx = p_c.shape[0] // b
    return pl.pallas_call(
        _ctx_attn_kernel,
        grid=(b,),
        in_specs=[
            pl.BlockSpec((ctx, NA_WIDTH), lambda i: (i, 0)),
            pl.BlockSpec((ctx, NA_WIDTH), lambda i: (i, 1)),
            pl.BlockSpec((ctx, NA_WIDTH), lambda i: (i, 2)),
        ],
        out_specs=pl.BlockSpec((ctx, NA_WIDTH), lambda i: (i, 0)),
        out_shape=jax.ShapeDtypeStruct((b * ctx, NA_WIDTH), BF16),
        compiler_params=_cparams(("arbitrary",)),
        name="ctx_attention",
    )(p_c, p_c, p_c)


def _gm_block(u_ref, v_ref, lnw_ref, ws_ref, bs_ref, o_ref):
    for ck in range(u_ref.shape[0] // GM_CHUNK):
        rows = slice(ck * GM_CHUNK, (ck + 1) * GM_CHUNK)
        uf = _gelu_tanh(u_ref[rows, :])
        vf = _gelu_tanh(v_ref[rows, :])
        for g in range(GM_GROUPS):
            sl = slice(g * GM_DIM, (g + 1) * GM_DIM)
            vg = vf[:, sl]
            mu = jnp.mean(vg, axis=-1, keepdims=True)
            dv = vg - mu
            var = jnp.mean(dv * dv, axis=-1, keepdims=True)
            vn = dv * lax.rsqrt(var + EPS) * lnw_ref[:, sl]
            mixed = jnp.dot(ws_ref[g].astype(BF16), vn.astype(BF16), preferred_element_type=F32) + bs_ref[g]
            o_ref[rows, sl] = (uf[:, sl] * mixed).astype(o_ref.dtype)


HG_PAIR = 2
HG_LEVELS = (32, 16, 8)
HG_DIAG = 8
HG_PREP_ROWS = 256
HG_SCAN_UNROLL = 8


def _split3_dot(tri, g):
    g0 = g.astype(BF16)
    r1 = g - g0.astype(F32)
    g1 = r1.astype(BF16)
    g2 = (r1 - g1.astype(F32)).astype(BF16)
    dot = functools.partial(jnp.dot, preferred_element_type=F32)
    return dot(tri, g0) + dot(tri, g1) + dot(tri, g2)


def _hg_kernel(ql_ref, ffl_ref, fbl_ref, il_ref,
               qc_ref, ffc_ref, fbc_ref, ic_ref,
               lbp_ref, tri_ref, code_ref, lsum_ref, ol_ref, oc_ref,
               qh_s, kf_s, kb_s, ef_s, eb_s, of_s, ob_s, st_s):
    c = HG_CHUNK
    hd = HG_DIM
    n_ctx = qc_ref.shape[0]
    n_lat = ql_ref.shape[0]
    scale = hd ** -0.5
    nt = (((1,), (1,)), ((), ()))
    tn = (((0,), (0,)), ((), ()))

    def prep(q_ref, ff_ref, fb_ref, base, n):
        step = HG_PREP_ROWS
        for t0 in range(0, n, step):
            src = slice(t0, t0 + step)
            dst = slice(base + t0, base + t0 + step)
            qh = _silu(q_ref[src, :]) * scale
            for h in range(HG_PAIR):
                qh_s[h, dst, :] = qh[:, h * hd:(h + 1) * hd]
            for d, (f_ref, k_s, e_s) in enumerate(((ff_ref, kf_s, ef_s), (fb_ref, kb_s, eb_s))):
                x = f_ref[src, :]
                sp = jnp.maximum(-x, 0.0) + jnp.log(1.0 + jnp.exp(-jnp.abs(x)))
                lb = lbp_ref[d, 0:1, :]
                y = lbp_ref[d, 1:2, :] - sp
                ey = jnp.exp(y)
                log_f = jnp.where(lb > 0.0, jnp.log(lb + ey), y)
                one_minus_f = jnp.maximum(lbp_ref[d, 2:3, :] - ey, 0.0)
                e = _split3_dot(tri_ref[d], log_f) * LOG2E
                for h in range(HG_PAIR):
                    e_s[h, dst, :] = e[:, h * hd:(h + 1) * hd]
                    k_s[h, dst, :] = one_minus_f[:, h * hd:(h + 1) * hd]

    prep(qc_ref, ffc_ref, fbc_ref, 0, n_ctx)
    prep(ql_ref, ffl_ref, fbl_ref, n_ctx, n_lat)

    diag_code = len(HG_LEVELS) + 1

    st_s[...] = jnp.zeros_like(st_s)

    def chunk(off, v, h, d, out):
        fwd = d == 0
        k_s, e_s = (kf_s, ef_s) if fwd else (kb_s, eb_s)
        q = qh_s[h, pl.ds(off, c), :]
        k = k_s[h, pl.ds(off, c), :]
        e = e_s[h, pl.ds(off, c), :]
        code = code_ref[d]
        prods = []
        for i in range(c // HG_DIAG):
            bs = slice(i * HG_DIAG, (i + 1) * HG_DIAG)
            row_prods = []
            for s in range(HG_DIAG):
                kr = k_s[h, pl.ds(off + i * HG_DIAG + s, 1), :]
                er = e_s[h, pl.ds(off + i * HG_DIAG + s, 1), :]
                decay = jnp.exp2(jnp.minimum(e[bs] - er, 0.0))
                row_prods.append((q[bs] * kr * decay).astype(BF16))
            prods.append(jnp.concatenate(row_prods, axis=1))
        diag = jnp.dot(jnp.concatenate(prods, axis=0), lsum_ref[...], preferred_element_type=F32)
        yield
        att = None
        for li, w in enumerate(HG_LEVELS):
            zeros = jnp.zeros((w, hd), F32)
            q_parts, k_parts = [], []
            for a in range(0, c, 2 * w):
                lo, hi = slice(a, a + w), slice(a + w, a + 2 * w)
                if fwd:
                    ref = e_s[h, pl.ds(off + a + w - 1, 1), :]
                    q_parts += [zeros, q[hi] * jnp.exp2(e[hi] - ref)]
                    k_parts += [k[lo] * jnp.exp2(ref - e[lo]), zeros]
                else:
                    ref = e_s[h, pl.ds(off + a + w, 1), :]
                    q_parts += [q[lo] * jnp.exp2(e[lo] - ref), zeros]
                    k_parts += [zeros, k[hi] * jnp.exp2(ref - e[hi])]
            qs = jnp.concatenate(q_parts, axis=0).astype(BF16)
            ks = jnp.concatenate(k_parts, axis=0).astype(BF16)
            a_w = lax.dot_general(qs, ks, nt, preferred_element_type=F32)
            att = a_w if att is None else jnp.where(code == li + 1, a_w, att)
            yield
        e_end = e_s[h, pl.ds(off + (c - 1 if fwd else 0), 1), :]
        st = st_s[h, d]
        qi = (q * jnp.exp2(e)).astype(BF16)
        ki = (k * jnp.exp2(e_end - e)).astype(BF16)
        o = lax.dot_general(qi, st.astype(BF16), nt, preferred_element_type=F32)
        yield
        st_new = st * jnp.exp2(e_end) + lax.dot_general(v, ki, tn, preferred_element_type=F32)
        yield
        att = jnp.where(code == diag_code, diag, att)
        out.append((o + jnp.dot(att.astype(BF16), v, preferred_element_type=F32), st_new))
        yield

    n_stages = len(HG_LEVELS) + 4

    def scan(v_ref, base, n):
        def body(i, carry):
            rf = pl.multiple_of(i * c, c)
            rb = pl.multiple_of((n - 1 - i) * c, c)
            vf = v_ref[pl.ds(rf, c), :].astype(BF16)
            vb = v_ref[pl.ds(rb, c), :].astype(BF16)
            chains = []
            for h in range(HG_PAIR):
                hs = slice(h * hd, (h + 1) * hd)
                for d, r, v in ((0, rf, vf), (1, rb, vb)):
                    out = []
                    chains.append((h, d, r, out, chunk(base + r, v[:, hs], h, d, out)))
            for _ in range(n_stages):
                for chain in chains:
                    next(chain[-1])
            for h, d, r, out, _ in chains:
                o, st_new = out[0]
                (of_s if d == 0 else ob_s)[h, pl.ds(base + r, c), :] = o
                st_s[h, d] = st_new
            return carry

        lax.fori_loop(0, n, body, 0, unroll=HG_SCAN_UNROLL)

    scan(ic_ref, 0, n_ctx // c)
    scan(il_ref, n_ctx, n_lat // c)

    def finish(o_ref, base, n):
        step = HG_PREP_ROWS
        for t0 in range(0, n, step):
            src = slice(t0, t0 + step)
            dst = slice(base + t0, base + t0 + step)
            for h in range(HG_PAIR):
                o_ref[src, h * hd:(h + 1) * hd] = of_s[h, dst, :] + ob_s[h, dst, :]

    finish(oc_ref, 0, n_ctx)
    finish(ol_ref, n_ctx, n_lat)


def _hg_constants():
    c = HG_CHUNK
    row, col = np.arange(c)[:, None], np.arange(c)[None, :]
    blocks = np.eye(HG_PREP_ROWS // c)
    tri = np.stack([np.kron(blocks, col <= row), np.kron(blocks, col >= row)]).astype(np.float32)
    same8 = (row // HG_DIAG) == (col // HG_DIAG)
    diag_code = len(HG_LEVELS) + 1
    code_f = np.where(same8 & (col <= row), diag_code, 0)
    code_b = np.where(same8 & (col >= row), diag_code, 0)
    for li, w in enumerate(HG_LEVELS):
        same = (row // (2 * w)) == (col // (2 * w))
        t_hi, s_hi = (row % (2 * w)) >= w, (col % (2 * w)) >= w
        code_f = np.where(same & t_hi & ~s_hi, li + 1, code_f)
        code_b = np.where(same & ~t_hi & s_hi, li + 1, code_b)
    codes = np.stack([code_f, code_b]).astype(np.int32)
    lane_sum = (np.arange(HG_DIAG * HG_DIM)[:, None] // HG_DIM == col % HG_DIAG).astype(np.float32)
    return jnp.asarray(tri, BF16), jnp.asarray(codes), jnp.asarray(lane_sum, BF16)


def _hgrn2(p_l, p_c, lbp, b):
    tri, codes, lane_sum = _hg_constants()
    t = p_l.shape[0] // b
    ctx = p_c.shape[0] // b
    hd = HG_DIM
    pw = HG_PAIR * hd
    n = t + ctx
    first = _HG0 * LANE // pw

    def col(stream):
        return lambda i, j: (i, first + stream * (HG_HEADS // HG_PAIR) + j)

    lat_specs = [pl.BlockSpec((t, pw), col(s)) for s in range(4)]
    ctx_specs = [pl.BlockSpec((ctx, pw), col(s)) for s in range(4)]
    big = lambda: pltpu.VMEM((HG_PAIR, n, hd), F32)
    return pl.pallas_call(
        _hg_kernel,
        grid=(b, HG_HEADS // HG_PAIR),
        in_specs=lat_specs + ctx_specs + [
            pl.BlockSpec((2, 3, pw), lambda i, j: (0, 0, j)),
            pl.BlockSpec(tri.shape, lambda i, j: (0, 0, 0)),
            pl.BlockSpec(codes.shape, lambda i, j: (0, 0, 0)),
            pl.BlockSpec(lane_sum.shape, lambda i, j: (0, 0)),
        ],
        out_specs=[
            pl.BlockSpec((t, pw), lambda i, j: (i, j)),
            pl.BlockSpec((ctx, pw), lambda i, j: (i, j)),
        ],
        out_shape=[
            jax.ShapeDtypeStruct((b * t, HG_WIDTH), F32),
            jax.ShapeDtypeStruct((b * ctx, HG_WIDTH), F32),
        ],
        scratch_shapes=[big() for _ in range(7)] + [pltpu.VMEM((HG_PAIR, 2, hd, hd), F32)],
        compiler_params=_cparams(("arbitrary", "arbitrary")),
        name="hgrn2",
    )(*([p_l] * 4 + [p_c] * 4 + [lbp, tri, codes, lane_sum]))


def _outproj_kernel(oa_ref, os_ref, hg_ref, hnw_ref, u_ref, v_ref, lnw_ref, ws_ref, bs_ref,
                    wa_ref, wb_ref, wc_ref, x_ref, g_ref, o_ref, ob_ref, oc_ref):
    dot = functools.partial(jnp.dot, preferred_element_type=F32)
    y = dot(oa_ref[...], wa_ref[...])
    for h in range(HG_HEADS):
        hs = slice(h * HG_DIM, (h + 1) * HG_DIM)
        o = os_ref[:, hs]
        o = o * lax.rsqrt(jnp.mean(o * o, axis=-1, keepdims=True) + EPS) * hnw_ref[...]
        ob_ref[:, hs] = (o * _silu(hg_ref[:, hs])).astype(ob_ref.dtype)
    y = y + dot(ob_ref[...], wb_ref[...])
    _gm_block(u_ref, v_ref, lnw_ref, ws_ref, bs_ref, oc_ref)
    y = y + dot(oc_ref[...], wc_ref[...])
    o_ref[...] = x_ref[...] + g_ref[0] * y


def _outproj(oa, osum, p2, hg_norm_w, gm_ln_w, gm_ws, gm_bs, w, w_layer, x2d, mod, layer, rows_per_mod,
             mod_row0, tm):
    m, d = x2d.shape
    mrow = lambda i: layer * MOD_ROWS + mod_row0 + (i * tm) // rows_per_mod
    nb = NA_WIDTH // HG_WIDTH
    assert tm % GM_CHUNK == 0
    return pl.pallas_call(
        _outproj_kernel,
        grid=(m // tm,),
        in_specs=[
            pl.BlockSpec((tm, NA_WIDTH), lambda i: (i, 0)),
            pl.BlockSpec((tm, HG_WIDTH), lambda i: (i, 0)),
            pl.BlockSpec((tm, HG_WIDTH), lambda i: (i, _HG_G)),
            pl.BlockSpec((1, HG_DIM), lambda i: (0, 0)),
            pl.BlockSpec((tm, GM_WIDTH), lambda i: (i, _GM_U)),
            pl.BlockSpec((tm, GM_WIDTH), lambda i: (i, _GM_V)),
            pl.BlockSpec((1, GM_WIDTH), lambda i: (0, 0)),
            pl.BlockSpec((GM_GROUPS, GM_CHUNK, GM_CHUNK), lambda i: (0, 0, 0)),
            pl.BlockSpec((GM_GROUPS, GM_CHUNK, 1), lambda i: (0, 0, 0)),
            pl.BlockSpec((None, NA_WIDTH, d), lambda i: (w_layer, 0, 0)),
            pl.BlockSpec((None, HG_WIDTH, d), lambda i: (w_layer, nb, 0)),
            pl.BlockSpec((None, GM_WIDTH, d), lambda i: (w_layer, nb + 1, 0)),
            pl.BlockSpec((tm, d), lambda i: (i, 0)),
            pl.BlockSpec((1, 1, d), lambda i: (mrow(i), 0, 2)),
        ],
        out_specs=pl.BlockSpec((tm, d), lambda i: (i, 0)),
        out_shape=jax.ShapeDtypeStruct((m, d), F32),
        scratch_shapes=[pltpu.VMEM((tm, HG_WIDTH), BF16), pltpu.VMEM((tm, GM_WIDTH), BF16)],
        compiler_params=_cparams(("arbitrary",)),
        name="outproj",
    )(oa, osum, p2, hg_norm_w.reshape(1, HG_DIM), p2, p2, gm_ln_w.reshape(1, GM_WIDTH), gm_ws,
      gm_bs.reshape(GM_GROUPS, GM_CHUNK, 1), w, w, w, x2d, mod)


def _mlp_kernel(x_ref, nw_ref, sh_ref, sc_ref, g_ref, w1_ref, w2_ref, fnw_ref, o_ref, h_ref, *, final_norm):
    j = pl.program_id(1)

    @pl.when(j == 0)
    def _():
        _norm_modulate_store(x_ref, nw_ref, sh_ref, sc_ref, h_ref)
        o_ref[...] = jnp.zeros_like(o_ref)

    h = h_ref[...]
    for c0 in range(0, w1_ref.shape[1], MLP_TH_INNER):
        cols = slice(c0, c0 + MLP_TH_INNER)
        a = jnp.maximum(jnp.dot(h, w1_ref[:, cols], preferred_element_type=F32), 0.0)
        o_ref[...] += jnp.dot((a * a).astype(BF16), w2_ref[cols, :], preferred_element_type=F32)

    @pl.when(j == pl.num_programs(1) - 1)
    def _():
        y = x_ref[...] + g_ref[0] * o_ref[...]
        if final_norm:
            y = y * lax.rsqrt(jnp.mean(y * y, axis=-1, keepdims=True) + EPS) * fnw_ref[...]
        o_ref[...] = y


def _mlp(x2d, nw, mod, w1, w2, w_layer, fnw, layer, rows_per_mod, mod_row0, tm, final_norm):
    m, d = x2d.shape
    hid = w1.shape[2]
    th = MLP_TH
    mrow = lambda i: layer * MOD_ROWS + mod_row0 + (i * tm) // rows_per_mod
    return pl.pallas_call(
        functools.partial(_mlp_kernel, final_norm=final_norm),
        grid=(m // tm, hid // th),
        in_specs=[
            pl.BlockSpec((tm, d), lambda i, j: (i, 0)),
            pl.BlockSpec((None, 1, d), lambda i, j: (layer, 0, 0)),
            pl.BlockSpec((1, 1, d), lambda i, j: (mrow(i), 0, 3)),
            pl.BlockSpec((1, 1, d), lambda i, j: (mrow(i), 0, 4)),
            pl.BlockSpec((1, 1, d), lambda i, j: (mrow(i), 0, 5)),
            pl.BlockSpec((None, d, th), lambda i, j: (w_layer, 0, j)),
            pl.BlockSpec((None, th, d), lambda i, j: (w_layer, j, 0)),
            pl.BlockSpec((1, d), lambda i, j: (0, 0)),
        ],
        out_specs=pl.BlockSpec((tm, d), lambda i, j: (i, 0)),
        out_shape=jax.ShapeDtypeStruct((m, d), F32),
        scratch_shapes=[pltpu.VMEM((tm, d), BF16)],
        compiler_params=_cparams(("arbitrary", "arbitrary"), VMEM_LIMIT_MLP_V7X),
        name="mlp",
    )(x2d, nw, mod, mod, mod, w1, w2, fnw)


def kernel(x, c, ctx, c_ctx, ada_w, ada_b, norm1_w, norm2_w, w_in, na_rpb, hg_lb_logits, hg_norm_w,
           gm_ln_w, gm_ws, gm_bs, w_out, mlp_w1, mlp_w2, final_norm_w):
    bsz, seq, d = x.shape
    n_ctx = ctx.shape[1]
    depth = ada_w.shape[0]
    assert bsz < MOD_ROWS and d == D_MODEL and seq % 512 == 0 and n_ctx % 256 == 0
    assert seq // NA_GROUP_TOK >= NA_WIN_BLOCKS and seq // GRID_W >= 2 * NA_WIN_ROWS

    lb = jnp.cumsum(jax.nn.softmax(hg_lb_logits.astype(F32), axis=0), axis=0)
    lb = lb - lb[:1]
    lbp = jnp.stack([lb, jnp.log1p(-lb), 1.0 - lb], axis=2)

    cond = jnp.zeros((MOD_ROWS, d), F32).at[:bsz].set(c).at[bsz].set(c_ctx)
    mod = _ada(cond, ada_w, ada_b).reshape(depth * MOD_ROWS, 1, 6 * d)

    w_in_b = w_in[:1].astype(BF16)
    nw1 = norm1_w.reshape(depth, 1, d)
    nw2 = norm2_w.reshape(depth, 1, d)
    fnw = final_norm_w.reshape(1, d)

    xl = x.reshape(bsz * seq, d)
    xc = ctx.reshape(bsz * n_ctx, d)
    n_c = bsz * n_ctx
    tm_l, tm_c = TM_LATENT, min(TM_CONTEXT, n_c)
    for l in range(depth):
        need_ctx = l < depth - 1
        qkv_l, p_l = _inproj(xl, nw1, mod, w_in_b, l, seq, 0, tm_l)
        qkv_c, p_c = _inproj(xc, nw1, mod, w_in_b, l, n_c, bsz, tm_c)

        to_cast = [(w_out, l), (mlp_w1, l), (mlp_w2, l)] + ([(w_in, l + 1)] if need_ctx else [])
        oa_l, (w_out_b, w1_b, w2_b, *w_in_next) = _na_latent(qkv_l, qkv_c, _na_bias_slabs(na_rpb[l]),
                                                             to_cast, bsz)
        w_in_b = w_in_next[0] if w_in_next else None
        os_l, os_c = _hgrn2(p_l, p_c, lbp[l], bsz)
        mix = (hg_norm_w[l], gm_ln_w[l], gm_ws[l], gm_bs[l])
        xl = _outproj(oa_l, os_l, p_l, *mix, w_out_b, 0, xl, mod, l, seq, 0, tm_l)
        xl = _mlp(xl, nw2, mod, w1_b, w2_b, 0, fnw, l, seq, 0, tm_l, final_norm=not need_ctx)
        if need_ctx:
            oa_c = _ctx_attention(qkv_c, bsz)
            xc = _outproj(oa_c, os_c, p_c, *mix, w_out_b, 0, xc, mod, l, n_c, bsz, tm_c)
            xc = _mlp(xc, nw2, mod, w1_b, w2_b, 0, fnw, l, n_c, bsz, tm_c, final_norm=False)
    return xl.reshape(bsz, seq, d)
```

```python
import functools

import numpy as np
import jax
import jax.numpy as jnp
from jax import lax
from jax.experimental import pallas as pl
from jax.experimental.pallas import tpu as pltpu

F32 = jnp.float32
BF16 = jnp.bfloat16

D_MODEL = 2048
DEPTH = 2
GRID_W = 64
EPS = 1e-6

NA_HEAD_DIM = 128
NA_HEADS = 8
NA_WIDTH = NA_HEADS * NA_HEAD_DIM
NA_WIN_ROWS = 8
NA_WIN_COLS = 16

HG_HEADS = 4
HG_DIM = 128
HG_WIDTH = HG_HEADS * HG_DIM
HG_CHUNK = 64

GM_GROUPS = 4
GM_DIM = 128
GM_WIDTH = GM_GROUPS * GM_DIM
GM_CHUNK = 128

IN_WIDTH = 3 * NA_WIDTH + 5 * HG_WIDTH + 2 * GM_WIDTH
MLP_HIDDEN = 4 * D_MODEL
LANE = 128
LOG2E = 1.4426950408889634
BF16_SUBLANES = 16

QKV_WIDTH = 3 * NA_WIDTH
REST_WIDTH = IN_WIDTH - QKV_WIDTH
REST_MAIN = QKV_WIDTH
REST_TAIL = REST_WIDTH - REST_MAIN
NA_SCORE_SCALE = NA_HEAD_DIM ** -0.5 * LOG2E
_QA, _KA, _VA = 0, NA_HEADS, 2 * NA_HEADS
_HG0 = 0
_GM_U = 5 * HG_WIDTH // GM_WIDTH
_GM_V = _GM_U + 1
_HG_G = 4

VMEM_LIMIT_V7X = 56 * 1024 * 1024
VMEM_LIMIT_MLP_V7X = 60 * 1024 * 1024
ADA_TN = 1536
TM_LATENT = 512
TM_CONTEXT = 512
MLP_TH = 2048
MLP_TH_INNER = 512
NORM_CHUNK_ROWS = 32
NORM_UNROLL = 4
MOD_ROWS = 16


def _cparams(sem, vmem_limit=VMEM_LIMIT_V7X):
    return pltpu.CompilerParams(dimension_semantics=sem, vmem_limit_bytes=vmem_limit)


def _silu(x):
    h = 0.5 * x
    return h + h * jnp.tanh(h)


def _gelu_tanh(x):
    return 0.5 * x * (1.0 + jnp.tanh(0.7978845608028654 * (x + 0.044715 * (x * x * x))))


def _norm_modulate_store(x_ref, nw_ref, sh_ref, sc_ref, h_ref, row0=0, n_rows=None, inline=False):
    n_rows = x_ref.shape[0] if n_rows is None else n_rows
    gain = nw_ref[...] * (1.0 + sc_ref[0])
    shift = sh_ref[0]

    def body(i, carry):
        r = row0 + i * NORM_CHUNK_ROWS
        rows = pl.ds(r if inline else pl.multiple_of(r, NORM_CHUNK_ROWS), NORM_CHUNK_ROWS)
        x = x_ref[rows, :]
        rs = lax.rsqrt(jnp.mean(x * x, axis=-1, keepdims=True) + EPS)
        h_ref[rows, :] = (x * rs * gain + shift).astype(h_ref.dtype)
        return carry

    if inline:
        for i in range(n_rows // NORM_CHUNK_ROWS):
            body(i, 0)
    else:
        lax.fori_loop(0, n_rows // NORM_CHUNK_ROWS, body, 0, unroll=NORM_UNROLL)


def _ada_kernel(c_ref, w_ref, b_ref, o_ref):
    s = _silu(c_ref[...]).astype(BF16)
    o_ref[0] = jnp.dot(s, w_ref[0].astype(BF16), preferred_element_type=F32) + b_ref[0]


def _ada(cond, ada_w, ada_b):
    depth, d, n = ada_w.shape
    tn = ADA_TN
    return pl.pallas_call(
        _ada_kernel,
        grid=(depth, n // tn),
        in_specs=[
            pl.BlockSpec((cond.shape[0], d), lambda l, j: (0, 0)),
            pl.BlockSpec((1, d, tn), lambda l, j: (l, 0, j)),
            pl.BlockSpec((1, 1, tn), lambda l, j: (l, 0, j)),
        ],
        out_specs=pl.BlockSpec((1, cond.shape[0], tn), lambda l, j: (l, 0, j)),
        out_shape=jax.ShapeDtypeStruct((depth, cond.shape[0], n), F32),
        compiler_params=_cparams(("arbitrary", "arbitrary")),
        name="ada",
    )(cond, ada_w, ada_b.reshape(depth, 1, n))


def _inproj_qkv_kernel(x_ref, nw_ref, sh_ref, sc_ref, w_ref, o_ref, h_ref):
    half = x_ref.shape[0] // 2
    _norm_modulate_store(x_ref, nw_ref, sh_ref, sc_ref, h_ref, 0, half)
    for r0 in (0, half):
        rows = slice(r0, r0 + half)
        h = h_ref[rows, :]
        q = jnp.dot(h, w_ref[:, :NA_WIDTH], preferred_element_type=F32)
        o_ref[rows, :NA_WIDTH] = (q * NA_SCORE_SCALE).astype(o_ref.dtype)
        if r0 == 0:
            _norm_modulate_store(x_ref, nw_ref, sh_ref, sc_ref, h_ref, half, half, inline=True)
        o_ref[rows, NA_WIDTH:] = jnp.dot(h, w_ref[:, NA_WIDTH:], preferred_element_type=F32).astype(o_ref.dtype)


def _inproj_rest_kernel(h_ref, wa_ref, wb_ref, o_ref):
    h = h_ref[...]
    o_ref[:, :REST_MAIN] = jnp.dot(h, wa_ref[...], preferred_element_type=F32)
    o_ref[:, REST_MAIN:] = jnp.dot(h, wb_ref[...], preferred_element_type=F32)


def _inproj(x2d, nw, mod, w, layer, rows_per_mod, mod_row0, tm):
    m, d = x2d.shape
    mrow = lambda i: layer * MOD_ROWS + mod_row0 + (i * tm) // rows_per_mod
    assert w.shape[0] == 1 and w.shape[2] == QKV_WIDTH + REST_MAIN + REST_TAIL
    assert (QKV_WIDTH + REST_MAIN) % REST_TAIL == 0
    qkv, h = pl.pallas_call(
        _inproj_qkv_kernel,
        grid=(m // tm,),
        in_specs=[
            pl.BlockSpec((tm, d), lambda i: (i, 0)),
            pl.BlockSpec((None, 1, d), lambda i: (layer, 0, 0)),
            pl.BlockSpec((1, 1, d), lambda i: (mrow(i), 0, 0)),
            pl.BlockSpec((1, 1, d), lambda i: (mrow(i), 0, 1)),
            pl.BlockSpec((None, d, QKV_WIDTH), lambda i: (0, 0, 0)),
        ],
        out_specs=[pl.BlockSpec((tm, QKV_WIDTH), lambda i: (i, 0)), pl.BlockSpec((tm, d), lambda i: (i, 0))],
        out_shape=[jax.ShapeDtypeStruct((m, QKV_WIDTH), BF16), jax.ShapeDtypeStruct((m, d), BF16)],
        compiler_params=_cparams(("arbitrary",)),
        name="inproj_qkv",
    )(x2d, nw, mod, mod, w)
    rest = pl.pallas_call(
        _inproj_rest_kernel,
        grid=(m // tm,),
        in_specs=[
            pl.BlockSpec((tm, d), lambda i: (i, 0)),
            pl.BlockSpec((None, d, REST_MAIN), lambda i: (0, 0, 1)),
            pl.BlockSpec((None, d, REST_TAIL), lambda i: (0, 0, (QKV_WIDTH + REST_MAIN) // REST_TAIL)),
        ],
        out_specs=pl.BlockSpec((tm, REST_WIDTH), lambda i: (i, 0)),
        out_shape=jax.ShapeDtypeStruct((m, REST_WIDTH), F32),
        compiler_params=_cparams(("arbitrary",)),
        name="inproj_rest",
    )(h, w, w)
    return qkv, rest


NA_GROUP_ROWS = 4
NA_GROUP_TOK = NA_GROUP_ROWS * GRID_W
NA_WIN_BLOCKS = 3
NA_GROUPS_PER_STEP = 8
NA_LOOP_UNROLL = 1


NA_MASKED_SLAB = 2 * NA_WIN_ROWS - 1


def _na_bias_slabs(rpb):
    h = rpb.shape[0]
    kcol = np.arange(GRID_W)[:, None]
    qcol = np.arange(GRID_W)[None, :]
    wstart = np.clip(qcol - NA_WIN_COLS // 2, 0, GRID_W - NA_WIN_COLS)
    valid_col = (kcol >= wstart) & (kcol < wstart + NA_WIN_COLS)
    pad = GRID_W - NA_WIN_COLS
    padded = jnp.pad(rpb.astype(F32), ((0, 0), (0, 0), (pad, pad)))
    m = jnp.tile(padded, (1, 1, GRID_W + 1))[:, :, :GRID_W * 2 * GRID_W]
    m = m.reshape(h, rpb.shape[1], GRID_W, 2 * GRID_W)[..., :GRID_W]
    slabs = jnp.where(valid_col, m[..., ::-1] * LOG2E, -1e30)
    slabs = jnp.concatenate([slabs, jnp.full((h, 1, GRID_W, GRID_W), -1e30, F32)], axis=1)
    return jnp.concatenate([slabs, slabs], axis=-1)


def _na_slab_index(rows):
    n_groups = rows // NA_GROUP_ROWS
    krel = np.arange(NA_WIN_BLOCKS * NA_GROUP_ROWS)[:, None]
    qrel = np.arange(NA_GROUP_ROWS)[None, :]
    idx = []
    for g in (0, 1, n_groups - 1):
        krow = int(np.clip(g - 1, 0, n_groups - NA_WIN_BLOCKS)) * NA_GROUP_ROWS + krel
        qrow = g * NA_GROUP_ROWS + qrel
        r0 = np.clip(qrow - NA_WIN_ROWS // 2, 0, rows - NA_WIN_ROWS)
        in_window = (krow >= r0) & (krow < r0 + NA_WIN_ROWS)
        idx.append(np.where(in_window, krow - qrow + NA_WIN_ROWS - 1, NA_MASKED_SLAB))
    return np.stack(idx).tolist()


def _na_kernel(q_ref, k_ref, v_ref, kc_ref, vc_ref, slab_ref, *refs, n_weights, slab_index):
    w_refs, o_ref, w_bf16_refs = refs[:n_weights], refs[n_weights], refs[n_weights + 1:2 * n_weights + 1]
    vt_ref, bias_ref = refs[2 * n_weights + 1:]
    n_groups = q_ref.shape[0] // NA_GROUP_TOK
    gt = NA_GROUP_TOK
    nt = (((1,), (1,)), ((), ()))
    dot = functools.partial(jnp.dot, preferred_element_type=F32)

    for src_ref, dst_ref in zip(w_refs, w_bf16_refs):
        dst_ref[...] = src_ref[...].astype(BF16)

    @pl.when(pl.program_id(1) == 0)
    def _():
        left = lax.broadcasted_iota(jnp.int32, (GRID_W, 2 * GRID_W), 1) < GRID_W
        for kind, per_key_row in enumerate(slab_index):
            for j, per_query_row in enumerate(per_key_row):
                for a in range(0, NA_GROUP_ROWS, 2):
                    tile = jnp.where(left, slab_ref[0, per_query_row[a]], slab_ref[0, per_query_row[a + 1]])
                    bias_ref[kind, j * GRID_W:(j + 1) * GRID_W, a * GRID_W:(a + 2) * GRID_W] = tile

    for b in range(n_groups):
        vt_ref[b] = v_ref[b * gt:(b + 1) * gt, :].T
    kc = kc_ref[...]
    vct = vc_ref[...].T

    def group(g):
        blk = jnp.clip(g - 1, 0, n_groups - NA_WIN_BLOCKS)
        kind = jnp.where(g == 0, 0, jnp.where(g == n_groups - 1, 2, 1))
        qg = q_ref[pl.ds(pl.multiple_of(g * gt, gt), gt), :]
        kw = k_ref[pl.ds(pl.multiple_of(blk * gt, gt), NA_WIN_BLOCKS * gt), :]
        s_lat = lax.dot_general(kw, qg, nt, preferred_element_type=F32) + bias_ref[kind]
        s_ctx = lax.dot_general(kc, qg, nt, preferred_element_type=F32)
        yield
        m = jnp.maximum(jnp.max(s_lat, axis=0, keepdims=True), jnp.max(s_ctx, axis=0, keepdims=True))
        p_lat = jnp.exp2(s_lat - m)
        p_ctx = jnp.exp2(s_ctx - m)
        l = jnp.sum(p_lat, axis=0, keepdims=True) + jnp.sum(p_ctx, axis=0, keepdims=True)
        yield
        ot = dot(vct, p_ctx.astype(BF16))
        for j in range(NA_WIN_BLOCKS):
            ot = ot + dot(vt_ref[blk + j], p_lat[j * gt:(j + 1) * gt, :].astype(BF16))
        yield
        o_ref[pl.ds(pl.multiple_of(g * gt, gt), gt), :] = (ot / l).T.astype(o_ref.dtype)
        yield

    def body(i, carry):
        groups = [group(i * NA_GROUPS_PER_STEP + u) for u in range(NA_GROUPS_PER_STEP)]
        for _ in range(4):
            for grp in groups:
                next(grp)
        return carry

    lax.fori_loop(0, n_groups // NA_GROUPS_PER_STEP, body, 0, unroll=NA_LOOP_UNROLL)


def _na_latent(p_l, p_c, slabs, weights, b):
    t = p_l.shape[0] // b
    ctx = p_c.shape[0] // b
    hd = NA_HEAD_DIM
    n_steps = NA_HEADS * b
    w_specs_in, w_specs_out, w_shapes = [], [], []
    for w, layer in weights:
        rows, cols = w.shape[1] // n_steps, w.shape[2]
        assert w.shape[1] % n_steps == 0 and rows % BF16_SUBLANES == 0
        w_specs_in.append(pl.BlockSpec((None, rows, cols), lambda h, i, layer=layer: (layer, h * b + i, 0)))
        w_specs_out.append(pl.BlockSpec((None, rows, cols), lambda h, i: (0, h * b + i, 0)))
        w_shapes.append(jax.ShapeDtypeStruct((1,) + w.shape[1:], BF16))
    oa, *w_bf16 = pl.pallas_call(
        functools.partial(_na_kernel, n_weights=len(weights), slab_index=_na_slab_index(t // GRID_W)),
        grid=(NA_HEADS, b),
        in_specs=[
            pl.BlockSpec((t, hd), lambda h, i: (i, _QA + h)),
            pl.BlockSpec((t, hd), lambda h, i: (i, _KA + h)),
            pl.BlockSpec((t, hd), lambda h, i: (i, _VA + h)),
            pl.BlockSpec((ctx, hd), lambda h, i: (i, _KA + h)),
            pl.BlockSpec((ctx, hd), lambda h, i: (i, _VA + h)),
            pl.BlockSpec((1,) + slabs.shape[1:], lambda h, i: (h, 0, 0, 0)),
        ] + w_specs_in,
        out_specs=[pl.BlockSpec((t, hd), lambda h, i: (i, h))] + w_specs_out,
        out_shape=[jax.ShapeDtypeStruct((b * t, NA_WIDTH), BF16)] + w_shapes,
        scratch_shapes=[
            pltpu.VMEM((t // NA_GROUP_TOK, hd, NA_GROUP_TOK), BF16),
            pltpu.VMEM((3, NA_WIN_BLOCKS * NA_GROUP_TOK, NA_GROUP_TOK), F32),
        ],
        compiler_params=_cparams(("arbitrary", "arbitrary")),
        name="na_latent",
    )(p_l, p_l, p_l, p_c, p_c, slabs, *(w for w, _ in weights))
    return oa, w_bf16


def _ctx_attn_kernel(q_ref, k_ref, v_ref, o_ref):
    for h in range(NA_HEADS):
        hs = slice(h * NA_HEAD_DIM, (h + 1) * NA_HEAD_DIM)
        s = lax.dot_general(q_ref[:, hs], k_ref[:, hs], (((1,), (1,)), ((), ())), preferred_element_type=F32)
        p = jnp.exp2(s - jnp.max(s, axis=-1, keepdims=True))
        l = jnp.sum(p, axis=-1, keepdims=True)
        o = jnp.dot(p.astype(BF16), v_ref[:, hs], preferred_element_type=F32)
        o_ref[:, hs] = (o / l).astype(o_ref.dtype)


def _ctx_attention(p_c, b):
    ctx = p_c.shape[0] // b
    return pl.pallas_call(
        _ctx_attn_kernel,
        grid=(b,),
        in_specs=[
            pl.BlockSpec((ctx, NA_WIDTH), lambda i: (i, 0)),
            pl.BlockSpec((ctx, NA_WIDTH), lambda i: (i, 1)),
            pl.BlockSpec((ctx, NA_WIDTH), lambda i: (i, 2)),
        ],
        out_specs=pl.BlockSpec((ctx, NA_WIDTH), lambda i: (i, 0)),
        out_shape=jax.ShapeDtypeStruct((b * ctx, NA_WIDTH), BF16),
        compiler_params=_cparams(("arbitrary",)),
        name="ctx_attention",
    )(p_c, p_c, p_c)


def _gm_block(u_ref, v_ref, lnw_ref, ws_ref, bs_ref, o_ref):
    for ck in range(u_ref.shape[0] // GM_CHUNK):
        rows = slice(ck * GM_CHUNK, (ck + 1) * GM_CHUNK)
        uf = _gelu_tanh(u_ref[rows, :])
        vf = _gelu_tanh(v_ref[rows, :])
        for g in range(GM_GROUPS):
            sl = slice(g * GM_DIM, (g + 1) * GM_DIM)
            vg = vf[:, sl]
            mu = jnp.mean(vg, axis=-1, keepdims=True)
            dv = vg - mu
            var = jnp.mean(dv * dv, axis=-1, keepdims=True)
            vn = dv * lax.rsqrt(var + EPS) * lnw_ref[:, sl]
            mixed = jnp.dot(ws_ref[g].astype(BF16), vn.astype(BF16), preferred_element_type=F32) + bs_ref[g]
            o_ref[rows, sl] = (uf[:, sl] * mixed).astype(o_ref.dtype)


HG_PAIR = 2
HG_LEVELS = (32, 16, 8)
HG_DIAG = 8
HG_PREP_ROWS = 256
HG_SCAN_UNROLL = 8


def _split3_dot(tri, g):
    g0 = g.astype(BF16)
    r1 = g - g0.astype(F32)
    g1 = r1.astype(BF16)
    g2 = (r1 - g1.astype(F32)).astype(BF16)
    dot = functools.partial(jnp.dot, preferred_element_type=F32)
    return dot(tri, g0) + dot(tri, g1) + dot(tri, g2)


def _hg_kernel(ql_ref, ffl_ref, fbl_ref, il_ref,
               qc_ref, ffc_ref, fbc_ref, ic_ref,
               lbp_ref, tri_ref, code_ref, lsum_ref, ol_ref, oc_ref,
               qh_s, kf_s, kb_s, ef_s, eb_s, of_s, ob_s, st_s):
    c = HG_CHUNK
    hd = HG_DIM
    n_ctx = qc_ref.shape[0]
    n_lat = ql_ref.shape[0]
    scale = hd ** -0.5
    nt = (((1,), (1,)), ((), ()))
    tn = (((0,), (0,)), ((), ()))

    def prep(q_ref, ff_ref, fb_ref, base, n):
        step = HG_PREP_ROWS
        for t0 in range(0, n, step):
            src = slice(t0, t0 + step)
            dst = slice(base + t0, base + t0 + step)
            qh = _silu(q_ref[src, :]) * scale
            for h in range(HG_PAIR):
                qh_s[h, dst, :] = qh[:, h * hd:(h + 1) * hd]
            for d, (f_ref, k_s, e_s) in enumerate(((ff_ref, kf_s, ef_s), (fb_ref, kb_s, eb_s))):
                x = f_ref[src, :]
                sp = jnp.maximum(-x, 0.0) + jnp.log(1.0 + jnp.exp(-jnp.abs(x)))
                lb = lbp_ref[d, 0:1, :]
                y = lbp_ref[d, 1:2, :] - sp
                ey = jnp.exp(y)
                log_f = jnp.where(lb > 0.0, jnp.log(lb + ey), y)
                one_minus_f = jnp.maximum(lbp_ref[d, 2:3, :] - ey, 0.0)
                e = _split3_dot(tri_ref[d], log_f) * LOG2E
                for h in range(HG_PAIR):
                    e_s[h, dst, :] = e[:, h * hd:(h + 1) * hd]
                    k_s[h, dst, :] = one_minus_f[:, h * hd:(h + 1) * hd]

    prep(qc_ref, ffc_ref, fbc_ref, 0, n_ctx)
    prep(ql_ref, ffl_ref, fbl_ref, n_ctx, n_lat)

    diag_code = len(HG_LEVELS) + 1

    st_s[...] = jnp.zeros_like(st_s)

    def chunk(off, v, h, d, out):
        fwd = d == 0
        k_s, e_s = (kf_s, ef_s) if fwd else (kb_s, eb_s)
        q = qh_s[h, pl.ds(off, c), :]
        k = k_s[h, pl.ds(off, c), :]
        e = e_s[h, pl.ds(off, c), :]
        code = code_ref[d]
        prods = []
        for i in range(c // HG_DIAG):
            bs = slice(i * HG_DIAG, (i + 1) * HG_DIAG)
            row_prods = []
            for s in range(HG_DIAG):
                kr = k_s[h, pl.ds(off + i * HG_DIAG + s, 1), :]
                er = e_s[h, pl.ds(off + i * HG_DIAG + s, 1), :]
                decay = jnp.exp2(jnp.minimum(e[bs] - er, 0.0))
                row_prods.append((q[bs] * kr * decay).astype(BF16))
            prods.append(jnp.concatenate(row_prods, axis=1))
        diag = jnp.dot(jnp.concatenate(prods, axis=0), lsum_ref[...], preferred_element_type=F32)
        yield
        att = None
        for li, w in enumerate(HG_LEVELS):
            zeros = jnp.zeros((w, hd), F32)
            q_parts, k_parts = [], []
            for a in range(0, c, 2 * w):
                lo, hi = slice(a, a + w), slice(a + w, a + 2 * w)
                if fwd:
                    ref = e_s[h, pl.ds(off + a + w - 1, 1), :]
                    q_parts += [zeros, q[hi] * jnp.exp2(e[hi] - ref)]
                    k_parts += [k[lo] * jnp.exp2(ref - e[lo]), zeros]
                else:
                    ref = e_s[h, pl.ds(off + a + w, 1), :]
                    q_parts += [q[lo] * jnp.exp2(e[lo] - ref), zeros]
                    k_parts += [zeros, k[hi] * jnp.exp2(ref - e[hi])]
            qs = jnp.concatenate(q_parts, axis=0).astype(BF16)
            ks = jnp.concatenate(k_parts, axis=0).astype(BF16)
            a_w = lax.dot_general(qs, ks, nt, preferred_element_type=F32)
            att = a_w if att is None else jnp.where(code == li + 1, a_w, att)
            yield
        e_end = e_s[h, pl.ds(off + (c - 1 if fwd else 0), 1), :]
        st = st_s[h, d]
        qi = (q * jnp.exp2(e)).astype(BF16)
        ki = (k * jnp.exp2(e_end - e)).astype(BF16)
        o = lax.dot_general(qi, st.astype(BF16), nt, preferred_element_type=F32)
        yield
        st_new = st * jnp.exp2(e_end) + lax.dot_general(v, ki, tn, preferred_element_type=F32)
        yield
        att = jnp.where(code == diag_code, diag, att)
        out.append((o + jnp.dot(att.astype(BF16), v, preferred_element_type=F32), st_new))
        yield

    n_stages = len(HG_LEVELS) + 4

    def scan(v_ref, base, n):
        def body(i, carry):
            rf = pl.multiple_of(i * c, c)
            rb = pl.multiple_of((n - 1 - i) * c, c)
            vf = v_ref[pl.ds(rf, c), :].astype(BF16)
            vb = v_ref[pl.ds(rb, c), :].astype(BF16)
            chains = []
            for h in range(HG_PAIR):
                hs = slice(h * hd, (h + 1) * hd)
                for d, r, v in ((0, rf, vf), (1, rb, vb)):
                    out = []
                    chains.append((h, d, r, out, chunk(base + r, v[:, hs], h, d, out)))
            for _ in range(n_stages):
                for chain in chains:
                    next(chain[-1])
            for h, d, r, out, _ in chains:
                o, st_new = out[0]
                (of_s if d == 0 else ob_s)[h, pl.ds(base + r, c), :] = o
                st_s[h, d] = st_new
            return carry

        lax.fori_loop(0, n, body, 0, unroll=HG_SCAN_UNROLL)

    scan(ic_ref, 0, n_ctx // c)
    scan(il_ref, n_ctx, n_lat // c)

    def finish(o_ref, base, n):
        step = HG_PREP_ROWS
        for t0 in range(0, n, step):
            src = slice(t0, t0 + step)
            dst = slice(base + t0, base + t0 + step)
            for h in range(HG_PAIR):
                o_ref[src, h * hd:(h + 1) * hd] = of_s[h, dst, :] + ob_s[h, dst, :]

    finish(oc_ref, 0, n_ctx)
    finish(ol_ref, n_ctx, n_lat)


def _hg_constants():
    c = HG_CHUNK
    row, col = np.arange(c)[:, None], np.arange(c)[None, :]
    blocks = np.eye(HG_PREP_ROWS // c)
    tri = np.stack([np.kron(blocks, col <= row), np.kron(blocks, col >= row)]).astype(np.float32)
    same8 = (row // HG_DIAG) == (col // HG_DIAG)
    diag_code = len(HG_LEVELS) + 1
    code_f = np.where(same8 & (col <= row), diag_code, 0)
    code_b = np.where(same8 & (col >= row), diag_code, 0)
    for li, w in enumerate(HG_LEVELS):
        same = (row // (2 * w)) == (col // (2 * w))
        t_hi, s_hi = (row % (2 * w)) >= w, (col % (2 * w)) >= w
        code_f = np.where(same & t_hi & ~s_hi, li + 1, code_f)
        code_b = np.where(same & ~t_hi & s_hi, li + 1, code_b)
    codes = np.stack([code_f, code_b]).astype(np.int32)
    lane_sum = (np.arange(HG_DIAG * HG_DIM)[:, None] // HG_DIM == col % HG_DIAG).astype(np.float32)
    return jnp.asarray(tri, BF16), jnp.asarray(codes), jnp.asarray(lane_sum, BF16)


def _hgrn2(p_l, p_c, lbp, b):
    tri, codes, lane_sum = _hg_constants()
    t = p_l.shape[0] // b
    ctx = p_c.shape[0] // b
    hd = HG_DIM
    pw = HG_PAIR * hd
    n = t + ctx
    first = _HG0 * LANE // pw

    def col(stream):
        return lambda i, j: (i, first + stream * (HG_HEADS // HG_PAIR) + j)

    lat_specs = [pl.BlockSpec((t, pw), col(s)) for s in range(4)]
    ctx_specs = [pl.BlockSpec((ctx, pw), col(s)) for s in range(4)]
    big = lambda: pltpu.VMEM((HG_PAIR, n, hd), F32)
    return pl.pallas_call(
        _hg_kernel,
        grid=(b, HG_HEADS // HG_PAIR),
        in_specs=lat_specs + ctx_specs + [
            pl.BlockSpec((2, 3, pw), lambda i, j: (0, 0, j)),
            pl.BlockSpec(tri.shape, lambda i, j: (0, 0, 0)),
            pl.BlockSpec(codes.shape, lambda i, j: (0, 0, 0)),
            pl.BlockSpec(lane_sum.shape, lambda i, j: (0, 0)),
        ],
        out_specs=[
            pl.BlockSpec((t, pw), lambda i, j: (i, j)),
            pl.BlockSpec((ctx, pw), lambda i, j: (i, j)),
        ],
        out_shape=[
            jax.ShapeDtypeStruct((b * t, HG_WIDTH), F32),
            jax.ShapeDtypeStruct((b * ctx, HG_WIDTH), F32),
        ],
        scratch_shapes=[big() for _ in range(7)] + [pltpu.VMEM((HG_PAIR, 2, hd, hd), F32)],
        compiler_params=_cparams(("arbitrary", "arbitrary")),
        name="hgrn2",
    )(*([p_l] * 4 + [p_c] * 4 + [lbp, tri, codes, lane_sum]))


def _outproj_kernel(oa_ref, os_ref, hg_ref, hnw_ref, u_ref, v_ref, lnw_ref, ws_ref, bs_ref,
                    wa_ref, wb_ref, wc_ref, x_ref, g_ref, o_ref, ob_ref, oc_ref):
    dot = functools.partial(jnp.dot, preferred_element_type=F32)
    y = dot(oa_ref[...], wa_ref[...])
    for h in range(HG_HEADS):
        hs = slice(h * HG_DIM, (h + 1) * HG_DIM)
        o = os_ref[:, hs]
        o = o * lax.rsqrt(jnp.mean(o * o, axis=-1, keepdims=True) + EPS) * hnw_ref[...]
        ob_ref[:, hs] = (o * _silu(hg_ref[:, hs])).astype(ob_ref.dtype)
    y = y + dot(ob_ref[...], wb_ref[...])
    _gm_block(u_ref, v_ref, lnw_ref, ws_ref, bs_ref, oc_ref)
    y = y + dot(oc_ref[...], wc_ref[...])
    o_ref[...] = x_ref[...] + g_ref[0] * y


def _outproj(oa, osum, p2, hg_norm_w, gm_ln_w, gm_ws, gm_bs, w, w_layer, x2d, mod, layer, rows_per_mod,
             mod_row0, tm):
    m, d = x2d.shape
    mrow = lambda i: layer * MOD_ROWS + mod_row0 + (i * tm) // rows_per_mod
    nb = NA_WIDTH // HG_WIDTH
    assert tm % GM_CHUNK == 0
    return pl.pallas_call(
        _outproj_kernel,
        grid=(m // tm,),
        in_specs=[
            pl.BlockSpec((tm, NA_WIDTH), lambda i: (i, 0)),
            pl.BlockSpec((tm, HG_WIDTH), lambda i: (i, 0)),
            pl.BlockSpec((tm, HG_WIDTH), lambda i: (i, _HG_G)),
            pl.BlockSpec((1, HG_DIM), lambda i: (0, 0)),
            pl.BlockSpec((tm, GM_WIDTH), lambda i: (i, _GM_U)),
            pl.BlockSpec((tm, GM_WIDTH), lambda i: (i, _GM_V)),
            pl.BlockSpec((1, GM_WIDTH), lambda i: (0, 0)),
            pl.BlockSpec((GM_GROUPS, GM_CHUNK, GM_CHUNK), lambda i: (0, 0, 0)),
            pl.BlockSpec((GM_GROUPS, GM_CHUNK, 1), lambda i: (0, 0, 0)),
            pl.BlockSpec((None, NA_WIDTH, d), lambda i: (w_layer, 0, 0)),
            pl.BlockSpec((None, HG_WIDTH, d), lambda i: (w_layer, nb, 0)),
            pl.BlockSpec((None, GM_WIDTH, d), lambda i: (w_layer, nb + 1, 0)),
            pl.BlockSpec((tm, d), lambda i: (i, 0)),
            pl.BlockSpec((1, 1, d), lambda i: (mrow(i), 0, 2)),
        ],
        out_specs=pl.BlockSpec((tm, d), lambda i: (i, 0)),
        out_shape=jax.ShapeDtypeStruct((m, d), F32),
        scratch_shapes=[pltpu.VMEM((tm, HG_WIDTH), BF16), pltpu.VMEM((tm, GM_WIDTH), BF16)],
        compiler_params=_cparams(("arbitrary",)),
        name="outproj",
    )(oa, osum, p2, hg_norm_w.reshape(1, HG_DIM), p2, p2, gm_ln_w.reshape(1, GM_WIDTH), gm_ws,
      gm_bs.reshape(GM_GROUPS, GM_CHUNK, 1), w, w, w, x2d, mod)


def _mlp_kernel(x_ref, nw_ref, sh_ref, sc_ref, g_ref, w1_ref, w2_ref, fnw_ref, o_ref, h_ref, *, final_norm):
    j = pl.program_id(1)

    @pl.when(j == 0)
    def _():
        _norm_modulate_store(x_ref, nw_ref, sh_ref, sc_ref, h_ref)
        o_ref[...] = jnp.zeros_like(o_ref)

    h = h_ref[...]
    for c0 in range(0, w1_ref.shape[1], MLP_TH_INNER):
        cols = slice(c0, c0 + MLP_TH_INNER)
        a = jnp.maximum(jnp.dot(h, w1_ref[:, cols], preferred_element_type=F32), 0.0)
        o_ref[...] += jnp.dot((a * a).astype(BF16), w2_ref[cols, :], preferred_element_type=F32)

    @pl.when(j == pl.num_programs(1) - 1)
    def _():
        y = x_ref[...] + g_ref[0] * o_ref[...]
        if final_norm:
            y = y * lax.rsqrt(jnp.mean(y * y, axis=-1, keepdims=True) + EPS) * fnw_ref[...]
        o_ref[...] = y


def _mlp(x2d, nw, mod, w1, w2, w_layer, fnw, layer, rows_per_mod, mod_row0, tm, final_norm):
    m, d = x2d.shape
    hid = w1.shape[2]
    th = MLP_TH
    mrow = lambda i: layer * MOD_ROWS + mod_row0 + (i * tm) // rows_per_mod
    return pl.pallas_call(
        functools.partial(_mlp_kernel, final_norm=final_norm),
        grid=(m // tm, hid // th),
        in_specs=[
            pl.BlockSpec((tm, d), lambda i, j: (i, 0)),
            pl.BlockSpec((None, 1, d), lambda i, j: (layer, 0, 0)),
            pl.BlockSpec((1, 1, d), lambda i, j: (mrow(i), 0, 3)),
            pl.BlockSpec((1, 1, d), lambda i, j: (mrow(i), 0, 4)),
            pl.BlockSpec((1, 1, d), lambda i, j: (mrow(i), 0, 5)),
            pl.BlockSpec((None, d, th), lambda i, j: (w_layer, 0, j)),
            pl.BlockSpec((None, th, d), lambda i, j: (w_layer, j, 0)),
            pl.BlockSpec((1, d), lambda i, j: (0, 0)),
        ],
        out_specs=pl.BlockSpec((tm, d), lambda i, j: (i, 0)),
        out_shape=jax.ShapeDtypeStruct((m, d), F32),
        scratch_shapes=[pltpu.VMEM((tm, d), BF16)],
        compiler_params=_cparams(("arbitrary", "arbitrary"), VMEM_LIMIT_MLP_V7X),
        name="mlp",
    )(x2d, nw, mod, mod, mod, w1, w2, fnw)


def kernel(x, c, ctx, c_ctx, ada_w, ada_b, norm1_w, norm2_w, w_in, na_rpb, hg_lb_logits, hg_norm_w,
           gm_ln_w, gm_ws, gm_bs, w_out, mlp_w1, mlp_w2, final_norm_w):
    bsz, seq, d = x.shape
    n_ctx = ctx.shape[1]
    depth = ada_w.shape[0]
    assert bsz < MOD_ROWS and d == D_MODEL and seq % 512 == 0 and n_ctx % 256 == 0
    assert seq // NA_GROUP_TOK >= NA_WIN_BLOCKS and seq // GRID_W >= 2 * NA_WIN_ROWS

    lb = jnp.cumsum(jax.nn.softmax(hg_lb_logits.astype(F32), axis=0), axis=0)
    lb = lb - lb[:1]
    lbp = jnp.stack([lb, jnp.log1p(-lb), 1.0 - lb], axis=2)

    cond = jnp.zeros((MOD_ROWS, d), F32).at[:bsz].set(c).at[bsz].set(c_ctx)
    mod = _ada(cond, ada_w, ada_b).reshape(depth * MOD_ROWS, 1, 6 * d)

    w_in_b = w_in[:1].astype(BF16)
    nw1 = norm1_w.reshape(depth, 1, d)
    nw2 = norm2_w.reshape(depth, 1, d)
    fnw = final_norm_w.reshape(1, d)

    xl = x.reshape(bsz * seq, d)
    xc = ctx.reshape(bsz * n_ctx, d)
    n_c = bsz * n_ctx
    tm_l, tm_c = TM_LATENT, min(TM_CONTEXT, n_c)
    for l in range(depth):
        need_ctx = l < depth - 1
        qkv_l, p_l = _inproj(xl, nw1, mod, w_in_b, l, seq, 0, tm_l)
        qkv_c, p_c = _inproj(xc, nw1, mod, w_in_b, l, n_c, bsz, tm_c)

        to_cast = [(w_out, l), (mlp_w1, l), (mlp_w2, l)] + ([(w_in, l + 1)] if need_ctx else [])
        oa_l, (w_out_b, w1_b, w2_b, *w_in_next) = _na_latent(qkv_l, qkv_c, _na_bias_slabs(na_rpb[l]),
                                                             to_cast, bsz)
        w_in_b = w_in_next[0] if w_in_next else None
        os_l, os_c = _hgrn2(p_l, p_c, lbp[l], bsz)
        mix = (hg_norm_w[l], gm_ln_w[l], gm_ws[l], gm_bs[l])
        xl = _outproj(oa_l, os_l, p_l, *mix, w_out_b, 0, xl, mod, l, seq, 0, tm_l)
        xl = _mlp(xl, nw2, mod, w1_b, w2_b, 0, fnw, l, seq, 0, tm_l, final_norm=not need_ctx)
        if need_ctx:
            oa_c = _ctx_attention(qkv_c, bsz)
            xc = _outproj(oa_c, os_c, p_c, *mix, w_out_b, 0, xc, mod, l, n_c, bsz, tm_c)
            xc = _mlp(xc, nw2, mod, w1_b, w2_b, 0, fnw, l, n_c, bsz, tm_c, final_norm=False)
    return xl.reshape(bsz, seq, d)
```

```python
import functools

import numpy as np
import jax
import jax.numpy as jnp
from jax import lax
from jax.experimental import pallas as pl
from jax.experimental.pallas import tpu as pltpu

F32 = jnp.float32
BF16 = jnp.bfloat16

D_MODEL = 2048
DEPTH = 2
GRID_W = 64
EPS = 1e-6

NA_HEAD_DIM = 128
NA_HEADS = 8
NA_WIDTH = NA_HEADS * NA_HEAD_DIM
NA_WIN_ROWS = 8
NA_WIN_COLS = 16

HG_HEADS = 4
HG_DIM = 128
HG_WIDTH = HG_HEADS * HG_DIM
HG_CHUNK = 64

GM_GROUPS = 4
GM_DIM = 128
GM_WIDTH = GM_GROUPS * GM_DIM
GM_CHUNK = 128

IN_WIDTH = 3 * NA_WIDTH + 5 * HG_WIDTH + 2 * GM_WIDTH
MLP_HIDDEN = 4 * D_MODEL
LANE = 128
LOG2E = 1.4426950408889634
BF16_SUBLANES = 16

QKV_WIDTH = 3 * NA_WIDTH
REST_WIDTH = IN_WIDTH - QKV_WIDTH
REST_MAIN = QKV_WIDTH
REST_TAIL = REST_WIDTH - REST_MAIN
NA_SCORE_SCALE = NA_HEAD_DIM ** -0.5 * LOG2E
_QA, _KA, _VA = 0, NA_HEADS, 2 * NA_HEADS
_HG0 = 0
_GM_U = 5 * HG_WIDTH // GM_WIDTH
_GM_V = _GM_U + 1
_HG_G = 4

VMEM_LIMIT_V7X = 56 * 1024 * 1024
VMEM_LIMIT_MLP_V7X = 60 * 1024 * 1024
ADA_TN = 1536
TM_LATENT = 512
TM_CONTEXT = 512
MLP_TH = 2048
MLP_TH_INNER = 512
NORM_CHUNK_ROWS = 32
NORM_UNROLL = 4
MOD_ROWS = 16


def _cparams(sem, vmem_limit=VMEM_LIMIT_V7X):
    return pltpu.CompilerParams(dimension_semantics=sem, vmem_limit_bytes=vmem_limit)


def _silu(x):
    h = 0.5 * x
    return h + h * jnp.tanh(h)


def _gelu_tanh(x):
    return 0.5 * x * (1.0 + jnp.tanh(0.7978845608028654 * (x + 0.044715 * (x * x * x))))


def _norm_modulate_store(x_ref, nw_ref, sh_ref, sc_ref, h_ref, row0=0, n_rows=None, inline=False):
    n_rows = x_ref.shape[0] if n_rows is None else n_rows
    gain = nw_ref[...] * (1.0 + sc_ref[0])
    shift = sh_ref[0]

    def body(i, carry):
        r = row0 + i * NORM_CHUNK_ROWS
        rows = pl.ds(r if inline else pl.multiple_of(r, NORM_CHUNK_ROWS), NORM_CHUNK_ROWS)
        x = x_ref[rows, :]
        rs = lax.rsqrt(jnp.mean(x * x, axis=-1, keepdims=True) + EPS)
        h_ref[rows, :] = (x * rs * gain + shift).astype(h_ref.dtype)
        return carry

    if inline:
        for i in range(n_rows // NORM_CHUNK_ROWS):
            body(i, 0)
    else:
        lax.fori_loop(0, n_rows // NORM_CHUNK_ROWS, body, 0, unroll=NORM_UNROLL)


def _ada_kernel(c_ref, w_ref, b_ref, o_ref):
    s = _silu(c_ref[...]).astype(BF16)
    o_ref[0] = jnp.dot(s, w_ref[0].astype(BF16), preferred_element_type=F32) + b_ref[0]


def _ada(cond, ada_w, ada_b):
    depth, d, n = ada_w.shape
    tn = ADA_TN
    return pl.pallas_call(
        _ada_kernel,
        grid=(depth, n // tn),
        in_specs=[
            pl.BlockSpec((cond.shape[0], d), lambda l, j: (0, 0)),
            pl.BlockSpec((1, d, tn), lambda l, j: (l, 0, j)),
            pl.BlockSpec((1, 1, tn), lambda l, j: (l, 0, j)),
        ],
        out_specs=pl.BlockSpec((1, cond.shape[0], tn), lambda l, j: (l, 0, j)),
        out_shape=jax.ShapeDtypeStruct((depth, cond.shape[0], n), F32),
        compiler_params=_cparams(("arbitrary", "arbitrary")),
        name="ada",
    )(cond, ada_w, ada_b.reshape(depth, 1, n))


def _inproj_qkv_kernel(x_ref, nw_ref, sh_ref, sc_ref, w_ref, o_ref, h_ref):
    half = x_ref.shape[0] // 2
    _norm_modulate_store(x_ref, nw_ref, sh_ref, sc_ref, h_ref, 0, half)
    for r0 in (0, half):
        rows = slice(r0, r0 + half)
        h = h_ref[rows, :]
        q = jnp.dot(h, w_ref[:, :NA_WIDTH], preferred_element_type=F32)
        o_ref[rows, :NA_WIDTH] = (q * NA_SCORE_SCALE).astype(o_ref.dtype)
        if r0 == 0:
            _norm_modulate_store(x_ref, nw_ref, sh_ref, sc_ref, h_ref, half, half, inline=True)
        o_ref[rows, NA_WIDTH:] = jnp.dot(h, w_ref[:, NA_WIDTH:], preferred_element_type=F32).astype(o_ref.dtype)


def _inproj_rest_kernel(h_ref, wa_ref, wb_ref, o_ref):
    h = h_ref[...]
    o_ref[:, :REST_MAIN] = jnp.dot(h, wa_ref[...], preferred_element_type=F32)
    o_ref[:, REST_MAIN:] = jnp.dot(h, wb_ref[...], preferred_element_type=F32)


def _inproj(x2d, nw, mod, w, layer, rows_per_mod, mod_row0, tm):
    m, d = x2d.shape
    mrow = lambda i: layer * MOD_ROWS + mod_row0 + (i * tm) // rows_per_mod
    assert w.shape[0] == 1 and w.shape[2] == QKV_WIDTH + REST_MAIN + REST_TAIL
    assert (QKV_WIDTH + REST_MAIN) % REST_TAIL == 0
    qkv, h = pl.pallas_call(
        _inproj_qkv_kernel,
        grid=(m // tm,),
        in_specs=[
            pl.BlockSpec((tm, d), lambda i: (i, 0)),
            pl.BlockSpec((None, 1, d), lambda i: (layer, 0, 0)),
            pl.BlockSpec((1, 1, d), lambda i: (mrow(i), 0, 0)),
            pl.BlockSpec((1, 1, d), lambda i: (mrow(i), 0, 1)),
            pl.BlockSpec((None, d, QKV_WIDTH), lambda i: (0, 0, 0)),
        ],
        out_specs=[pl.BlockSpec((tm, QKV_WIDTH), lambda i: (i, 0)), pl.BlockSpec((tm, d), lambda i: (i, 0))],
        out_shape=[jax.ShapeDtypeStruct((m, QKV_WIDTH), BF16), jax.ShapeDtypeStruct((m, d), BF16)],
        compiler_params=_cparams(("arbitrary",)),
        name="inproj_qkv",
    )(x2d, nw, mod, mod, w)
    rest = pl.pallas_call(
        _inproj_rest_kernel,
        grid=(m // tm,),
        in_specs=[
            pl.BlockSpec((tm, d), lambda i: (i, 0)),
            pl.BlockSpec((None, d, REST_MAIN), lambda i: (0, 0, 1)),
            pl.BlockSpec((None, d, REST_TAIL), lambda i: (0, 0, (QKV_WIDTH + REST_MAIN) // REST_TAIL)),
        ],
        out_specs=pl.BlockSpec((tm, REST_WIDTH), lambda i: (i, 0)),
        out_shape=jax.ShapeDtypeStruct((m, REST_WIDTH), F32),
        compiler_params=_cparams(("arbitrary",)),
        name="inproj_rest",
    )(h, w, w)
    return qkv, rest


NA_GROUP_ROWS = 4
NA_GROUP_TOK = NA_GROUP_ROWS * GRID_W
NA_WIN_BLOCKS = 3
NA_GROUPS_PER_STEP = 8
NA_LOOP_UNROLL = 1


NA_MASKED_SLAB = 2 * NA_WIN_ROWS - 1


def _na_bias_slabs(rpb):
    h = rpb.shape[0]
    kcol = np.arange(GRID_W)[:, None]
    qcol = np.arange(GRID_W)[None, :]
    wstart = np.clip(qcol - NA_WIN_COLS // 2, 0, GRID_W - NA_WIN_COLS)
    valid_col = (kcol >= wstart) & (kcol < wstart + NA_WIN_COLS)
    pad = GRID_W - NA_WIN_COLS
    padded = jnp.pad(rpb.astype(F32), ((0, 0), (0, 0), (pad, pad)))
    m = jnp.tile(padded, (1, 1, GRID_W + 1))[:, :, :GRID_W * 2 * GRID_W]
    m = m.reshape(h, rpb.shape[1], GRID_W, 2 * GRID_W)[..., :GRID_W]
    slabs = jnp.where(valid_col, m[..., ::-1] * LOG2E, -1e30)
    slabs = jnp.concatenate([slabs, jnp.full((h, 1, GRID_W, GRID_W), -1e30, F32)], axis=1)
    return jnp.concatenate([slabs, slabs], axis=-1)


def _na_slab_index(rows):
    n_groups = rows // NA_GROUP_ROWS
    krel = np.arange(NA_WIN_BLOCKS * NA_GROUP_ROWS)[:, None]
    qrel = np.arange(NA_GROUP_ROWS)[None, :]
    idx = []
    for g in (0, 1, n_groups - 1):
        krow = int(np.clip(g - 1, 0, n_groups - NA_WIN_BLOCKS)) * NA_GROUP_ROWS + krel
        qrow = g * NA_GROUP_ROWS + qrel
        r0 = np.clip(qrow - NA_WIN_ROWS // 2, 0, rows - NA_WIN_ROWS)
        in_window = (krow >= r0) & (krow < r0 + NA_WIN_ROWS)
        idx.append(np.where(in_window, krow - qrow + NA_WIN_ROWS - 1, NA_MASKED_SLAB))
    return np.stack(idx).tolist()


def _na_kernel(q_ref, k_ref, v_ref, kc_ref, vc_ref, slab_ref, *refs, n_weights, slab_index):
    w_refs, o_ref, w_bf16_refs = refs[:n_weights], refs[n_weights], refs[n_weights + 1:2 * n_weights + 1]
    vt_ref, bias_ref = refs[2 * n_weights + 1:]
    n_groups = q_ref.shape[0] // NA_GROUP_TOK
    gt = NA_GROUP_TOK
    nt = (((1,), (1,)), ((), ()))
    dot = functools.partial(jnp.dot, preferred_element_type=F32)

    for src_ref, dst_ref in zip(w_refs, w_bf16_refs):
        dst_ref[...] = src_ref[...].astype(BF16)

    @pl.when(pl.program_id(1) == 0)
    def _():
        left = lax.broadcasted_iota(jnp.int32, (GRID_W, 2 * GRID_W), 1) < GRID_W
        for kind, per_key_row in enumerate(slab_index):
            for j, per_query_row in enumerate(per_key_row):
                for a in range(0, NA_GROUP_ROWS, 2):
                    tile = jnp.where(left, slab_ref[0, per_query_row[a]], slab_ref[0, per_query_row[a + 1]])
                    bias_ref[kind, j * GRID_W:(j + 1) * GRID_W, a * GRID_W:(a + 2) * GRID_W] = tile

    for b in range(n_groups):
        vt_ref[b] = v_ref[b * gt:(b + 1) * gt, :].T
    kc = kc_ref[...]
    vct = vc_ref[...].T

    def group(g):
        blk = jnp.clip(g - 1, 0, n_groups - NA_WIN_BLOCKS)
        kind = jnp.where(g == 0, 0, jnp.where(g == n_groups - 1, 2, 1))
        qg = q_ref[pl.ds(pl.multiple_of(g * gt, gt), gt), :]
        kw = k_ref[pl.ds(pl.multiple_of(blk * gt, gt), NA_WIN_BLOCKS * gt), :]
        s_lat = lax.dot_general(kw, qg, nt, preferred_element_type=F32) + bias_ref[kind]
        s_ctx = lax.dot_general(kc, qg, nt, preferred_element_type=F32)
        yield
        m = jnp.maximum(jnp.max(s_lat, axis=0, keepdims=True), jnp.max(s_ctx, axis=0, keepdims=True))
        p_lat = jnp.exp2(s_lat - m)
        p_ctx = jnp.exp2(s_ctx - m)
        l = jnp.sum(p_lat, axis=0, keepdims=True) + jnp.sum(p_ctx, axis=0, keepdims=True)
        yield
        ot = dot(vct, p_ctx.astype(BF16))
        for j in range(NA_WIN_BLOCKS):
            ot = ot + dot(vt_ref[blk + j], p_lat[j * gt:(j + 1) * gt, :].astype(BF16))
        yield
        o_ref[pl.ds(pl.multiple_of(g * gt, gt), gt), :] = (ot / l).T.astype(o_ref.dtype)
        yield

    def body(i, carry):
        groups = [group(i * NA_GROUPS_PER_STEP + u) for u in range(NA_GROUPS_PER_STEP)]
        for _ in range(4):
            for grp in groups:
                next(grp)
        return carry

    lax.fori_loop(0, n_groups // NA_GROUPS_PER_STEP, body, 0, unroll=NA_LOOP_UNROLL)


def _na_latent(p_l, p_c, slabs, weights, b):
    t = p_l.shape[0] // b
    ctx = p_c.shape[0] // b
    hd = NA_HEAD_DIM
    n_steps = NA_HEADS * b
    w_specs_in, w_specs_out, w_shapes = [], [], []
    for w, layer in weights:
        rows, cols = w.shape[1] // n_steps, w.shape[2]
        assert w.shape[1] % n_steps == 0 and rows % BF16_SUBLANES == 0
        w_specs_in.append(pl.BlockSpec((None, rows, cols), lambda h, i, layer=layer: (layer, h * b + i, 0)))
        w_specs_out.append(pl.BlockSpec((None, rows, cols), lambda h, i: (0, h * b + i, 0)))
        w_shapes.append(jax.ShapeDtypeStruct((1,) + w.shape[1:], BF16))
    oa, *w_bf16 = pl.pallas_call(
        functools.partial(_na_kernel, n_weights=len(weights), slab_index=_na_slab_index(t // GRID_W)),
        grid=(NA_HEADS, b),
        in_specs=[
            pl.BlockSpec((t, hd), lambda h, i: (i, _QA + h)),
            pl.BlockSpec((t, hd), lambda h, i: (i, _KA + h)),
            pl.BlockSpec((t, hd), lambda h, i: (i, _VA + h)),
            pl.BlockSpec((ctx, hd), lambda h, i: (i, _KA + h)),
            pl.BlockSpec((ctx, hd), lambda h, i: (i, _VA + h)),
            pl.BlockSpec((1,) + slabs.shape[1:], lambda h, i: (h, 0, 0, 0)),
        ] + w_specs_in,
        out_specs=[pl.BlockSpec((t, hd), lambda h, i: (i, h))] + w_specs_out,
        out_shape=[jax.ShapeDtypeStruct((b * t, NA_WIDTH), BF16)] + w_shapes,
        scratch_shapes=[
            pltpu.VMEM((t // NA_GROUP_TOK, hd, NA_GROUP_TOK), BF16),
            pltpu.VMEM((3, NA_WIN_BLOCKS * NA_GROUP_TOK, NA_GROUP_TOK), F32),
        ],
        compiler_params=_cparams(("arbitrary", "arbitrary")),
        name="na_latent",
    )(p_l, p_l, p_l, p_c, p_c, slabs, *(w for w, _ in weights))
    return oa, w_bf16


def _ctx_attn_kernel(q_ref, k_ref, v_ref, o_ref):
    for h in range(NA_HEADS):
        hs = slice(h * NA_HEAD_DIM, (h + 1) * NA_HEAD_DIM)
        s = lax.dot_general(q_ref[:, hs], k_ref[:, hs], (((1,), (1,)), ((), ())), preferred_element_type=F32)
        p = jnp.exp2(s - jnp.max(s, axis=-1, keepdims=True))
        l = jnp.sum(p, axis=-1, keepdims=True)
        o = jnp.dot(p.astype(BF16), v_ref[:, hs], preferred_element_type=F32)
        o_ref[:, hs] = (o / l).astype(o_ref.dtype)


def _ctx_attention(p_c, b):
    ctx = p_c.shape[0] // b
    return pl.pallas_call(
        _ctx_attn_kernel,
        grid=(b,),
        in_specs=[
            pl.BlockSpec((ctx, NA_WIDTH), lambda i: (i, 0)),
            pl.BlockSpec((ctx, NA_WIDTH), lambda i: (i, 1)),
            pl.BlockSpec((ctx, NA_WIDTH), lambda i: (i, 2)),
        ],
        out_specs=pl.BlockSpec((ctx, NA_WIDTH), lambda i: (i, 0)),
        out_shape=jax.ShapeDtypeStruct((b * ctx, NA_WIDTH), BF16),
        compiler_params=_cparams(("arbitrary",)),
        name="ctx_attention",
    )(p_c, p_c, p_c)


def _gm_block(u_ref, v_ref, lnw_ref, ws_ref, bs_ref, o_ref):
    for ck in range(u_ref.shape[0] // GM_CHUNK):
        rows = slice(ck * GM_CHUNK, (ck + 1) * GM_CHUNK)
        uf = _gelu_tanh(u_ref[rows, :])
        vf = _gelu_tanh(v_ref[rows, :])
        for g in range(GM_GROUPS):
            sl = slice(g * GM_DIM, (g + 1) * GM_DIM)
            vg = vf[:, sl]
            mu = jnp.mean(vg, axis=-1, keepdims=True)
            dv = vg - mu
            var = jnp.mean(dv * dv, axis=-1, keepdims=True)
            vn = dv * lax.rsqrt(var + EPS) * lnw_ref[:, sl]
            mixed = jnp.dot(ws_ref[g].astype(BF16), vn.astype(BF16), preferred_element_type=F32) + bs_ref[g]
            o_ref[rows, sl] = (uf[:, sl] * mixed).astype(o_ref.dtype)


HG_PAIR = 2
HG_LEVELS = (32, 16, 8)
HG_DIAG = 8
HG_PREP_ROWS = 256
HG_SCAN_UNROLL = 8


def _split3_dot(tri, g):
    g0 = g.astype(BF16)
    r1 = g - g0.astype(F32)
    g1 = r1.astype(BF16)
    g2 = (r1 - g1.astype(F32)).astype(BF16)
    dot = functools.partial(jnp.dot, preferred_element_type=F32)
    return dot(tri, g0) + dot(tri, g1) + dot(tri, g2)


def _hg_kernel(ql_ref, ffl_ref, fbl_ref, il_ref,
               qc_ref, ffc_ref, fbc_ref, ic_ref,
               lbp_ref, tri_ref, code_ref, lsum_ref, ol_ref, oc_ref,
               qh_s, kf_s, kb_s, ef_s, eb_s, of_s, ob_s, st_s):
    c = HG_CHUNK
    hd = HG_DIM
    n_ctx = qc_ref.shape[0]
    n_lat = ql_ref.shape[0]
    scale = hd ** -0.5
    nt = (((1,), (1,)), ((), ()))
    tn = (((0,), (0,)), ((), ()))

    def prep(q_ref, ff_ref, fb_ref, base, n):
        step = HG_PREP_ROWS
        for t0 in range(0, n, step):
            src = slice(t0, t0 + step)
            dst = slice(base + t0, base + t0 + step)
            qh = _silu(q_ref[src, :]) * scale
            for h in range(HG_PAIR):
                qh_s[h, dst, :] = qh[:, h * hd:(h + 1) * hd]
            for d, (f_ref, k_s, e_s) in enumerate(((ff_ref, kf_s, ef_s), (fb_ref, kb_s, eb_s))):
                x = f_ref[src, :]
                sp = jnp.maximum(-x, 0.0) + jnp.log(1.0 + jnp.exp(-jnp.abs(x)))
                lb = lbp_ref[d, 0:1, :]
                y = lbp_ref[d, 1:2, :] - sp
                ey = jnp.exp(y)
                log_f = jnp.where(lb > 0.0, jnp.log(lb + ey), y)
                one_minus_f = jnp.maximum(lbp_ref[d, 2:3, :] - ey, 0.0)
                e = _split3_dot(tri_ref[d], log_f) * LOG2E
                for h in range(HG_PAIR):
                    e_s[h, dst, :] = e[:, h * hd:(h + 1) * hd]
                    k_s[h, dst, :] = one_minus_f[:, h * hd:(h + 1) * hd]

    prep(qc_ref, ffc_ref, fbc_ref, 0, n_ctx)
    prep(ql_ref, ffl_ref, fbl_ref, n_ctx, n_lat)

    diag_code = len(HG_LEVELS) + 1

    st_s[...] = jnp.zeros_like(st_s)

    def chunk(off, v, h, d, out):
        fwd = d == 0
        k_s, e_s = (kf_s, ef_s) if fwd else (kb_s, eb_s)
        q = qh_s[h, pl.ds(off, c), :]
        k = k_s[h, pl.ds(off, c), :]
        e = e_s[h, pl.ds(off, c), :]
        code = code_ref[d]
        prods = []
        for i in range(c // HG_DIAG):
            bs = slice(i * HG_DIAG, (i + 1) * HG_DIAG)
            row_prods = []
            for s in range(HG_DIAG):
                kr = k_s[h, pl.ds(off + i * HG_DIAG + s, 1), :]
                er = e_s[h, pl.ds(off + i * HG_DIAG + s, 1), :]
                decay = jnp.exp2(jnp.minimum(e[bs] - er, 0.0))
                row_prods.append((q[bs] * kr * decay).astype(BF16))
            prods.append(jnp.concatenate(row_prods, axis=1))
        diag = jnp.dot(jnp.concatenate(prods, axis=0), lsum_ref[...], preferred_element_type=F32)
        yield
        att = None
        for li, w in enumerate(HG_LEVELS):
            zeros = jnp.zeros((w, hd), F32)
            q_parts, k_parts = [], []
            for a in range(0, c, 2 * w):
                lo, hi = slice(a, a + w), slice(a + w, a + 2 * w)
                if fwd:
                    ref = e_s[h, pl.ds(off + a + w - 1, 1), :]
                    q_parts += [zeros, q[hi] * jnp.exp2(e[hi] - ref)]
                    k_parts += [k[lo] * jnp.exp2(ref - e[lo]), zeros]
                else:
                    ref = e_s[h, pl.ds(off + a + w, 1), :]
                    q_parts += [q[lo] * jnp.exp2(e[lo] - ref), zeros]
                    k_parts += [zeros, k[hi] * jnp.exp2(ref - e[hi])]
            qs = jnp.concatenate(q_parts, axis=0).astype(BF16)
            ks = jnp.concatenate(k_parts, axis=0).astype(BF16)
            a_w = lax.dot_general(qs, ks, nt, preferred_element_type=F32)
            att = a_w if att is None else jnp.where(code == li + 1, a_w, att)
            yield
        e_end = e_s[h, pl.ds(off + (c - 1 if fwd else 0), 1), :]
        st = st_s[h, d]
        qi = (q * jnp.exp2(e)).astype(BF16)
        ki = (k * jnp.exp2(e_end - e)).astype(BF16)
        o = lax.dot_general(qi, st.astype(BF16), nt, preferred_element_type=F32)
        yield
        st_new = st * jnp.exp2(e_end) + lax.dot_general(v, ki, tn, preferred_element_type=F32)
        yield
        att = jnp.where(code == diag_code, diag, att)
        out.append((o + jnp.dot(att.astype(BF16), v, preferred_element_type=F32), st_new))
        yield

    n_stages = len(HG_LEVELS) + 4

    def scan(v_ref, base, n):
        def body(i, carry):
            rf = pl.multiple_of(i * c, c)
            rb = pl.multiple_of((n - 1 - i) * c, c)
            vf = v_ref[pl.ds(rf, c), :].astype(BF16)
            vb = v_ref[pl.ds(rb, c), :].astype(BF16)
            chains = []
            for h in range(HG_PAIR):
                hs = slice(h * hd, (h + 1) * hd)
                for d, r, v in ((0, rf, vf), (1, rb, vb)):
                    out = []
                    chains.append((h, d, r, out, chunk(base + r, v[:, hs], h, d, out)))
            for _ in range(n_stages):
                for chain in chains:
                    next(chain[-1])
            for h, d, r, out, _ in chains:
                o, st_new = out[0]
                (of_s if d == 0 else ob_s)[h, pl.ds(base + r, c), :] = o
                st_s[h, d] = st_new
            return carry

        lax.fori_loop(0, n, body, 0, unroll=HG_SCAN_UNROLL)

    scan(ic_ref, 0, n_ctx // c)
    scan(il_ref, n_ctx, n_lat // c)

    def finish(o_ref, base, n):
        step = HG_PREP_ROWS
        for t0 in range(0, n, step):
            src = slice(t0, t0 + step)
            dst = slice(base + t0, base + t0 + step)
            for h in range(HG_PAIR):
                o_ref[src, h * hd:(h + 1) * hd] = of_s[h, dst, :] + ob_s[h, dst, :]

    finish(oc_ref, 0, n_ctx)
    finish(ol_ref, n_ctx, n_lat)


def _hg_constants():
    c = HG_CHUNK
    row, col = np.arange(c)[:, None], np.arange(c)[None, :]
    blocks = np.eye(HG_PREP_ROWS // c)
    tri = np.stack([np.kron(blocks, col <= row), np.kron(blocks, col >= row)]).astype(np.float32)
    same8 = (row // HG_DIAG) == (col // HG_DIAG)
    diag_code = len(HG_LEVELS) + 1
    code_f = np.where(same8 & (col <= row), diag_code, 0)
    code_b = np.where(same8 & (col >= row), diag_code, 0)
    for li, w in enumerate(HG_LEVELS):
        same = (row // (2 * w)) == (col // (2 * w))
        t_hi, s_hi = (row % (2 * w)) >= w, (col % (2 * w)) >= w
        code_f = np.where(same & t_hi & ~s_hi, li + 1, code_f)
        code_b = np.where(same & ~t_hi & s_hi, li + 1, code_b)
    codes = np.stack([code_f, code_b]).astype(np.int32)
    lane_sum = (np.arange(HG_DIAG * HG_DIM)[:, None] // HG_DIM == col % HG_DIAG).astype(np.float32)
    return jnp.asarray(tri, BF16), jnp.asarray(codes), jnp.asarray(lane_sum, BF16)


def _hgrn2(p_l, p_c, lbp, b):
    tri, codes, lane_sum = _hg_constants()
    t = p_l.shape[0] // b
    ctx = p_c.shape[0] // b
    hd = HG_DIM
    pw = HG_PAIR * hd
    n = t + ctx
    first = _HG0 * LANE // pw

    def col(stream):
        return lambda i, j: (i, first + stream * (HG_HEADS // HG_PAIR) + j)

    lat_specs = [pl.BlockSpec((t, pw), col(s)) for s in range(4)]
    ctx_specs = [pl.BlockSpec((ctx, pw), col(s)) for s in range(4)]
    big = lambda: pltpu.VMEM((HG_PAIR, n, hd), F32)
    return pl.pallas_call(
        _hg_kernel,
        grid=(b, HG_HEADS // HG_PAIR),
        in_specs=lat_specs + ctx_specs + [
            pl.BlockSpec((2, 3, pw), lambda i, j: (0, 0, j)),
            pl.BlockSpec(tri.shape, lambda i, j: (0, 0, 0)),
            pl.BlockSpec(codes.shape, lambda i, j: (0, 0, 0)),
            pl.BlockSpec(lane_sum.shape, lambda i, j: (0, 0)),
        ],
        out_specs=[
            pl.BlockSpec((t, pw), lambda i, j: (i, j)),
            pl.BlockSpec((ctx, pw), lambda i, j: (i, j)),
        ],
        out_shape=[
            jax.ShapeDtypeStruct((b * t, HG_WIDTH), F32),
            jax.ShapeDtypeStruct((b * ctx, HG_WIDTH), F32),
        ],
        scratch_shapes=[big() for _ in range(7)] + [pltpu.VMEM((HG_PAIR, 2, hd, hd), F32)],
        compiler_params=_cparams(("arbitrary", "arbitrary")),
        name="hgrn2",
    )(*([p_l] * 4 + [p_c] * 4 + [lbp, tri, codes, lane_sum]))


def _outproj_kernel(oa_ref, os_ref, hg_ref, hnw_ref, u_ref, v_ref, lnw_ref, ws_ref, bs_ref,
                    wa_ref, wb_ref, wc_ref, x_ref, g_ref, o_ref, ob_ref, oc_ref):
    dot = functools.partial(jnp.dot, preferred_element_type=F32)
    y = dot(oa_ref[...], wa_ref[...])
    for h in range(HG_HEADS):
        hs = slice(h * HG_DIM, (h + 1) * HG_DIM)
        o = os_ref[:, hs]
        o = o * lax.rsqrt(jnp.mean(o * o, axis=-1, keepdims=True) + EPS) * hnw_ref[...]
        ob_ref[:, hs] = (o * _silu(hg_ref[:, hs])).astype(ob_ref.dtype)
    y = y + dot(ob_ref[...], wb_ref[...])
    _gm_block(u_ref, v_ref, lnw_ref, ws_ref, bs_ref, oc_ref)
    y = y + dot(oc_ref[...], wc_ref[...])
    o_ref[...] = x_ref[...] + g_ref[0] * y


def _outproj(oa, osum, p2, hg_norm_w, gm_ln_w, gm_ws, gm_bs, w, w_layer, x2d, mod, layer, rows_per_mod,
             mod_row0, tm):
    m, d = x2d.shape
    mrow = lambda i: layer * MOD_ROWS + mod_row0 + (i * tm) // rows_per_mod
    nb = NA_WIDTH // HG_WIDTH
    assert tm % GM_CHUNK == 0
    return pl.pallas_call(
        _outproj_kernel,
        grid=(m // tm,),
        in_specs=[
            pl.BlockSpec((tm, NA_WIDTH), lambda i: (i, 0)),
            pl.BlockSpec((tm, HG_WIDTH), lambda i: (i, 0)),
            pl.BlockSpec((tm, HG_WIDTH), lambda i: (i, _HG_G)),
            pl.BlockSpec((1, HG_DIM), lambda i: (0, 0)),
            pl.BlockSpec((tm, GM_WIDTH), lambda i: (i, _GM_U)),
            pl.BlockSpec((tm, GM_WIDTH), lambda i: (i, _GM_V)),
            pl.BlockSpec((1, GM_WIDTH), lambda i: (0, 0)),
            pl.BlockSpec((GM_GROUPS, GM_CHUNK, GM_CHUNK), lambda i: (0, 0, 0)),
            pl.BlockSpec((GM_GROUPS, GM_CHUNK, 1), lambda i: (0, 0, 0)),
            pl.BlockSpec((None, NA_WIDTH, d), lambda i: (w_layer, 0, 0)),
            pl.BlockSpec((None, HG_WIDTH, d), lambda i: (w_layer, nb, 0)),
            pl.BlockSpec((None, GM_WIDTH, d), lambda i: (w_layer, nb + 1, 0)),
            pl.BlockSpec((tm, d), lambda i: (i, 0)),
            pl.BlockSpec((1, 1, d), lambda i: (mrow(i), 0, 2)),
        ],
        out_specs=pl.BlockSpec((tm, d), lambda i: (i, 0)),
        out_shape=jax.ShapeDtypeStruct((m, d), F32),
        scratch_shapes=[pltpu.VMEM((tm, HG_WIDTH), BF16), pltpu.VMEM((tm, GM_WIDTH), BF16)],
        compiler_params=_cparams(("arbitrary",)),
        name="outproj",
    )(oa, osum, p2, hg_norm_w.reshape(1, HG_DIM), p2, p2, gm_ln_w.reshape(1, GM_WIDTH), gm_ws,
      gm_bs.reshape(GM_GROUPS, GM_CHUNK, 1), w, w, w, x2d, mod)


def _mlp_kernel(x_ref, nw_ref, sh_ref, sc_ref, g_ref, w1_ref, w2_ref, fnw_ref, o_ref, h_ref, *, final_norm):
    j = pl.program_id(1)

    def passes(rows, init, between_first_dots=None):
        h = h_ref[rows, :]
        for c0 in range(0, w1_ref.shape[1], MLP_TH_INNER):
            cols = slice(c0, c0 + MLP_TH_INNER)
            a = jnp.maximum(jnp.dot(h, w1_ref[:, cols], preferred_element_type=F32), 0.0)
            if c0 == 0 and between_first_dots is not None:
                between_first_dots()
            y = jnp.dot((a * a).astype(BF16), w2_ref[cols, :], preferred_element_type=F32)
            if init and c0 == 0:
                o_ref[rows, :] = y
            else:
                o_ref[rows, :] += y

    @pl.when(j == 0)
    def _():
        half = x_ref.shape[0] // 2
        _norm_modulate_store(x_ref, nw_ref, sh_ref, sc_ref, h_ref, 0, half)
        norm_rest = functools.partial(_norm_modulate_store, x_ref, nw_ref, sh_ref, sc_ref, h_ref, half, half, True)
        passes(slice(0, half), init=True, between_first_dots=norm_rest)
        passes(slice(half, 2 * half), init=True)

    @pl.when(j != 0)
    def _():
        passes(slice(None), init=False)

    @pl.when(j == pl.num_programs(1) - 1)
    def _():
        y = x_ref[...] + g_ref[0] * o_ref[...]
        if final_norm:
            y = y * lax.rsqrt(jnp.mean(y * y, axis=-1, keepdims=True) + EPS) * fnw_ref[...]
        o_ref[...] = y


def _mlp(x2d, nw, mod, w1, w2, w_layer, fnw, layer, rows_per_mod, mod_row0, tm, final_norm):
    m, d = x2d.shape
    hid = w1.shape[2]
    th = MLP_TH
    mrow = lambda i: layer * MOD_ROWS + mod_row0 + (i * tm) // rows_per_mod
    return pl.pallas_call(
        functools.partial(_mlp_kernel, final_norm=final_norm),
        grid=(m // tm, hid // th),
        in_specs=[
            pl.BlockSpec((tm, d), lambda i, j: (i, 0)),
            pl.BlockSpec((None, 1, d), lambda i, j: (layer, 0, 0)),
            pl.BlockSpec((1, 1, d), lambda i, j: (mrow(i), 0, 3)),
            pl.BlockSpec((1, 1, d), lambda i, j: (mrow(i), 0, 4)),
            pl.BlockSpec((1, 1, d), lambda i, j: (mrow(i), 0, 5)),
            pl.BlockSpec((None, d, th), lambda i, j: (w_layer, 0, j)),
            pl.BlockSpec((None, th, d), lambda i, j: (w_layer, j, 0)),
            pl.BlockSpec((1, d), lambda i, j: (0, 0)),
        ],
        out_specs=pl.BlockSpec((tm, d), lambda i, j: (i, 0)),
        out_shape=jax.ShapeDtypeStruct((m, d), F32),
        scratch_shapes=[pltpu.VMEM((tm, d), BF16)],
        compiler_params=_cparams(("arbitrary", "arbitrary"), VMEM_LIMIT_MLP_V7X),
        name="mlp",
    )(x2d, nw, mod, mod, mod, w1, w2, fnw)


def kernel(x, c, ctx, c_ctx, ada_w, ada_b, norm1_w, norm2_w, w_in, na_rpb, hg_lb_logits, hg_norm_w,
           gm_ln_w, gm_ws, gm_bs, w_out, mlp_w1, mlp_w2, final_norm_w):
    bsz, seq, d = x.shape
    n_ctx = ctx.shape[1]
    depth = ada_w.shape[0]
    assert bsz < MOD_ROWS and d == D_MODEL and seq % 512 == 0 and n_ctx % 256 == 0
    assert seq // NA_GROUP_TOK >= NA_WIN_BLOCKS and seq // GRID_W >= 2 * NA_WIN_ROWS

    lb = jnp.cumsum(jax.nn.softmax(hg_lb_logits.astype(F32), axis=0), axis=0)
    lb = lb - lb[:1]
    lbp = jnp.stack([lb, jnp.log1p(-lb), 1.0 - lb], axis=2)

    cond = jnp.zeros((MOD_ROWS, d), F32).at[:bsz].set(c).at[bsz].set(c_ctx)
    mod = _ada(cond, ada_w, ada_b).reshape(depth * MOD_ROWS, 1, 6 * d)

    w_in_b = w_in[:1].astype(BF16)
    nw1 = norm1_w.reshape(depth, 1, d)
    nw2 = norm2_w.reshape(depth, 1, d)
    fnw = final_norm_w.reshape(1, d)

    xl = x.reshape(bsz * seq, d)
    xc = ctx.reshape(bsz * n_ctx, d)
    n_c = bsz * n_ctx
    tm_l, tm_c = TM_LATENT, min(TM_CONTEXT, n_c)
    for l in range(depth):
        need_ctx = l < depth - 1
        qkv_l, p_l = _inproj(xl, nw1, mod, w_in_b, l, seq, 0, tm_l)
        qkv_c, p_c = _inproj(xc, nw1, mod, w_in_b, l, n_c, bsz, tm_c)

        to_cast = [(w_out, l), (mlp_w1, l), (mlp_w2, l)] + ([(w_in, l + 1)] if need_ctx else [])
        oa_l, (w_out_b, w1_b, w2_b, *w_in_next) = _na_latent(qkv_l, qkv_c, _na_bias_slabs(na_rpb[l]),
                                                             to_cast, bsz)
        w_in_b = w_in_next[0] if w_in_next else None
        os_l, os_c = _hgrn2(p_l, p_c, lbp[l], bsz)
        mix = (hg_norm_w[l], gm_ln_w[l], gm_ws[l], gm_bs[l])
        xl = _outproj(oa_l, os_l, p_l, *mix, w_out_b, 0, xl, mod, l, seq, 0, tm_l)
        xl = _mlp(xl, nw2, mod, w1_b, w2_b, 0, fnw, l, seq, 0, tm_l, final_norm=not need_ctx)
        if need_ctx:
            oa_c = _ctx_attention(qkv_c, bsz)
            xc = _outproj(oa_c, os_c, p_c, *mix, w_out_b, 0, xc, mod, l, n_c, bsz, tm_c)
            xc = _mlp(xc, nw2, mod, w1_b, w2_b, 0, fnw, l, n_c, bsz, tm_c, final_norm=False)
    return xl.reshape(bsz, seq, d)
```

```python
import functools

import numpy as np
import jax
import jax.numpy as jnp
from jax import lax
from jax.experimental import pallas as pl
from jax.experimental.pallas import tpu as pltpu

F32 = jnp.float32
BF16 = jnp.bfloat16

D_MODEL = 2048
DEPTH = 2
GRID_W = 64
EPS = 1e-6

NA_HEAD_DIM = 128
NA_HEADS = 8
NA_WIDTH = NA_HEADS * NA_HEAD_DIM
NA_WIN_ROWS = 8
NA_WIN_COLS = 16

HG_HEADS = 4
HG_DIM = 128
HG_WIDTH = HG_HEADS * HG_DIM
HG_CHUNK = 64

GM_GROUPS = 4
GM_DIM = 128
GM_WIDTH = GM_GROUPS * GM_DIM
GM_CHUNK = 128

IN_WIDTH = 3 * NA_WIDTH + 5 * HG_WIDTH + 2 * GM_WIDTH
MLP_HIDDEN = 4 * D_MODEL
LANE = 128
LOG2E = 1.4426950408889634
BF16_SUBLANES = 16

QKV_WIDTH = 3 * NA_WIDTH
REST_WIDTH = IN_WIDTH - QKV_WIDTH
REST_MAIN = QKV_WIDTH
REST_TAIL = REST_WIDTH - REST_MAIN
NA_SCORE_SCALE = NA_HEAD_DIM ** -0.5 * LOG2E
_QA, _KA, _VA = 0, NA_HEADS, 2 * NA_HEADS
_HG0 = 0
_GM_U = 5 * HG_WIDTH // GM_WIDTH
_GM_V = _GM_U + 1
_HG_G = 4

VMEM_LIMIT_V7X = 56 * 1024 * 1024
VMEM_LIMIT_MLP_V7X = 60 * 1024 * 1024
ADA_TN = 1536
TM_LATENT = 512
TM_CONTEXT = 512
MLP_TH = 2048
MLP_TH_INNER = 512
NORM_CHUNK_ROWS = 32
NORM_UNROLL = 4
MOD_ROWS = 16


def _cparams(sem, vmem_limit=VMEM_LIMIT_V7X):
    return pltpu.CompilerParams(dimension_semantics=sem, vmem_limit_bytes=vmem_limit)


def _silu(x):
    h = 0.5 * x
    return h + h * jnp.tanh(h)


def _gelu_tanh(x):
    return 0.5 * x * (1.0 + jnp.tanh(0.7978845608028654 * (x + 0.044715 * (x * x * x))))


def _norm_modulate_store(x_ref, nw_ref, sh_ref, sc_ref, h_ref, row0=0, n_rows=None, inline=False):
    n_rows = x_ref.shape[0] if n_rows is None else n_rows
    gain = nw_ref[...] * (1.0 + sc_ref[0])
    shift = sh_ref[0]

    def body(i, carry):
        r = row0 + i * NORM_CHUNK_ROWS
        rows = pl.ds(r if inline else pl.multiple_of(r, NORM_CHUNK_ROWS), NORM_CHUNK_ROWS)
        x = x_ref[rows, :]
        rs = lax.rsqrt(jnp.mean(x * x, axis=-1, keepdims=True) + EPS)
        h_ref[rows, :] = (x * rs * gain + shift).astype(h_ref.dtype)
        return carry

    if inline:
        for i in range(n_rows // NORM_CHUNK_ROWS):
            body(i, 0)
    else:
        lax.fori_loop(0, n_rows // NORM_CHUNK_ROWS, body, 0, unroll=NORM_UNROLL)


def _ada_kernel(c_ref, w_ref, b_ref, o_ref):
    s = _silu(c_ref[...]).astype(BF16)
    o_ref[0] = jnp.dot(s, w_ref[0].astype(BF16), preferred_element_type=F32) + b_ref[0]


def _ada(cond, ada_w, ada_b):
    depth, d, n = ada_w.shape
    tn = ADA_TN
    return pl.pallas_call(
        _ada_kernel,
        grid=(depth, n // tn),
        in_specs=[
            pl.BlockSpec((cond.shape[0], d), lambda l, j: (0, 0)),
            pl.BlockSpec((1, d, tn), lambda l, j: (l, 0, j)),
            pl.BlockSpec((1, 1, tn), lambda l, j: (l, 0, j)),
        ],
        out_specs=pl.BlockSpec((1, cond.shape[0], tn), lambda l, j: (l, 0, j)),
        out_shape=jax.ShapeDtypeStruct((depth, cond.shape[0], n), F32),
        compiler_params=_cparams(("arbitrary", "arbitrary")),
        name="ada",
    )(cond, ada_w, ada_b.reshape(depth, 1, n))


def _inproj_qkv_kernel(x_ref, nw_ref, sh_ref, sc_ref, w_ref, o_ref, h_ref):
    half = x_ref.shape[0] // 4
    _norm_modulate_store(x_ref, nw_ref, sh_ref, sc_ref, h_ref, 0, half)
    for r0 in range(0, x_ref.shape[0], half):
        rows = slice(r0, r0 + half)
        h = h_ref[rows, :]
        q = jnp.dot(h, w_ref[:, :NA_WIDTH], preferred_element_type=F32)
        o_ref[rows, :NA_WIDTH] = (q * NA_SCORE_SCALE).astype(o_ref.dtype)
        if r0 + half < x_ref.shape[0]:
            _norm_modulate_store(x_ref, nw_ref, sh_ref, sc_ref, h_ref, r0 + half, half, inline=True)
        o_ref[rows, NA_WIDTH:] = jnp.dot(h, w_ref[:, NA_WIDTH:], preferred_element_type=F32).astype(o_ref.dtype)


def _inproj_rest_kernel(h_ref, wa_ref, wb_ref, o_ref):
    h = h_ref[...]
    o_ref[:, :REST_MAIN] = jnp.dot(h, wa_ref[...], preferred_element_type=F32)
    o_ref[:, REST_MAIN:] = jnp.dot(h, wb_ref[...], preferred_element_type=F32)


def _inproj(x2d, nw, mod, w, layer, rows_per_mod, mod_row0, tm):
    m, d = x2d.shape
    mrow = lambda i: layer * MOD_ROWS + mod_row0 + (i * tm) // rows_per_mod
    assert w.shape[0] == 1 and w.shape[2] == QKV_WIDTH + REST_MAIN + REST_TAIL
    assert (QKV_WIDTH + REST_MAIN) % REST_TAIL == 0
    qkv, h = pl.pallas_call(
        _inproj_qkv_kernel,
        grid=(m // tm,),
        in_specs=[
            pl.BlockSpec((tm, d), lambda i: (i, 0)),
            pl.BlockSpec((None, 1, d), lambda i: (layer, 0, 0)),
            pl.BlockSpec((1, 1, d), lambda i: (mrow(i), 0, 0)),
            pl.BlockSpec((1, 1, d), lambda i: (mrow(i), 0, 1)),
            pl.BlockSpec((None, d, QKV_WIDTH), lambda i: (0, 0, 0)),
        ],
        out_specs=[pl.BlockSpec((tm, QKV_WIDTH), lambda i: (i, 0)), pl.BlockSpec((tm, d), lambda i: (i, 0))],
        out_shape=[jax.ShapeDtypeStruct((m, QKV_WIDTH), BF16), jax.ShapeDtypeStruct((m, d), BF16)],
        compiler_params=_cparams(("arbitrary",)),
        name="inproj_qkv",
    )(x2d, nw, mod, mod, w)
    rest = pl.pallas_call(
        _inproj_rest_kernel,
        grid=(m // tm,),
        in_specs=[
            pl.BlockSpec((tm, d), lambda i: (i, 0)),
            pl.BlockSpec((None, d, REST_MAIN), lambda i: (0, 0, 1)),
            pl.BlockSpec((None, d, REST_TAIL), lambda i: (0, 0, (QKV_WIDTH + REST_MAIN) // REST_TAIL)),
        ],
        out_specs=pl.BlockSpec((tm, REST_WIDTH), lambda i: (i, 0)),
        out_shape=jax.ShapeDtypeStruct((m, REST_WIDTH), F32),
        compiler_params=_cparams(("arbitrary",)),
        name="inproj_rest",
    )(h, w, w)
    return qkv, rest


NA_GROUP_ROWS = 4
NA_GROUP_TOK = NA_GROUP_ROWS * GRID_W
NA_WIN_BLOCKS = 3
NA_GROUPS_PER_STEP = 8
NA_LOOP_UNROLL = 1


NA_MASKED_SLAB = 2 * NA_WIN_ROWS - 1


def _na_bias_slabs(rpb):
    h = rpb.shape[0]
    kcol = np.arange(GRID_W)[:, None]
    qcol = np.arange(GRID_W)[None, :]
    wstart = np.clip(qcol - NA_WIN_COLS // 2, 0, GRID_W - NA_WIN_COLS)
    valid_col = (kcol >= wstart) & (kcol < wstart + NA_WIN_COLS)
    pad = GRID_W - NA_WIN_COLS
    padded = jnp.pad(rpb.astype(F32), ((0, 0), (0, 0), (pad, pad)))
    m = jnp.tile(padded, (1, 1, GRID_W + 1))[:, :, :GRID_W * 2 * GRID_W]
    m = m.reshape(h, rpb.shape[1], GRID_W, 2 * GRID_W)[..., :GRID_W]
    slabs = jnp.where(valid_col, m[..., ::-1] * LOG2E, -1e30)
    slabs = jnp.concatenate([slabs, jnp.full((h, 1, GRID_W, GRID_W), -1e30, F32)], axis=1)
    return jnp.concatenate([slabs, slabs], axis=-1)


def _na_slab_index(rows):
    n_groups = rows // NA_GROUP_ROWS
    krel = np.arange(NA_WIN_BLOCKS * NA_GROUP_ROWS)[:, None]
    qrel = np.arange(NA_GROUP_ROWS)[None, :]
    idx = []
    for g in (0, 1, n_groups - 1):
        krow = int(np.clip(g - 1, 0, n_groups - NA_WIN_BLOCKS)) * NA_GROUP_ROWS + krel
        qrow = g * NA_GROUP_ROWS + qrel
        r0 = np.clip(qrow - NA_WIN_ROWS // 2, 0, rows - NA_WIN_ROWS)
        in_window = (krow >= r0) & (krow < r0 + NA_WIN_ROWS)
        idx.append(np.where(in_window, krow - qrow + NA_WIN_ROWS - 1, NA_MASKED_SLAB))
    return np.stack(idx).tolist()


def _na_kernel(q_ref, k_ref, v_ref, kc_ref, vc_ref, slab_ref, *refs, n_weights, slab_index):
    w_refs, o_ref, w_bf16_refs = refs[:n_weights], refs[n_weights], refs[n_weights + 1:2 * n_weights + 1]
    vt_ref, bias_ref = refs[2 * n_weights + 1:]
    n_groups = q_ref.shape[0] // NA_GROUP_TOK
    gt = NA_GROUP_TOK
    nt = (((1,), (1,)), ((), ()))
    dot = functools.partial(jnp.dot, preferred_element_type=F32)

    for src_ref, dst_ref in zip(w_refs, w_bf16_refs):
        dst_ref[...] = src_ref[...].astype(BF16)

    @pl.when(pl.program_id(1) == 0)
    def _():
        left = lax.broadcasted_iota(jnp.int32, (GRID_W, 2 * GRID_W), 1) < GRID_W
        for kind, per_key_row in enumerate(slab_index):
            for j, per_query_row in enumerate(per_key_row):
                for a in range(0, NA_GROUP_ROWS, 2):
                    tile = jnp.where(left, slab_ref[0, per_query_row[a]], slab_ref[0, per_query_row[a + 1]])
                    bias_ref[kind, j * GRID_W:(j + 1) * GRID_W, a * GRID_W:(a + 2) * GRID_W] = tile

    for b in range(n_groups):
        vt_ref[b] = v_ref[b * gt:(b + 1) * gt, :].T
    kc = kc_ref[...]
    vct = vc_ref[...].T

    def group(g):
        blk = jnp.clip(g - 1, 0, n_groups - NA_WIN_BLOCKS)
        kind = jnp.where(g == 0, 0, jnp.where(g == n_groups - 1, 2, 1))
        qg = q_ref[pl.ds(pl.multiple_of(g * gt, gt), gt), :]
        kw = k_ref[pl.ds(pl.multiple_of(blk * gt, gt), NA_WIN_BLOCKS * gt), :]
        s_lat = lax.dot_general(kw, qg, nt, preferred_element_type=F32) + bias_ref[kind]
        s_ctx = lax.dot_general(kc, qg, nt, preferred_element_type=F32)
        yield
        m = jnp.maximum(jnp.max(s_lat, axis=0, keepdims=True), jnp.max(s_ctx, axis=0, keepdims=True))
        p_lat = jnp.exp2(s_lat - m)
        p_ctx = jnp.exp2(s_ctx - m)
        l = jnp.sum(p_lat, axis=0, keepdims=True) + jnp.sum(p_ctx, axis=0, keepdims=True)
        yield
        ot = dot(vct, p_ctx.astype(BF16))
        for j in range(NA_WIN_BLOCKS):
            ot = ot + dot(vt_ref[blk + j], p_lat[j * gt:(j + 1) * gt, :].astype(BF16))
        yield
        o_ref[pl.ds(pl.multiple_of(g * gt, gt), gt), :] = (ot / l).T.astype(o_ref.dtype)
        yield

    def body(i, carry):
        groups = [group(i * NA_GROUPS_PER_STEP + u) for u in range(NA_GROUPS_PER_STEP)]
        for _ in range(4):
            for grp in groups:
                next(grp)
        return carry

    lax.fori_loop(0, n_groups // NA_GROUPS_PER_STEP, body, 0, unroll=NA_LOOP_UNROLL)


def _na_latent(p_l, p_c, slabs, weights, b):
    t = p_l.shape[0] // b
    ctx = p_c.shape[0] // b
    hd = NA_HEAD_DIM
    n_steps = NA_HEADS * b
    w_specs_in, w_specs_out, w_shapes = [], [], []
    for w, layer in weights:
        rows, cols = w.shape[1] // n_steps, w.shape[2]
        assert w.shape[1] % n_steps == 0 and rows % BF16_SUBLANES == 0
        w_specs_in.append(pl.BlockSpec((None, rows, cols), lambda h, i, layer=layer: (layer, h * b + i, 0)))
        w_specs_out.append(pl.BlockSpec((None, rows, cols), lambda h, i: (0, h * b + i, 0)))
        w_shapes.append(jax.ShapeDtypeStruct((1,) + w.shape[1:], BF16))
    oa, *w_bf16 = pl.pallas_call(
        functools.partial(_na_kernel, n_weights=len(weights), slab_index=_na_slab_index(t // GRID_W)),
        grid=(NA_HEADS, b),
        in_specs=[
            pl.BlockSpec((t, hd), lambda h, i: (i, _QA + h)),
            pl.BlockSpec((t, hd), lambda h, i: (i, _KA + h)),
            pl.BlockSpec((t, hd), lambda h, i: (i, _VA + h)),
            pl.BlockSpec((ctx, hd), lambda h, i: (i, _KA + h)),
            pl.BlockSpec((ctx, hd), lambda h, i: (i, _VA + h)),
            pl.BlockSpec((1,) + slabs.shape[1:], lambda h, i: (h, 0, 0, 0)),
        ] + w_specs_in,
        out_specs=[pl.BlockSpec((t, hd), lambda h, i: (i, h))] + w_specs_out,
        out_shape=[jax.ShapeDtypeStruct((b * t, NA_WIDTH), BF16)] + w_shapes,
        scratch_shapes=[
            pltpu.VMEM((t // NA_GROUP_TOK, hd, NA_GROUP_TOK), BF16),
            pltpu.VMEM((3, NA_WIN_BLOCKS * NA_GROUP_TOK, NA_GROUP_TOK), F32),
        ],
        compiler_params=_cparams(("arbitrary", "arbitrary")),
        name="na_latent",
    )(p_l, p_l, p_l, p_c, p_c, slabs, *(w for w, _ in weights))
    return oa, w_bf16


def _ctx_attn_kernel(q_ref, k_ref, v_ref, o_ref):
    for h in range(NA_HEADS):
        hs = slice(h * NA_HEAD_DIM, (h + 1) * NA_HEAD_DIM)
        s = lax.dot_general(q_ref[:, hs], k_ref[:, hs], (((1,), (1,)), ((), ())), preferred_element_type=F32)
        p = jnp.exp2(s - jnp.max(s, axis=-1, keepdims=True))
        l = jnp.sum(p, axis=-1, keepdims=True)
        o = jnp.dot(p.astype(BF16), v_ref[:, hs], preferred_element_type=F32)
        o_ref[:, hs] = (o / l).astype(o_ref.dtype)


def _ctx_attention(p_c, b):
    ctx = p_c.shape[0] // b
    return pl.pallas_call(
        _ctx_attn_kernel,
        grid=(b,),
        in_specs=[
            pl.BlockSpec((ctx, NA_WIDTH), lambda i: (i, 0)),
            pl.BlockSpec((ctx, NA_WIDTH), lambda i: (i, 1)),
            pl.BlockSpec((ctx, NA_WIDTH), lambda i: (i, 2)),
        ],
        out_specs=pl.BlockSpec((ctx, NA_WIDTH), lambda i: (i, 0)),
        out_shape=jax.ShapeDtypeStruct((b * ctx, NA_WIDTH), BF16),
        compiler_params=_cparams(("arbitrary",)),
        name="ctx_attention",
    )(p_c, p_c, p_c)


def _gm_block(u_ref, v_ref, lnw_ref, ws_ref, bs_ref, o_ref):
    for ck in range(u_ref.shape[0] // GM_CHUNK):
        rows = slice(ck * GM_CHUNK, (ck + 1) * GM_CHUNK)
        uf = _gelu_tanh(u_ref[rows, :])
        vf = _gelu_tanh(v_ref[rows, :])
        for g in range(GM_GROUPS):
            sl = slice(g * GM_DIM, (g + 1) * GM_DIM)
            vg = vf[:, sl]
            mu = jnp.mean(vg, axis=-1, keepdims=True)
            dv = vg - mu
            var = jnp.mean(dv * dv, axis=-1, keepdims=True)
            vn = dv * lax.rsqrt(var + EPS) * lnw_ref[:, sl]
            mixed = jnp.dot(ws_ref[g].astype(BF16), vn.astype(BF16), preferred_element_type=F32) + bs_ref[g]
            o_ref[rows, sl] = (uf[:, sl] * mixed).astype(o_ref.dtype)


HG_PAIR = 2
HG_LEVELS = (32, 16, 8)
HG_DIAG = 8
HG_PREP_ROWS = 256
HG_SCAN_UNROLL = 8


def _split3_dot(tri, g):
    g0 = g.astype(BF16)
    r1 = g - g0.astype(F32)
    g1 = r1.astype(BF16)
    g2 = (r1 - g1.astype(F32)).astype(BF16)
    dot = functools.partial(jnp.dot, preferred_element_type=F32)
    return dot(tri, g0) + dot(tri, g1) + dot(tri, g2)


def _hg_kernel(ql_ref, ffl_ref, fbl_ref, il_ref,
               qc_ref, ffc_ref, fbc_ref, ic_ref,
               lbp_ref, tri_ref, code_ref, lsum_ref, ol_ref, oc_ref,
               qh_s, kf_s, kb_s, ef_s, eb_s, of_s, ob_s, st_s):
    c = HG_CHUNK
    hd = HG_DIM
    n_ctx = qc_ref.shape[0]
    n_lat = ql_ref.shape[0]
    scale = hd ** -0.5
    nt = (((1,), (1,)), ((), ()))
    tn = (((0,), (0,)), ((), ()))

    def prep(q_ref, ff_ref, fb_ref, base, n):
        step = HG_PREP_ROWS
        for t0 in range(0, n, step):
            src = slice(t0, t0 + step)
            dst = slice(base + t0, base + t0 + step)
            qh = _silu(q_ref[src, :]) * scale
            for h in range(HG_PAIR):
                qh_s[h, dst, :] = qh[:, h * hd:(h + 1) * hd]
            for d, (f_ref, k_s, e_s) in enumerate(((ff_ref, kf_s, ef_s), (fb_ref, kb_s, eb_s))):
                x = f_ref[src, :]
                sp = jnp.maximum(-x, 0.0) + jnp.log(1.0 + jnp.exp(-jnp.abs(x)))
                lb = lbp_ref[d, 0:1, :]
                y = lbp_ref[d, 1:2, :] - sp
                ey = jnp.exp(y)
                log_f = jnp.where(lb > 0.0, jnp.log(lb + ey), y)
                one_minus_f = jnp.maximum(lbp_ref[d, 2:3, :] - ey, 0.0)
                e = _split3_dot(tri_ref[d], log_f) * LOG2E
                for h in range(HG_PAIR):
                    e_s[h, dst, :] = e[:, h * hd:(h + 1) * hd]
                    k_s[h, dst, :] = one_minus_f[:, h * hd:(h + 1) * hd]

    prep(qc_ref, ffc_ref, fbc_ref, 0, n_ctx)
    prep(ql_ref, ffl_ref, fbl_ref, n_ctx, n_lat)

    diag_code = len(HG_LEVELS) + 1

    st_s[...] = jnp.zeros_like(st_s)

    def chunk(off, v, h, d, out):
        fwd = d == 0
        k_s, e_s = (kf_s, ef_s) if fwd else (kb_s, eb_s)
        q = qh_s[h, pl.ds(off, c), :]
        k = k_s[h, pl.ds(off, c), :]
        e = e_s[h, pl.ds(off, c), :]
        code = code_ref[d]
        prods = []
        for i in range(c // HG_DIAG):
            bs = slice(i * HG_DIAG, (i + 1) * HG_DIAG)
            row_prods = []
            for s in range(HG_DIAG):
                kr = k_s[h, pl.ds(off + i * HG_DIAG + s, 1), :]
                er = e_s[h, pl.ds(off + i * HG_DIAG + s, 1), :]
                decay = jnp.exp2(jnp.minimum(e[bs] - er, 0.0))
                row_prods.append((q[bs] * kr * decay).astype(BF16))
            prods.append(jnp.concatenate(row_prods, axis=1))
        diag = jnp.dot(jnp.concatenate(prods, axis=0), lsum_ref[...], preferred_element_type=F32)
        yield
        att = None
        for li, w in enumerate(HG_LEVELS):
            zeros = jnp.zeros((w, hd), F32)
            q_parts, k_parts = [], []
            for a in range(0, c, 2 * w):
                lo, hi = slice(a, a + w), slice(a + w, a + 2 * w)
                if fwd:
                    ref = e_s[h, pl.ds(off + a + w - 1, 1), :]
                    q_parts += [zeros, q[hi] * jnp.exp2(e[hi] - ref)]
                    k_parts += [k[lo] * jnp.exp2(ref - e[lo]), zeros]
                else:
                    ref = e_s[h, pl.ds(off + a + w, 1), :]
                    q_parts += [q[lo] * jnp.exp2(e[lo] - ref), zeros]
                    k_parts += [zeros, k[hi] * jnp.exp2(ref - e[hi])]
            qs = jnp.concatenate(q_parts, axis=0).astype(BF16)
            ks = jnp.concatenate(k_parts, axis=0).astype(BF16)
            a_w = lax.dot_general(qs, ks, nt, preferred_element_type=F32)
            att = a_w if att is None else jnp.where(code == li + 1, a_w, att)
            yield
        e_end = e_s[h, pl.ds(off + (c - 1 if fwd else 0), 1), :]
        st = st_s[h, d]
        qi = (q * jnp.exp2(e)).astype(BF16)
        ki = (k * jnp.exp2(e_end - e)).astype(BF16)
        o = lax.dot_general(qi, st.astype(BF16), nt, preferred_element_type=F32)
        yield
        st_new = st * jnp.exp2(e_end) + lax.dot_general(v, ki, tn, preferred_element_type=F32)
        yield
        att = jnp.where(code == diag_code, diag, att)
        out.append((o + jnp.dot(att.astype(BF16), v, preferred_element_type=F32), st_new))
        yield

    n_stages = len(HG_LEVELS) + 4

    def scan(v_ref, base, n):
        def body(i, carry):
            rf = pl.multiple_of(i * c, c)
            rb = pl.multiple_of((n - 1 - i) * c, c)
            vf = v_ref[pl.ds(rf, c), :].astype(BF16)
            vb = v_ref[pl.ds(rb, c), :].astype(BF16)
            chains = []
            for h in range(HG_PAIR):
                hs = slice(h * hd, (h + 1) * hd)
                for d, r, v in ((0, rf, vf), (1, rb, vb)):
                    out = []
                    chains.append((h, d, r, out, chunk(base + r, v[:, hs], h, d, out)))
            for _ in range(n_stages):
                for chain in chains:
                    next(chain[-1])
            for h, d, r, out, _ in chains:
                o, st_new = out[0]
                (of_s if d == 0 else ob_s)[h, pl.ds(base + r, c), :] = o
                st_s[h, d] = st_new
            return carry

        lax.fori_loop(0, n, body, 0, unroll=HG_SCAN_UNROLL)

    scan(ic_ref, 0, n_ctx // c)
    scan(il_ref, n_ctx, n_lat // c)

    def finish(o_ref, base, n):
        step = HG_PREP_ROWS
        for t0 in range(0, n, step):
            src = slice(t0, t0 + step)
            dst = slice(base + t0, base + t0 + step)
            for h in range(HG_PAIR):
                o_ref[src, h * hd:(h + 1) * hd] = of_s[h, dst, :] + ob_s[h, dst, :]

    finish(oc_ref, 0, n_ctx)
    finish(ol_ref, n_ctx, n_lat)


def _hg_constants():
    c = HG_CHUNK
    row, col = np.arange(c)[:, None], np.arange(c)[None, :]
    blocks = np.eye(HG_PREP_ROWS // c)
    tri = np.stack([np.kron(blocks, col <= row), np.kron(blocks, col >= row)]).astype(np.float32)
    same8 = (row // HG_DIAG) == (col // HG_DIAG)
    diag_code = len(HG_LEVELS) + 1
    code_f = np.where(same8 & (col <= row), diag_code, 0)
    code_b = np.where(same8 & (col >= row), diag_code, 0)
    for li, w in enumerate(HG_LEVELS):
        same = (row // (2 * w)) == (col // (2 * w))
        t_hi, s_hi = (row % (2 * w)) >= w, (col % (2 * w)) >= w
        code_f = np.where(same & t_hi & ~s_hi, li + 1, code_f)
        code_b = np.where(same & ~t_hi & s_hi, li + 1, code_b)
    codes = np.stack([code_f, code_b]).astype(np.int32)
    lane_sum = (np.arange(HG_DIAG * HG_DIM)[:, None] // HG_DIM == col % HG_DIAG).astype(np.float32)
    return jnp.asarray(tri, BF16), jnp.asarray(codes), jnp.asarray(lane_sum, BF16)


def _hgrn2(p_l, p_c, lbp, b):
    tri, codes, lane_sum = _hg_constants()
    t = p_l.shape[0] // b
    ctx = p_c.shape[0] // b
    hd = HG_DIM
    pw = HG_PAIR * hd
    n = t + ctx
    first = _HG0 * LANE // pw

    def col(stream):
        return lambda i, j: (i, first + stream * (HG_HEADS // HG_PAIR) + j)

    lat_specs = [pl.BlockSpec((t, pw), col(s)) for s in range(4)]
    ctx_specs = [pl.BlockSpec((ctx, pw), col(s)) for s in range(4)]
    big = lambda: pltpu.VMEM((HG_PAIR, n, hd), F32)
    return pl.pallas_call(
        _hg_kernel,
        grid=(b, HG_HEADS // HG_PAIR),
        in_specs=lat_specs + ctx_specs + [
            pl.BlockSpec((2, 3, pw), lambda i, j: (0, 0, j)),
            pl.BlockSpec(tri.shape, lambda i, j: (0, 0, 0)),
            pl.BlockSpec(codes.shape, lambda i, j: (0, 0, 0)),
            pl.BlockSpec(lane_sum.shape, lambda i, j: (0, 0)),
        ],
        out_specs=[
            pl.BlockSpec((t, pw), lambda i, j: (i, j)),
            pl.BlockSpec((ctx, pw), lambda i, j: (i, j)),
        ],
        out_shape=[
            jax.ShapeDtypeStruct((b * t, HG_WIDTH), F32),
            jax.ShapeDtypeStruct((b * ctx, HG_WIDTH), F32),
        ],
        scratch_shapes=[big() for _ in range(7)] + [pltpu.VMEM((HG_PAIR, 2, hd, hd), F32)],
        compiler_params=_cparams(("arbitrary", "arbitrary")),
        name="hgrn2",
    )(*([p_l] * 4 + [p_c] * 4 + [lbp, tri, codes, lane_sum]))


def _outproj_kernel(oa_ref, os_ref, hg_ref, hnw_ref, u_ref, v_ref, lnw_ref, ws_ref, bs_ref,
                    wa_ref, wb_ref, wc_ref, x_ref, g_ref, o_ref, ob_ref, oc_ref):
    dot = functools.partial(jnp.dot, preferred_element_type=F32)
    y = dot(oa_ref[...], wa_ref[...])
    for h in range(HG_HEADS):
        hs = slice(h * HG_DIM, (h + 1) * HG_DIM)
        o = os_ref[:, hs]
        o = o * lax.rsqrt(jnp.mean(o * o, axis=-1, keepdims=True) + EPS) * hnw_ref[...]
        ob_ref[:, hs] = (o * _silu(hg_ref[:, hs])).astype(ob_ref.dtype)
    y = y + dot(ob_ref[...], wb_ref[...])
    _gm_block(u_ref, v_ref, lnw_ref, ws_ref, bs_ref, oc_ref)
    y = y + dot(oc_ref[...], wc_ref[...])
    o_ref[...] = x_ref[...] + g_ref[0] * y


def _outproj(oa, osum, p2, hg_norm_w, gm_ln_w, gm_ws, gm_bs, w, w_layer, x2d, mod, layer, rows_per_mod,
             mod_row0, tm):
    m, d = x2d.shape
    mrow = lambda i: layer * MOD_ROWS + mod_row0 + (i * tm) // rows_per_mod
    nb = NA_WIDTH // HG_WIDTH
    assert tm % GM_CHUNK == 0
    return pl.pallas_call(
        _outproj_kernel,
        grid=(m // tm,),
        in_specs=[
            pl.BlockSpec((tm, NA_WIDTH), lambda i: (i, 0)),
            pl.BlockSpec((tm, HG_WIDTH), lambda i: (i, 0)),
            pl.BlockSpec((tm, HG_WIDTH), lambda i: (i, _HG_G)),
            pl.BlockSpec((1, HG_DIM), lambda i: (0, 0)),
            pl.BlockSpec((tm, GM_WIDTH), lambda i: (i, _GM_U)),
            pl.BlockSpec((tm, GM_WIDTH), lambda i: (i, _GM_V)),
            pl.BlockSpec((1, GM_WIDTH), lambda i: (0, 0)),
            pl.BlockSpec((GM_GROUPS, GM_CHUNK, GM_CHUNK), lambda i: (0, 0, 0)),
            pl.BlockSpec((GM_GROUPS, GM_CHUNK, 1), lambda i: (0, 0, 0)),
            pl.BlockSpec((None, NA_WIDTH, d), lambda i: (w_layer, 0, 0)),
            pl.BlockSpec((None, HG_WIDTH, d), lambda i: (w_layer, nb, 0)),
            pl.BlockSpec((None, GM_WIDTH, d), lambda i: (w_layer, nb + 1, 0)),
            pl.BlockSpec((tm, d), lambda i: (i, 0)),
            pl.BlockSpec((1, 1, d), lambda i: (mrow(i), 0, 2)),
        ],
        out_specs=pl.BlockSpec((tm, d), lambda i: (i, 0)),
        out_shape=jax.ShapeDtypeStruct((m, d), F32),
        scratch_shapes=[pltpu.VMEM((tm, HG_WIDTH), BF16), pltpu.VMEM((tm, GM_WIDTH), BF16)],
        compiler_params=_cparams(("arbitrary",)),
        name="outproj",
    )(oa, osum, p2, hg_norm_w.reshape(1, HG_DIM), p2, p2, gm_ln_w.reshape(1, GM_WIDTH), gm_ws,
      gm_bs.reshape(GM_GROUPS, GM_CHUNK, 1), w, w, w, x2d, mod)


def _mlp_kernel(x_ref, nw_ref, sh_ref, sc_ref, g_ref, w1_ref, w2_ref, fnw_ref, o_ref, h_ref, *, final_norm):
    j = pl.program_id(1)

    @pl.when(j == 0)
    def _():
        _norm_modulate_store(x_ref, nw_ref, sh_ref, sc_ref, h_ref)
        o_ref[...] = jnp.zeros_like(o_ref)

    h = h_ref[...]
    for c0 in range(0, w1_ref.shape[1], MLP_TH_INNER):
        cols = slice(c0, c0 + MLP_TH_INNER)
        a = jnp.maximum(jnp.dot(h, w1_ref[:, cols], preferred_element_type=F32), 0.0)
        o_ref[...] += jnp.dot((a * a).astype(BF16), w2_ref[cols, :], preferred_element_type=F32)

    @pl.when(j == pl.num_programs(1) - 1)
    def _():
        y = x_ref[...] + g_ref[0] * o_ref[...]
        if final_norm:
            y = y * lax.rsqrt(jnp.mean(y * y, axis=-1, keepdims=True) + EPS) * fnw_ref[...]
        o_ref[...] = y


def _mlp(x2d, nw, mod, w1, w2, w_layer, fnw, layer, rows_per_mod, mod_row0, tm, final_norm):
    m, d = x2d.shape
    hid = w1.shape[2]
    th = MLP_TH
    mrow = lambda i: layer * MOD_ROWS + mod_row0 + (i * tm) // rows_per_mod
    return pl.pallas_call(
        functools.partial(_mlp_kernel, final_norm=final_norm),
        grid=(m // tm, hid // th),
        in_specs=[
            pl.BlockSpec((tm, d), lambda i, j: (i, 0)),
            pl.BlockSpec((None, 1, d), lambda i, j: (layer, 0, 0)),
            pl.BlockSpec((1, 1, d), lambda i, j: (mrow(i), 0, 3)),
            pl.BlockSpec((1, 1, d), lambda i, j: (mrow(i), 0, 4)),
            pl.BlockSpec((1, 1, d), lambda i, j: (mrow(i), 0, 5)),
            pl.BlockSpec((None, d, th), lambda i, j: (w_layer, 0, j)),
            pl.BlockSpec((None, th, d), lambda i, j: (w_layer, j, 0)),
            pl.BlockSpec((1, d), lambda i, j: (0, 0)),
        ],
        out_specs=pl.BlockSpec((tm, d), lambda i, j: (i, 0)),
        out_shape=jax.ShapeDtypeStruct((m, d), F32),
        scratch_shapes=[pltpu.VMEM((tm, d), BF16)],
        compiler_params=_cparams(("arbitrary", "arbitrary"), VMEM_LIMIT_MLP_V7X),
        name="mlp",
    )(x2d, nw, mod, mod, mod, w1, w2, fnw)


def kernel(x, c, ctx, c_ctx, ada_w, ada_b, norm1_w, norm2_w, w_in, na_rpb, hg_lb_logits, hg_norm_w,
           gm_ln_w, gm_ws, gm_bs, w_out, mlp_w1, mlp_w2, final_norm_w):
    bsz, seq, d = x.shape
    n_ctx = ctx.shape[1]
    depth = ada_w.shape[0]
    assert bsz < MOD_ROWS and d == D_MODEL and seq % 512 == 0 and n_ctx % 256 == 0
    assert seq // NA_GROUP_TOK >= NA_WIN_BLOCKS and seq // GRID_W >= 2 * NA_WIN_ROWS

    lb = jnp.cumsum(jax.nn.softmax(hg_lb_logits.astype(F32), axis=0), axis=0)
    lb = lb - lb[:1]
    lbp = jnp.stack([lb, jnp.log1p(-lb), 1.0 - lb], axis=2)

    cond = jnp.zeros((MOD_ROWS, d), F32).at[:bsz].set(c).at[bsz].set(c_ctx)
    mod = _ada(cond, ada_w, ada_b).reshape(depth * MOD_ROWS, 1, 6 * d)

    w_in_b = w_in[:1].astype(BF16)
    nw1 = norm1_w.reshape(depth, 1, d)
    nw2 = norm2_w.reshape(depth, 1, d)
    fnw = final_norm_w.reshape(1, d)

    xl = x.reshape(bsz * seq, d)
    xc = ctx.reshape(bsz * n_ctx, d)
    n_c = bsz * n_ctx
    tm_l, tm_c = TM_LATENT, min(TM_CONTEXT, n_c)
    for l in range(depth):
        need_ctx = l < depth - 1
        qkv_l, p_l = _inproj(xl, nw1, mod, w_in_b, l, seq, 0, tm_l)
        qkv_c, p_c = _inproj(xc, nw1, mod, w_in_b, l, n_c, bsz, tm_c)

        to_cast = [(w_out, l), (mlp_w1, l), (mlp_w2, l)] + ([(w_in, l + 1)] if need_ctx else [])
        oa_l, (w_out_b, w1_b, w2_b, *w_in_next) = _na_latent(qkv_l, qkv_c, _na_bias_slabs(na_rpb[l]),
                                                             to_cast, bsz)
        w_in_b = w_in_next[0] if w_in_next else None
        os_l, os_c = _hgrn2(p_l, p_c, lbp[l], bsz)
        mix = (hg_norm_w[l], gm_ln_w[l], gm_ws[l], gm_bs[l])
        xl = _outproj(oa_l, os_l, p_l, *mix, w_out_b, 0, xl, mod, l, seq, 0, tm_l)
        xl = _mlp(xl, nw2, mod, w1_b, w2_b, 0, fnw, l, seq, 0, tm_l, final_norm=not need_ctx)
        if need_ctx:
            oa_c = _ctx_attention(qkv_c, bsz)
            xc = _outproj(oa_c, os_c, p_c, *mix, w_out_b, 0, xc, mod, l, n_c, bsz, tm_c)
            xc = _mlp(xc, nw2, mod, w1_b, w2_b, 0, fnw, l, n_c, bsz, tm_c, final_norm=False)
    return xl.reshape(bsz, seq, d)
```
